```python
import math
import jax
import jax.numpy as jnp
from jax import lax
import numpy as np

D_MODEL = 1024
BATCH = 8
SEQ = 2048
DEPTH = 1
DEC_BATCH = 1
DEC_SEQ = 16384
PAST_LEN = 128

RET_HEADS = 4
RET_DK = 128
RET_DV = 256
RET_CHUNK = 128
ROPE_THETA = 10000.0
ATT_GROUPS = ((128, 1), (512, 4), (2048, 16))
N_GROUPS = 3
ATT_HEADS = 8
ATT_DH = 64
ATT_BLOCK = 64
T5_BUCKETS = 32
T5_MAX_DIST = 1024
N_EXPERTS = 32
TOP_K = 4
D_FF = 1024
SWIGLU_ALPHA = 1.702
SWIGLU_LIMIT = 7.0
MOE_BLOCK = 128
PLE_DIM = 256
EPS = 1e-6

RET_QK_W = RET_HEADS * RET_DK
RET_V_W = RET_HEADS * RET_DV
ATT_W = ATT_HEADS * ATT_DH
IN_SPLITS = (RET_QK_W, RET_QK_W, RET_V_W, RET_V_W) + (ATT_W,) * (3 * N_GROUPS) + (D_MODEL, D_MODEL)
N_IN = 2 * RET_QK_W + 2 * RET_V_W + 3 * N_GROUPS * ATT_W + 2 * D_MODEL

kernel_name = 'hybrid_retention_dilated_moe_encoder'


def rms_norm(x, g):
    xf = x.astype(jnp.float32)
    y = xf * lax.rsqrt(jnp.mean(xf * xf, axis=-1, keepdims=True) + EPS)
    return (y * g.astype(jnp.float32)).astype(x.dtype)


def head_group_norm(x, g):
    xf = x.astype(jnp.float32)
    mu = jnp.mean(xf, axis=-1, keepdims=True)
    xc = xf - mu
    var = jnp.mean(xc * xc, axis=-1, keepdims=True)
    return (xc * lax.rsqrt(var + EPS) * g.astype(jnp.float32)).astype(x.dtype)


def rope(x, pos):
    half = x.shape[-1] // 2
    freq = ROPE_THETA ** (-jnp.arange(half, dtype=jnp.float32) / half)
    ang = pos.astype(jnp.float32)[:, None] * freq[None, :]
    cos = jnp.cos(ang)[None, :, None, :]
    sin = jnp.sin(ang)[None, :, None, :]
    x1 = x[..., :half].astype(jnp.float32)
    x2 = x[..., half:].astype(jnp.float32)
    return jnp.concatenate([x1 * cos - x2 * sin, x1 * sin + x2 * cos], axis=-1).astype(x.dtype)


def retention_dir(q, k, v, log_gamma, strict):
    B, S, H, DK = q.shape
    DV = v.shape[-1]
    C = RET_CHUNK
    N = S // C
    dt = q.dtype
    q = q.reshape(B, N, C, H, DK)
    k = k.reshape(B, N, C, H, DK)
    v = v.reshape(B, N, C, H, DV)
    idx = jnp.arange(C, dtype=jnp.float32)
    diff = idx[:, None] - idx[None, :]
    allowed = (diff > 0) if strict else (diff >= 0)
    decay = jnp.where(allowed[None], jnp.exp(log_gamma[:, None, None] * jnp.maximum(diff, 0.0)[None]), 0.0).astype(dt)
    scores = jnp.einsum('bnqhd,bnkhd->bnhqk', q, k) * decay[None, None]
    intra = jnp.einsum('bnhqk,bnkhe->bnqhe', scores, v)
    k_dec = k * jnp.exp(log_gamma[None, :] * (C - 1.0 - idx)[:, None]).astype(dt)[:, :, None]
    chunk_kv = jnp.einsum('bnkhd,bnkhe->nbhde', k_dec, v)
    chunk_decay = jnp.exp(log_gamma * C).astype(chunk_kv.dtype)[None, :, None, None]

    def step(state, kv):
        return chunk_decay * state + kv, state

    _, prev_states = lax.scan(step, jnp.zeros_like(chunk_kv[0]), chunk_kv)
    q_dec = q * jnp.exp(log_gamma[None, :] * (idx + 1.0)[:, None]).astype(dt)[:, :, None]
    cross = jnp.einsum('bnqhd,nbhde->bnqhe', q_dec, prev_states)
    return (intra + cross).reshape(B, S, H, DV)


def retention_branch(q, k, v, g, log_gamma, gn_g):
    B, S = q.shape[:2]
    fwd = retention_dir(q, k, v, log_gamma[0], False)
    bwd = retention_dir(q[:, ::-1], k[:, ::-1], v[:, ::-1], log_gamma[1], True)[:, ::-1]
    o = head_group_norm(fwd + bwd, gn_g.reshape(RET_HEADS, RET_DV))
    return jax.nn.silu(g) * o.reshape(B, S, RET_V_W)


def t5_bucket(rel):
    half = T5_BUCKETS // 2
    exact = half // 2
    n = jnp.abs(rel)
    large = exact + (jnp.log(jnp.maximum(n, 1).astype(jnp.float32) / exact)
                     / math.log(T5_MAX_DIST / exact) * (half - exact)).astype(jnp.int32)
    large = jnp.minimum(large, half - 1)
    return jnp.where(rel > 0, half, 0) + jnp.where(n < exact, n, large)


def dilated_window_attention(q, k, v, bias_table, dilation, radius):
    B, S, H, DH = q.shape
    L = S // dilation
    nb = -(-L // ATT_BLOCK)
    Lp = nb * ATT_BLOCK
    Bd = B * dilation

    def to_sub(t):
        t = t.reshape(B, L, dilation, H, DH).transpose(0, 2, 1, 3, 4).reshape(Bd, L, H, DH)
        t = jnp.pad(t, ((0, 0), (0, Lp - L), (0, 0), (0, 0)))
        return t.reshape(Bd, nb, ATT_BLOCK, H, DH)

    def band(t):
        tp = jnp.pad(to_sub(t), ((0, 0), (1, 1), (0, 0), (0, 0), (0, 0)))
        return jnp.concatenate([tp[:, :-2], tp[:, 1:-1], tp[:, 2:]], axis=2)

    qs = to_sub(q)
    ks = band(k)
    vs = band(v)
    qi = jnp.arange(ATT_BLOCK)
    ki = jnp.arange(3 * ATT_BLOCK) - ATT_BLOCK
    rel = ki[None, :] - qi[:, None]
    bias = bias_table[t5_bucket(rel * dilation)].transpose(2, 0, 1).astype(jnp.float32)
    kpos = jnp.arange(nb)[:, None] * ATT_BLOCK + ki[None, :]
    valid = (jnp.abs(rel) <= radius)[None] & ((kpos >= 0) & (kpos < L))[:, None, :]
    s = jnp.einsum('znqhd,znkhd->znhqk', qs, ks).astype(jnp.float32) + bias[None, None]
    s = jnp.where(valid[None, :, None], s, -jnp.inf)
    m = jnp.max(s, axis=-1, keepdims=True)
    e = jnp.exp(s - m)
    denom = jnp.sum(e, axis=-1, keepdims=True)
    o = jnp.einsum('znhqk,znkhd->znqhd', (e / denom).astype(v.dtype), vs)
    lse = (m + jnp.log(denom))[..., 0]

    def from_sub(t):
        t = t[:, :L]
        t = t.reshape((B, dilation, L) + t.shape[2:])
        t = jnp.moveaxis(t, 1, 2)
        return t.reshape((B, S) + t.shape[3:])

    o = from_sub(o.reshape(Bd, Lp, H, DH))
    lse = from_sub(lse.transpose(0, 1, 3, 2).reshape(Bd, Lp, H))
    return o, lse


def moe(h, w_router, b_router, w_gate, b_gate, w_up, b_up, w_down, b_down):
    B, S, D = h.shape
    T = B * S
    x2 = h.reshape(T, D)
    logits = (x2 @ w_router + b_router).astype(jnp.float32)
    top_val, top_idx = lax.top_k(logits, TOP_K)
    gates = jax.nn.softmax(top_val, axis=-1)
    A = T * TOP_K
    flat_e = top_idx.reshape(A)
    order = jnp.argsort(flat_e)
    sorted_e = flat_e[order]
    counts = jnp.zeros((N_EXPERTS,), jnp.int32).at[flat_e].add(1)
    padded = (counts + MOE_BLOCK - 1) // MOE_BLOCK * MOE_BLOCK
    pad_end = jnp.cumsum(padded)
    pad_start = pad_end - padded
    start = jnp.cumsum(counts) - counts
    dest = pad_start[sorted_e] + jnp.arange(A, dtype=jnp.int32) - start[sorted_e]
    n_blocks = -(-A // MOE_BLOCK) + N_EXPERTS
    slot_token = jnp.zeros((n_blocks * MOE_BLOCK,), jnp.int32).at[dest].set((order // TOP_K).astype(jnp.int32))
    block_expert = jnp.minimum(
        jnp.searchsorted(pad_end, jnp.arange(n_blocks, dtype=jnp.int32) * MOE_BLOCK, side='right'),
        N_EXPERTS - 1)
    xs = x2[slot_token].reshape(n_blocks, MOE_BLOCK, D)

    def expert_block(args):
        xb, e = args
        g = xb @ w_gate[e] + b_gate[e]
        u = xb @ w_up[e] + b_up[e]
        g = jnp.minimum(g, SWIGLU_LIMIT)
        u = jnp.clip(u, -SWIGLU_LIMIT, SWIGLU_LIMIT)
        glu = g * jax.nn.sigmoid(SWIGLU_ALPHA * g)
        return ((u + 1.0) * glu) @ w_down[e] + b_down[e]

    ys = lax.map(expert_block, (xs, block_expert)).reshape(n_blocks * MOE_BLOCK, D)
    slot_of = jnp.zeros((A,), jnp.int32).at[order].set(dest).reshape(T, TOP_K)
    y = jnp.einsum('tkd,tk->td', ys[slot_of], gates.astype(ys.dtype))
    return y.reshape(B, S, D)


def encoder_layer(x, p_i, lp, rel_bias):
    B, S, _ = x.shape
    pos = jnp.arange(S)
    h = rms_norm(x, lp['norm_mix_g'])
    z = h @ lp['w_in']
    parts = jnp.split(z, np.cumsum(IN_SPLITS)[:-1].tolist(), axis=-1)
    rq, rk, rv, rg = parts[0], parts[1], parts[2], parts[3]
    att_parts = parts[4:4 + 3 * N_GROUPS]
    gate_ret, gate_att = parts[-2], parts[-1]
    rq = rope(rq.reshape(B, S, RET_HEADS, RET_DK), pos)
    rk = rope(rk.reshape(B, S, RET_HEADS, RET_DK), pos) * (RET_DK ** -0.5)
    rv = rv.reshape(B, S, RET_HEADS, RET_DV)
    log_gamma = jax.nn.log_sigmoid(lp['ret_decay_logit'].astype(jnp.float32))
    y_ret = retention_branch(rq, rk, rv, rg, log_gamma, lp['ret_gn_g'])
    outs = []
    lses = []
    for gi, (window, dil) in enumerate(ATT_GROUPS):
        aq = att_parts[3 * gi].reshape(B, S, ATT_HEADS, ATT_DH)
        ak = att_parts[3 * gi + 1].reshape(B, S, ATT_HEADS, ATT_DH)
        av = att_parts[3 * gi + 2].reshape(B, S, ATT_HEADS, ATT_DH)
        aq = rms_norm(aq, lp['att_q_norm_g'][gi]) * (ATT_DH ** -0.5)
        ak = rms_norm(ak, lp['att_k_norm_g'][gi])
        o_g, lse_g = dilated_window_attention(
            aq, ak, av, rel_bias[:, gi * ATT_HEADS:(gi + 1) * ATT_HEADS], dil, window // (2 * dil))
        outs.append(o_g)
        lses.append(lse_g)
    wts = jax.nn.softmax(jnp.stack(lses, axis=0), axis=0)
    y_att = jnp.sum(wts[..., None].astype(x.dtype) * jnp.stack(outs, axis=0), axis=0).reshape(B, S, ATT_W)
    merged = (jax.nn.sigmoid(gate_ret) * (y_ret @ lp['w_ret_proj'])
              + jax.nn.sigmoid(gate_att) * (y_att @ lp['w_att_proj']))
    x = x + merged @ lp['w_out']
    h2 = rms_norm(x, lp['norm_ffn_g'])
    x = x + moe(h2, lp['w_router'], lp['b_router'], lp['w_gate'], lp['b_gate'],
                lp['w_up'], lp['b_up'], lp['w_down'], lp['b_down'])
    h3 = rms_norm(x, lp['norm_ple_g'])
    x = x + jax.nn.sigmoid(h3 @ lp['w_ple_gate']) * (p_i @ lp['w_ple_proj'])
    return x


def setup_inputs(seed: int = 0) -> dict:
    key = jax.random.key(seed)
    ks = jax.random.split(key, 26)
    f32 = jnp.float32

    def w(k, shape, fan_in):
        return jax.random.normal(k, shape, f32) * (fan_in ** -0.5)

    def gain(k, shape):
        return 1.0 + 0.05 * jax.random.normal(k, shape, f32)

    def small(k, shape, scale):
        return scale * jax.random.normal(k, shape, f32)

    decay_init = jnp.asarray(np.log(2.0 ** (5.0 + np.arange(RET_HEADS)) - 1.0), f32)
    return {
        'x_prompt': jax.random.normal(ks[0], (BATCH, SEQ, D_MODEL), f32),
        'x_sample': jax.random.normal(ks[1], (DEC_BATCH, DEC_SEQ, D_MODEL), f32),
        'p_prompt': jax.random.normal(ks[2], (DEPTH, BATCH, SEQ, PLE_DIM), f32),
        'p_sample': jax.random.normal(ks[3], (DEPTH, DEC_BATCH, DEC_SEQ, PLE_DIM), f32),
        'norm_mix_g': gain(ks[4], (DEPTH, D_MODEL)),
        'w_in': w(ks[5], (DEPTH, D_MODEL, N_IN), D_MODEL),
        'ret_decay_logit': decay_init + small(ks[6], (DEPTH, 2, RET_HEADS), 0.05),
        'ret_gn_g': gain(ks[7], (DEPTH, RET_V_W)),
        'att_q_norm_g': gain(ks[8], (DEPTH, N_GROUPS, ATT_DH)),
        'att_k_norm_g': gain(ks[9], (DEPTH, N_GROUPS, ATT_DH)),
        'rel_bias': small(ks[10], (T5_BUCKETS, N_GROUPS * ATT_HEADS), 0.1),
        'w_ret_proj': w(ks[11], (DEPTH, RET_V_W, D_MODEL), RET_V_W),
        'w_att_proj': w(ks[12], (DEPTH, ATT_W, D_MODEL), ATT_W),
        'w_out': w(ks[13], (DEPTH, D_MODEL, D_MODEL), D_MODEL),
        'norm_ffn_g': gain(ks[14], (DEPTH, D_MODEL)),
        'w_router': w(ks[15], (DEPTH, D_MODEL, N_EXPERTS), D_MODEL),
        'b_router': small(ks[16], (DEPTH, N_EXPERTS), 0.01),
        'w_gate': w(ks[17], (DEPTH, N_EXPERTS, D_MODEL, D_FF), D_MODEL),
        'b_gate': small(ks[18], (DEPTH, N_EXPERTS, D_FF), 0.02),
        'w_up': w(ks[19], (DEPTH, N_EXPERTS, D_MODEL, D_FF), D_MODEL),
        'b_up': small(ks[20], (DEPTH, N_EXPERTS, D_FF), 0.02),
        'w_down': w(ks[21], (DEPTH, N_EXPERTS, D_FF, D_MODEL), D_FF),
        'b_down': small(ks[22], (DEPTH, N_EXPERTS, D_MODEL), 0.02),
        'norm_ple_g': gain(ks[23], (DEPTH, D_MODEL)),
        'w_ple_gate': w(ks[24], (DEPTH, D_MODEL, D_MODEL), D_MODEL),
        'w_ple_proj': w(ks[25], (DEPTH, PLE_DIM, D_MODEL), PLE_DIM),
    }


def reference(x_prompt, x_sample, p_prompt, p_sample, norm_mix_g, w_in, ret_decay_logit, ret_gn_g,
              att_q_norm_g, att_k_norm_g, rel_bias, w_ret_proj, w_att_proj, w_out, norm_ffn_g,
              w_router, b_router, w_gate, b_gate, w_up, b_up, w_down, b_down,
              norm_ple_g, w_ple_gate, w_ple_proj):
    def run(x, p):
        for i in range(DEPTH):
            lp = {
                'norm_mix_g': norm_mix_g[i], 'w_in': w_in[i],
                'ret_decay_logit': ret_decay_logit[i], 'ret_gn_g': ret_gn_g[i],
                'att_q_norm_g': att_q_norm_g[i], 'att_k_norm_g': att_k_norm_g[i],
                'w_ret_proj': w_ret_proj[i], 'w_att_proj': w_att_proj[i], 'w_out': w_out[i],
                'norm_ffn_g': norm_ffn_g[i], 'w_router': w_router[i], 'b_router': b_router[i],
                'w_gate': w_gate[i], 'b_gate': b_gate[i], 'w_up': w_up[i], 'b_up': b_up[i],
                'w_down': w_down[i], 'b_down': b_down[i],
                'norm_ple_g': norm_ple_g[i], 'w_ple_gate': w_ple_gate[i], 'w_ple_proj': w_ple_proj[i],
            }
            x = encoder_layer(x, p[i], lp, rel_bias)
        return x

    y_prompt = run(x_prompt, p_prompt)
    y_sample = run(x_sample, p_sample)
    return (y_prompt, y_sample)
```

```python
import functools
import math

import jax
import jax.numpy as jnp
import numpy as np
from jax import lax
from jax.experimental import pallas as pl
from jax.experimental.pallas import tpu as pltpu

F32 = jnp.float32
BF16 = jnp.bfloat16

D_MODEL = 1024
N_PROMPT_SEQ = 8
PROMPT_SEQ = 2048
SAMPLE_SEQ = 16384
T_PROMPT = N_PROMPT_SEQ * PROMPT_SEQ
T_ALL = T_PROMPT + SAMPLE_SEQ

RET_HEADS = 4
RET_DK = 128
RET_DV = 256
RET_CHUNK = 128
ROPE_THETA = 10000.0
ATT_GROUPS = ((128, 1), (512, 4), (2048, 16))
N_GROUPS = 3
ATT_HEADS = 8
ATT_DH = 64
ATT_BLOCK = 64
ATT_W = ATT_HEADS * ATT_DH
T5_BUCKETS = 32
T5_MAX_DIST = 1024
N_EXPERTS = 32
TOP_K = 4
D_FF = 1024
SWIGLU_ALPHA = 1.702
SWIGLU_LIMIT = 7.0
PLE_DIM = 256
EPS = 1e-6

RET_QK_W = RET_HEADS * RET_DK
RET_V_W = RET_HEADS * RET_DV
N_IN = 2 * RET_QK_W + 2 * RET_V_W + 3 * N_GROUPS * ATT_W + 2 * D_MODEL

COL_RQ = 0
COL_RK = RET_QK_W
COL_RV = 2 * RET_QK_W
COL_RG = COL_RV + RET_V_W
COL_ATT = COL_RG + RET_V_W
COL_GATE_RET = COL_ATT + 3 * N_GROUPS * ATT_W
COL_GATE_ATT = COL_GATE_RET + D_MODEL

LANES = 128
VMEM_LIMIT = 56 * 1024 * 1024

SEG = 2048
N_SEG = T_ALL // SEG
N_PROMPT_SEG = T_PROMPT // SEG
COL_BLK = 512
N_COL_BLK = N_IN // COL_BLK
QB = 128
KW = 256
NEG = -1e30
MERGE_TM = 512
FINAL_TM = 512
MOE_BM = 512
N_SLOT_BLOCKS = T_ALL * TOP_K // MOE_BM + N_EXPERTS
N_SLOTS = N_SLOT_BLOCKS * MOE_BM
PACK_W = D_MODEL // 2
DISPATCH_TM = 256
COMBINE_TM = 1024


def _cparams(sem, vmem=VMEM_LIMIT):
    return pltpu.CompilerParams(dimension_semantics=sem, vmem_limit_bytes=vmem)


def _pack_bf16_pair(x):
    w = x.shape[-1] // 2
    hi = pltpu.bitcast(x[:, :w].astype(BF16).astype(F32), jnp.uint32)
    lo = pltpu.bitcast(x[:, w:].astype(BF16).astype(F32), jnp.uint32)
    return hi | (lo >> 16)


def _unpack_bf16_pair(p):
    hi = pltpu.bitcast(p & jnp.uint32(0xFFFF0000), F32)
    lo = pltpu.bitcast(p << 16, F32)
    return jnp.concatenate([hi, lo], axis=-1)


def _in_proj_kernel(x_ref, g_ref, w_ref, cos_ref, sin_ref, z_ref, h_ref, p_ref):
    j = pl.program_id(1)

    @pl.when(j == 0)
    def _():
        xf = x_ref[...]
        ms = jnp.mean(xf * xf, axis=-1, keepdims=True)
        h_ref[...] = (xf * lax.rsqrt(ms + EPS) * g_ref[...]).astype(BF16)

    acc = jnp.dot(h_ref[...], w_ref[...], preferred_element_type=F32)
    n_slab = COL_BLK // LANES
    for s in range(n_slab):
        p_ref[s] = acc[:, s * LANES:(s + 1) * LANES]

    is_rope = j < (COL_RV // COL_BLK)
    att0 = COL_ATT // COL_BLK
    is_d4 = (j >= att0 + 3) & (j < att0 + 6)
    is_d16 = (j >= att0 + 6) & (j < att0 + 9)

    @pl.when(is_rope)
    def _():
        scale = jnp.where(j == COL_RK // COL_BLK, RET_DK ** -0.5, 1.0).astype(F32)
        c = cos_ref[...]
        sn = sin_ref[...]
        for s in range(n_slab):
            xs = p_ref[s]
            r = xs * c + pltpu.roll(xs, RET_DK // 2, axis=1) * sn
            z_ref[:, s * LANES:(s + 1) * LANES] = (r * scale).astype(BF16)

    for dil, pred in ((4, is_d4), (16, is_d16)):
        @pl.when(pred)
        def _(dil=dil):
            rows = SEG // dil
            for rho in range(dil):
                for s in range(n_slab):
                    piece = p_ref[s, pl.ds(rho, rows, stride=dil), :]
                    z_ref[rho * rows:(rho + 1) * rows, s * LANES:(s + 1) * LANES] = piece.astype(BF16)

    @pl.when(jnp.logical_not(is_rope | is_d4 | is_d16))
    def _():
        for s in range(n_slab):
            z_ref[:, s * LANES:(s + 1) * LANES] = p_ref[s].astype(BF16)


def _in_proj(x_all, norm_g, w_in_bf, cos_t, sin_t):
    def pos_blk(i, j):
        return (jnp.maximum(i - N_PROMPT_SEG, 0), 0)

    return pl.pallas_call(
        _in_proj_kernel,
        grid=(N_SEG, N_COL_BLK),
        in_specs=[
            pl.BlockSpec((SEG, D_MODEL), lambda i, j: (i, 0)),
            pl.BlockSpec((1, D_MODEL), lambda i, j: (0, 0)),
            pl.BlockSpec((D_MODEL, COL_BLK), lambda i, j: (0, j)),
            pl.BlockSpec((SEG, LANES), pos_blk),
            pl.BlockSpec((SEG, LANES), pos_blk),
        ],
        out_specs=pl.BlockSpec((SEG, COL_BLK), lambda i, j: (i, j)),
        out_shape=jax.ShapeDtypeStruct((T_ALL, N_IN), BF16),
        scratch_shapes=[
            pltpu.VMEM((SEG, D_MODEL), BF16),
            pltpu.VMEM((COL_BLK // LANES, SEG, LANES), F32),
        ],
        compiler_params=_cparams(("arbitrary", "arbitrary")),
        name="in_proj",
    )(x_all, norm_g, w_in_bf, cos_t, sin_t)


RET_CHUNKS_PER_SEG = SEG // RET_CHUNK
RET_MAX_CHUNKS = SAMPLE_SEQ // RET_CHUNK


def _retention_kernel(seg_ref, phase_ref, reset_ref, cbase_ref,
                      q_ref, k_ref, v_ref, g_ref, mask_ref, dec_ref, cdec_ref, gn_ref,
                      y_ref, sb_ref, sf_ref, sr_ref):
    step = pl.program_id(1)
    phase = phase_ref[step]
    reset = reset_ref[step]
    cbase = cbase_ref[step]
    kdec_f = dec_ref[0, 0]
    qdec_f = dec_ref[0, 1]
    kdec_b = dec_ref[0, 2]
    qdec_b = dec_ref[0, 3]
    cd_f = cdec_ref[0, 0]
    cd_b = cdec_ref[0, 1]

    def kv_outer(kd, v):
        return lax.dot_general(kd, v, (((0,), (0,)), ((), ())), preferred_element_type=F32)

    @pl.when((phase == 0) & (reset == 1))
    def _():
        sr_ref[...] = jnp.zeros_like(sr_ref)

    @pl.when((phase == 1) & (reset == 1))
    def _():
        sf_ref[...] = jnp.zeros_like(sf_ref)

    @pl.when(phase == 0)
    def _():
        def body(it, carry):
            c = RET_CHUNKS_PER_SEG - 1 - it
            r0 = pl.multiple_of(c * RET_CHUNK, RET_CHUNK)
            k = k_ref[pl.ds(r0, RET_CHUNK), :].astype(F32)
            v = v_ref[pl.ds(r0, RET_CHUNK), :]
            state = sr_ref[...]
            sb_ref[cbase + c] = state.astype(BF16)
            kd = (k * kdec_b).astype(BF16)
            sr_ref[...] = cd_b * state + kv_outer(kd, v)
            return carry

        lax.fori_loop(0, RET_CHUNKS_PER_SEG, body, 0)

    @pl.when(phase == 1)
    def _():
        msk = mask_ref[0]
        gn = gn_ref[0]

        def body(c, carry):
            r0 = pl.multiple_of(c * RET_CHUNK, RET_CHUNK)
            qb = q_ref[pl.ds(r0, RET_CHUNK), :]
            kb = k_ref[pl.ds(r0, RET_CHUNK), :]
            v = v_ref[pl.ds(r0, RET_CHUNK), :]
            q = qb.astype(F32)
            k = kb.astype(F32)
            s = lax.dot_general(qb, kb, (((1,), (1,)), ((), ())), preferred_element_type=F32)
            a = (s * msk).astype(BF16)
            o = jnp.dot(a, v, preferred_element_type=F32)
            state_f = sf_ref[...]
            o += jnp.dot((q * qdec_f).astype(BF16), state_f.astype(BF16), preferred_element_type=F32)
            o += jnp.dot((q * qdec_b).astype(BF16), sb_ref[cbase + c], preferred_element_type=F32)
            sf_ref[...] = cd_f * state_f + kv_outer((k * kdec_f).astype(BF16), v)
            mu = jnp.mean(o, axis=-1, keepdims=True)
            oc = o - mu
            var = jnp.mean(oc * oc, axis=-1, keepdims=True)
            on = oc * lax.rsqrt(var + EPS) * gn
            gate = g_ref[pl.ds(r0, RET_CHUNK), :].astype(F32)
            y_ref[pl.ds(r0, RET_CHUNK), :] = (gate * jax.nn.sigmoid(gate) * on).astype(BF16)
            return carry

        lax.fori_loop(0, RET_CHUNKS_PER_SEG, body, 0)


def _retention_schedule():
    seg, phase, reset, cbase = [], [], [], []
    for p in range(N_PROMPT_SEG):
        for ph in (0, 1):
            seg.append(p); phase.append(ph); reset.append(1); cbase.append(0)
    n_s = N_SEG - N_PROMPT_SEG
    for i in range(n_s):
        t = n_s - 1 - i
        seg.append(N_PROMPT_SEG + t); phase.append(0); reset.append(int(i == 0)); cbase.append(t * RET_CHUNKS_PER_SEG)
    for t in range(n_s):
        seg.append(N_PROMPT_SEG + t); phase.append(1); reset.append(int(t == 0)); cbase.append(t * RET_CHUNKS_PER_SEG)
    hold = list(seg)
    for i in range(len(seg)):
        if phase[i] == 0:
            nxt = next(j for j in range(i + 1, len(seg)) if phase[j] == 1)
            hold[i] = seg[nxt]
    arr = lambda a: jnp.asarray(np.asarray(a, np.int32))
    return arr(seg), arr(phase), arr(reset), arr(cbase), arr(hold)


def _retention(z, ret_mask, ret_dec, ret_cdec, gn_g):
    seg, phase, reset, cbase, hold = _retention_schedule()
    n_steps = int(seg.shape[0])
    qk_blk = lambda col: (lambda h, s, seg_r, ph_r, rs_r, cb_r, hold_r: (seg_r[s], col // RET_DK + h))
    hold_blk = lambda col, w: (lambda h, s, seg_r, ph_r, rs_r, cb_r, hold_r: (hold_r[s], col // w + h))
    v_blk = lambda h, s, seg_r, ph_r, rs_r, cb_r, hold_r: (seg_r[s], COL_RV // RET_DV + h)
    per_head = lambda h, s, *_: (h, 0, 0)
    per_head4 = lambda h, s, *_: (h, 0, 0, 0)
    grid_spec = pltpu.PrefetchScalarGridSpec(
        num_scalar_prefetch=5,
        grid=(RET_HEADS, n_steps),
        in_specs=[
            pl.BlockSpec((SEG, RET_DK), hold_blk(COL_RQ, RET_DK)),
            pl.BlockSpec((SEG, RET_DK), qk_blk(COL_RK)),
            pl.BlockSpec((SEG, RET_DV), v_blk),
            pl.BlockSpec((SEG, RET_DV), hold_blk(COL_RG, RET_DV)),
            pl.BlockSpec((1, RET_CHUNK, RET_CHUNK), per_head),
            pl.BlockSpec((1, 4, RET_CHUNK, 1), per_head4),
            pl.BlockSpec((1, 2, 1, RET_DV), per_head4),
            pl.BlockSpec((1, 1, RET_DV), per_head),
        ],
        out_specs=pl.BlockSpec((SEG, RET_DV), lambda h, s, seg_r, ph_r, rs_r, cb_r, hold_r: (hold_r[s], h)),
        scratch_shapes=[
            pltpu.VMEM((RET_MAX_CHUNKS, RET_DK, RET_DV), BF16),
            pltpu.VMEM((RET_DK, RET_DV), F32),
            pltpu.VMEM((RET_DK, RET_DV), F32),
        ],
    )

    def kernel(seg_r, ph_r, rs_r, cb_r, hold_r, *refs):
        _retention_kernel(seg_r, ph_r, rs_r, cb_r, *refs)

    return pl.pallas_call(
        kernel,
        grid_spec=grid_spec,
        out_shape=jax.ShapeDtypeStruct((T_ALL, RET_V_W), BF16),
        compiler_params=_cparams(("arbitrary", "arbitrary")),
        name="retention",
    )(seg, phase, reset, cbase, hold, z, z, z, z, ret_mask, ret_dec, ret_cdec, gn_g)


def _attention_kernel(first_ref, last_ref, q_ref, kp_ref, km_ref, kn_ref, vp_ref, vm_ref, vn_ref,
                      bias_ref, gq_ref, gk_ref, o_ref, lse_ref, kall, vall, oacc, lacc, *, dil):
    nb = SEG // dil // ATT_BLOCK
    nqb = nb // 2
    n_slab = ATT_W // LANES
    c = pl.program_id(0)
    is_first = first_ref[c]
    is_last = last_ref[c]
    lane = lax.broadcasted_iota(jnp.int32, (1, LANES), 1)
    lo = lane < ATT_DH
    gq = gq_ref[...]
    gk = gk_ref[...]

    def head_norm(x, g):
        x2 = x * x
        s_lo = jnp.sum(jnp.where(lo, x2, 0.0), axis=-1, keepdims=True)
        s_hi = jnp.sum(jnp.where(lo, 0.0, x2), axis=-1, keepdims=True)
        ms = jnp.where(lo, s_lo, s_hi) * (1.0 / ATT_DH)
        return x * lax.rsqrt(ms + EPS) * g

    def norm_block(src):
        parts = [head_norm(src[:, s * LANES:(s + 1) * LANES].astype(F32), gk).astype(BF16) for s in range(n_slab)]
        return jnp.concatenate(parts, axis=-1)

    def fill_main(it, carry):
        rho = it // nb
        blk = it % nb
        kall[rho, blk + 1] = norm_block(km_ref[rho, blk])
        vall[rho, blk + 1] = vm_ref[rho, blk]
        return carry

    lax.fori_loop(0, dil * nb, fill_main, 0)

    def fill_halo(rho, carry):
        kall[rho, 0] = norm_block(kp_ref[rho, 0])
        kall[rho, nb + 1] = norm_block(kn_ref[rho, 0])
        vall[rho, 0] = vp_ref[rho, 0]
        vall[rho, nb + 1] = vn_ref[rho, 0]
        return carry

    lax.fori_loop(0, dil, fill_halo, 0)

    col = lax.broadcasted_iota(jnp.int32, (1, KW), 1)

    def body(it, carry):
        rho = it // nqb
        qb = it % nqb
        edge = jnp.where(((col < ATT_BLOCK) & (qb == 0) & (is_first == 1))
                         | ((col >= KW - ATT_BLOCK) & (qb == nqb - 1) & (is_last == 1)), NEG, 0.0).astype(F32)
        start = rho + qb * (QB * dil)
        lse_tile = jnp.zeros((QB, LANES), F32)
        for s in range(n_slab):
            sl = slice(s * LANES, (s + 1) * LANES)
            qx = q_ref[rho, pl.ds(2 * qb, 2), :, sl].reshape(QB, LANES).astype(F32)
            qn = head_norm(qx, gq) * (ATT_DH ** -0.5)
            kw = kall[rho, pl.ds(2 * qb, 4), :, sl].reshape(KW, LANES)
            vw = vall[rho, pl.ds(2 * qb, 4), :, sl].reshape(KW, LANES)
            outs = []
            for hh in range(2):
                sel = lo if hh == 0 else jnp.logical_not(lo)
                qm = jnp.where(sel, qn, 0.0).astype(BF16)
                sc = lax.dot_general(qm, kw, (((1,), (1,)), ((), ())), preferred_element_type=F32)
                sc = sc + bias_ref[2 * s + hh] + edge
                m = jnp.max(sc, axis=-1, keepdims=True)
                e = jnp.exp(sc - m)
                l = jnp.sum(e, axis=-1, keepdims=True)
                o_h = jnp.dot(e.astype(BF16), vw, preferred_element_type=F32)
                outs.append(o_h * (1.0 / l))
                lse_tile = lse_tile + jnp.where(lane == 2 * s + hh, m + jnp.log(l), 0.0)
            o_slab = jnp.where(lo, outs[0], outs[1])
            if dil == 1:
                oacc[s, pl.ds(start, QB), :] = o_slab
            else:
                oacc[s, pl.ds(start, QB, stride=dil), :] = o_slab
        if dil == 1:
            lacc[pl.ds(start, QB), :] = lse_tile
        else:
            lacc[pl.ds(start, QB, stride=dil), :] = lse_tile
        return carry

    lax.fori_loop(0, dil * nqb, body, 0)

    for s in range(n_slab):
        o_ref[:, s * LANES:(s + 1) * LANES] = oacc[s].astype(BF16)
    lse_ref[...] = lacc[...]


def _attention_group(z, bias_g, gq, gk, first, last, gi, dil):
    nb = SEG // dil // ATT_BLOCK
    z5 = z.reshape(N_SEG, dil, nb, ATT_BLOCK, N_IN)
    cq = (COL_ATT + 3 * gi * ATT_W) // ATT_W
    ck, cv = cq + 1, cq + 2
    main = lambda cb: pl.BlockSpec((None, dil, nb, ATT_BLOCK, ATT_W), lambda c, f, l: (c, 0, 0, 0, cb))
    prev = lambda cb: pl.BlockSpec((None, dil, 1, ATT_BLOCK, ATT_W),
                                   lambda c, f, l: (c - 1 + f[c], 0, nb - 1, 0, cb))
    nxt = lambda cb: pl.BlockSpec((None, dil, 1, ATT_BLOCK, ATT_W),
                                  lambda c, f, l: (c + 1 - l[c], 0, 0, 0, cb))
    grid_spec = pltpu.PrefetchScalarGridSpec(
        num_scalar_prefetch=2,
        grid=(N_SEG,),
        in_specs=[
            main(cq), prev(ck), main(ck), nxt(ck), prev(cv), main(cv), nxt(cv),
            pl.BlockSpec((ATT_HEADS, QB, KW), lambda c, f, l: (0, 0, 0)),
            pl.BlockSpec((1, LANES), lambda c, f, l: (0, 0)),
            pl.BlockSpec((1, LANES), lambda c, f, l: (0, 0)),
        ],
        out_specs=[
            pl.BlockSpec((SEG, ATT_W), lambda c, f, l: (c, 0)),
            pl.BlockSpec((SEG, LANES), lambda c, f, l: (c, 0)),
        ],
        scratch_shapes=[
            pltpu.VMEM((dil, nb + 2, ATT_BLOCK, ATT_W), BF16),
            pltpu.VMEM((dil, nb + 2, ATT_BLOCK, ATT_W), BF16),
            pltpu.VMEM((ATT_W // LANES, SEG, LANES), F32),
            pltpu.VMEM((SEG, LANES), F32),
        ],
    )
    return pl.pallas_call(
        functools.partial(_attention_kernel, dil=dil),
        grid_spec=grid_spec,
        out_shape=[jax.ShapeDtypeStruct((T_ALL, ATT_W), BF16), jax.ShapeDtypeStruct((T_ALL, LANES), F32)],
        compiler_params=_cparams(("arbitrary",)),
        name=f"attention_d{dil}",
    )(first, last, z5, z5, z5, z5, z5, z5, z5, bias_g, gq, gk)


def _merge_kernel(yret_ref, o0_ref, o1_ref, o2_ref, l0_ref, l1_ref, l2_ref,
                  gret_a_ref, gret_b_ref, gatt_a_ref, gatt_b_ref, x_ref,
                  wret_ref, watt_ref, wout_ref, expand_ref, nffn_ref, wr_ref, br_ref,
                  x1_ref, hp_ref, idx_ref, gate_ref, rank_ref, cnt_ref, carry_ref):
    i = pl.program_id(0)
    tm = MERGE_TM

    @pl.when(i == 0)
    def _():
        carry_ref[...] = jnp.zeros_like(carry_ref)

    l0, l1, l2 = l0_ref[...], l1_ref[...], l2_ref[...]
    lm = jnp.maximum(jnp.maximum(l0, l1), l2)
    e0, e1, e2 = jnp.exp(l0 - lm), jnp.exp(l1 - lm), jnp.exp(l2 - lm)
    inv = 1.0 / (e0 + e1 + e2)
    y_att = jnp.zeros((tm, ATT_W), F32)
    for e, o_ref in ((e0, o0_ref), (e1, o1_ref), (e2, o2_ref)):
        w = e * inv
        w_hi = w.astype(BF16)
        w_lo = (w - w_hi.astype(F32)).astype(BF16)
        wide = (jnp.dot(w_hi, expand_ref[...], preferred_element_type=F32)
                + jnp.dot(w_lo, expand_ref[...], preferred_element_type=F32))
        y_att = y_att + wide * o_ref[...].astype(F32)

    p_ret = jnp.dot(yret_ref[...], wret_ref[...], preferred_element_type=F32)
    p_att = jnp.dot(y_att.astype(BF16), watt_ref[...], preferred_element_type=F32)
    g_ret = jnp.concatenate([gret_a_ref[...], gret_b_ref[...]], axis=-1).astype(F32)
    g_att = jnp.concatenate([gatt_a_ref[...], gatt_b_ref[...]], axis=-1).astype(F32)
    merged = jax.nn.sigmoid(g_ret) * p_ret + jax.nn.sigmoid(g_att) * p_att
    x1 = x_ref[...] + jnp.dot(merged.astype(BF16), wout_ref[...], preferred_element_type=F32)
    x1_ref[...] = x1

    ms = jnp.mean(x1 * x1, axis=-1, keepdims=True)
    h2 = x1 * lax.rsqrt(ms + EPS) * nffn_ref[...]
    hp_ref[...] = _pack_bf16_pair(h2)

    logits = jnp.dot(h2, wr_ref[...], preferred_element_type=F32, precision=lax.Precision.HIGHEST) + br_ref[...]
    lane = lax.broadcasted_iota(jnp.int32, (tm, LANES), 1)
    lane_f = lane.astype(F32)
    work = logits
    vals, idxs = [], []
    for _ in range(TOP_K):
        m = jnp.max(work, axis=-1, keepdims=True)
        ix = jnp.min(jnp.where(work == m, lane_f, float(LANES)), axis=-1, keepdims=True)
        vals.append(m)
        idxs.append(ix)
        work = jnp.where(lane_f == ix, -3e38, work)
    es = [jnp.exp(v - vals[0]) for v in vals]
    den = es[0] + es[1] + es[2] + es[3]
    onehot = jnp.zeros((tm, LANES), F32)
    for ix in idxs:
        onehot = onehot + jnp.where(lane_f == ix, 1.0, 0.0)
    row = lax.broadcasted_iota(jnp.int32, (tm, tm), 0)
    colm = lax.broadcasted_iota(jnp.int32, (tm, tm), 1)
    tri = jnp.where(colm < row, 1.0, 0.0).astype(BF16)
    before = jnp.dot(tri, onehot.astype(BF16), preferred_element_type=F32) + carry_ref[...]
    idx_out = jnp.zeros((tm, LANES), F32)
    gate_out = jnp.zeros((tm, LANES), F32)
    rank_out = jnp.zeros((tm, LANES), F32)
    for k in range(TOP_K):
        rk = jnp.sum(jnp.where(lane_f == idxs[k], before, 0.0), axis=-1, keepdims=True)
        sel = lane == k
        idx_out = jnp.where(sel, idxs[k], idx_out)
        gate_out = jnp.where(sel, es[k] / den, gate_out)
        rank_out = jnp.where(sel, rk, rank_out)
    idx_ref[...] = idx_out.astype(jnp.int32)
    gate_ref[...] = gate_out
    rank_ref[...] = rank_out.astype(jnp.int32)
    total = carry_ref[...] + jnp.sum(onehot, axis=0, keepdims=True)
    carry_ref[...] = total
    cnt_ref[...] = jnp.broadcast_to(total, cnt_ref.shape)


def _merge(y_ret, o_list, lse_list, z, x_all, w_ret, w_att, w_out, expand, n_ffn, w_router, b_router):
    tm = MERGE_TM
    row = lambda w: pl.BlockSpec((tm, w), lambda i: (i, 0))
    full = lambda a: pl.BlockSpec(a.shape, lambda i: (0,) * a.ndim)
    zcol = lambda col: pl.BlockSpec((tm, COL_BLK), lambda i: (i, col // COL_BLK))
    return pl.pallas_call(
        _merge_kernel,
        grid=(T_ALL // tm,),
        in_specs=[row(RET_V_W), row(ATT_W), row(ATT_W), row(ATT_W), row(LANES), row(LANES), row(LANES),
                  zcol(COL_GATE_RET), zcol(COL_GATE_RET + COL_BLK), zcol(COL_GATE_ATT),
                  zcol(COL_GATE_ATT + COL_BLK), row(D_MODEL),
                  full(w_ret), full(w_att), full(w_out), full(expand), full(n_ffn), full(w_router), full(b_router)],
        out_specs=[row(D_MODEL), row(PACK_W), row(LANES), row(LANES), row(LANES),
                   pl.BlockSpec((8, LANES), lambda i: (0, 0))],
        out_shape=[jax.ShapeDtypeStruct((T_ALL, D_MODEL), F32),
                   jax.ShapeDtypeStruct((T_ALL, PACK_W), jnp.uint32),
                   jax.ShapeDtypeStruct((T_ALL, LANES), jnp.int32),
                   jax.ShapeDtypeStruct((T_ALL, LANES), F32),
                   jax.ShapeDtypeStruct((T_ALL, LANES), jnp.int32),
                   jax.ShapeDtypeStruct((8, LANES), F32)],
        scratch_shapes=[pltpu.VMEM((1, LANES), F32)],
        compiler_params=_cparams(("arbitrary",)),
        name="merge_router",
    )(y_ret, *o_list, *lse_list, z, z, z, z, x_all, w_ret, w_att, w_out, expand, n_ffn, w_router, b_router)


def _dispatch_kernel(dest_ref, h_ref, xs_in_ref, xs_ref, buf_ref, sem_ref):
    del xs_in_ref
    i = pl.program_id(0)
    n = pl.num_programs(0)
    slot = i % 2
    tm = DISPATCH_TM

    def wait_slot(sl):
        for _ in range(TOP_K):
            pltpu.make_async_copy(buf_ref.at[sl], xs_ref.at[pl.ds(0, tm)], sem_ref.at[sl]).wait()

    @pl.when(i >= 1)
    def _():
        wait_slot(1 - slot)

    buf_ref[slot] = h_ref[...]

    def issue(r, carry):
        for k in range(TOP_K):
            d = dest_ref[r * TOP_K + k]
            pltpu.make_async_copy(buf_ref.at[slot, pl.ds(r, 1)], xs_ref.at[pl.ds(d, 1)], sem_ref.at[slot]).start()
        return carry

    lax.fori_loop(0, tm, issue, 0, unroll=4)

    @pl.when(i == n - 1)
    def _():
        wait_slot(slot)


def _dispatch(dest_flat, h_packed, xs_init):
    tm = DISPATCH_TM
    return pl.pallas_call(
        _dispatch_kernel,
        grid=(T_ALL // tm,),
        in_specs=[pl.BlockSpec((tm * TOP_K,), lambda i: (i,), memory_space=pltpu.SMEM),
                  pl.BlockSpec((tm, PACK_W), lambda i: (i, 0)),
                  pl.BlockSpec(memory_space=pl.ANY)],
        out_specs=pl.BlockSpec(memory_space=pl.ANY),
        out_shape=jax.ShapeDtypeStruct((N_SLOTS, PACK_W), jnp.uint32),
        scratch_shapes=[pltpu.VMEM((2, tm, PACK_W), jnp.uint32), pltpu.SemaphoreType.DMA((2,))],
        input_output_aliases={2: 0},
        compiler_params=_cparams(("arbitrary",)),
        name="dispatch",
    )(dest_flat, h_packed, xs_init)


def _expert_kernel(be_ref, nused_ref, xs_ref, wg_ref, bg_ref, wu_ref, bu_ref, wd_ref, bd_ref, ys_ref):
    b = pl.program_id(0)

    @pl.when(b < nused_ref[0])
    def _():
        x = _unpack_bf16_pair(xs_ref[...]).astype(BF16)
        g = jnp.dot(x, wg_ref[0], preferred_element_type=F32) + bg_ref[0]
        u = jnp.dot(x, wu_ref[0], preferred_element_type=F32) + bu_ref[0]
        g = jnp.minimum(g, SWIGLU_LIMIT)
        u = jnp.clip(u, -SWIGLU_LIMIT, SWIGLU_LIMIT)
        glu = g * jax.nn.sigmoid(SWIGLU_ALPHA * g)
        act = ((u + 1.0) * glu).astype(BF16)
        y = jnp.dot(act, wd_ref[0], preferred_element_type=F32) + bd_ref[0]
        ys_ref[...] = _pack_bf16_pair(y)

    @pl.when(b >= nused_ref[0])
    def _():
        ys_ref[...] = jnp.zeros_like(ys_ref)


def _experts(block_expert, n_used, xs, wg, bg, wu, bu, wd, bd):
    blk = lambda b, be, nu: (jnp.minimum(b, nu[0] - 1), 0)
    wsp = lambda: pl.BlockSpec((1, D_MODEL, D_FF), lambda b, be, nu: (be[b], 0, 0))
    bsp = lambda: pl.BlockSpec((1, 1, D_FF), lambda b, be, nu: (be[b], 0, 0))
    grid_spec = pltpu.PrefetchScalarGridSpec(
        num_scalar_prefetch=2,
        grid=(N_SLOT_BLOCKS,),
        in_specs=[pl.BlockSpec((MOE_BM, PACK_W), blk), wsp(), bsp(), wsp(), bsp(), wsp(), bsp()],
        out_specs=pl.BlockSpec((MOE_BM, PACK_W), lambda b, be, nu: (b, 0)),
    )
    return pl.pallas_call(
        _expert_kernel,
        grid_spec=grid_spec,
        out_shape=jax.ShapeDtypeStruct((N_SLOTS, PACK_W), jnp.uint32),
        compiler_params=_cparams(("arbitrary",)),
        name="experts",
    )(block_expert, n_used, xs, wg, bg, wu, bu, wd, bd)


def _combine_kernel(dest_ref, ys_ref, out_ref, sem_ref):
    tm = COMBINE_TM

    def issue(r, carry):
        d = dest_ref[r]
        pltpu.make_async_copy(ys_ref.at[pl.ds(d, 1)], out_ref.at[0, pl.ds(r, 1)], sem_ref.at[0]).start()
        return carry

    lax.fori_loop(0, tm, issue, 0, unroll=8)
    pltpu.make_async_copy(ys_ref.at[pl.ds(0, tm)], out_ref.at[0], sem_ref.at[0]).wait()


def _combine(dest_kmajor, ys):
    tm = COMBINE_TM
    n_t = T_ALL // tm
    return pl.pallas_call(
        _combine_kernel,
        grid=(TOP_K, n_t),
        in_specs=[pl.BlockSpec((tm,), lambda k, i: (k * n_t + i,), memory_space=pltpu.SMEM),
                  pl.BlockSpec(memory_space=pl.ANY)],
        out_specs=pl.BlockSpec((1, tm, PACK_W), lambda k, i: (k, i, 0)),
        out_shape=jax.ShapeDtypeStruct((TOP_K, T_ALL, PACK_W), jnp.uint32),
        scratch_shapes=[pltpu.SemaphoreType.DMA((1,))],
        compiler_params=_cparams(("arbitrary", "arbitrary")),
        name="combine",
    )(dest_kmajor, ys)


def _final_kernel(x1_ref, yg_ref, gate_ref, p_ref, nple_ref, wpg_ref, wpp_ref, out_ref):
    x2 = x1_ref[...]
    gates = gate_ref[...]
    for k in range(TOP_K):
        x2 = x2 + gates[:, k:k + 1] * _unpack_bf16_pair(yg_ref[k])
    ms = jnp.mean(x2 * x2, axis=-1, keepdims=True)
    h3 = (x2 * lax.rsqrt(ms + EPS) * nple_ref[...]).astype(BF16)
    gate = jax.nn.sigmoid(jnp.dot(h3, wpg_ref[...], preferred_element_type=F32))
    proj = jnp.dot(p_ref[...].astype(BF16), wpp_ref[...], preferred_element_type=F32)
    out_ref[...] = x2 + gate * proj


def _final(x1, yg, gates, p, n_ple, w_pg, w_pp, row0, n_rows):
    tm = FINAL_TM
    off = row0 // tm
    full = lambda a: pl.BlockSpec(a.shape, lambda i: (0,) * a.ndim)
    return pl.pallas_call(
        _final_kernel,
        grid=(n_rows // tm,),
        in_specs=[pl.BlockSpec((tm, D_MODEL), lambda i: (i + off, 0)),
                  pl.BlockSpec((TOP_K, tm, PACK_W), lambda i: (0, i + off, 0)),
                  pl.BlockSpec((tm, LANES), lambda i: (i + off, 0)),
                  pl.BlockSpec((tm, PLE_DIM), lambda i: (i, 0)),
                  full(n_ple), full(w_pg), full(w_pp)],
        out_specs=pl.BlockSpec((tm, D_MODEL), lambda i: (i, 0)),
        out_shape=jax.ShapeDtypeStruct((n_rows, D_MODEL), F32),
        compiler_params=_cparams(("arbitrary",)),
        name="final_ple",
    )(x1, yg, gates, p, n_ple, w_pg, w_pp)


def _rope_tables():
    half = RET_DK // 2
    freq = ROPE_THETA ** (-jnp.arange(half, dtype=F32) / half)
    ang = jnp.arange(SAMPLE_SEQ, dtype=F32)[:, None] * freq[None, :]
    cos, sin = jnp.cos(ang), jnp.sin(ang)
    return jnp.concatenate([cos, cos], axis=-1), jnp.concatenate([-sin, sin], axis=-1)


def _retention_tables(decay_logit):
    lg = jax.nn.log_sigmoid(decay_logit.astype(F32))
    c = RET_CHUNK
    idx = jnp.arange(c, dtype=F32)
    diff = idx[:, None] - idx[None, :]
    lf = lg[0][:, None, None]
    lb = lg[1][:, None, None]
    mask = jnp.where(diff[None] >= 0, jnp.exp(lf * jnp.maximum(diff, 0.0)[None]),
                     jnp.exp(lb * jnp.maximum(-diff, 0.0)[None]))
    kdec_f = jnp.exp(lg[0][:, None] * (c - 1.0 - idx)[None, :])
    qdec_f = jnp.exp(lg[0][:, None] * (idx + 1.0)[None, :])
    kdec_b = jnp.exp(lg[1][:, None] * idx[None, :])
    qdec_b = jnp.exp(lg[1][:, None] * (c - idx)[None, :])
    dec = jnp.stack([kdec_f, qdec_f, kdec_b, qdec_b], axis=1)[..., None]
    cdec = jnp.exp(lg * c).T
    cdec = jnp.broadcast_to(cdec[:, :, None, None], (RET_HEADS, 2, 1, RET_DV))
    return mask, dec, cdec


def _t5_bucket(rel):
    half = T5_BUCKETS // 2
    exact = half // 2
    n = jnp.abs(rel)
    large = exact + (jnp.log(jnp.maximum(n, 1).astype(F32) / exact)
                     / math.log(T5_MAX_DIST / exact) * (half - exact)).astype(jnp.int32)
    large = jnp.minimum(large, half - 1)
    return jnp.where(rel > 0, half, 0) + jnp.where(n < exact, n, large)


def _attention_bias(rel_bias, gi, dil, radius):
    qi = jnp.arange(QB)
    ki = jnp.arange(KW) - ATT_BLOCK
    rel = ki[None, :] - qi[:, None]
    tab = rel_bias[:, gi * ATT_HEADS:(gi + 1) * ATT_HEADS].astype(F32)
    bias = tab[_t5_bucket(rel * dil)].transpose(2, 0, 1)
    return jnp.where((jnp.abs(rel) <= radius)[None], bias, NEG)


def _seq_edge_flags():
    first = np.zeros((N_SEG,), np.int32)
    last = np.zeros((N_SEG,), np.int32)
    first[:N_PROMPT_SEG] = 1
    last[:N_PROMPT_SEG] = 1
    first[N_PROMPT_SEG] = 1
    last[N_SEG - 1] = 1
    return jnp.asarray(first), jnp.asarray(last)


def _pad_lanes(a, value=0.0):
    return jnp.pad(a, ((0, 0), (0, LANES - a.shape[-1])), constant_values=value)


def kernel(x_prompt, x_sample, p_prompt, p_sample, norm_mix_g, w_in, ret_decay_logit, ret_gn_g,
           att_q_norm_g, att_k_norm_g, rel_bias, w_ret_proj, w_att_proj, w_out, norm_ffn_g,
           w_router, b_router, w_gate, b_gate, w_up, b_up, w_down, b_down,
           norm_ple_g, w_ple_gate, w_ple_proj):
    assert norm_mix_g.shape[0] == 1, "one layer"
    x_all = jnp.concatenate([x_prompt.reshape(T_PROMPT, D_MODEL), x_sample.reshape(SAMPLE_SEQ, D_MODEL)], axis=0)

    cos_t, sin_t = _rope_tables()
    z = _in_proj(x_all, norm_mix_g.astype(F32), w_in[0].astype(BF16), cos_t, sin_t)

    ret_mask, ret_dec, ret_cdec = _retention_tables(ret_decay_logit[0])
    y_ret = _retention(z, ret_mask, ret_dec, ret_cdec, ret_gn_g[0].reshape(RET_HEADS, 1, RET_DV).astype(F32))

    first, last = _seq_edge_flags()
    o_list, lse_list = [], []
    for gi, (window, dil) in enumerate(ATT_GROUPS):
        bias_g = _attention_bias(rel_bias, gi, dil, window // (2 * dil))
        gq = jnp.tile(att_q_norm_g[0, gi].astype(F32), LANES // ATT_DH)[None, :]
        gk = jnp.tile(att_k_norm_g[0, gi].astype(F32), LANES // ATT_DH)[None, :]
        o_g, lse_g = _attention_group(z, bias_g, gq, gk, first, last, gi, dil)
        o_list.append(o_g)
        lse_list.append(lse_g)

    expand = (jnp.arange(LANES)[:, None] == (jnp.arange(ATT_W)[None, :] // ATT_DH)).astype(BF16)
    w_router_p = _pad_lanes(w_router[0].astype(F32))
    b_router_p = _pad_lanes(b_router.astype(F32), NEG)
    x1, h_packed, idx, gates, rank, cnt = _merge(
        y_ret, o_list, lse_list, z, x_all, w_ret_proj[0].astype(BF16), w_att_proj[0].astype(BF16),
        w_out[0].astype(BF16), expand, norm_ffn_g.astype(F32), w_router_p, b_router_p)

    counts = cnt[0, :N_EXPERTS].astype(jnp.int32)
    padded = (counts + MOE_BM - 1) // MOE_BM * MOE_BM
    pad_end = jnp.cumsum(padded)
    pad_start = pad_end - padded
    top_idx = idx[:, :TOP_K]
    dest = pad_start[top_idx] + rank[:, :TOP_K]
    n_used = (pad_end[-1] // MOE_BM).astype(jnp.int32).reshape(1)
    block_expert = jnp.minimum(
        jnp.searchsorted(pad_end, jnp.arange(N_SLOT_BLOCKS, dtype=jnp.int32) * MOE_BM, side='right'),
        N_EXPERTS - 1).astype(jnp.int32)

    xs = _dispatch(dest.reshape(-1), h_packed, jnp.zeros((N_SLOTS, PACK_W), jnp.uint32))
    ys = _experts(block_expert, n_used, xs,
                  w_gate[0].astype(BF16), b_gate[0].reshape(N_EXPERTS, 1, D_FF).astype(F32),
                  w_up[0].astype(BF16), b_up[0].reshape(N_EXPERTS, 1, D_FF).astype(F32),
                  w_down[0].astype(BF16), b_down[0].reshape(N_EXPERTS, 1, D_MODEL).astype(F32))
    yg = _combine(dest.T.reshape(-1), ys)

    n_ple = norm_ple_g.astype(F32)
    w_pg = w_ple_gate[0].astype(BF16)
    w_pp = w_ple_proj[0].astype(BF16)
    y_p = _final(x1, yg, gates, p_prompt[0].reshape(T_PROMPT, PLE_DIM), n_ple, w_pg, w_pp, 0, T_PROMPT)
    y_s = _final(x1, yg, gates, p_sample[0].reshape(SAMPLE_SEQ, PLE_DIM), n_ple, w_pg, w_pp, T_PROMPT, SAMPLE_SEQ)
    return (y_p.reshape(x_prompt.shape), y_s.reshape(x_sample.shape))
```

```python
import functools
import math

import jax
import jax.numpy as jnp
import numpy as np
from jax import lax
from jax.experimental import pallas as pl
from jax.experimental.pallas import tpu as pltpu

F32 = jnp.float32
BF16 = jnp.bfloat16

D_MODEL = 1024
N_PROMPT_SEQ = 8
PROMPT_SEQ = 2048
SAMPLE_SEQ = 16384
T_PROMPT = N_PROMPT_SEQ * PROMPT_SEQ
T_ALL = T_PROMPT + SAMPLE_SEQ

RET_HEADS = 4
RET_DK = 128
RET_DV = 256
RET_CHUNK = 128
ROPE_THETA = 10000.0
ATT_GROUPS = ((128, 1), (512, 4), (2048, 16))
N_GROUPS = 3
ATT_HEADS = 8
ATT_DH = 64
ATT_BLOCK = 64
ATT_W = ATT_HEADS * ATT_DH
T5_BUCKETS = 32
T5_MAX_DIST = 1024
N_EXPERTS = 32
TOP_K = 4
D_FF = 1024
SWIGLU_ALPHA = 1.702
SWIGLU_LIMIT = 7.0
PLE_DIM = 256
EPS = 1e-6

RET_QK_W = RET_HEADS * RET_DK
RET_V_W = RET_HEADS * RET_DV
N_IN = 2 * RET_QK_W + 2 * RET_V_W + 3 * N_GROUPS * ATT_W + 2 * D_MODEL

COL_RQ = 0
COL_RK = RET_QK_W
COL_RV = 2 * RET_QK_W
COL_RG = COL_RV + RET_V_W
COL_ATT = COL_RG + RET_V_W
COL_GATE_RET = COL_ATT + 3 * N_GROUPS * ATT_W
COL_GATE_ATT = COL_GATE_RET + D_MODEL

LANES = 128
VMEM_LIMIT = 56 * 1024 * 1024

SEG = 2048
N_SEG = T_ALL // SEG
N_PROMPT_SEG = T_PROMPT // SEG
COL_BLK = 512
N_COL_BLK = N_IN // COL_BLK
QB = 128
KW = 256
NEG = -1e30
MERGE_TM = 512
FINAL_TM = 512
MOE_BM = 512
N_SLOT_BLOCKS = T_ALL * TOP_K // MOE_BM + N_EXPERTS
N_SLOTS = N_SLOT_BLOCKS * MOE_BM
PACK_W = D_MODEL // 2
DISPATCH_TM = 256
COMBINE_TM = 1024


def _cparams(sem, vmem=VMEM_LIMIT):
    return pltpu.CompilerParams(dimension_semantics=sem, vmem_limit_bytes=vmem)


def _pack_bf16_pair(x):
    w = x.shape[-1] // 2
    hi = pltpu.bitcast(x[:, :w].astype(BF16).astype(F32), jnp.uint32)
    lo = pltpu.bitcast(x[:, w:].astype(BF16).astype(F32), jnp.uint32)
    return hi | (lo >> 16)


def _unpack_bf16_pair(p):
    hi = pltpu.bitcast(p & jnp.uint32(0xFFFF0000), F32)
    lo = pltpu.bitcast(p << 16, F32)
    return jnp.concatenate([hi, lo], axis=-1)


def _in_proj_kernel(xp_ref, xs_ref, g_ref, w_ref, cos_ref, sin_ref, z_ref, h_ref, p_ref):
    i = pl.program_id(0)
    j = pl.program_id(1)

    def norm_into_h(x_ref):
        xf = x_ref[...]
        ms = jnp.mean(xf * xf, axis=-1, keepdims=True)
        h_ref[...] = (xf * lax.rsqrt(ms + EPS) * g_ref[...]).astype(BF16)

    @pl.when((j == 0) & (i < N_PROMPT_SEG))
    def _():
        norm_into_h(xp_ref)

    @pl.when((j == 0) & (i >= N_PROMPT_SEG))
    def _():
        norm_into_h(xs_ref)

    acc = jnp.dot(h_ref[...], w_ref[...], preferred_element_type=F32)
    n_slab = COL_BLK // LANES
    for s in range(n_slab):
        p_ref[s] = acc[:, s * LANES:(s + 1) * LANES]

    is_rope = j < (COL_RV // COL_BLK)
    att0 = COL_ATT // COL_BLK
    is_d4 = (j >= att0 + 3) & (j < att0 + 6)
    is_d16 = (j >= att0 + 6) & (j < att0 + 9)

    @pl.when(is_rope)
    def _():
        scale = jnp.where(j == COL_RK // COL_BLK, RET_DK ** -0.5, 1.0).astype(F32)
        c = cos_ref[...]
        sn = sin_ref[...]
        for s in range(n_slab):
            xs = p_ref[s]
            r = xs * c + pltpu.roll(xs, RET_DK // 2, axis=1) * sn
            z_ref[:, s * LANES:(s + 1) * LANES] = (r * scale).astype(BF16)

    for dil, pred in ((4, is_d4), (16, is_d16)):
        @pl.when(pred)
        def _(dil=dil):
            rows = SEG // dil
            for rho in range(dil):
                for s in range(n_slab):
                    piece = p_ref[s, pl.ds(rho, rows, stride=dil), :]
                    z_ref[rho * rows:(rho + 1) * rows, s * LANES:(s + 1) * LANES] = piece.astype(BF16)

    @pl.when(jnp.logical_not(is_rope | is_d4 | is_d16))
    def _():
        for s in range(n_slab):
            z_ref[:, s * LANES:(s + 1) * LANES] = p_ref[s].astype(BF16)


def _in_proj(x_p, x_s, norm_g, w_in_bf, cos_t, sin_t):
    def pos_blk(i, j):
        return (jnp.maximum(i - N_PROMPT_SEG, 0), 0)

    return pl.pallas_call(
        _in_proj_kernel,
        grid=(N_SEG, N_COL_BLK),
        in_specs=[
            pl.BlockSpec((SEG, D_MODEL), lambda i, j: (jnp.minimum(i, N_PROMPT_SEG - 1), 0),
                         pipeline_mode=pl.Buffered(1)),
            pl.BlockSpec((SEG, D_MODEL), pos_blk, pipeline_mode=pl.Buffered(1)),
            pl.BlockSpec((1, D_MODEL), lambda i, j: (0, 0)),
            pl.BlockSpec((D_MODEL, COL_BLK), lambda i, j: (0, j)),
            pl.BlockSpec((SEG, LANES), pos_blk),
            pl.BlockSpec((SEG, LANES), pos_blk),
        ],
        out_specs=pl.BlockSpec((SEG, COL_BLK), lambda i, j: (i, j)),
        out_shape=jax.ShapeDtypeStruct((T_ALL, N_IN), BF16),
        scratch_shapes=[
            pltpu.VMEM((SEG, D_MODEL), BF16),
            pltpu.VMEM((COL_BLK // LANES, SEG, LANES), F32),
        ],
        compiler_params=_cparams(("arbitrary", "arbitrary")),
        name="in_proj",
    )(x_p, x_s, norm_g, w_in_bf, cos_t, sin_t)


RET_CHUNKS_PER_SEG = SEG // RET_CHUNK
RET_MAX_CHUNKS = SAMPLE_SEQ // RET_CHUNK
RET_GROUP = 4


def _retention_kernel(seg_ref, phase_ref, reset_ref, cbase_ref,
                      q_ref, k_ref, v_ref, g_ref, mask_ref, dec_ref, cdec_ref, gn_ref,
                      y_ref, sb_ref, sf_ref, sr_ref):
    step = pl.program_id(1)
    phase = phase_ref[step]
    reset = reset_ref[step]
    cbase = cbase_ref[step]
    kdec_f = dec_ref[0, 0]
    qdec_f = dec_ref[0, 1]
    kdec_b = dec_ref[0, 2]
    qdec_b = dec_ref[0, 3]
    cd_f = cdec_ref[0, 0]
    cd_b = cdec_ref[0, 1]

    def kv_outer(kd, v):
        return lax.dot_general(kd, v, (((0,), (0,)), ((), ())), preferred_element_type=F32)

    @pl.when((phase == 0) & (reset == 1))
    def _():
        sr_ref[...] = jnp.zeros_like(sr_ref)

    @pl.when((phase == 1) & (reset == 1))
    def _():
        sf_ref[...] = jnp.zeros_like(sf_ref)

    n_groups = RET_CHUNKS_PER_SEG // RET_GROUP

    def chunk_rows(c):
        return pl.ds(pl.multiple_of(c * RET_CHUNK, RET_CHUNK), RET_CHUNK)

    @pl.when(phase == 0)
    def _():
        def body(it, carry):
            top = RET_CHUNKS_PER_SEG - 1 - it * RET_GROUP
            kvs = []
            for j in range(RET_GROUP):
                rows = chunk_rows(top - j)
                kd = (k_ref[rows, :].astype(F32) * kdec_b).astype(BF16)
                kvs.append(kv_outer(kd, v_ref[rows, :]))
            state = sr_ref[...]
            for j in range(RET_GROUP):
                sb_ref[cbase + top - j] = state.astype(BF16)
                state = cd_b * state + kvs[j]
            sr_ref[...] = state
            return carry

        lax.fori_loop(0, n_groups, body, 0)

    @pl.when(phase == 1)
    def _():
        msk = mask_ref[0]
        gn = gn_ref[0]

        def body(it, carry):
            c0 = it * RET_GROUP
            lhs, vs, kvs = [], [], []
            for j in range(RET_GROUP):
                rows = chunk_rows(c0 + j)
                qb = q_ref[rows, :]
                kb = k_ref[rows, :]
                v = v_ref[rows, :]
                q = qb.astype(F32)
                s = lax.dot_general(qb, kb, (((1,), (1,)), ((), ())), preferred_element_type=F32)
                lhs.append(jnp.concatenate(
                    [(s * msk).astype(BF16), (q * qdec_f).astype(BF16), (q * qdec_b).astype(BF16)], axis=-1))
                vs.append(v)
                kvs.append(kv_outer((kb.astype(F32) * kdec_f).astype(BF16), v))
            state = sf_ref[...]
            for j in range(RET_GROUP):
                c = c0 + j
                rhs = jnp.concatenate([vs[j], state.astype(BF16), sb_ref[cbase + c]], axis=0)
                o = jnp.dot(lhs[j], rhs, preferred_element_type=F32)
                state = cd_f * state + kvs[j]
                mu = jnp.mean(o, axis=-1, keepdims=True)
                oc = o - mu
                var = jnp.mean(oc * oc, axis=-1, keepdims=True)
                on = oc * lax.rsqrt(var + EPS) * gn
                rows = chunk_rows(c)
                gate = g_ref[rows, :].astype(F32)
                y_ref[rows, :] = (gate * jax.nn.sigmoid(gate) * on).astype(BF16)
            sf_ref[...] = state
            return carry

        lax.fori_loop(0, n_groups, body, 0)


def _retention_schedule():
    seg, phase, reset, cbase = [], [], [], []
    for p in range(N_PROMPT_SEG):
        for ph in (0, 1):
            seg.append(p); phase.append(ph); reset.append(1); cbase.append(0)
    n_s = N_SEG - N_PROMPT_SEG
    for i in range(n_s):
        t = n_s - 1 - i
        seg.append(N_PROMPT_SEG + t); phase.append(0); reset.append(int(i == 0)); cbase.append(t * RET_CHUNKS_PER_SEG)
    for t in range(n_s):
        seg.append(N_PROMPT_SEG + t); phase.append(1); reset.append(int(t == 0)); cbase.append(t * RET_CHUNKS_PER_SEG)
    hold = list(seg)
    for i in range(len(seg)):
        if phase[i] == 0:
            nxt = next(j for j in range(i + 1, len(seg)) if phase[j] == 1)
            hold[i] = seg[nxt]
    arr = lambda a: jnp.asarray(np.asarray(a, np.int32))
    return arr(seg), arr(phase), arr(reset), arr(cbase), arr(hold)


def _retention(z, ret_mask, ret_dec, ret_cdec, gn_g):
    seg, phase, reset, cbase, hold = _retention_schedule()
    n_steps = int(seg.shape[0])
    qk_blk = lambda col: (lambda h, s, seg_r, ph_r, rs_r, cb_r, hold_r: (seg_r[s], col // RET_DK + h))
    hold_blk = lambda col, w: (lambda h, s, seg_r, ph_r, rs_r, cb_r, hold_r: (hold_r[s], col // w + h))
    v_blk = lambda h, s, seg_r, ph_r, rs_r, cb_r, hold_r: (seg_r[s], COL_RV // RET_DV + h)
    per_head = lambda h, s, *_: (h, 0, 0)
    per_head4 = lambda h, s, *_: (h, 0, 0, 0)
    grid_spec = pltpu.PrefetchScalarGridSpec(
        num_scalar_prefetch=5,
        grid=(RET_HEADS, n_steps),
        in_specs=[
            pl.BlockSpec((SEG, RET_DK), hold_blk(COL_RQ, RET_DK)),
            pl.BlockSpec((SEG, RET_DK), qk_blk(COL_RK)),
            pl.BlockSpec((SEG, RET_DV), v_blk),
            pl.BlockSpec((SEG, RET_DV), hold_blk(COL_RG, RET_DV)),
            pl.BlockSpec((1, RET_CHUNK, RET_CHUNK), per_head),
            pl.BlockSpec((1, 4, RET_CHUNK, 1), per_head4),
            pl.BlockSpec((1, 2, 1, RET_DV), per_head4),
            pl.BlockSpec((1, 1, RET_DV), per_head),
        ],
        out_specs=pl.BlockSpec((SEG, RET_DV), lambda h, s, seg_r, ph_r, rs_r, cb_r, hold_r: (hold_r[s], h)),
        scratch_shapes=[
            pltpu.VMEM((RET_MAX_CHUNKS, RET_DK, RET_DV), BF16),
            pltpu.VMEM((RET_DK, RET_DV), F32),
            pltpu.VMEM((RET_DK, RET_DV), F32),
        ],
    )

    def kernel(seg_r, ph_r, rs_r, cb_r, hold_r, *refs):
        _retention_kernel(seg_r, ph_r, rs_r, cb_r, *refs)

    return pl.pallas_call(
        kernel,
        grid_spec=grid_spec,
        out_shape=jax.ShapeDtypeStruct((T_ALL, RET_V_W), BF16),
        compiler_params=_cparams(("arbitrary", "arbitrary")),
        name="retention",
    )(seg, phase, reset, cbase, hold, z, z, z, z, ret_mask, ret_dec, ret_cdec, gn_g)


def _attention_kernel(first_ref, last_ref, q_ref, kp_ref, km_ref, kn_ref, vp_ref, vm_ref, vn_ref,
                      bias_ref, gq_ref, gk_ref, o_ref, lse_ref, kall, vall, oacc, lacc, *, dil):
    nb = SEG // dil // ATT_BLOCK
    nqb = nb // 2
    n_slab = ATT_W // LANES
    c = pl.program_id(0)
    is_first = first_ref[c]
    is_last = last_ref[c]
    lane = lax.broadcasted_iota(jnp.int32, (1, LANES), 1)
    lo = lane < ATT_DH
    gq = gq_ref[...]
    gk = gk_ref[...]

    def head_norm(x, g):
        x2 = x * x
        s_lo = jnp.sum(jnp.where(lo, x2, 0.0), axis=-1, keepdims=True)
        s_hi = jnp.sum(jnp.where(lo, 0.0, x2), axis=-1, keepdims=True)
        ms = jnp.where(lo, s_lo, s_hi) * (1.0 / ATT_DH)
        return x * lax.rsqrt(ms + EPS) * g

    def norm_block(src):
        parts = [head_norm(src[:, s * LANES:(s + 1) * LANES].astype(F32), gk).astype(BF16) for s in range(n_slab)]
        return jnp.concatenate(parts, axis=-1)

    def fill_main(it, carry):
        rho = it // nb
        blk = it % nb
        kall[rho, blk + 1] = norm_block(km_ref[rho, blk])
        vall[rho, blk + 1] = vm_ref[rho, blk]
        return carry

    lax.fori_loop(0, dil * nb, fill_main, 0)

    def fill_halo(rho, carry):
        kall[rho, 0] = norm_block(kp_ref[rho, 0])
        kall[rho, nb + 1] = norm_block(kn_ref[rho, 0])
        vall[rho, 0] = vp_ref[rho, 0]
        vall[rho, nb + 1] = vn_ref[rho, 0]
        return carry

    lax.fori_loop(0, dil, fill_halo, 0)

    col = lax.broadcasted_iota(jnp.int32, (1, KW), 1)

    def body(it, carry):
        rho = it // nqb
        qb = it % nqb
        edge = jnp.where(((col < ATT_BLOCK) & (qb == 0) & (is_first == 1))
                         | ((col >= KW - ATT_BLOCK) & (qb == nqb - 1) & (is_last == 1)), NEG, 0.0).astype(F32)
        start = rho + qb * (QB * dil)
        lse_tile = jnp.zeros((QB, LANES), F32)
        for s in range(n_slab):
            sl = slice(s * LANES, (s + 1) * LANES)
            qx = q_ref[rho, pl.ds(2 * qb, 2), :, sl].reshape(QB, LANES).astype(F32)
            qn = head_norm(qx, gq) * (ATT_DH ** -0.5)
            kw = kall[rho, pl.ds(2 * qb, 4), :, sl].reshape(KW, LANES)
            vw = vall[rho, pl.ds(2 * qb, 4), :, sl].reshape(KW, LANES)
            outs = []
            for hh in range(2):
                sel = lo if hh == 0 else jnp.logical_not(lo)
                qm = jnp.where(sel, qn, 0.0).astype(BF16)
                sc = lax.dot_general(qm, kw, (((1,), (1,)), ((), ())), preferred_element_type=F32)
                sc = sc + bias_ref[2 * s + hh] + edge
                m = jnp.max(sc, axis=-1, keepdims=True)
                e = jnp.exp(sc - m)
                l = jnp.sum(e, axis=-1, keepdims=True)
                o_h = jnp.dot(e.astype(BF16), vw, preferred_element_type=F32)
                outs.append(o_h * (1.0 / l))
                lse_tile = lse_tile + jnp.where(lane == 2 * s + hh, m + jnp.log(l), 0.0)
            o_slab = jnp.where(lo, outs[0], outs[1])
            if dil == 1:
                oacc[s, pl.ds(start, QB), :] = o_slab
            else:
                oacc[s, pl.ds(start, QB, stride=dil), :] = o_slab
        if dil == 1:
            lacc[pl.ds(start, QB), :] = lse_tile
        else:
            lacc[pl.ds(start, QB, stride=dil), :] = lse_tile
        return carry

    lax.fori_loop(0, dil * nqb, body, 0)

    for s in range(n_slab):
        o_ref[:, s * LANES:(s + 1) * LANES] = oacc[s].astype(BF16)
    lse_ref[...] = lacc[...]


def _attention_group(z, bias_g, gq, gk, first, last, gi, dil):
    nb = SEG // dil // ATT_BLOCK
    z5 = z.reshape(N_SEG, dil, nb, ATT_BLOCK, N_IN)
    cq = (COL_ATT + 3 * gi * ATT_W) // ATT_W
    ck, cv = cq + 1, cq + 2
    main = lambda cb: pl.BlockSpec((None, dil, nb, ATT_BLOCK, ATT_W), lambda c, f, l: (c, 0, 0, 0, cb))
    prev = lambda cb: pl.BlockSpec((None, dil, 1, ATT_BLOCK, ATT_W),
                                   lambda c, f, l: (c - 1 + f[c], 0, nb - 1, 0, cb))
    nxt = lambda cb: pl.BlockSpec((None, dil, 1, ATT_BLOCK, ATT_W),
                                  lambda c, f, l: (c + 1 - l[c], 0, 0, 0, cb))
    grid_spec = pltpu.PrefetchScalarGridSpec(
        num_scalar_prefetch=2,
        grid=(N_SEG,),
        in_specs=[
            main(cq), prev(ck), main(ck), nxt(ck), prev(cv), main(cv), nxt(cv),
            pl.BlockSpec((ATT_HEADS, QB, KW), lambda c, f, l: (0, 0, 0)),
            pl.BlockSpec((1, LANES), lambda c, f, l: (0, 0)),
            pl.BlockSpec((1, LANES), lambda c, f, l: (0, 0)),
        ],
        out_specs=[
            pl.BlockSpec((SEG, ATT_W), lambda c, f, l: (c, 0)),
            pl.BlockSpec((SEG, LANES), lambda c, f, l: (c, 0)),
        ],
        scratch_shapes=[
            pltpu.VMEM((dil, nb + 2, ATT_BLOCK, ATT_W), BF16),
            pltpu.VMEM((dil, nb + 2, ATT_BLOCK, ATT_W), BF16),
            pltpu.VMEM((ATT_W // LANES, SEG, LANES), F32),
            pltpu.VMEM((SEG, LANES), F32),
        ],
    )
    return pl.pallas_call(
        functools.partial(_attention_kernel, dil=dil),
        grid_spec=grid_spec,
        out_shape=[jax.ShapeDtypeStruct((T_ALL, ATT_W), BF16), jax.ShapeDtypeStruct((T_ALL, LANES), F32)],
        compiler_params=_cparams(("arbitrary",)),
        name=f"attention_d{dil}",
    )(first, last, z5, z5, z5, z5, z5, z5, z5, bias_g, gq, gk)


def _merge_kernel(yret_ref, o0_ref, o1_ref, o2_ref, l0_ref, l1_ref, l2_ref,
                  gret_a_ref, gret_b_ref, gatt_a_ref, gatt_b_ref, xp_ref, xs_ref,
                  wret_ref, watt_ref, wout_ref, expand_ref, nffn_ref, wr_ref, br_ref,
                  x1_ref, hp_ref, idx_ref, gate_ref, rank_ref, cnt_ref, carry_ref):
    i = pl.program_id(0)
    tm = MERGE_TM

    @pl.when(i == 0)
    def _():
        carry_ref[...] = jnp.zeros_like(carry_ref)

    l0, l1, l2 = l0_ref[...], l1_ref[...], l2_ref[...]
    lm = jnp.maximum(jnp.maximum(l0, l1), l2)
    e0, e1, e2 = jnp.exp(l0 - lm), jnp.exp(l1 - lm), jnp.exp(l2 - lm)
    inv = 1.0 / (e0 + e1 + e2)
    y_att = jnp.zeros((tm, ATT_W), F32)
    for e, o_ref in ((e0, o0_ref), (e1, o1_ref), (e2, o2_ref)):
        w = e * inv
        w_hi = w.astype(BF16)
        w_lo = (w - w_hi.astype(F32)).astype(BF16)
        wide = (jnp.dot(w_hi, expand_ref[...], preferred_element_type=F32)
                + jnp.dot(w_lo, expand_ref[...], preferred_element_type=F32))
        y_att = y_att + wide * o_ref[...].astype(F32)

    p_ret = jnp.dot(yret_ref[...], wret_ref[...], preferred_element_type=F32)
    p_att = jnp.dot(y_att.astype(BF16), watt_ref[...], preferred_element_type=F32)
    g_ret = jnp.concatenate([gret_a_ref[...], gret_b_ref[...]], axis=-1).astype(F32)
    g_att = jnp.concatenate([gatt_a_ref[...], gatt_b_ref[...]], axis=-1).astype(F32)
    merged = jax.nn.sigmoid(g_ret) * p_ret + jax.nn.sigmoid(g_att) * p_att
    x_in = jnp.where(i < T_PROMPT // tm, xp_ref[...], xs_ref[...])
    x1 = x_in + jnp.dot(merged.astype(BF16), wout_ref[...], preferred_element_type=F32)
    x1_ref[...] = x1

    ms = jnp.mean(x1 * x1, axis=-1, keepdims=True)
    h2 = x1 * lax.rsqrt(ms + EPS) * nffn_ref[...]
    hp_ref[...] = _pack_bf16_pair(h2)

    logits = jnp.dot(h2, wr_ref[...], preferred_element_type=F32, precision=lax.Precision.HIGHEST) + br_ref[...]
    lane = lax.broadcasted_iota(jnp.int32, (tm, LANES), 1)
    lane_f = lane.astype(F32)
    work = logits
    vals, idxs = [], []
    for _ in range(TOP_K):
        m = jnp.max(work, axis=-1, keepdims=True)
        ix = jnp.min(jnp.where(work == m, lane_f, float(LANES)), axis=-1, keepdims=True)
        vals.append(m)
        idxs.append(ix)
        work = jnp.where(lane_f == ix, -3e38, work)
    es = [jnp.exp(v - vals[0]) for v in vals]
    den = es[0] + es[1] + es[2] + es[3]
    onehot = jnp.zeros((tm, LANES), F32)
    for ix in idxs:
        onehot = onehot + jnp.where(lane_f == ix, 1.0, 0.0)
    row = lax.broadcasted_iota(jnp.int32, (tm, tm), 0)
    colm = lax.broadcasted_iota(jnp.int32, (tm, tm), 1)
    tri = jnp.where(colm < row, 1.0, 0.0).astype(BF16)
    before = jnp.dot(tri, onehot.astype(BF16), preferred_element_type=F32) + carry_ref[...]
    idx_out = jnp.zeros((tm, LANES), F32)
    gate_out = jnp.zeros((tm, LANES), F32)
    rank_out = jnp.zeros((tm, LANES), F32)
    for k in range(TOP_K):
        rk = jnp.sum(jnp.where(lane_f == idxs[k], before, 0.0), axis=-1, keepdims=True)
        sel = lane == k
        idx_out = jnp.where(sel, idxs[k], idx_out)
        gate_out = jnp.where(sel, es[k] / den, gate_out)
        rank_out = jnp.where(sel, rk, rank_out)
    idx_ref[...] = idx_out.astype(jnp.int32)
    gate_ref[...] = gate_out
    rank_ref[...] = rank_out.astype(jnp.int32)
    total = carry_ref[...] + jnp.sum(onehot, axis=0, keepdims=True)
    carry_ref[...] = total
    cnt_ref[...] = jnp.broadcast_to(total, cnt_ref.shape)


def _merge(y_ret, o_list, lse_list, z, x_p, x_s, w_ret, w_att, w_out, expand, n_ffn, w_router, b_router):
    tm = MERGE_TM
    n_p = T_PROMPT // tm
    row = lambda w: pl.BlockSpec((tm, w), lambda i: (i, 0))
    full = lambda a: pl.BlockSpec(a.shape, lambda i: (0,) * a.ndim)
    zcol = lambda col: pl.BlockSpec((tm, COL_BLK), lambda i: (i, col // COL_BLK))
    return pl.pallas_call(
        _merge_kernel,
        grid=(T_ALL // tm,),
        in_specs=[row(RET_V_W), row(ATT_W), row(ATT_W), row(ATT_W), row(LANES), row(LANES), row(LANES),
                  zcol(COL_GATE_RET), zcol(COL_GATE_RET + COL_BLK), zcol(COL_GATE_ATT),
                  zcol(COL_GATE_ATT + COL_BLK),
                  pl.BlockSpec((tm, D_MODEL), lambda i: (jnp.minimum(i, n_p - 1), 0)),
                  pl.BlockSpec((tm, D_MODEL), lambda i: (jnp.maximum(i - n_p, 0), 0)),
                  full(w_ret), full(w_att), full(w_out), full(expand), full(n_ffn), full(w_router), full(b_router)],
        out_specs=[row(D_MODEL), row(PACK_W), row(LANES), row(LANES), row(LANES),
                   pl.BlockSpec((8, LANES), lambda i: (0, 0))],
        out_shape=[jax.ShapeDtypeStruct((T_ALL, D_MODEL), F32),
                   jax.ShapeDtypeStruct((T_ALL, PACK_W), jnp.uint32),
                   jax.ShapeDtypeStruct((T_ALL, LANES), jnp.int32),
                   jax.ShapeDtypeStruct((T_ALL, LANES), F32),
                   jax.ShapeDtypeStruct((T_ALL, LANES), jnp.int32),
                   jax.ShapeDtypeStruct((8, LANES), F32)],
        scratch_shapes=[pltpu.VMEM((1, LANES), F32)],
        compiler_params=_cparams(("arbitrary",)),
        name="merge_router",
    )(y_ret, *o_list, *lse_list, z, z, z, z, x_p, x_s, w_ret, w_att, w_out, expand, n_ffn, w_router, b_router)


def _dispatch_kernel(zrow_ref, dest_ref, h_ref, xs_ref, buf_ref, zero_ref, sem_ref, zsem_ref):
    i = pl.program_id(0)
    n = pl.num_programs(0)
    slot = i % 2
    tm = DISPATCH_TM

    @pl.when(i == 0)
    def _():
        zero_ref[...] = jnp.zeros_like(zero_ref)

        def zero_copy(e):
            row0 = pl.multiple_of(zrow_ref[e], MOE_BM)
            return pltpu.make_async_copy(zero_ref, xs_ref.at[pl.ds(row0, MOE_BM)], zsem_ref.at[0])

        for e in range(N_EXPERTS):
            @pl.when(zrow_ref[e] >= 0)
            def _(e=e):
                zero_copy(e).start()
        for e in range(N_EXPERTS):
            @pl.when(zrow_ref[e] >= 0)
            def _(e=e):
                zero_copy(e).wait()

        def tail_copy(b):
            row0 = pl.multiple_of(b * MOE_BM, MOE_BM)
            return pltpu.make_async_copy(zero_ref, xs_ref.at[pl.ds(row0, MOE_BM)], zsem_ref.at[0])

        n_used = zrow_ref[N_EXPERTS]
        lax.fori_loop(n_used, N_SLOT_BLOCKS, lambda b, c: (tail_copy(b).start(), c)[1], 0)
        lax.fori_loop(n_used, N_SLOT_BLOCKS, lambda b, c: (tail_copy(b).wait(), c)[1], 0)

    def wait_slot(sl):
        for _ in range(TOP_K):
            pltpu.make_async_copy(buf_ref.at[sl], xs_ref.at[pl.ds(0, tm)], sem_ref.at[sl]).wait()

    @pl.when(i >= 1)
    def _():
        wait_slot(1 - slot)

    buf_ref[slot] = h_ref[...]

    def issue(r, carry):
        for k in range(TOP_K):
            d = dest_ref[r * TOP_K + k]
            pltpu.make_async_copy(buf_ref.at[slot, pl.ds(r, 1)], xs_ref.at[pl.ds(d, 1)],
                                  sem_ref.at[slot]).start(priority=k % 2)
        return carry

    lax.fori_loop(0, tm, issue, 0, unroll=4)

    @pl.when(i == n - 1)
    def _():
        wait_slot(slot)


def _dispatch(zrow, dest_flat, h_packed):
    tm = DISPATCH_TM
    return pl.pallas_call(
        _dispatch_kernel,
        grid=(T_ALL // tm,),
        in_specs=[pl.BlockSpec(memory_space=pltpu.SMEM),
                  pl.BlockSpec((tm * TOP_K,), lambda i: (i,), memory_space=pltpu.SMEM),
                  pl.BlockSpec((tm, PACK_W), lambda i: (i, 0))],
        out_specs=pl.BlockSpec(memory_space=pl.ANY),
        out_shape=jax.ShapeDtypeStruct((N_SLOTS, PACK_W), jnp.uint32),
        scratch_shapes=[pltpu.VMEM((2, tm, PACK_W), jnp.uint32), pltpu.VMEM((MOE_BM, PACK_W), jnp.uint32),
                        pltpu.SemaphoreType.DMA((2,)), pltpu.SemaphoreType.DMA((1,))],
        compiler_params=_cparams(("arbitrary",)),
        name="dispatch",
    )(zrow, dest_flat, h_packed)


def _expert_kernel(be_ref, nused_ref, xs_ref, wg_ref, bg_ref, wu_ref, bu_ref, wd_ref, bd_ref, ys_ref, wbf_ref):
    b = pl.program_id(0)
    active = b < nused_ref[0]
    new_expert = (b == 0) | (be_ref[b] != be_ref[jnp.maximum(b - 1, 0)])

    @pl.when(active & new_expert)
    def _():
        rows = 128
        for wi, w_ref in enumerate((wg_ref, wu_ref, wd_ref)):
            for r in range(0, D_MODEL, rows):
                wbf_ref[wi, r:r + rows, :] = w_ref[0, r:r + rows, :].astype(BF16)

    @pl.when(active)
    def _():
        x = _unpack_bf16_pair(xs_ref[...]).astype(BF16)
        g = jnp.dot(x, wbf_ref[0], preferred_element_type=F32) + bg_ref[0]
        u = jnp.dot(x, wbf_ref[1], preferred_element_type=F32) + bu_ref[0]
        g = jnp.minimum(g, SWIGLU_LIMIT)
        u = jnp.clip(u, -SWIGLU_LIMIT, SWIGLU_LIMIT)
        glu = g * jax.nn.sigmoid(SWIGLU_ALPHA * g)
        act = ((u + 1.0) * glu).astype(BF16)
        y = jnp.dot(act, wbf_ref[2], preferred_element_type=F32) + bd_ref[0]
        ys_ref[...] = _pack_bf16_pair(y)

    @pl.when(jnp.logical_not(active))
    def _():
        ys_ref[...] = jnp.zeros_like(ys_ref)


def _experts(block_expert, n_used, xs, wg, bg, wu, bu, wd, bd):
    assert D_FF == D_MODEL
    blk = lambda b, be, nu: (jnp.minimum(b, nu[0] - 1), 0)
    wsp = lambda: pl.BlockSpec((1, D_MODEL, D_FF), lambda b, be, nu: (be[b], 0, 0))
    bsp = lambda: pl.BlockSpec((1, 1, D_FF), lambda b, be, nu: (be[b], 0, 0))
    grid_spec = pltpu.PrefetchScalarGridSpec(
        num_scalar_prefetch=2,
        grid=(N_SLOT_BLOCKS,),
        in_specs=[pl.BlockSpec((MOE_BM, PACK_W), blk), wsp(), bsp(), wsp(), bsp(), wsp(), bsp()],
        out_specs=pl.BlockSpec((MOE_BM, PACK_W), lambda b, be, nu: (b, 0)),
        scratch_shapes=[pltpu.VMEM((3, D_MODEL, D_FF), BF16)],
    )
    return pl.pallas_call(
        _expert_kernel,
        grid_spec=grid_spec,
        out_shape=jax.ShapeDtypeStruct((N_SLOTS, PACK_W), jnp.uint32),
        compiler_params=_cparams(("arbitrary",)),
        name="experts",
    )(block_expert, n_used, xs, wg, bg, wu, bu, wd, bd)


def _combine_kernel(dest_ref, ys_ref, out_ref, sem_ref):
    tm = COMBINE_TM

    def issue(r2, carry):
        for par in range(2):
            r = 2 * r2 + par
            d = dest_ref[r]
            pltpu.make_async_copy(ys_ref.at[pl.ds(d, 1)], out_ref.at[0, pl.ds(r, 1)],
                                  sem_ref.at[0]).start(priority=par)
        return carry

    lax.fori_loop(0, tm // 2, issue, 0, unroll=4)
    pltpu.make_async_copy(ys_ref.at[pl.ds(0, tm)], out_ref.at[0], sem_ref.at[0]).wait()


def _combine(dest_kmajor, ys):
    tm = COMBINE_TM
    n_t = T_ALL // tm
    return pl.pallas_call(
        _combine_kernel,
        grid=(TOP_K, n_t),
        in_specs=[pl.BlockSpec((tm,), lambda k, i: (k * n_t + i,), memory_space=pltpu.SMEM),
                  pl.BlockSpec(memory_space=pl.ANY)],
        out_specs=pl.BlockSpec((1, tm, PACK_W), lambda k, i: (k, i, 0)),
        out_shape=jax.ShapeDtypeStruct((TOP_K, T_ALL, PACK_W), jnp.uint32),
        scratch_shapes=[pltpu.SemaphoreType.DMA((1,))],
        compiler_params=_cparams(("arbitrary", "arbitrary")),
        name="combine",
    )(dest_kmajor, ys)


def _final_kernel(x1_ref, yg_ref, gate_ref, p_ref, nple_ref, wpg_ref, wpp_ref, out_ref):
    x2 = x1_ref[...]
    gates = gate_ref[...]
    for k in range(TOP_K):
        x2 = x2 + gates[:, k:k + 1] * _unpack_bf16_pair(yg_ref[k])
    ms = jnp.mean(x2 * x2, axis=-1, keepdims=True)
    h3 = (x2 * lax.rsqrt(ms + EPS) * nple_ref[...]).astype(BF16)
    gate = jax.nn.sigmoid(jnp.dot(h3, wpg_ref[...], preferred_element_type=F32))
    proj = jnp.dot(p_ref[...].astype(BF16), wpp_ref[...], preferred_element_type=F32)
    out_ref[...] = x2 + gate * proj


def _final(x1, yg, gates, p, n_ple, w_pg, w_pp, row0, n_rows):
    tm = FINAL_TM
    off = row0 // tm
    full = lambda a: pl.BlockSpec(a.shape, lambda i: (0,) * a.ndim)
    return pl.pallas_call(
        _final_kernel,
        grid=(n_rows // tm,),
        in_specs=[pl.BlockSpec((tm, D_MODEL), lambda i: (i + off, 0)),
                  pl.BlockSpec((TOP_K, tm, PACK_W), lambda i: (0, i + off, 0)),
                  pl.BlockSpec((tm, LANES), lambda i: (i + off, 0)),
                  pl.BlockSpec((tm, PLE_DIM), lambda i: (i, 0)),
                  full(n_ple), full(w_pg), full(w_pp)],
        out_specs=pl.BlockSpec((tm, D_MODEL), lambda i: (i, 0)),
        out_shape=jax.ShapeDtypeStruct((n_rows, D_MODEL), F32),
        compiler_params=_cparams(("arbitrary",)),
        name="final_ple",
    )(x1, yg, gates, p, n_ple, w_pg, w_pp)


def _rope_tables():
    half = RET_DK // 2
    freq = ROPE_THETA ** (-jnp.arange(half, dtype=F32) / half)
    ang = jnp.arange(SAMPLE_SEQ, dtype=F32)[:, None] * freq[None, :]
    cos, sin = jnp.cos(ang), jnp.sin(ang)
    return jnp.concatenate([cos, cos], axis=-1), jnp.concatenate([-sin, sin], axis=-1)


def _retention_tables(decay_logit):
    lg = jax.nn.log_sigmoid(decay_logit.astype(F32))
    c = RET_CHUNK
    idx = jnp.arange(c, dtype=F32)
    diff = idx[:, None] - idx[None, :]
    lf = lg[0][:, None, None]
    lb = lg[1][:, None, None]
    mask = jnp.where(diff[None] >= 0, jnp.exp(lf * jnp.maximum(diff, 0.0)[None]),
                     jnp.exp(lb * jnp.maximum(-diff, 0.0)[None]))
    kdec_f = jnp.exp(lg[0][:, None] * (c - 1.0 - idx)[None, :])
    qdec_f = jnp.exp(lg[0][:, None] * (idx + 1.0)[None, :])
    kdec_b = jnp.exp(lg[1][:, None] * idx[None, :])
    qdec_b = jnp.exp(lg[1][:, None] * (c - idx)[None, :])
    dec = jnp.stack([kdec_f, qdec_f, kdec_b, qdec_b], axis=1)[..., None]
    cdec = jnp.exp(lg * c).T
    cdec = jnp.broadcast_to(cdec[:, :, None, None], (RET_HEADS, 2, 1, RET_DV))
    return mask, dec, cdec


def _t5_bucket(rel):
    half = T5_BUCKETS // 2
    exact = half // 2
    n = np.abs(rel)
    ratio = np.log(np.maximum(n, 1).astype(np.float32) / np.float32(exact)) / np.float32(math.log(T5_MAX_DIST / exact))
    large = exact + (ratio * np.float32(half - exact)).astype(np.int32)
    large = np.minimum(large, half - 1)
    return np.where(rel > 0, half, 0) + np.where(n < exact, n, large)


def _attention_bias(rel_bias, gi, dil, radius):
    qi = np.arange(QB)
    ki = np.arange(KW) - ATT_BLOCK
    rel = ki[None, :] - qi[:, None]
    onehot = jnp.asarray(_t5_bucket(rel * dil)[..., None] == np.arange(T5_BUCKETS), F32)
    tab = rel_bias[:, gi * ATT_HEADS:(gi + 1) * ATT_HEADS].astype(F32)
    bias = jnp.einsum('qkb,bh->hqk', onehot, tab, precision=lax.Precision.HIGHEST)
    return jnp.where(jnp.asarray(np.abs(rel) <= radius)[None], bias, NEG)


def _seq_edge_flags():
    first = np.zeros((N_SEG,), np.int32)
    last = np.zeros((N_SEG,), np.int32)
    first[:N_PROMPT_SEG] = 1
    last[:N_PROMPT_SEG] = 1
    first[N_PROMPT_SEG] = 1
    last[N_SEG - 1] = 1
    return jnp.asarray(first), jnp.asarray(last)


def _pad_lanes(a, value=0.0):
    return jnp.pad(a, ((0, 0), (0, LANES - a.shape[-1])), constant_values=value)


def kernel(x_prompt, x_sample, p_prompt, p_sample, norm_mix_g, w_in, ret_decay_logit, ret_gn_g,
           att_q_norm_g, att_k_norm_g, rel_bias, w_ret_proj, w_att_proj, w_out, norm_ffn_g,
           w_router, b_router, w_gate, b_gate, w_up, b_up, w_down, b_down,
           norm_ple_g, w_ple_gate, w_ple_proj):
    assert norm_mix_g.shape[0] == 1, "one layer"
    x_p = x_prompt.reshape(T_PROMPT, D_MODEL)
    x_s = x_sample.reshape(SAMPLE_SEQ, D_MODEL)

    cos_t, sin_t = _rope_tables()
    z = _in_proj(x_p, x_s, norm_mix_g.astype(F32), w_in[0].astype(BF16), cos_t, sin_t)

    ret_mask, ret_dec, ret_cdec = _retention_tables(ret_decay_logit[0])
    y_ret = _retention(z, ret_mask, ret_dec, ret_cdec, ret_gn_g[0].reshape(RET_HEADS, 1, RET_DV).astype(F32))

    first, last = _seq_edge_flags()
    o_list, lse_list = [], []
    for gi, (window, dil) in enumerate(ATT_GROUPS):
        bias_g = _attention_bias(rel_bias, gi, dil, window // (2 * dil))
        gq = jnp.tile(att_q_norm_g[0, gi].astype(F32), LANES // ATT_DH)[None, :]
        gk = jnp.tile(att_k_norm_g[0, gi].astype(F32), LANES // ATT_DH)[None, :]
        o_g, lse_g = _attention_group(z, bias_g, gq, gk, first, last, gi, dil)
        o_list.append(o_g)
        lse_list.append(lse_g)

    expand = (jnp.arange(LANES)[:, None] == (jnp.arange(ATT_W)[None, :] // ATT_DH)).astype(BF16)
    w_router_p = _pad_lanes(w_router[0].astype(F32))
    b_router_p = _pad_lanes(b_router.astype(F32), NEG)
    x1, h_packed, idx, gates, rank, cnt = _merge(
        y_ret, o_list, lse_list, z, x_p, x_s, w_ret_proj[0].astype(BF16), w_att_proj[0].astype(BF16),
        w_out[0].astype(BF16), expand, norm_ffn_g.astype(F32), w_router_p, b_router_p)

    counts = cnt[0, :N_EXPERTS].astype(jnp.int32)
    padded = (counts + MOE_BM - 1) // MOE_BM * MOE_BM
    pad_end = jnp.cumsum(padded)
    pad_start = pad_end - padded
    top_idx = idx[:, :TOP_K]
    dest = pad_start[top_idx] + rank[:, :TOP_K]
    n_used = (pad_end[-1] // MOE_BM).astype(jnp.int32).reshape(1)
    blk_row0 = jnp.arange(N_SLOT_BLOCKS, dtype=jnp.int32) * MOE_BM
    block_expert = jnp.minimum(jnp.sum((pad_end[None, :] <= blk_row0[:, None]).astype(jnp.int32), axis=1),
                               N_EXPERTS - 1).astype(jnp.int32)
    zrow = jnp.concatenate([jnp.where(padded > 0, pad_end - MOE_BM, -1).astype(jnp.int32), n_used])

    xs = _dispatch(zrow, dest.reshape(-1), h_packed)
    ys = _experts(block_expert, n_used, xs,
                  w_gate[0], b_gate[0].reshape(N_EXPERTS, 1, D_FF).astype(F32),
                  w_up[0], b_up[0].reshape(N_EXPERTS, 1, D_FF).astype(F32),
                  w_down[0], b_down[0].reshape(N_EXPERTS, 1, D_MODEL).astype(F32))
    yg = _combine(dest.T.reshape(-1), ys)

    n_ple = norm_ple_g.astype(F32)
    w_pg = w_ple_gate[0].astype(BF16)
    w_pp = w_ple_proj[0].astype(BF16)
    y_p = _final(x1, yg, gates, p_prompt[0].reshape(T_PROMPT, PLE_DIM), n_ple, w_pg, w_pp, 0, T_PROMPT)
    y_s = _final(x1, yg, gates, p_sample[0].reshape(SAMPLE_SEQ, PLE_DIM), n_ple, w_pg, w_pp, T_PROMPT, SAMPLE_SEQ)
    return (y_p.reshape(x_prompt.shape), y_s.reshape(x_sample.shape))
```

```python
import functools
import math

import jax
import jax.numpy as jnp
import numpy as np
from jax import lax
from jax.experimental import pallas as pl
from jax.experimental.pallas import tpu as pltpu
from jax.experimental.pallas import tpu_sc as plsc

F32 = jnp.float32
BF16 = jnp.bfloat16

D_MODEL = 1024
N_PROMPT_SEQ = 8
PROMPT_SEQ = 2048
SAMPLE_SEQ = 16384
T_PROMPT = N_PROMPT_SEQ * PROMPT_SEQ
T_ALL = T_PROMPT + SAMPLE_SEQ

RET_HEADS = 4
RET_DK = 128
RET_DV = 256
RET_CHUNK = 128
ROPE_THETA = 10000.0
ATT_GROUPS = ((128, 1), (512, 4), (2048, 16))
N_GROUPS = 3
ATT_HEADS = 8
ATT_DH = 64
ATT_BLOCK = 64
ATT_W = ATT_HEADS * ATT_DH
T5_BUCKETS = 32
T5_MAX_DIST = 1024
N_EXPERTS = 32
TOP_K = 4
D_FF = 1024
SWIGLU_ALPHA = 1.702
SWIGLU_LIMIT = 7.0
PLE_DIM = 256
EPS = 1e-6

RET_QK_W = RET_HEADS * RET_DK
RET_V_W = RET_HEADS * RET_DV
N_IN = 2 * RET_QK_W + 2 * RET_V_W + 3 * N_GROUPS * ATT_W + 2 * D_MODEL

COL_RQ = 0
COL_RK = RET_QK_W
COL_RV = 2 * RET_QK_W
COL_RG = COL_RV + RET_V_W
COL_ATT = COL_RG + RET_V_W
COL_GATE_RET = COL_ATT + 3 * N_GROUPS * ATT_W
COL_GATE_ATT = COL_GATE_RET + D_MODEL

LANES = 128
VMEM_LIMIT = 56 * 1024 * 1024

SEG = 2048
N_SEG = T_ALL // SEG
N_PROMPT_SEG = T_PROMPT // SEG
COL_BLK = 512
N_COL_BLK = N_IN // COL_BLK
QB = 128
KW = 256
NEG = -1e30
LOG2E = math.log2(math.e)
LN2 = math.log(2.0)
MERGE_TM = 512
FINAL_TM = 512
MOE_BM = 512
N_SLOT_BLOCKS = T_ALL * TOP_K // MOE_BM + N_EXPERTS
N_SLOTS = N_SLOT_BLOCKS * MOE_BM
HALF_W = D_MODEL // 4
SC_WINDOW = 128


def _cparams(sem, vmem=VMEM_LIMIT):
    return pltpu.CompilerParams(dimension_semantics=sem, vmem_limit_bytes=vmem)


def _pack_bf16_pair(x):
    w = x.shape[-1] // 2
    hi = pltpu.bitcast(x[:, :w].astype(BF16).astype(F32), jnp.uint32)
    lo = pltpu.bitcast(x[:, w:].astype(BF16).astype(F32), jnp.uint32)
    return hi | (lo >> 16)


def _unpack_bf16_pair(p):
    hi = pltpu.bitcast(p & jnp.uint32(0xFFFF0000), F32)
    lo = pltpu.bitcast(p << 16, F32)
    return jnp.concatenate([hi, lo], axis=-1)


def _pack_row_halves(x):
    half = x.shape[-1] // 2
    return _pack_bf16_pair(x[:, :half]), _pack_bf16_pair(x[:, half:])


def _unpack_row_halves(pa, pb):
    return jnp.concatenate([_unpack_bf16_pair(pa), _unpack_bf16_pair(pb)], axis=-1)


def _in_proj_kernel(xp_ref, xs_ref, g_ref, w_ref, cos_ref, sin_ref, z_ref, h_ref, p_ref):
    i = pl.program_id(0)
    j = pl.program_id(1)

    def norm_into_h(x_ref):
        xf = x_ref[...]
        ms = jnp.mean(xf * xf, axis=-1, keepdims=True)
        h_ref[...] = (xf * lax.rsqrt(ms + EPS) * g_ref[...]).astype(BF16)

    @pl.when((j == 0) & (i < N_PROMPT_SEG))
    def _():
        norm_into_h(xp_ref)

    @pl.when((j == 0) & (i >= N_PROMPT_SEG))
    def _():
        norm_into_h(xs_ref)

    n_slab = COL_BLK // LANES

    def project():
        return jnp.dot(h_ref[...], w_ref[...], preferred_element_type=F32)

    is_rope = j < (COL_RV // COL_BLK)
    att0 = COL_ATT // COL_BLK
    is_d4 = (j >= att0 + 3) & (j < att0 + 6)
    is_d16 = (j >= att0 + 6) & (j < att0 + 9)

    @pl.when(is_rope)
    def _():
        acc = project()
        scale = jnp.where(j == COL_RK // COL_BLK, RET_DK ** -0.5, 1.0).astype(F32)
        c = cos_ref[...]
        sn = sin_ref[...]
        for s in range(n_slab):
            xs = acc[:, s * LANES:(s + 1) * LANES]
            r = xs * c + pltpu.roll(xs, RET_DK // 2, axis=1) * sn
            z_ref[:, s * LANES:(s + 1) * LANES] = (r * scale).astype(BF16)

    for dil, pred in ((4, is_d4), (16, is_d16)):
        @pl.when(pred)
        def _(dil=dil):
            acc = project()
            for s in range(n_slab):
                p_ref[s] = acc[:, s * LANES:(s + 1) * LANES]
            rows = SEG // dil
            for rho in range(dil):
                for s in range(n_slab):
                    piece = p_ref[s, pl.ds(rho, rows, stride=dil), :]
                    z_ref[rho * rows:(rho + 1) * rows, s * LANES:(s + 1) * LANES] = piece.astype(BF16)

    @pl.when(jnp.logical_not(is_rope | is_d4 | is_d16))
    def _():
        z_ref[...] = project().astype(BF16)


def _in_proj(x_p, x_s, norm_g, w_in_bf, cos_t, sin_t):
    def pos_blk(i, j):
        return (jnp.maximum(i - N_PROMPT_SEG, 0), 0)

    return pl.pallas_call(
        _in_proj_kernel,
        grid=(N_SEG, N_COL_BLK),
        in_specs=[
            pl.BlockSpec((SEG, D_MODEL), lambda i, j: (jnp.minimum(i, N_PROMPT_SEG - 1), 0),
                         pipeline_mode=pl.Buffered(1)),
            pl.BlockSpec((SEG, D_MODEL), pos_blk, pipeline_mode=pl.Buffered(1)),
            pl.BlockSpec((1, D_MODEL), lambda i, j: (0, 0)),
            pl.BlockSpec((D_MODEL, COL_BLK), lambda i, j: (0, j)),
            pl.BlockSpec((SEG, LANES), pos_blk),
            pl.BlockSpec((SEG, LANES), pos_blk),
        ],
        out_specs=pl.BlockSpec((SEG, COL_BLK), lambda i, j: (i, j)),
        out_shape=jax.ShapeDtypeStruct((T_ALL, N_IN), BF16),
        scratch_shapes=[
            pltpu.VMEM((SEG, D_MODEL), BF16),
            pltpu.VMEM((COL_BLK // LANES, SEG, LANES), F32),
        ],
        compiler_params=_cparams(("arbitrary", "arbitrary")),
        name="in_proj",
    )(x_p, x_s, norm_g, w_in_bf, cos_t, sin_t)


RET_CHUNKS_PER_SEG = SEG // RET_CHUNK
RET_MAX_CHUNKS = SAMPLE_SEQ // RET_CHUNK
RET_GROUP = 4


def _retention_kernel(seg_ref, phase_ref, reset_ref, cbase_ref,
                      q_ref, k_ref, v_ref, g_ref, mask_ref, dec_ref, cdec_ref, gn_ref,
                      y_ref, sb_ref, sf_ref, sr_ref):
    step = pl.program_id(1)
    phase = phase_ref[step]
    reset = reset_ref[step]
    cbase = cbase_ref[step]
    kdec_f = dec_ref[0, 0]
    qdec_f = dec_ref[0, 1]
    kdec_b = dec_ref[0, 2]
    qdec_b = dec_ref[0, 3]
    cd_f = cdec_ref[0, 0]
    cd_b = cdec_ref[0, 1]

    def kv_outer(kd, v):
        return lax.dot_general(kd, v, (((0,), (0,)), ((), ())), preferred_element_type=F32)

    @pl.when((phase == 0) & (reset == 1))
    def _():
        sr_ref[...] = jnp.zeros_like(sr_ref)

    @pl.when((phase == 1) & (reset == 1))
    def _():
        sf_ref[...] = jnp.zeros_like(sf_ref)

    n_groups = RET_CHUNKS_PER_SEG // RET_GROUP

    def chunk_rows(c):
        return pl.ds(pl.multiple_of(c * RET_CHUNK, RET_CHUNK), RET_CHUNK)

    @pl.when(phase == 0)
    def _():
        def body(it, carry):
            top = RET_CHUNKS_PER_SEG - 1 - it * RET_GROUP
            kvs = []
            for j in range(RET_GROUP):
                rows = chunk_rows(top - j)
                kd = (k_ref[rows, :].astype(F32) * kdec_b).astype(BF16)
                kvs.append(kv_outer(kd, v_ref[rows, :]))
            state = sr_ref[...]
            for j in range(RET_GROUP):
                sb_ref[cbase + top - j] = state.astype(BF16)
                state = cd_b * state + kvs[j]
            sr_ref[...] = state
            return carry

        lax.fori_loop(0, n_groups, body, 0)

    @pl.when(phase == 1)
    def _():
        msk = mask_ref[0]
        gn = gn_ref[0]

        def body(it, carry):
            c0 = it * RET_GROUP
            lhs, vs, kvs = [], [], []
            for j in range(RET_GROUP):
                rows = chunk_rows(c0 + j)
                qb = q_ref[rows, :]
                kb = k_ref[rows, :]
                v = v_ref[rows, :]
                q = qb.astype(F32)
                s = lax.dot_general(qb, kb, (((1,), (1,)), ((), ())), preferred_element_type=F32)
                lhs.append(jnp.concatenate(
                    [(s * msk).astype(BF16), (q * qdec_f).astype(BF16), (q * qdec_b).astype(BF16)], axis=-1))
                vs.append(v)
                kvs.append(kv_outer((kb.astype(F32) * kdec_f).astype(BF16), v))
            state = sf_ref[...]
            for j in range(RET_GROUP):
                c = c0 + j
                rhs = jnp.concatenate([vs[j], state.astype(BF16), sb_ref[cbase + c]], axis=0)
                o = jnp.dot(lhs[j], rhs, preferred_element_type=F32)
                state = cd_f * state + kvs[j]
                mu = jnp.mean(o, axis=-1, keepdims=True)
                oc = o - mu
                var = jnp.mean(oc * oc, axis=-1, keepdims=True)
                on = oc * lax.rsqrt(var + EPS) * gn
                rows = chunk_rows(c)
                gate = g_ref[rows, :].astype(F32)
                y_ref[rows, :] = (gate * jax.nn.sigmoid(gate) * on).astype(BF16)
            sf_ref[...] = state
            return carry

        lax.fori_loop(0, n_groups, body, 0)


def _retention_schedule():
    seg, phase, reset, cbase = [], [], [], []
    for p in range(N_PROMPT_SEG):
        for ph in (0, 1):
            seg.append(p); phase.append(ph); reset.append(1); cbase.append(0)
    n_s = N_SEG - N_PROMPT_SEG
    for i in range(n_s):
        t = n_s - 1 - i
        seg.append(N_PROMPT_SEG + t); phase.append(0); reset.append(int(i == 0)); cbase.append(t * RET_CHUNKS_PER_SEG)
    for t in range(n_s):
        seg.append(N_PROMPT_SEG + t); phase.append(1); reset.append(int(t == 0)); cbase.append(t * RET_CHUNKS_PER_SEG)
    hold = list(seg)
    for i in range(len(seg)):
        if phase[i] == 0:
            nxt = next(j for j in range(i + 1, len(seg)) if phase[j] == 1)
            hold[i] = seg[nxt]
    arr = lambda a: jnp.asarray(np.asarray(a, np.int32))
    return arr(seg), arr(phase), arr(reset), arr(cbase), arr(hold)


def _retention(z, ret_mask, ret_dec, ret_cdec, gn_g):
    seg, phase, reset, cbase, hold = _retention_schedule()
    n_steps = int(seg.shape[0])
    qk_blk = lambda col: (lambda h, s, seg_r, ph_r, rs_r, cb_r, hold_r: (seg_r[s], col // RET_DK + h))
    hold_blk = lambda col, w: (lambda h, s, seg_r, ph_r, rs_r, cb_r, hold_r: (hold_r[s], col // w + h))
    v_blk = lambda h, s, seg_r, ph_r, rs_r, cb_r, hold_r: (seg_r[s], COL_RV // RET_DV + h)
    per_head = lambda h, s, *_: (h, 0, 0)
    per_head4 = lambda h, s, *_: (h, 0, 0, 0)
    grid_spec = pltpu.PrefetchScalarGridSpec(
        num_scalar_prefetch=5,
        grid=(RET_HEADS, n_steps),
        in_specs=[
            pl.BlockSpec((SEG, RET_DK), hold_blk(COL_RQ, RET_DK)),
            pl.BlockSpec((SEG, RET_DK), qk_blk(COL_RK)),
            pl.BlockSpec((SEG, RET_DV), v_blk),
            pl.BlockSpec((SEG, RET_DV), hold_blk(COL_RG, RET_DV)),
            pl.BlockSpec((1, RET_CHUNK, RET_CHUNK), per_head),
            pl.BlockSpec((1, 4, RET_CHUNK, 1), per_head4),
            pl.BlockSpec((1, 2, 1, RET_DV), per_head4),
            pl.BlockSpec((1, 1, RET_DV), per_head),
        ],
        out_specs=pl.BlockSpec((SEG, RET_DV), lambda h, s, seg_r, ph_r, rs_r, cb_r, hold_r: (hold_r[s], h)),
        scratch_shapes=[
            pltpu.VMEM((RET_MAX_CHUNKS, RET_DK, RET_DV), BF16),
            pltpu.VMEM((RET_DK, RET_DV), F32),
            pltpu.VMEM((RET_DK, RET_DV), F32),
        ],
    )

    def kernel(seg_r, ph_r, rs_r, cb_r, hold_r, *refs):
        _retention_kernel(seg_r, ph_r, rs_r, cb_r, *refs)

    return pl.pallas_call(
        kernel,
        grid_spec=grid_spec,
        out_shape=jax.ShapeDtypeStruct((T_ALL, RET_V_W), BF16),
        compiler_params=_cparams(("arbitrary", "arbitrary")),
        name="retention",
    )(seg, phase, reset, cbase, hold, z, z, z, z, ret_mask, ret_dec, ret_cdec, gn_g)


def _attention_kernel(first_ref, last_ref, q_ref, kp_ref, km_ref, kn_ref, vp_ref, vm_ref, vn_ref,
                      bias_ref, gq_ref, gk_ref, o_ref, lse_ref, kall, vall, oacc, lacc, bvar, *, dil):
    nb = SEG // dil // ATT_BLOCK
    nqb = nb // 2
    n_slab = ATT_W // LANES
    c = pl.program_id(0)
    is_first = first_ref[c]
    is_last = last_ref[c]
    lane = lax.broadcasted_iota(jnp.int32, (1, LANES), 1)
    lo = lane < ATT_DH
    gq = gq_ref[...]
    gk = gk_ref[...]

    @pl.when(c == 0)
    def _():
        col = lax.broadcasted_iota(jnp.int32, (1, KW), 1)
        left = jnp.where(col < ATT_BLOCK, NEG, 0.0).astype(F32)
        right = jnp.where(col >= KW - ATT_BLOCK, NEG, 0.0).astype(F32)
        for h in range(ATT_HEADS):
            b = bias_ref[h]
            bvar[0, h] = b
            bvar[1, h] = b + left
            bvar[2, h] = b + right
            bvar[3, h] = b + left + right

    def head_norm(x, g):
        x2 = x * x
        s_lo = jnp.sum(jnp.where(lo, x2, 0.0), axis=-1, keepdims=True)
        s_hi = jnp.sum(jnp.where(lo, 0.0, x2), axis=-1, keepdims=True)
        ms = jnp.where(lo, s_lo, s_hi) * (1.0 / ATT_DH)
        return x * lax.rsqrt(ms + EPS) * g

    def norm_block(src):
        even, odd = [], []
        for s in range(n_slab):
            xn = head_norm(src[:, s * LANES:(s + 1) * LANES].astype(F32), gk)
            even.append(jnp.where(lo, xn, 0.0).astype(BF16))
            odd.append(jnp.where(lo, 0.0, xn).astype(BF16))
        return jnp.concatenate(even, axis=-1), jnp.concatenate(odd, axis=-1)

    def fill_main(it, carry):
        rho = it // nb
        blk = it % nb
        kall[0, rho, blk + 1], kall[1, rho, blk + 1] = norm_block(km_ref[rho, blk])
        vall[rho, blk + 1] = vm_ref[rho, blk]
        return carry

    lax.fori_loop(0, dil * nb, fill_main, 0, unroll=4)

    def fill_halo(rho, carry):
        kall[0, rho, 0], kall[1, rho, 0] = norm_block(kp_ref[rho, 0])
        kall[0, rho, nb + 1], kall[1, rho, nb + 1] = norm_block(kn_ref[rho, 0])
        vall[rho, 0] = vp_ref[rho, 0]
        vall[rho, nb + 1] = vn_ref[rho, 0]
        return carry

    lax.fori_loop(0, dil, fill_halo, 0)

    def body(it, carry):
        rho = it // nqb
        qb = it % nqb
        var = (jnp.where((qb == 0) & (is_first == 1), 1, 0)
               + jnp.where((qb == nqb - 1) & (is_last == 1), 2, 0))
        start = rho + qb * (QB * dil)
        lse_tile = jnp.zeros((QB, LANES), F32)
        for s in range(n_slab):
            sl = slice(s * LANES, (s + 1) * LANES)
            qx = q_ref[rho, pl.ds(2 * qb, 2), :, sl].reshape(QB, LANES).astype(F32)
            qn = head_norm(qx, gq).astype(BF16)
            vw = vall[rho, pl.ds(2 * qb, 4), :, sl].reshape(KW, LANES)
            outs = []
            for hh in range(2):
                kw = kall[hh, rho, pl.ds(2 * qb, 4), :, sl].reshape(KW, LANES)
                sc = lax.dot_general(qn, kw, (((1,), (1,)), ((), ())), preferred_element_type=F32)
                sc = sc + bvar[var, 2 * s + hh]
                m = jnp.max(sc, axis=-1, keepdims=True)
                e = jnp.exp2(sc - m)
                l = jnp.sum(e, axis=-1, keepdims=True)
                o_h = jnp.dot(e.astype(BF16), vw, preferred_element_type=F32)
                outs.append(o_h * (1.0 / l))
                lse_tile = jnp.where(lane == 2 * s + hh, (m + jnp.log2(l)) * LN2, lse_tile)
            o_slab = jnp.where(lo, outs[0], outs[1])
            if dil == 1:
                oacc[s, pl.ds(start, QB), :] = o_slab
            else:
                oacc[s, pl.ds(start, QB, stride=dil), :] = o_slab
        if dil == 1:
            lacc[pl.ds(start, QB), :] = lse_tile
        else:
            lacc[pl.ds(start, QB, stride=dil), :] = lse_tile
        return carry

    lax.fori_loop(0, dil * nqb, body, 0)

    for s in range(n_slab):
        o_ref[:, s * LANES:(s + 1) * LANES] = oacc[s].astype(BF16)
    lse_ref[...] = lacc[...]


def _attention_group(z, bias_g, gq, gk, first, last, gi, dil):
    nb = SEG // dil // ATT_BLOCK
    z5 = z.reshape(N_SEG, dil, nb, ATT_BLOCK, N_IN)
    cq = (COL_ATT + 3 * gi * ATT_W) // ATT_W
    ck, cv = cq + 1, cq + 2
    main = lambda cb: pl.BlockSpec((None, dil, nb, ATT_BLOCK, ATT_W), lambda c, f, l: (c, 0, 0, 0, cb))
    prev = lambda cb: pl.BlockSpec((None, dil, 1, ATT_BLOCK, ATT_W),
                                   lambda c, f, l: (c - 1 + f[c], 0, nb - 1, 0, cb))
    nxt = lambda cb: pl.BlockSpec((None, dil, 1, ATT_BLOCK, ATT_W),
                                  lambda c, f, l: (c + 1 - l[c], 0, 0, 0, cb))
    grid_spec = pltpu.PrefetchScalarGridSpec(
        num_scalar_prefetch=2,
        grid=(N_SEG,),
        in_specs=[
            main(cq), prev(ck), main(ck), nxt(ck), prev(cv), main(cv), nxt(cv),
            pl.BlockSpec((ATT_HEADS, QB, KW), lambda c, f, l: (0, 0, 0)),
            pl.BlockSpec((1, LANES), lambda c, f, l: (0, 0)),
            pl.BlockSpec((1, LANES), lambda c, f, l: (0, 0)),
        ],
        out_specs=[
            pl.BlockSpec((SEG, ATT_W), lambda c, f, l: (c, 0)),
            pl.BlockSpec((SEG, LANES), lambda c, f, l: (c, 0)),
        ],
        scratch_shapes=[
            pltpu.VMEM((2, dil, nb + 2, ATT_BLOCK, ATT_W), BF16),
            pltpu.VMEM((dil, nb + 2, ATT_BLOCK, ATT_W), BF16),
            pltpu.VMEM((ATT_W // LANES, SEG, LANES), F32),
            pltpu.VMEM((SEG, LANES), F32),
            pltpu.VMEM((4, ATT_HEADS, QB, KW), F32),
        ],
    )
    return pl.pallas_call(
        functools.partial(_attention_kernel, dil=dil),
        grid_spec=grid_spec,
        out_shape=[jax.ShapeDtypeStruct((T_ALL, ATT_W), BF16), jax.ShapeDtypeStruct((T_ALL, LANES), F32)],
        compiler_params=_cparams(("arbitrary",)),
        name=f"attention_d{dil}",
    )(first, last, z5, z5, z5, z5, z5, z5, z5, bias_g, gq, gk)


def _merge_kernel(yret_ref, o0_ref, o1_ref, o2_ref, l0_ref, l1_ref, l2_ref,
                  gret_a_ref, gret_b_ref, gatt_a_ref, gatt_b_ref, xp_ref, xs_ref,
                  wret_ref, watt_ref, wout_ref, expand_ref, nffn_ref, wr_ref, br_ref,
                  x1_ref, hpa_ref, hpb_ref, idx_ref, gate_ref, rank_ref, cnt_ref, carry_ref):
    i = pl.program_id(0)
    tm = MERGE_TM

    @pl.when(i == 0)
    def _():
        carry_ref[...] = jnp.zeros_like(carry_ref)

    l0, l1, l2 = l0_ref[...], l1_ref[...], l2_ref[...]
    lm = jnp.maximum(jnp.maximum(l0, l1), l2)
    e0, e1, e2 = jnp.exp(l0 - lm), jnp.exp(l1 - lm), jnp.exp(l2 - lm)
    inv = 1.0 / (e0 + e1 + e2)
    y_att = jnp.zeros((tm, ATT_W), F32)
    for e, o_ref in ((e0, o0_ref), (e1, o1_ref), (e2, o2_ref)):
        w = e * inv
        w_hi = w.astype(BF16)
        w_lo = (w - w_hi.astype(F32)).astype(BF16)
        wide = (jnp.dot(w_hi, expand_ref[...], preferred_element_type=F32)
                + jnp.dot(w_lo, expand_ref[...], preferred_element_type=F32))
        y_att = y_att + wide * o_ref[...].astype(F32)

    p_ret = jnp.dot(yret_ref[...], wret_ref[...], preferred_element_type=F32)
    p_att = jnp.dot(y_att.astype(BF16), watt_ref[...], preferred_element_type=F32)
    g_ret = jnp.concatenate([gret_a_ref[...], gret_b_ref[...]], axis=-1).astype(F32)
    g_att = jnp.concatenate([gatt_a_ref[...], gatt_b_ref[...]], axis=-1).astype(F32)
    merged = jax.nn.sigmoid(g_ret) * p_ret + jax.nn.sigmoid(g_att) * p_att
    x_in = jnp.where(i < T_PROMPT // tm, xp_ref[...], xs_ref[...])
    x1 = x_in + jnp.dot(merged.astype(BF16), wout_ref[...], preferred_element_type=F32)
    x1_ref[...] = x1

    ms = jnp.mean(x1 * x1, axis=-1, keepdims=True)
    h2 = x1 * lax.rsqrt(ms + EPS) * nffn_ref[...]
    hpa_ref[...], hpb_ref[...] = _pack_row_halves(h2)

    logits = jnp.dot(h2, wr_ref[...], preferred_element_type=F32, precision=lax.Precision.HIGHEST) + br_ref[...]
    lane = lax.broadcasted_iota(jnp.int32, (tm, LANES), 1)
    lane_f = lane.astype(F32)
    work = logits
    vals, idxs = [], []
    for _ in range(TOP_K):
        m = jnp.max(work, axis=-1, keepdims=True)
        ix = jnp.min(jnp.where(work == m, lane_f, float(LANES)), axis=-1, keepdims=True)
        vals.append(m)
        idxs.append(ix)
        work = jnp.where(lane_f == ix, -3e38, work)
    es = [jnp.exp(v - vals[0]) for v in vals]
    den = es[0] + es[1] + es[2] + es[3]
    onehot = jnp.zeros((tm, LANES), F32)
    for ix in idxs:
        onehot = onehot + jnp.where(lane_f == ix, 1.0, 0.0)
    row = lax.broadcasted_iota(jnp.int32, (tm, tm), 0)
    colm = lax.broadcasted_iota(jnp.int32, (tm, tm), 1)
    tri = jnp.where(colm < row, 1.0, 0.0).astype(BF16)
    before = jnp.dot(tri, onehot.astype(BF16), preferred_element_type=F32) + carry_ref[...]
    idx_out = jnp.zeros((tm, LANES), F32)
    gate_out = jnp.zeros((tm, LANES), F32)
    rank_out = jnp.zeros((tm, LANES), F32)
    for k in range(TOP_K):
        rk = jnp.sum(jnp.where(lane_f == idxs[k], before, 0.0), axis=-1, keepdims=True)
        sel = lane == k
        idx_out = jnp.where(sel, idxs[k], idx_out)
        gate_out = jnp.where(sel, es[k] / den, gate_out)
        rank_out = jnp.where(sel, rk, rank_out)
    idx_ref[...] = idx_out.astype(jnp.int32)
    gate_ref[...] = gate_out
    rank_ref[...] = rank_out.astype(jnp.int32)
    total = carry_ref[...] + jnp.sum(onehot, axis=0, keepdims=True)
    carry_ref[...] = total
    cnt_ref[...] = jnp.broadcast_to(total, cnt_ref.shape)


def _merge(y_ret, o_list, lse_list, z, x_p, x_s, w_ret, w_att, w_out, expand, n_ffn, w_router, b_router):
    tm = MERGE_TM
    n_p = T_PROMPT // tm
    row = lambda w: pl.BlockSpec((tm, w), lambda i: (i, 0))
    full = lambda a: pl.BlockSpec(a.shape, lambda i: (0,) * a.ndim)
    zcol = lambda col: pl.BlockSpec((tm, COL_BLK), lambda i: (i, col // COL_BLK))
    return pl.pallas_call(
        _merge_kernel,
        grid=(T_ALL // tm,),
        in_specs=[row(RET_V_W), row(ATT_W), row(ATT_W), row(ATT_W), row(LANES), row(LANES), row(LANES),
                  zcol(COL_GATE_RET), zcol(COL_GATE_RET + COL_BLK), zcol(COL_GATE_ATT),
                  zcol(COL_GATE_ATT + COL_BLK),
                  pl.BlockSpec((tm, D_MODEL), lambda i: (jnp.minimum(i, n_p - 1), 0)),
                  pl.BlockSpec((tm, D_MODEL), lambda i: (jnp.maximum(i - n_p, 0), 0)),
                  full(w_ret), full(w_att), full(w_out), full(expand), full(n_ffn), full(w_router), full(b_router)],
        out_specs=[row(D_MODEL), row(HALF_W), row(HALF_W), row(LANES), row(LANES), row(LANES),
                   pl.BlockSpec((8, LANES), lambda i: (0, 0))],
        out_shape=[jax.ShapeDtypeStruct((T_ALL, D_MODEL), F32),
                   jax.ShapeDtypeStruct((T_ALL, HALF_W), jnp.uint32),
                   jax.ShapeDtypeStruct((T_ALL, HALF_W), jnp.uint32),
                   jax.ShapeDtypeStruct((T_ALL, LANES), jnp.int32),
                   jax.ShapeDtypeStruct((T_ALL, LANES), F32),
                   jax.ShapeDtypeStruct((T_ALL, LANES), jnp.int32),
                   jax.ShapeDtypeStruct((8, LANES), F32)],
        scratch_shapes=[pltpu.VMEM((1, LANES), F32)],
        compiler_params=_cparams(("arbitrary",)),
        name="merge_router",
    )(y_ret, *o_list, *lse_list, z, z, z, z, x_p, x_s, w_ret, w_att, w_out, expand, n_ffn, w_router, b_router)


def _sc_mesh():
    return plsc.VectorSubcoreMesh(core_axis_name="core", subcore_axis_name="subcore")


def _sc_scatter_rows(x, idx_kmajor, n_out):
    n_rows, width = x.shape

    @pl.kernel(out_type=jax.ShapeDtypeStruct((n_out, width), x.dtype), mesh=_sc_mesh(), scratch_types=[])
    def scatter(x_hbm, i_hbm, o_hbm):
        def body(x_vmem, i_vmem):
            for k in range(TOP_K):
                pltpu.sync_copy(x_vmem, o_hbm.at[i_vmem.at[k]])

        pltpu.emit_pipeline(
            body,
            grid=(n_rows // SC_WINDOW,),
            in_specs=[pl.BlockSpec((SC_WINDOW, width), lambda i: (i, 0)),
                      pl.BlockSpec((TOP_K, SC_WINDOW), lambda i: (0, i))],
            out_specs=[],
            core_axis_name=("core", "subcore"),
            dimension_semantics=(pltpu.PARALLEL,),
        )(x_hbm, i_hbm)

    return scatter(x, idx_kmajor)


def _sc_gather_rows(data, idx):
    n_idx = idx.shape[0]
    width = data.shape[1]

    @pl.kernel(out_type=jax.ShapeDtypeStruct((n_idx, width), data.dtype), mesh=_sc_mesh(), scratch_types=[])
    def gather(x_hbm, i_hbm, o_hbm):
        def body(i_vmem, o_vmem):
            pltpu.sync_copy(x_hbm.at[i_vmem.at[0]], o_vmem)

        pltpu.emit_pipeline(
            body,
            grid=(n_idx // SC_WINDOW,),
            in_specs=[pl.BlockSpec((1, SC_WINDOW), lambda i: (0, i))],
            out_specs=[pl.BlockSpec((SC_WINDOW, width), lambda i: (i, 0))],
            core_axis_name=("core", "subcore"),
            dimension_semantics=(pltpu.PARALLEL,),
        )(i_hbm, o_hbm)

    return gather(data, idx.reshape(1, n_idx))


def _expert_kernel(be_ref, nused_ref, nvalid_ref, xa_ref, xb_ref, wg_ref, bg_ref, wu_ref, bu_ref, wd_ref, bd_ref,
                   ya_ref, yb_ref, wbf_ref):
    b = pl.program_id(0)
    active = b < nused_ref[0]
    new_expert = (b == 0) | (be_ref[b] != be_ref[jnp.maximum(b - 1, 0)])

    @pl.when(active & new_expert)
    def _():
        rows = 128
        for wi, w_ref in enumerate((wg_ref, wu_ref, wd_ref)):
            for r in range(0, D_MODEL, rows):
                wbf_ref[wi, r:r + rows, :] = w_ref[0, r:r + rows, :].astype(BF16)

    @pl.when(active)
    def _():
        valid = lax.broadcasted_iota(jnp.int32, (MOE_BM, HALF_W), 0) < nvalid_ref[b]
        zero = jnp.zeros((MOE_BM, HALF_W), jnp.uint32)
        x = _unpack_row_halves(jnp.where(valid, xa_ref[...], zero), jnp.where(valid, xb_ref[...], zero)).astype(BF16)
        g = jnp.dot(x, wbf_ref[0], preferred_element_type=F32) + bg_ref[0]
        u = jnp.dot(x, wbf_ref[1], preferred_element_type=F32) + bu_ref[0]
        g = jnp.minimum(g, SWIGLU_LIMIT)
        u = jnp.clip(u, -SWIGLU_LIMIT, SWIGLU_LIMIT)
        glu = g * jax.nn.sigmoid(SWIGLU_ALPHA * g)
        act = ((u + 1.0) * glu).astype(BF16)
        y = jnp.dot(act, wbf_ref[2], preferred_element_type=F32) + bd_ref[0]
        ya_ref[...], yb_ref[...] = _pack_row_halves(y)

    @pl.when(jnp.logical_not(active))
    def _():
        ya_ref[...] = jnp.zeros_like(ya_ref)
        yb_ref[...] = jnp.zeros_like(yb_ref)


def _experts(block_expert, n_used, n_valid, xs_a, xs_b, wg, bg, wu, bu, wd, bd):
    assert D_FF == D_MODEL
    blk = lambda b, be, nu, nv: (jnp.minimum(b, nu[0] - 1), 0)
    wsp = lambda: pl.BlockSpec((1, D_MODEL, D_FF), lambda b, be, nu, nv: (be[b], 0, 0))
    bsp = lambda: pl.BlockSpec((1, 1, D_FF), lambda b, be, nu, nv: (be[b], 0, 0))
    xsp = lambda: pl.BlockSpec((MOE_BM, HALF_W), blk)
    ysp = lambda: pl.BlockSpec((MOE_BM, HALF_W), lambda b, be, nu, nv: (b, 0))
    slot_arr = jax.ShapeDtypeStruct((N_SLOTS, HALF_W), jnp.uint32)
    grid_spec = pltpu.PrefetchScalarGridSpec(
        num_scalar_prefetch=3,
        grid=(N_SLOT_BLOCKS,),
        in_specs=[xsp(), xsp(), wsp(), bsp(), wsp(), bsp(), wsp(), bsp()],
        out_specs=[ysp(), ysp()],
        scratch_shapes=[pltpu.VMEM((3, D_MODEL, D_FF), BF16)],
    )
    return pl.pallas_call(
        _expert_kernel,
        grid_spec=grid_spec,
        out_shape=[slot_arr, slot_arr],
        compiler_params=_cparams(("arbitrary",)),
        name="experts",
    )(block_expert, n_used, n_valid, xs_a, xs_b, wg, bg, wu, bu, wd, bd)


def _final_kernel(x1_ref, yga_ref, ygb_ref, gate_ref, p_ref, nple_ref, wpg_ref, wpp_ref, out_ref):
    x2 = x1_ref[...]
    gates = gate_ref[...]
    for k in range(TOP_K):
        x2 = x2 + gates[:, k:k + 1] * _unpack_row_halves(yga_ref[k], ygb_ref[k])
    ms = jnp.mean(x2 * x2, axis=-1, keepdims=True)
    h3 = (x2 * lax.rsqrt(ms + EPS) * nple_ref[...]).astype(BF16)
    gate = jax.nn.sigmoid(jnp.dot(h3, wpg_ref[...], preferred_element_type=F32))
    proj = jnp.dot(p_ref[...].astype(BF16), wpp_ref[...], preferred_element_type=F32)
    out_ref[...] = x2 + gate * proj


def _final(x1, yg_a, yg_b, gates, p, n_ple, w_pg, w_pp, row0, n_rows):
    tm = FINAL_TM
    off = row0 // tm
    full = lambda a: pl.BlockSpec(a.shape, lambda i: (0,) * a.ndim)
    return pl.pallas_call(
        _final_kernel,
        grid=(n_rows // tm,),
        in_specs=[pl.BlockSpec((tm, D_MODEL), lambda i: (i + off, 0)),
                  pl.BlockSpec((TOP_K, tm, HALF_W), lambda i: (0, i + off, 0)),
                  pl.BlockSpec((TOP_K, tm, HALF_W), lambda i: (0, i + off, 0)),
                  pl.BlockSpec((tm, LANES), lambda i: (i + off, 0)),
                  pl.BlockSpec((tm, PLE_DIM), lambda i: (i, 0)),
                  full(n_ple), full(w_pg), full(w_pp)],
        out_specs=pl.BlockSpec((tm, D_MODEL), lambda i: (i, 0)),
        out_shape=jax.ShapeDtypeStruct((n_rows, D_MODEL), F32),
        compiler_params=_cparams(("arbitrary",)),
        name="final_ple",
    )(x1, yg_a, yg_b, gates, p, n_ple, w_pg, w_pp)


def _rope_tables():
    half = RET_DK // 2
    freq = ROPE_THETA ** (-jnp.arange(half, dtype=F32) / half)
    ang = jnp.arange(SAMPLE_SEQ, dtype=F32)[:, None] * freq[None, :]
    cos, sin = jnp.cos(ang), jnp.sin(ang)
    return jnp.concatenate([cos, cos], axis=-1), jnp.concatenate([-sin, sin], axis=-1)


def _retention_tables(decay_logit):
    lg = jax.nn.log_sigmoid(decay_logit.astype(F32))
    c = RET_CHUNK
    idx = jnp.arange(c, dtype=F32)
    diff = idx[:, None] - idx[None, :]
    lf = lg[0][:, None, None]
    lb = lg[1][:, None, None]
    mask = jnp.where(diff[None] >= 0, jnp.exp(lf * jnp.maximum(diff, 0.0)[None]),
                     jnp.exp(lb * jnp.maximum(-diff, 0.0)[None]))
    kdec_f = jnp.exp(lg[0][:, None] * (c - 1.0 - idx)[None, :])
    qdec_f = jnp.exp(lg[0][:, None] * (idx + 1.0)[None, :])
    kdec_b = jnp.exp(lg[1][:, None] * idx[None, :])
    qdec_b = jnp.exp(lg[1][:, None] * (c - idx)[None, :])
    dec = jnp.stack([kdec_f, qdec_f, kdec_b, qdec_b], axis=1)[..., None]
    cdec = jnp.exp(lg * c).T
    cdec = jnp.broadcast_to(cdec[:, :, None, None], (RET_HEADS, 2, 1, RET_DV))
    return mask, dec, cdec


def _t5_bucket(rel):
    half = T5_BUCKETS // 2
    exact = half // 2
    n = np.abs(rel)
    ratio = np.log(np.maximum(n, 1).astype(np.float32) / np.float32(exact)) / np.float32(math.log(T5_MAX_DIST / exact))
    large = exact + (ratio * np.float32(half - exact)).astype(np.int32)
    large = np.minimum(large, half - 1)
    return np.where(rel > 0, half, 0) + np.where(n < exact, n, large)


def _attention_bias(rel_bias, gi, dil, radius):
    qi = np.arange(QB)
    ki = np.arange(KW) - ATT_BLOCK
    rel = ki[None, :] - qi[:, None]
    onehot = jnp.asarray(_t5_bucket(rel * dil)[..., None] == np.arange(T5_BUCKETS), F32)
    tab = rel_bias[:, gi * ATT_HEADS:(gi + 1) * ATT_HEADS].astype(F32)
    bias = jnp.einsum('qkb,bh->hqk', onehot, tab, precision=lax.Precision.HIGHEST)
    return jnp.where(jnp.asarray(np.abs(rel) <= radius)[None], bias * LOG2E, NEG)


def _seq_edge_flags():
    first = np.zeros((N_SEG,), np.int32)
    last = np.zeros((N_SEG,), np.int32)
    first[:N_PROMPT_SEG] = 1
    last[:N_PROMPT_SEG] = 1
    first[N_PROMPT_SEG] = 1
    last[N_SEG - 1] = 1
    return jnp.asarray(first), jnp.asarray(last)


def _pad_lanes(a, value=0.0):
    return jnp.pad(a, ((0, 0), (0, LANES - a.shape[-1])), constant_values=value)


def kernel(x_prompt, x_sample, p_prompt, p_sample, norm_mix_g, w_in, ret_decay_logit, ret_gn_g,
           att_q_norm_g, att_k_norm_g, rel_bias, w_ret_proj, w_att_proj, w_out, norm_ffn_g,
           w_router, b_router, w_gate, b_gate, w_up, b_up, w_down, b_down,
           norm_ple_g, w_ple_gate, w_ple_proj):
    assert norm_mix_g.shape[0] == 1, "one layer"
    x_p = x_prompt.reshape(T_PROMPT, D_MODEL)
    x_s = x_sample.reshape(SAMPLE_SEQ, D_MODEL)

    cos_t, sin_t = _rope_tables()
    z = _in_proj(x_p, x_s, norm_mix_g.astype(F32), w_in[0].astype(BF16), cos_t, sin_t)

    ret_mask, ret_dec, ret_cdec = _retention_tables(ret_decay_logit[0])
    y_ret = _retention(z, ret_mask, ret_dec, ret_cdec, ret_gn_g[0].reshape(RET_HEADS, 1, RET_DV).astype(F32))

    first, last = _seq_edge_flags()
    o_list, lse_list = [], []
    for gi, (window, dil) in enumerate(ATT_GROUPS):
        bias_g = _attention_bias(rel_bias, gi, dil, window // (2 * dil))
        gq = jnp.tile(att_q_norm_g[0, gi].astype(F32) * (ATT_DH ** -0.5 * LOG2E), LANES // ATT_DH)[None, :]
        gk = jnp.tile(att_k_norm_g[0, gi].astype(F32), LANES // ATT_DH)[None, :]
        o_g, lse_g = _attention_group(z, bias_g, gq, gk, first, last, gi, dil)
        o_list.append(o_g)
        lse_list.append(lse_g)

    expand = (jnp.arange(LANES)[:, None] == (jnp.arange(ATT_W)[None, :] // ATT_DH)).astype(BF16)
    w_router_p = _pad_lanes(w_router[0].astype(F32))
    b_router_p = _pad_lanes(b_router.astype(F32), NEG)
    x1, hp_a, hp_b, idx, gates, rank, cnt = _merge(
        y_ret, o_list, lse_list, z, x_p, x_s, w_ret_proj[0].astype(BF16), w_att_proj[0].astype(BF16),
        w_out[0].astype(BF16), expand, norm_ffn_g.astype(F32), w_router_p, b_router_p)

    counts = cnt[0, :N_EXPERTS].astype(jnp.int32)
    padded = (counts + MOE_BM - 1) // MOE_BM * MOE_BM
    pad_end = jnp.cumsum(padded)
    pad_start = pad_end - padded
    expert_ids = jnp.arange(N_EXPERTS, dtype=jnp.int32)
    top_idx = idx[:, :TOP_K]
    start_of = jnp.sum(jnp.where(top_idx[:, :, None] == expert_ids, pad_start, 0), axis=-1)
    dest_kmajor = (start_of + rank[:, :TOP_K]).T
    n_used = (pad_end[-1] // MOE_BM).astype(jnp.int32).reshape(1)
    blk_row0 = jnp.arange(N_SLOT_BLOCKS, dtype=jnp.int32) * MOE_BM
    block_expert = jnp.minimum(jnp.sum((pad_end[None, :] <= blk_row0[:, None]).astype(jnp.int32), axis=1),
                               N_EXPERTS - 1).astype(jnp.int32)
    slot_end = jnp.sum(jnp.where(block_expert[:, None] == expert_ids, pad_start + counts, 0), axis=-1)
    n_valid = jnp.clip(slot_end - blk_row0, 0, MOE_BM).astype(jnp.int32)

    xs_a = _sc_scatter_rows(hp_a, dest_kmajor, N_SLOTS)
    xs_b = _sc_scatter_rows(hp_b, dest_kmajor, N_SLOTS)
    ys_a, ys_b = _experts(block_expert, n_used, n_valid, xs_a, xs_b,
                          w_gate[0], b_gate[0].reshape(N_EXPERTS, 1, D_FF).astype(F32),
                          w_up[0], b_up[0].reshape(N_EXPERTS, 1, D_FF).astype(F32),
                          w_down[0], b_down[0].reshape(N_EXPERTS, 1, D_MODEL).astype(F32))
    dest_flat = dest_kmajor.reshape(-1)
    yg_a = _sc_gather_rows(ys_a, dest_flat).reshape(TOP_K, T_ALL, HALF_W)
    yg_b = _sc_gather_rows(ys_b, dest_flat).reshape(TOP_K, T_ALL, HALF_W)

    n_ple = norm_ple_g.astype(F32)
    w_pg = w_ple_gate[0].astype(BF16)
    w_pp = w_ple_proj[0].astype(BF16)
    y_p = _final(x1, yg_a, yg_b, gates, p_prompt[0].reshape(T_PROMPT, PLE_DIM), n_ple, w_pg, w_pp, 0, T_PROMPT)
    y_s = _final(x1, yg_a, yg_b, gates, p_sample[0].reshape(SAMPLE_SEQ, PLE_DIM), n_ple, w_pg, w_pp, T_PROMPT, SAMPLE_SEQ)
    return (y_p.reshape(x_prompt.shape), y_s.reshape(x_sample.shape))
```

```python
import functools
import math

import jax
import jax.numpy as jnp
import numpy as np
from jax import lax
from jax.experimental import pallas as pl
from jax.experimental.pallas import tpu as pltpu
from jax.experimental.pallas import tpu_sc as plsc

F32 = jnp.float32
BF16 = jnp.bfloat16

D_MODEL = 1024
N_PROMPT_SEQ = 8
PROMPT_SEQ = 2048
SAMPLE_SEQ = 16384
T_PROMPT = N_PROMPT_SEQ * PROMPT_SEQ
T_ALL = T_PROMPT + SAMPLE_SEQ

RET_HEADS = 4
RET_DK = 128
RET_DV = 256
RET_CHUNK = 128
ROPE_THETA = 10000.0
ATT_GROUPS = ((128, 1), (512, 4), (2048, 16))
N_GROUPS = 3
ATT_HEADS = 8
ATT_DH = 64
ATT_BLOCK = 64
ATT_W = ATT_HEADS * ATT_DH
T5_BUCKETS = 32
T5_MAX_DIST = 1024
N_EXPERTS = 32
TOP_K = 4
D_FF = 1024
SWIGLU_ALPHA = 1.702
SWIGLU_LIMIT = 7.0
PLE_DIM = 256
EPS = 1e-6

RET_QK_W = RET_HEADS * RET_DK
RET_V_W = RET_HEADS * RET_DV
N_IN = 2 * RET_QK_W + 2 * RET_V_W + 3 * N_GROUPS * ATT_W + 2 * D_MODEL

COL_RQ = 0
COL_RK = RET_QK_W
COL_RV = 2 * RET_QK_W
COL_RG = COL_RV + RET_V_W
COL_ATT = COL_RG + RET_V_W
COL_GATE_RET = COL_ATT + 3 * N_GROUPS * ATT_W
COL_GATE_ATT = COL_GATE_RET + D_MODEL

LANES = 128
VMEM_LIMIT = 56 * 1024 * 1024

SEG = 2048
N_SEG = T_ALL // SEG
N_PROMPT_SEG = T_PROMPT // SEG
COL_BLK = 512
N_COL_BLK = N_IN // COL_BLK
QB = 128
KW = 256
NEG = -1e30
LOG2E = math.log2(math.e)
LN2 = math.log(2.0)
MERGE_TM = 512
FINAL_TM = 512
MOE_BM = 512
N_SLOT_BLOCKS = T_ALL * TOP_K // MOE_BM + N_EXPERTS
N_SLOTS = N_SLOT_BLOCKS * MOE_BM
HALF_W = D_MODEL // 4
SC_WINDOW = 128


def _cparams(sem, vmem=VMEM_LIMIT):
    return pltpu.CompilerParams(dimension_semantics=sem, vmem_limit_bytes=vmem)


def _pack_bf16_pair(x):
    w = x.shape[-1] // 2
    hi = pltpu.bitcast(x[:, :w].astype(BF16).astype(F32), jnp.uint32)
    lo = pltpu.bitcast(x[:, w:].astype(BF16).astype(F32), jnp.uint32)
    return hi | (lo >> 16)


def _unpack_bf16_pair(p):
    hi = pltpu.bitcast(p & jnp.uint32(0xFFFF0000), F32)
    lo = pltpu.bitcast(p << 16, F32)
    return jnp.concatenate([hi, lo], axis=-1)


def _pack_row_halves(x):
    half = x.shape[-1] // 2
    return _pack_bf16_pair(x[:, :half]), _pack_bf16_pair(x[:, half:])


def _unpack_row_halves(pa, pb):
    return jnp.concatenate([_unpack_bf16_pair(pa), _unpack_bf16_pair(pb)], axis=-1)


def _in_proj_kernel(xp_ref, xs_ref, g_ref, w_ref, cos_ref, sin_ref, z_ref, h_ref, p_ref):
    i = pl.program_id(0)
    j = pl.program_id(1)

    def norm_into_h(x_ref):
        xf = x_ref[...]
        ms = jnp.mean(xf * xf, axis=-1, keepdims=True)
        h_ref[...] = (xf * lax.rsqrt(ms + EPS) * g_ref[...]).astype(BF16)

    @pl.when((j == 0) & (i < N_PROMPT_SEG))
    def _():
        norm_into_h(xp_ref)

    @pl.when((j == 0) & (i >= N_PROMPT_SEG))
    def _():
        norm_into_h(xs_ref)

    n_slab = COL_BLK // LANES

    def project():
        return jnp.dot(h_ref[...], w_ref[...], preferred_element_type=F32)

    is_rope = j < (COL_RV // COL_BLK)
    att0 = COL_ATT // COL_BLK
    is_d4 = (j >= att0 + 3) & (j < att0 + 6)
    is_d16 = (j >= att0 + 6) & (j < att0 + 9)

    @pl.when(is_rope)
    def _():
        acc = project()
        scale = jnp.where(j == COL_RK // COL_BLK, RET_DK ** -0.5, 1.0).astype(F32)
        c = cos_ref[...]
        sn = sin_ref[...]
        for s in range(n_slab):
            xs = acc[:, s * LANES:(s + 1) * LANES]
            r = xs * c + pltpu.roll(xs, RET_DK // 2, axis=1) * sn
            z_ref[:, s * LANES:(s + 1) * LANES] = (r * scale).astype(BF16)

    for dil, pred in ((4, is_d4), (16, is_d16)):
        @pl.when(pred)
        def _(dil=dil):
            acc = project()
            for s in range(n_slab):
                p_ref[s] = acc[:, s * LANES:(s + 1) * LANES]
            rows = SEG // dil
            for rho in range(dil):
                for s in range(n_slab):
                    piece = p_ref[s, pl.ds(rho, rows, stride=dil), :]
                    z_ref[rho * rows:(rho + 1) * rows, s * LANES:(s + 1) * LANES] = piece.astype(BF16)

    @pl.when(jnp.logical_not(is_rope | is_d4 | is_d16))
    def _():
        z_ref[...] = project().astype(BF16)


def _in_proj(x_p, x_s, norm_g, w_in_bf, cos_t, sin_t):
    def pos_blk(i, j):
        return (jnp.maximum(i - N_PROMPT_SEG, 0), 0)

    return pl.pallas_call(
        _in_proj_kernel,
        grid=(N_SEG, N_COL_BLK),
        in_specs=[
            pl.BlockSpec((SEG, D_MODEL), lambda i, j: (jnp.minimum(i, N_PROMPT_SEG - 1), 0),
                         pipeline_mode=pl.Buffered(1)),
            pl.BlockSpec((SEG, D_MODEL), pos_blk, pipeline_mode=pl.Buffered(1)),
            pl.BlockSpec((1, D_MODEL), lambda i, j: (0, 0)),
            pl.BlockSpec((D_MODEL, COL_BLK), lambda i, j: (0, j)),
            pl.BlockSpec((SEG, LANES), pos_blk),
            pl.BlockSpec((SEG, LANES), pos_blk),
        ],
        out_specs=pl.BlockSpec((SEG, COL_BLK), lambda i, j: (i, j)),
        out_shape=jax.ShapeDtypeStruct((T_ALL, N_IN), BF16),
        scratch_shapes=[
            pltpu.VMEM((SEG, D_MODEL), BF16),
            pltpu.VMEM((COL_BLK // LANES, SEG, LANES), F32),
        ],
        compiler_params=_cparams(("arbitrary", "arbitrary")),
        name="in_proj",
    )(x_p, x_s, norm_g, w_in_bf, cos_t, sin_t)


RET_CHUNKS_PER_SEG = SEG // RET_CHUNK
RET_MAX_CHUNKS = SAMPLE_SEQ // RET_CHUNK
RET_GROUP = 4


def _retention_kernel(seg_ref, phase_ref, reset_ref, cbase_ref,
                      q_ref, k_ref, v_ref, g_ref, mask_ref, dec_ref, cdec_ref, gn_ref,
                      y_ref, sb_ref, sf_ref, sr_ref):
    step = pl.program_id(1)
    phase = phase_ref[step]
    reset = reset_ref[step]
    cbase = cbase_ref[step]
    kdec_f = dec_ref[0, 0]
    qdec_f = dec_ref[0, 1]
    kdec_b = dec_ref[0, 2]
    qdec_b = dec_ref[0, 3]
    cd_f = cdec_ref[0, 0]
    cd_b = cdec_ref[0, 1]

    def kv_outer(kd, v):
        return lax.dot_general(kd, v, (((0,), (0,)), ((), ())), preferred_element_type=F32)

    @pl.when((phase == 0) & (reset == 1))
    def _():
        sr_ref[...] = jnp.zeros_like(sr_ref)

    @pl.when((phase == 1) & (reset == 1))
    def _():
        sf_ref[...] = jnp.zeros_like(sf_ref)

    n_groups = RET_CHUNKS_PER_SEG // RET_GROUP

    def chunk_rows(c):
        return pl.ds(pl.multiple_of(c * RET_CHUNK, RET_CHUNK), RET_CHUNK)

    @pl.when(phase == 0)
    def _():
        def body(it, carry):
            top = RET_CHUNKS_PER_SEG - 1 - it * RET_GROUP
            kvs = []
            for j in range(RET_GROUP):
                rows = chunk_rows(top - j)
                kd = (k_ref[rows, :].astype(F32) * kdec_b).astype(BF16)
                kvs.append(kv_outer(kd, v_ref[rows, :]))
            state = sr_ref[...]
            for j in range(RET_GROUP):
                sb_ref[cbase + top - j] = state.astype(BF16)
                state = cd_b * state + kvs[j]
            sr_ref[...] = state
            return carry

        lax.fori_loop(0, n_groups, body, 0)

    @pl.when(phase == 1)
    def _():
        msk = mask_ref[0]
        gn = gn_ref[0]

        def body(it, carry):
            c0 = it * RET_GROUP
            lhs, vs, kvs = [], [], []
            for j in range(RET_GROUP):
                rows = chunk_rows(c0 + j)
                qb = q_ref[rows, :]
                kb = k_ref[rows, :]
                v = v_ref[rows, :]
                q = qb.astype(F32)
                s = lax.dot_general(qb, kb, (((1,), (1,)), ((), ())), preferred_element_type=F32)
                lhs.append(jnp.concatenate(
                    [(s * msk).astype(BF16), (q * qdec_f).astype(BF16), (q * qdec_b).astype(BF16)], axis=-1))
                vs.append(v)
                kvs.append(kv_outer((kb.astype(F32) * kdec_f).astype(BF16), v))
            state = sf_ref[...]
            for j in range(RET_GROUP):
                c = c0 + j
                rhs = jnp.concatenate([vs[j], state.astype(BF16), sb_ref[cbase + c]], axis=0)
                o = jnp.dot(lhs[j], rhs, preferred_element_type=F32)
                state = cd_f * state + kvs[j]
                mu = jnp.mean(o, axis=-1, keepdims=True)
                oc = o - mu
                var = jnp.mean(oc * oc, axis=-1, keepdims=True)
                on = oc * lax.rsqrt(var + EPS) * gn
                rows = chunk_rows(c)
                gate = g_ref[rows, :].astype(F32)
                y_ref[rows, :] = (gate * jax.nn.sigmoid(gate) * on).astype(BF16)
            sf_ref[...] = state
            return carry

        lax.fori_loop(0, n_groups, body, 0)


def _retention_schedule():
    seg, phase, reset, cbase = [], [], [], []
    for p in range(N_PROMPT_SEG):
        for ph in (0, 1):
            seg.append(p); phase.append(ph); reset.append(1); cbase.append(0)
    n_s = N_SEG - N_PROMPT_SEG
    for i in range(n_s):
        t = n_s - 1 - i
        seg.append(N_PROMPT_SEG + t); phase.append(0); reset.append(int(i == 0)); cbase.append(t * RET_CHUNKS_PER_SEG)
    for t in range(n_s):
        seg.append(N_PROMPT_SEG + t); phase.append(1); reset.append(int(t == 0)); cbase.append(t * RET_CHUNKS_PER_SEG)
    hold = list(seg)
    for i in range(len(seg)):
        if phase[i] == 0:
            nxt = next(j for j in range(i + 1, len(seg)) if phase[j] == 1)
            hold[i] = seg[nxt]
    arr = lambda a: jnp.asarray(np.asarray(a, np.int32))
    return arr(seg), arr(phase), arr(reset), arr(cbase), arr(hold)


def _retention(z, ret_mask, ret_dec, ret_cdec, gn_g):
    seg, phase, reset, cbase, hold = _retention_schedule()
    n_steps = int(seg.shape[0])
    qk_blk = lambda col: (lambda h, s, seg_r, ph_r, rs_r, cb_r, hold_r: (seg_r[s], col // RET_DK + h))
    hold_blk = lambda col, w: (lambda h, s, seg_r, ph_r, rs_r, cb_r, hold_r: (hold_r[s], col // w + h))
    v_blk = lambda h, s, seg_r, ph_r, rs_r, cb_r, hold_r: (seg_r[s], COL_RV // RET_DV + h)
    per_head = lambda h, s, *_: (h, 0, 0)
    per_head4 = lambda h, s, *_: (h, 0, 0, 0)
    grid_spec = pltpu.PrefetchScalarGridSpec(
        num_scalar_prefetch=5,
        grid=(RET_HEADS, n_steps),
        in_specs=[
            pl.BlockSpec((SEG, RET_DK), hold_blk(COL_RQ, RET_DK)),
            pl.BlockSpec((SEG, RET_DK), qk_blk(COL_RK)),
            pl.BlockSpec((SEG, RET_DV), v_blk),
            pl.BlockSpec((SEG, RET_DV), hold_blk(COL_RG, RET_DV)),
            pl.BlockSpec((1, RET_CHUNK, RET_CHUNK), per_head),
            pl.BlockSpec((1, 4, RET_CHUNK, 1), per_head4),
            pl.BlockSpec((1, 2, 1, RET_DV), per_head4),
            pl.BlockSpec((1, 1, RET_DV), per_head),
        ],
        out_specs=pl.BlockSpec((SEG, RET_DV), lambda h, s, seg_r, ph_r, rs_r, cb_r, hold_r: (hold_r[s], h)),
        scratch_shapes=[
            pltpu.VMEM((RET_MAX_CHUNKS, RET_DK, RET_DV), BF16),
            pltpu.VMEM((RET_DK, RET_DV), F32),
            pltpu.VMEM((RET_DK, RET_DV), F32),
        ],
    )

    def kernel(seg_r, ph_r, rs_r, cb_r, hold_r, *refs):
        _retention_kernel(seg_r, ph_r, rs_r, cb_r, *refs)

    return pl.pallas_call(
        kernel,
        grid_spec=grid_spec,
        out_shape=jax.ShapeDtypeStruct((T_ALL, RET_V_W), BF16),
        compiler_params=_cparams(("arbitrary", "arbitrary")),
        name="retention",
    )(seg, phase, reset, cbase, hold, z, z, z, z, ret_mask, ret_dec, ret_cdec, gn_g)


def _attention_kernel(first_ref, last_ref, q_ref, kp_ref, km_ref, kn_ref, vp_ref, vm_ref, vn_ref,
                      bias_ref, gq_ref, gk_ref, o_ref, lse_ref, kall, vall, oacc, bvar, *, dil):
    nb = SEG // dil // ATT_BLOCK
    nqb = nb // 2
    n_slab = ATT_W // LANES
    c = pl.program_id(0)
    is_first = first_ref[c]
    is_last = last_ref[c]
    lane = lax.broadcasted_iota(jnp.int32, (1, LANES), 1)
    lo = lane < ATT_DH
    gq = gq_ref[...]
    gk = gk_ref[...]

    @pl.when(c == 0)
    def _():
        col = lax.broadcasted_iota(jnp.int32, (1, KW), 1)
        left = jnp.where(col < ATT_BLOCK, NEG, 0.0).astype(F32)
        right = jnp.where(col >= KW - ATT_BLOCK, NEG, 0.0).astype(F32)
        for h in range(ATT_HEADS):
            b = bias_ref[h]
            bvar[0, h] = b
            bvar[1, h] = b + left
            bvar[2, h] = b + right
            bvar[3, h] = b + left + right

    def head_norm(x, g):
        x2 = x * x
        s_lo = jnp.sum(jnp.where(lo, x2, 0.0), axis=-1, keepdims=True)
        s_hi = jnp.sum(jnp.where(lo, 0.0, x2), axis=-1, keepdims=True)
        ms = jnp.where(lo, s_lo, s_hi) * (1.0 / ATT_DH)
        return x * lax.rsqrt(ms + EPS) * g

    def norm_block(src):
        even, odd = [], []
        for s in range(n_slab):
            xn = head_norm(src[:, s * LANES:(s + 1) * LANES].astype(F32), gk)
            even.append(jnp.where(lo, xn, 0.0).astype(BF16))
            odd.append(jnp.where(lo, 0.0, xn).astype(BF16))
        return jnp.concatenate(even, axis=-1), jnp.concatenate(odd, axis=-1)

    lo_wide = lax.broadcasted_iota(jnp.int32, (1, ATT_W), 1) % LANES < ATT_DH

    def split_heads(v):
        zero = jnp.zeros_like(v)
        return jnp.where(lo_wide, v, zero), jnp.where(lo_wide, zero, v)

    def fill_main(it, carry):
        rho = it // nb
        blk = it % nb
        kall[0, rho, blk + 1], kall[1, rho, blk + 1] = norm_block(km_ref[rho, blk])
        vall[0, rho, blk + 1], vall[1, rho, blk + 1] = split_heads(vm_ref[rho, blk])
        return carry

    lax.fori_loop(0, dil * nb, fill_main, 0, unroll=4)

    def fill_halo(rho, carry):
        kall[0, rho, 0], kall[1, rho, 0] = norm_block(kp_ref[rho, 0])
        kall[0, rho, nb + 1], kall[1, rho, nb + 1] = norm_block(kn_ref[rho, 0])
        vall[0, rho, 0], vall[1, rho, 0] = split_heads(vp_ref[rho, 0])
        vall[0, rho, nb + 1], vall[1, rho, nb + 1] = split_heads(vn_ref[rho, 0])
        return carry

    lax.fori_loop(0, dil, fill_halo, 0)

    ones_even = jnp.broadcast_to(jnp.where(lo, 1.0, 0.0).astype(BF16), (KW, LANES))
    ones_odd = jnp.broadcast_to(jnp.where(lo, 0.0, 1.0).astype(BF16), (KW, LANES))

    def body(it, carry):
        rho = it // nqb
        qb = it % nqb
        var = (jnp.where((qb == 0) & (is_first == 1), 1, 0)
               + jnp.where((qb == nqb - 1) & (is_last == 1), 2, 0))
        start = rho + qb * (QB * dil)
        rows = pl.ds(start, QB) if dil == 1 else pl.ds(start, QB, stride=dil)
        for s in range(n_slab):
            sl = slice(s * LANES, (s + 1) * LANES)
            qx = q_ref[rho, pl.ds(2 * qb, 2), :, sl].reshape(QB, LANES).astype(F32)
            qn = head_norm(qx, gq).astype(BF16)
            es, ms = [], []
            for hh in range(2):
                kw = kall[hh, rho, pl.ds(2 * qb, 4), :, sl].reshape(KW, LANES)
                sc = lax.dot_general(qn, kw, (((1,), (1,)), ((), ())), preferred_element_type=F32)
                sc = sc + bvar[var, 2 * s + hh]
                m = jnp.max(sc, axis=-1, keepdims=True)
                es.append(jnp.exp2(sc - m).astype(BF16))
                ms.append(m)
            v_even = vall[0, rho, pl.ds(2 * qb, 4), :, sl].reshape(KW, LANES)
            v_odd = vall[1, rho, pl.ds(2 * qb, 4), :, sl].reshape(KW, LANES)
            rhs = jnp.concatenate([jnp.concatenate([v_even, ones_even], axis=1),
                                   jnp.concatenate([v_odd, ones_odd], axis=1)], axis=0)
            res = jnp.dot(jnp.concatenate(es, axis=1), rhs, preferred_element_type=F32)
            den = res[:, LANES:]
            oacc[s, rows, :] = res[:, :LANES] * (1.0 / den)
            lse_ref[s, rows, :] = (jnp.where(lo, ms[0], ms[1]) + jnp.log2(den)) * LN2
        return carry

    lax.fori_loop(0, dil * nqb, body, 0)

    for s in range(n_slab):
        o_ref[:, s * LANES:(s + 1) * LANES] = oacc[s].astype(BF16)


def _attention_group(z, bias_g, gq, gk, first, last, gi, dil):
    nb = SEG // dil // ATT_BLOCK
    z5 = z.reshape(N_SEG, dil, nb, ATT_BLOCK, N_IN)
    cq = (COL_ATT + 3 * gi * ATT_W) // ATT_W
    ck, cv = cq + 1, cq + 2
    main = lambda cb: pl.BlockSpec((None, dil, nb, ATT_BLOCK, ATT_W), lambda c, f, l: (c, 0, 0, 0, cb))
    prev = lambda cb: pl.BlockSpec((None, dil, 1, ATT_BLOCK, ATT_W),
                                   lambda c, f, l: (c - 1 + f[c], 0, nb - 1, 0, cb), pipeline_mode=pl.Buffered(1))
    nxt = lambda cb: pl.BlockSpec((None, dil, 1, ATT_BLOCK, ATT_W),
                                  lambda c, f, l: (c + 1 - l[c], 0, 0, 0, cb), pipeline_mode=pl.Buffered(1))
    grid_spec = pltpu.PrefetchScalarGridSpec(
        num_scalar_prefetch=2,
        grid=(N_SEG,),
        in_specs=[
            main(cq), prev(ck), main(ck), nxt(ck), prev(cv), main(cv), nxt(cv),
            pl.BlockSpec((ATT_HEADS, QB, KW), lambda c, f, l: (0, 0, 0)),
            pl.BlockSpec((1, LANES), lambda c, f, l: (0, 0)),
            pl.BlockSpec((1, LANES), lambda c, f, l: (0, 0)),
        ],
        out_specs=[
            pl.BlockSpec((SEG, ATT_W), lambda c, f, l: (c, 0)),
            pl.BlockSpec((ATT_W // LANES, SEG, LANES), lambda c, f, l: (0, c, 0)),
        ],
        scratch_shapes=[
            pltpu.VMEM((2, dil, nb + 2, ATT_BLOCK, ATT_W), BF16),
            pltpu.VMEM((2, dil, nb + 2, ATT_BLOCK, ATT_W), BF16),
            pltpu.VMEM((ATT_W // LANES, SEG, LANES), F32),
            pltpu.VMEM((4, ATT_HEADS, QB, KW), F32),
        ],
    )
    return pl.pallas_call(
        functools.partial(_attention_kernel, dil=dil),
        grid_spec=grid_spec,
        out_shape=[jax.ShapeDtypeStruct((T_ALL, ATT_W), BF16), jax.ShapeDtypeStruct((ATT_W // LANES, T_ALL, LANES), F32)],
        compiler_params=_cparams(("arbitrary",)),
        name=f"attention_d{dil}",
    )(first, last, z5, z5, z5, z5, z5, z5, z5, bias_g, gq, gk)


def _merge_kernel(yret_ref, o0_ref, o1_ref, o2_ref, l0_ref, l1_ref, l2_ref,
                  gret_a_ref, gret_b_ref, gatt_a_ref, gatt_b_ref, xp_ref, xs_ref,
                  wret_ref, watt_ref, wout_ref, nffn_ref, wr_ref, wrhi_ref, br_ref,
                  x1_ref, hpa_ref, hpb_ref, idx_ref, gate_ref, rank_ref, cnt_ref, carry_ref):
    i = pl.program_id(0)
    tm = MERGE_TM

    @pl.when(i == 0)
    def _():
        carry_ref[...] = jnp.zeros_like(carry_ref)

    l0, l1, l2 = [jnp.concatenate([r[s] for s in range(ATT_W // LANES)], axis=-1) for r in (l0_ref, l1_ref, l2_ref)]
    lm = jnp.maximum(jnp.maximum(l0, l1), l2)
    e0, e1, e2 = jnp.exp(l0 - lm), jnp.exp(l1 - lm), jnp.exp(l2 - lm)
    inv = 1.0 / (e0 + e1 + e2)
    y_att = ((e0 * inv) * o0_ref[...].astype(F32) + (e1 * inv) * o1_ref[...].astype(F32)
             + (e2 * inv) * o2_ref[...].astype(F32))

    p_ret = jnp.dot(yret_ref[...], wret_ref[...], preferred_element_type=F32)
    p_att = jnp.dot(y_att.astype(BF16), watt_ref[...], preferred_element_type=F32)
    g_ret = jnp.concatenate([gret_a_ref[...], gret_b_ref[...]], axis=-1).astype(F32)
    g_att = jnp.concatenate([gatt_a_ref[...], gatt_b_ref[...]], axis=-1).astype(F32)
    merged = jax.nn.sigmoid(g_ret) * p_ret + jax.nn.sigmoid(g_att) * p_att
    x_in = jnp.where(i < T_PROMPT // tm, xp_ref[...], xs_ref[...])
    x1 = x_in + jnp.dot(merged.astype(BF16), wout_ref[...], preferred_element_type=F32)
    x1_ref[...] = x1

    ms = jnp.mean(x1 * x1, axis=-1, keepdims=True)
    h2 = x1 * lax.rsqrt(ms + EPS) * nffn_ref[...]
    hpa_ref[...], hpb_ref[...] = _pack_row_halves(h2)

    h_hi = h2.astype(BF16)
    h_lo = (h2 - h_hi.astype(F32)).astype(BF16)
    p1 = jnp.dot(h_hi, wr_ref[...], preferred_element_type=F32)
    p2 = jnp.dot(h_lo, wrhi_ref[...], preferred_element_type=F32)
    logits = p1 + pltpu.roll(p1, LANES - N_EXPERTS, axis=1) + p2 + br_ref[...]
    lane = lax.broadcasted_iota(jnp.int32, (tm, LANES), 1)
    lane_f = lane.astype(F32)
    work = logits
    vals, idxs = [], []
    for _ in range(TOP_K):
        m = jnp.max(work, axis=-1, keepdims=True)
        ix = jnp.min(jnp.where(work == m, lane_f, float(LANES)), axis=-1, keepdims=True)
        vals.append(m)
        idxs.append(ix)
        work = jnp.where(lane_f == ix, -3e38, work)
    es = [jnp.exp(v - vals[0]) for v in vals]
    den = es[0] + es[1] + es[2] + es[3]
    onehot = jnp.zeros((tm, LANES), F32)
    for ix in idxs:
        onehot = onehot + jnp.where(lane_f == ix, 1.0, 0.0)
    row = lax.broadcasted_iota(jnp.int32, (tm, tm), 0)
    colm = lax.broadcasted_iota(jnp.int32, (tm, tm), 1)
    tri = jnp.where(colm < row, 1.0, 0.0).astype(BF16)
    before = jnp.dot(tri, onehot.astype(BF16), preferred_element_type=F32) + carry_ref[...]
    idx_out = jnp.zeros((tm, LANES), F32)
    gate_out = jnp.zeros((tm, LANES), F32)
    rank_out = jnp.zeros((tm, LANES), F32)
    for k in range(TOP_K):
        rk = jnp.sum(jnp.where(lane_f == idxs[k], before, 0.0), axis=-1, keepdims=True)
        sel = lane == k
        idx_out = jnp.where(sel, idxs[k], idx_out)
        gate_out = jnp.where(sel, es[k] / den, gate_out)
        rank_out = jnp.where(sel, rk, rank_out)
    idx_ref[...] = idx_out.astype(jnp.int32)
    gate_ref[...] = gate_out
    rank_ref[...] = rank_out.astype(jnp.int32)
    total = carry_ref[...] + jnp.sum(onehot, axis=0, keepdims=True)
    carry_ref[...] = total
    cnt_ref[...] = jnp.broadcast_to(total, cnt_ref.shape)


def _merge(y_ret, o_list, lse_list, z, x_p, x_s, w_ret, w_att, w_out, n_ffn, w_router, w_router_hi, b_router):
    tm = MERGE_TM
    n_p = T_PROMPT // tm
    row = lambda w: pl.BlockSpec((tm, w), lambda i: (i, 0))
    full = lambda a: pl.BlockSpec(a.shape, lambda i: (0,) * a.ndim)
    zcol = lambda col: pl.BlockSpec((tm, COL_BLK), lambda i: (i, col // COL_BLK))
    lse_spec = pl.BlockSpec((ATT_W // LANES, tm, LANES), lambda i: (0, i, 0))
    return pl.pallas_call(
        _merge_kernel,
        grid=(T_ALL // tm,),
        in_specs=[row(RET_V_W), row(ATT_W), row(ATT_W), row(ATT_W), lse_spec, lse_spec, lse_spec,
                  zcol(COL_GATE_RET), zcol(COL_GATE_RET + COL_BLK), zcol(COL_GATE_ATT),
                  zcol(COL_GATE_ATT + COL_BLK),
                  pl.BlockSpec((tm, D_MODEL), lambda i: (jnp.minimum(i, n_p - 1), 0)),
                  pl.BlockSpec((tm, D_MODEL), lambda i: (jnp.maximum(i - n_p, 0), 0)),
                  full(w_ret), full(w_att), full(w_out), full(n_ffn), full(w_router), full(w_router_hi),
                  full(b_router)],
        out_specs=[row(D_MODEL), row(HALF_W), row(HALF_W), row(LANES), row(LANES), row(LANES),
                   pl.BlockSpec((8, LANES), lambda i: (0, 0))],
        out_shape=[jax.ShapeDtypeStruct((T_ALL, D_MODEL), F32),
                   jax.ShapeDtypeStruct((T_ALL, HALF_W), jnp.uint32),
                   jax.ShapeDtypeStruct((T_ALL, HALF_W), jnp.uint32),
                   jax.ShapeDtypeStruct((T_ALL, LANES), jnp.int32),
                   jax.ShapeDtypeStruct((T_ALL, LANES), F32),
                   jax.ShapeDtypeStruct((T_ALL, LANES), jnp.int32),
                   jax.ShapeDtypeStruct((8, LANES), F32)],
        scratch_shapes=[pltpu.VMEM((1, LANES), F32)],
        compiler_params=_cparams(("arbitrary",)),
        name="merge_router",
    )(y_ret, *o_list, *lse_list, z, z, z, z, x_p, x_s, w_ret, w_att, w_out, n_ffn, w_router, w_router_hi, b_router)


def _sc_mesh():
    return plsc.VectorSubcoreMesh(core_axis_name="core", subcore_axis_name="subcore")


def _sc_scatter_rows(x, idx_kmajor, n_out):
    n_rows, width = x.shape

    @pl.kernel(out_type=jax.ShapeDtypeStruct((n_out, width), x.dtype), mesh=_sc_mesh(), scratch_types=[])
    def scatter(x_hbm, i_hbm, o_hbm):
        def body(x_vmem, i_vmem):
            for k in range(TOP_K):
                pltpu.sync_copy(x_vmem, o_hbm.at[i_vmem.at[k]])

        pltpu.emit_pipeline(
            body,
            grid=(n_rows // SC_WINDOW,),
            in_specs=[pl.BlockSpec((SC_WINDOW, width), lambda i: (i, 0)),
                      pl.BlockSpec((TOP_K, SC_WINDOW), lambda i: (0, i))],
            out_specs=[],
            core_axis_name=("core", "subcore"),
            dimension_semantics=(pltpu.PARALLEL,),
        )(x_hbm, i_hbm)

    return scatter(x, idx_kmajor)


def _sc_gather_rows(data, idx):
    n_idx = idx.shape[0]
    width = data.shape[1]

    @pl.kernel(out_type=jax.ShapeDtypeStruct((n_idx, width), data.dtype), mesh=_sc_mesh(), scratch_types=[])
    def gather(x_hbm, i_hbm, o_hbm):
        def body(i_vmem, o_vmem):
            pltpu.sync_copy(x_hbm.at[i_vmem.at[0]], o_vmem)

        pltpu.emit_pipeline(
            body,
            grid=(n_idx // SC_WINDOW,),
            in_specs=[pl.BlockSpec((1, SC_WINDOW), lambda i: (0, i))],
            out_specs=[pl.BlockSpec((SC_WINDOW, width), lambda i: (i, 0))],
            core_axis_name=("core", "subcore"),
            dimension_semantics=(pltpu.PARALLEL,),
        )(i_hbm, o_hbm)

    return gather(data, idx.reshape(1, n_idx))


def _expert_kernel(be_ref, nused_ref, nvalid_ref, xa_ref, xb_ref, wg_ref, bg_ref, wu_ref, bu_ref, wd_ref, bd_ref,
                   ya_ref, yb_ref, wbf_ref):
    b = pl.program_id(0)
    active = b < nused_ref[0]
    new_expert = (b == 0) | (be_ref[b] != be_ref[jnp.maximum(b - 1, 0)])

    @pl.when(active & new_expert)
    def _():
        rows = 128
        for wi, w_ref in enumerate((wg_ref, wu_ref, wd_ref)):
            for r in range(0, D_MODEL, rows):
                wbf_ref[wi, r:r + rows, :] = w_ref[0, r:r + rows, :].astype(BF16)

    @pl.when(active)
    def _():
        valid = lax.broadcasted_iota(jnp.int32, (MOE_BM, HALF_W), 0) < nvalid_ref[b]
        zero = jnp.zeros((MOE_BM, HALF_W), jnp.uint32)
        x = _unpack_row_halves(jnp.where(valid, xa_ref[...], zero), jnp.where(valid, xb_ref[...], zero)).astype(BF16)
        g = jnp.dot(x, wbf_ref[0], preferred_element_type=F32) + bg_ref[0]
        u = jnp.dot(x, wbf_ref[1], preferred_element_type=F32) + bu_ref[0]
        g = jnp.minimum(g, SWIGLU_LIMIT)
        u = jnp.clip(u, -SWIGLU_LIMIT, SWIGLU_LIMIT)
        glu = g * jax.nn.sigmoid(SWIGLU_ALPHA * g)
        act = ((u + 1.0) * glu).astype(BF16)
        y = jnp.dot(act, wbf_ref[2], preferred_element_type=F32) + bd_ref[0]
        ya_ref[...], yb_ref[...] = _pack_row_halves(y)

    @pl.when(jnp.logical_not(active))
    def _():
        ya_ref[...] = jnp.zeros_like(ya_ref)
        yb_ref[...] = jnp.zeros_like(yb_ref)


def _experts(block_expert, n_used, n_valid, xs_a, xs_b, wg, bg, wu, bu, wd, bd):
    assert D_FF == D_MODEL
    blk = lambda b, be, nu, nv: (jnp.minimum(b, nu[0] - 1), 0)
    wsp = lambda: pl.BlockSpec((1, D_MODEL, D_FF), lambda b, be, nu, nv: (be[b], 0, 0))
    bsp = lambda: pl.BlockSpec((1, 1, D_FF), lambda b, be, nu, nv: (be[b], 0, 0))
    xsp = lambda: pl.BlockSpec((MOE_BM, HALF_W), blk)
    ysp = lambda: pl.BlockSpec((MOE_BM, HALF_W), lambda b, be, nu, nv: (b, 0))
    slot_arr = jax.ShapeDtypeStruct((N_SLOTS, HALF_W), jnp.uint32)
    grid_spec = pltpu.PrefetchScalarGridSpec(
        num_scalar_prefetch=3,
        grid=(N_SLOT_BLOCKS,),
        in_specs=[xsp(), xsp(), wsp(), bsp(), wsp(), bsp(), wsp(), bsp()],
        out_specs=[ysp(), ysp()],
        scratch_shapes=[pltpu.VMEM((3, D_MODEL, D_FF), BF16)],
    )
    return pl.pallas_call(
        _expert_kernel,
        grid_spec=grid_spec,
        out_shape=[slot_arr, slot_arr],
        compiler_params=_cparams(("arbitrary",)),
        name="experts",
    )(block_expert, n_used, n_valid, xs_a, xs_b, wg, bg, wu, bu, wd, bd)


def _final_kernel(x1_ref, yga_ref, ygb_ref, gate_ref, p_ref, nple_ref, wpg_ref, wpp_ref, out_ref):
    x2 = x1_ref[...]
    gates = gate_ref[...]
    for k in range(TOP_K):
        x2 = x2 + gates[:, k:k + 1] * _unpack_row_halves(yga_ref[k], ygb_ref[k])
    ms = jnp.mean(x2 * x2, axis=-1, keepdims=True)
    h3 = (x2 * lax.rsqrt(ms + EPS) * nple_ref[...]).astype(BF16)
    gate = jax.nn.sigmoid(jnp.dot(h3, wpg_ref[...], preferred_element_type=F32))
    proj = jnp.dot(p_ref[...].astype(BF16), wpp_ref[...], preferred_element_type=F32)
    out_ref[...] = x2 + gate * proj


def _final(x1, yg_a, yg_b, gates, p, n_ple, w_pg, w_pp, row0, n_rows):
    tm = FINAL_TM
    off = row0 // tm
    full = lambda a: pl.BlockSpec(a.shape, lambda i: (0,) * a.ndim)
    return pl.pallas_call(
        _final_kernel,
        grid=(n_rows // tm,),
        in_specs=[pl.BlockSpec((tm, D_MODEL), lambda i: (i + off, 0)),
                  pl.BlockSpec((TOP_K, tm, HALF_W), lambda i: (0, i + off, 0)),
                  pl.BlockSpec((TOP_K, tm, HALF_W), lambda i: (0, i + off, 0)),
                  pl.BlockSpec((tm, LANES), lambda i: (i + off, 0)),
                  pl.BlockSpec((tm, PLE_DIM), lambda i: (i, 0)),
                  full(n_ple), full(w_pg), full(w_pp)],
        out_specs=pl.BlockSpec((tm, D_MODEL), lambda i: (i, 0)),
        out_shape=jax.ShapeDtypeStruct((n_rows, D_MODEL), F32),
        compiler_params=_cparams(("arbitrary",)),
        name="final_ple",
    )(x1, yg_a, yg_b, gates, p, n_ple, w_pg, w_pp)


def _rope_tables():
    half = RET_DK // 2
    freq = ROPE_THETA ** (-jnp.arange(half, dtype=F32) / half)
    ang = jnp.arange(SAMPLE_SEQ, dtype=F32)[:, None] * freq[None, :]
    cos, sin = jnp.cos(ang), jnp.sin(ang)
    return jnp.concatenate([cos, cos], axis=-1), jnp.concatenate([-sin, sin], axis=-1)


def _retention_tables(decay_logit):
    lg = jax.nn.log_sigmoid(decay_logit.astype(F32))
    c = RET_CHUNK
    idx = jnp.arange(c, dtype=F32)
    diff = idx[:, None] - idx[None, :]
    lf = lg[0][:, None, None]
    lb = lg[1][:, None, None]
    mask = jnp.where(diff[None] >= 0, jnp.exp(lf * jnp.maximum(diff, 0.0)[None]),
                     jnp.exp(lb * jnp.maximum(-diff, 0.0)[None]))
    kdec_f = jnp.exp(lg[0][:, None] * (c - 1.0 - idx)[None, :])
    qdec_f = jnp.exp(lg[0][:, None] * (idx + 1.0)[None, :])
    kdec_b = jnp.exp(lg[1][:, None] * idx[None, :])
    qdec_b = jnp.exp(lg[1][:, None] * (c - idx)[None, :])
    dec = jnp.stack([kdec_f, qdec_f, kdec_b, qdec_b], axis=1)[..., None]
    cdec = jnp.exp(lg * c).T
    cdec = jnp.broadcast_to(cdec[:, :, None, None], (RET_HEADS, 2, 1, RET_DV))
    return mask, dec, cdec


def _t5_bucket(rel):
    half = T5_BUCKETS // 2
    exact = half // 2
    n = np.abs(rel)
    ratio = np.log(np.maximum(n, 1).astype(np.float32) / np.float32(exact)) / np.float32(math.log(T5_MAX_DIST / exact))
    large = exact + (ratio * np.float32(half - exact)).astype(np.int32)
    large = np.minimum(large, half - 1)
    return np.where(rel > 0, half, 0) + np.where(n < exact, n, large)


def _attention_bias(rel_bias, gi, dil, radius):
    qi = np.arange(QB)
    ki = np.arange(KW) - ATT_BLOCK
    rel = ki[None, :] - qi[:, None]
    onehot = jnp.asarray(_t5_bucket(rel * dil)[..., None] == np.arange(T5_BUCKETS), F32)
    tab = rel_bias[:, gi * ATT_HEADS:(gi + 1) * ATT_HEADS].astype(F32)
    bias = jnp.einsum('qkb,bh->hqk', onehot, tab, precision=lax.Precision.HIGHEST)
    return jnp.where(jnp.asarray(np.abs(rel) <= radius)[None], bias * LOG2E, NEG)


def _seq_edge_flags():
    first = np.zeros((N_SEG,), np.int32)
    last = np.zeros((N_SEG,), np.int32)
    first[:N_PROMPT_SEG] = 1
    last[:N_PROMPT_SEG] = 1
    first[N_PROMPT_SEG] = 1
    last[N_SEG - 1] = 1
    return jnp.asarray(first), jnp.asarray(last)


def _pad_lanes(a, value=0.0):
    return jnp.pad(a, ((0, 0), (0, LANES - a.shape[-1])), constant_values=value)


def kernel(x_prompt, x_sample, p_prompt, p_sample, norm_mix_g, w_in, ret_decay_logit, ret_gn_g,
           att_q_norm_g, att_k_norm_g, rel_bias, w_ret_proj, w_att_proj, w_out, norm_ffn_g,
           w_router, b_router, w_gate, b_gate, w_up, b_up, w_down, b_down,
           norm_ple_g, w_ple_gate, w_ple_proj):
    assert norm_mix_g.shape[0] == 1, "one layer"
    x_p = x_prompt.reshape(T_PROMPT, D_MODEL)
    x_s = x_sample.reshape(SAMPLE_SEQ, D_MODEL)

    cos_t, sin_t = _rope_tables()
    z = _in_proj(x_p, x_s, norm_mix_g.astype(F32), w_in[0].astype(BF16), cos_t, sin_t)

    ret_mask, ret_dec, ret_cdec = _retention_tables(ret_decay_logit[0])
    y_ret = _retention(z, ret_mask, ret_dec, ret_cdec, ret_gn_g[0].reshape(RET_HEADS, 1, RET_DV).astype(F32))

    first, last = _seq_edge_flags()
    o_list, lse_list = [], []
    for gi, (window, dil) in enumerate(ATT_GROUPS):
        bias_g = _attention_bias(rel_bias, gi, dil, window // (2 * dil))
        gq = jnp.tile(att_q_norm_g[0, gi].astype(F32) * (ATT_DH ** -0.5 * LOG2E), LANES // ATT_DH)[None, :]
        gk = jnp.tile(att_k_norm_g[0, gi].astype(F32), LANES // ATT_DH)[None, :]
        o_g, lse_g = _attention_group(z, bias_g, gq, gk, first, last, gi, dil)
        o_list.append(o_g)
        lse_list.append(lse_g)

    w_r = w_router[0].astype(F32)
    w_r_hi = w_r.astype(BF16)
    w_r_lo = (w_r - w_r_hi.astype(F32)).astype(BF16)
    w_router_cat = _pad_lanes(jnp.concatenate([w_r_hi, w_r_lo], axis=1))
    w_router_hi = _pad_lanes(w_r_hi)
    b_router_p = _pad_lanes(b_router.astype(F32), NEG)
    x1, hp_a, hp_b, idx, gates, rank, cnt = _merge(
        y_ret, o_list, lse_list, z, x_p, x_s, w_ret_proj[0].astype(BF16), w_att_proj[0].astype(BF16),
        w_out[0].astype(BF16), norm_ffn_g.astype(F32), w_router_cat, w_router_hi, b_router_p)

    counts = cnt[0, :N_EXPERTS].astype(jnp.int32)
    padded = (counts + MOE_BM - 1) // MOE_BM * MOE_BM
    pad_end = jnp.cumsum(padded)
    pad_start = pad_end - padded
    expert_ids = jnp.arange(N_EXPERTS, dtype=jnp.int32)
    top_idx = idx[:, :TOP_K]
    start_of = jnp.sum(jnp.where(top_idx[:, :, None] == expert_ids, pad_start, 0), axis=-1)
    dest_kmajor = (start_of + rank[:, :TOP_K]).T
    n_used = (pad_end[-1] // MOE_BM).astype(jnp.int32).reshape(1)
    blk_row0 = jnp.arange(N_SLOT_BLOCKS, dtype=jnp.int32) * MOE_BM
    block_expert = jnp.minimum(jnp.sum((pad_end[None, :] <= blk_row0[:, None]).astype(jnp.int32), axis=1),
                               N_EXPERTS - 1).astype(jnp.int32)
    slot_end = jnp.sum(jnp.where(block_expert[:, None] == expert_ids, pad_start + counts, 0), axis=-1)
    n_valid = jnp.clip(slot_end - blk_row0, 0, MOE_BM).astype(jnp.int32)

    xs_a = _sc_scatter_rows(hp_a, dest_kmajor, N_SLOTS)
    xs_b = _sc_scatter_rows(hp_b, dest_kmajor, N_SLOTS)
    ys_a, ys_b = _experts(block_expert, n_used, n_valid, xs_a, xs_b,
                          w_gate[0], b_gate[0].reshape(N_EXPERTS, 1, D_FF).astype(F32),
                          w_up[0], b_up[0].reshape(N_EXPERTS, 1, D_FF).astype(F32),
                          w_down[0], b_down[0].reshape(N_EXPERTS, 1, D_MODEL).astype(F32))
    dest_flat = dest_kmajor.reshape(-1)
    yg_a = _sc_gather_rows(ys_a, dest_flat).reshape(TOP_K, T_ALL, HALF_W)
    yg_b = _sc_gather_rows(ys_b, dest_flat).reshape(TOP_K, T_ALL, HALF_W)

    n_ple = norm_ple_g.astype(F32)
    w_pg = w_ple_gate[0].astype(BF16)
    w_pp = w_ple_proj[0].astype(BF16)
    y_p = _final(x1, yg_a, yg_b, gates, p_prompt[0].reshape(T_PROMPT, PLE_DIM), n_ple, w_pg, w_pp, 0, T_PROMPT)
    y_s = _final(x1, yg_a, yg_b, gates, p_sample[0].reshape(SAMPLE_SEQ, PLE_DIM), n_ple, w_pg, w_pp, T_PROMPT, SAMPLE_SEQ)
    return (y_p.reshape(x_prompt.shape), y_s.reshape(x_sample.shape))
```

```python
import functools
import math

import jax
import jax.numpy as jnp
import numpy as np
from jax import lax
from jax.experimental import pallas as pl
from jax.experimental.pallas import tpu as pltpu
from jax.experimental.pallas import tpu_sc as plsc

F32 = jnp.float32
BF16 = jnp.bfloat16

D_MODEL = 1024
N_PROMPT_SEQ = 8
PROMPT_SEQ = 2048
SAMPLE_SEQ = 16384
T_PROMPT = N_PROMPT_SEQ * PROMPT_SEQ
T_ALL = T_PROMPT + SAMPLE_SEQ

RET_HEADS = 4
RET_DK = 128
RET_DV = 256
RET_CHUNK = 128
ROPE_THETA = 10000.0
ATT_GROUPS = ((128, 1), (512, 4), (2048, 16))
N_GROUPS = 3
ATT_HEADS = 8
ATT_DH = 64
ATT_BLOCK = 64
ATT_W = ATT_HEADS * ATT_DH
T5_BUCKETS = 32
T5_MAX_DIST = 1024
N_EXPERTS = 32
TOP_K = 4
D_FF = 1024
SWIGLU_ALPHA = 1.702
SWIGLU_LIMIT = 7.0
PLE_DIM = 256
EPS = 1e-6

RET_QK_W = RET_HEADS * RET_DK
RET_V_W = RET_HEADS * RET_DV
N_IN = 2 * RET_QK_W + 2 * RET_V_W + 3 * N_GROUPS * ATT_W + 2 * D_MODEL

COL_RQ = 0
COL_RK = RET_QK_W
COL_RV = 2 * RET_QK_W
COL_RG = COL_RV + RET_V_W
COL_ATT = COL_RG + RET_V_W
COL_GATE_RET = COL_ATT + 3 * N_GROUPS * ATT_W
COL_GATE_ATT = COL_GATE_RET + D_MODEL

LANES = 128
VMEM_LIMIT = 56 * 1024 * 1024
ATT_VMEM_LIMIT = 58 * 1024 * 1024

SEG = 2048
N_SEG = T_ALL // SEG
N_PROMPT_SEG = T_PROMPT // SEG
COL_BLK = 512
N_COL_BLK = N_IN // COL_BLK
QB = 128
KW = 256
NEG = -1e30
LOG2E = math.log2(math.e)
LN2 = math.log(2.0)
MERGE_TM = 512
FINAL_TM = 512
MOE_BM = 512
N_SLOT_BLOCKS = T_ALL * TOP_K // MOE_BM + N_EXPERTS
N_SLOTS = N_SLOT_BLOCKS * MOE_BM
HALF_W = D_MODEL // 4
SC_WINDOW = 128


def _cparams(sem, vmem=VMEM_LIMIT):
    return pltpu.CompilerParams(dimension_semantics=sem, vmem_limit_bytes=vmem)


def _sigmoid(x):
    return 0.5 * jnp.tanh(0.5 * x) + 0.5


def _pack_bf16_pair(x):
    w = x.shape[-1] // 2
    hi = pltpu.bitcast(x[:, :w].astype(BF16).astype(F32), jnp.uint32)
    lo = pltpu.bitcast(x[:, w:].astype(BF16).astype(F32), jnp.uint32)
    return hi | (lo >> 16)


def _unpack_bf16_pair(p):
    hi = pltpu.bitcast(p & jnp.uint32(0xFFFF0000), F32)
    lo = pltpu.bitcast(p << 16, F32)
    return jnp.concatenate([hi, lo], axis=-1)


def _pack_row_halves(x):
    half = x.shape[-1] // 2
    return _pack_bf16_pair(x[:, :half]), _pack_bf16_pair(x[:, half:])


def _unpack_row_halves(pa, pb):
    return jnp.concatenate([_unpack_bf16_pair(pa), _unpack_bf16_pair(pb)], axis=-1)


def _in_proj_kernel(xp_ref, xs_ref, g_ref, w_ref, cos_ref, sin_ref, z_ref, h_ref, p_ref, p2_ref):
    i = pl.program_id(0)
    j = pl.program_id(1)

    def norm_into_h(x_ref):
        xf = x_ref[...]
        ms = jnp.mean(xf * xf, axis=-1, keepdims=True)
        h_ref[...] = (xf * lax.rsqrt(ms + EPS) * g_ref[...]).astype(BF16)

    @pl.when((j == 0) & (i < N_PROMPT_SEG))
    def _():
        norm_into_h(xp_ref)

    @pl.when((j == 0) & (i >= N_PROMPT_SEG))
    def _():
        norm_into_h(xs_ref)

    n_slab = COL_BLK // LANES

    def project():
        return jnp.dot(h_ref[...], w_ref[...], preferred_element_type=F32)

    is_rope = j < (COL_RV // COL_BLK)
    att0 = COL_ATT // COL_BLK
    is_d4 = (j >= att0 + 3) & (j < att0 + 6)
    is_d16 = (j >= att0 + 6) & (j < att0 + 9)

    @pl.when(is_rope)
    def _():
        acc = project()
        scale = jnp.where(j == COL_RK // COL_BLK, RET_DK ** -0.5, 1.0).astype(F32)
        c = cos_ref[...]
        sn = sin_ref[...]
        for s in range(n_slab):
            xs = acc[:, s * LANES:(s + 1) * LANES]
            r = xs * c + pltpu.roll(xs, RET_DK // 2, axis=1) * sn
            z_ref[:, s * LANES:(s + 1) * LANES] = (r * scale).astype(BF16)

    @pl.when(is_d4)
    def _():
        acc = project()
        for s in range(n_slab):
            p_ref[s] = acc[:, s * LANES:(s + 1) * LANES]
        rows = SEG // 4
        for rho in range(4):
            for s in range(n_slab):
                piece = p_ref[s, pl.ds(rho, rows, stride=4), :]
                z_ref[rho * rows:(rho + 1) * rows, s * LANES:(s + 1) * LANES] = piece.astype(BF16)

    @pl.when(is_d16)
    def _():
        acc = project()
        for s in range(n_slab):
            p_ref[s] = acc[:, s * LANES:(s + 1) * LANES]
        quarter = SEG // 4
        rows = SEG // 16
        for r4 in range(4):
            for s in range(n_slab):
                p2_ref[s, r4 * quarter:(r4 + 1) * quarter, :] = p_ref[s, pl.ds(r4, quarter, stride=4), :]
        for r4 in range(4):
            for hi in range(4):
                rho = 4 * hi + r4
                for s in range(n_slab):
                    piece = p2_ref[s, pl.ds(r4 * quarter + hi, rows, stride=4), :]
                    z_ref[rho * rows:(rho + 1) * rows, s * LANES:(s + 1) * LANES] = piece.astype(BF16)

    @pl.when(jnp.logical_not(is_rope | is_d4 | is_d16))
    def _():
        z_ref[...] = project().astype(BF16)


def _in_proj(x_p, x_s, norm_g, w_in_bf, cos_t, sin_t):
    def pos_blk(i, j):
        return (jnp.maximum(i - N_PROMPT_SEG, 0), 0)

    return pl.pallas_call(
        _in_proj_kernel,
        grid=(N_SEG, N_COL_BLK),
        in_specs=[
            pl.BlockSpec((SEG, D_MODEL), lambda i, j: (jnp.minimum(i, N_PROMPT_SEG - 1), 0),
                         pipeline_mode=pl.Buffered(1)),
            pl.BlockSpec((SEG, D_MODEL), pos_blk, pipeline_mode=pl.Buffered(1)),
            pl.BlockSpec((1, D_MODEL), lambda i, j: (0, 0)),
            pl.BlockSpec((D_MODEL, COL_BLK), lambda i, j: (0, j)),
            pl.BlockSpec((SEG, LANES), pos_blk),
            pl.BlockSpec((SEG, LANES), pos_blk),
        ],
        out_specs=pl.BlockSpec((SEG, COL_BLK), lambda i, j: (i, j)),
        out_shape=jax.ShapeDtypeStruct((T_ALL, N_IN), BF16),
        scratch_shapes=[
            pltpu.VMEM((SEG, D_MODEL), BF16),
            pltpu.VMEM((COL_BLK // LANES, SEG, LANES), F32),
            pltpu.VMEM((COL_BLK // LANES, SEG, LANES), F32),
        ],
        compiler_params=_cparams(("arbitrary", "arbitrary")),
        name="in_proj",
    )(x_p, x_s, norm_g, w_in_bf, cos_t, sin_t)


RET_CHUNKS_PER_SEG = SEG // RET_CHUNK
RET_MAX_CHUNKS = SAMPLE_SEQ // RET_CHUNK
RET_GROUP = 4


def _retention_kernel(seg_ref, phase_ref, reset_ref, cbase_ref,
                      q_ref, k_ref, v_ref, g_ref, mask_ref, dec_ref, cdec_ref, gn_ref,
                      y_ref, sb_ref, sf_ref, sr_ref):
    step = pl.program_id(1)
    phase = phase_ref[step]
    reset = reset_ref[step]
    cbase = cbase_ref[step]
    kdec_f = dec_ref[0, 0]
    qdec_f = dec_ref[0, 1]
    kdec_b = dec_ref[0, 2]
    qdec_b = dec_ref[0, 3]
    cd_f = cdec_ref[0, 0]
    cd_b = cdec_ref[0, 1]

    def kv_outer(kd, v):
        return lax.dot_general(kd, v, (((0,), (0,)), ((), ())), preferred_element_type=F32)

    @pl.when((phase == 0) & (reset == 1))
    def _():
        sr_ref[...] = jnp.zeros_like(sr_ref)

    @pl.when((phase == 1) & (reset == 1))
    def _():
        sf_ref[...] = jnp.zeros_like(sf_ref)

    n_groups = RET_CHUNKS_PER_SEG // RET_GROUP

    def chunk_rows(c):
        return pl.ds(pl.multiple_of(c * RET_CHUNK, RET_CHUNK), RET_CHUNK)

    @pl.when(phase == 0)
    def _():
        def body(it, carry):
            top = RET_CHUNKS_PER_SEG - 1 - it * RET_GROUP
            kvs = []
            for j in range(RET_GROUP):
                rows = chunk_rows(top - j)
                kd = (k_ref[rows, :].astype(F32) * kdec_b).astype(BF16)
                kvs.append(kv_outer(kd, v_ref[rows, :]))
            state = sr_ref[...]
            for j in range(RET_GROUP):
                sb_ref[cbase + top - j] = state.astype(BF16)
                state = cd_b * state + kvs[j]
            sr_ref[...] = state
            return carry

        lax.fori_loop(0, n_groups, body, 0)

    @pl.when(phase == 1)
    def _():
        msk = mask_ref[0]
        gn = gn_ref[0]

        def body(it, carry):
            c0 = it * RET_GROUP
            lhs, vs, kvs = [], [], []
            for j in range(RET_GROUP):
                rows = chunk_rows(c0 + j)
                qb = q_ref[rows, :]
                kb = k_ref[rows, :]
                v = v_ref[rows, :]
                q = qb.astype(F32)
                s = lax.dot_general(qb, kb, (((1,), (1,)), ((), ())), preferred_element_type=F32)
                lhs.append(jnp.concatenate(
                    [(s * msk).astype(BF16), (q * qdec_f).astype(BF16), (q * qdec_b).astype(BF16)], axis=-1))
                vs.append(v)
                kvs.append(kv_outer((kb.astype(F32) * kdec_f).astype(BF16), v))
            state = sf_ref[...]
            for j in range(RET_GROUP):
                c = c0 + j
                rhs = jnp.concatenate([vs[j], state.astype(BF16), sb_ref[cbase + c]], axis=0)
                o = jnp.dot(lhs[j], rhs, preferred_element_type=F32)
                state = cd_f * state + kvs[j]
                mu = jnp.mean(o, axis=-1, keepdims=True)
                oc = o - mu
                var = jnp.mean(oc * oc, axis=-1, keepdims=True)
                on = oc * lax.rsqrt(var + EPS) * gn
                rows = chunk_rows(c)
                gate = g_ref[rows, :].astype(F32)
                y_ref[rows, :] = (gate * jax.nn.sigmoid(gate) * on).astype(BF16)
            sf_ref[...] = state
            return carry

        lax.fori_loop(0, n_groups, body, 0)


def _retention_schedule():
    seg, phase, reset, cbase = [], [], [], []
    for p in range(N_PROMPT_SEG):
        for ph in (0, 1):
            seg.append(p); phase.append(ph); reset.append(1); cbase.append(0)
    n_s = N_SEG - N_PROMPT_SEG
    for i in range(n_s):
        t = n_s - 1 - i
        seg.append(N_PROMPT_SEG + t); phase.append(0); reset.append(int(i == 0)); cbase.append(t * RET_CHUNKS_PER_SEG)
    for t in range(n_s):
        seg.append(N_PROMPT_SEG + t); phase.append(1); reset.append(int(t == 0)); cbase.append(t * RET_CHUNKS_PER_SEG)
    hold = list(seg)
    for i in range(len(seg)):
        if phase[i] == 0:
            nxt = next(j for j in range(i + 1, len(seg)) if phase[j] == 1)
            hold[i] = seg[nxt]
    arr = lambda a: jnp.asarray(np.asarray(a, np.int32))
    return arr(seg), arr(phase), arr(reset), arr(cbase), arr(hold)


def _retention(z, ret_mask, ret_dec, ret_cdec, gn_g):
    seg, phase, reset, cbase, hold = _retention_schedule()
    n_steps = int(seg.shape[0])
    qk_blk = lambda col: (lambda h, s, seg_r, ph_r, rs_r, cb_r, hold_r: (seg_r[s], col // RET_DK + h))
    hold_blk = lambda col, w: (lambda h, s, seg_r, ph_r, rs_r, cb_r, hold_r: (hold_r[s], col // w + h))
    v_blk = lambda h, s, seg_r, ph_r, rs_r, cb_r, hold_r: (seg_r[s], COL_RV // RET_DV + h)
    per_head = lambda h, s, *_: (h, 0, 0)
    per_head4 = lambda h, s, *_: (h, 0, 0, 0)
    grid_spec = pltpu.PrefetchScalarGridSpec(
        num_scalar_prefetch=5,
        grid=(RET_HEADS, n_steps),
        in_specs=[
            pl.BlockSpec((SEG, RET_DK), hold_blk(COL_RQ, RET_DK)),
            pl.BlockSpec((SEG, RET_DK), qk_blk(COL_RK)),
            pl.BlockSpec((SEG, RET_DV), v_blk),
            pl.BlockSpec((SEG, RET_DV), hold_blk(COL_RG, RET_DV)),
            pl.BlockSpec((1, RET_CHUNK, RET_CHUNK), per_head),
            pl.BlockSpec((1, 4, RET_CHUNK, 1), per_head4),
            pl.BlockSpec((1, 2, 1, RET_DV), per_head4),
            pl.BlockSpec((1, 1, RET_DV), per_head),
        ],
        out_specs=pl.BlockSpec((SEG, RET_DV), lambda h, s, seg_r, ph_r, rs_r, cb_r, hold_r: (hold_r[s], h)),
        scratch_shapes=[
            pltpu.VMEM((RET_MAX_CHUNKS, RET_DK, RET_DV), BF16),
            pltpu.VMEM((RET_DK, RET_DV), F32),
            pltpu.VMEM((RET_DK, RET_DV), F32),
        ],
    )

    def kernel(seg_r, ph_r, rs_r, cb_r, hold_r, *refs):
        _retention_kernel(seg_r, ph_r, rs_r, cb_r, *refs)

    return pl.pallas_call(
        kernel,
        grid_spec=grid_spec,
        out_shape=jax.ShapeDtypeStruct((T_ALL, RET_V_W), BF16),
        compiler_params=_cparams(("arbitrary", "arbitrary")),
        name="retention",
    )(seg, phase, reset, cbase, hold, z, z, z, z, ret_mask, ret_dec, ret_cdec, gn_g)


def _attention_kernel(first_ref, last_ref, q_ref, kp_ref, km_ref, kn_ref, vp_ref, vm_ref, vn_ref,
                      bias_ref, gq_ref, gk_ref, o_ref, lse_ref, kall, vall, qall, oacc, bvar, *, dil):
    nb = SEG // dil // ATT_BLOCK
    nqb = nb // 2
    n_slab = ATT_W // LANES
    c = pl.program_id(0)
    is_first = first_ref[c]
    is_last = last_ref[c]
    lane = lax.broadcasted_iota(jnp.int32, (1, LANES), 1)
    lo = lane < ATT_DH
    gq = gq_ref[...]
    gk = gk_ref[...]

    @pl.when(c == 0)
    def _():
        col = lax.broadcasted_iota(jnp.int32, (1, KW), 1)
        left = jnp.where(col < ATT_BLOCK, NEG, 0.0).astype(F32)
        right = jnp.where(col >= KW - ATT_BLOCK, NEG, 0.0).astype(F32)
        for h in range(ATT_HEADS):
            b = bias_ref[h]
            bvar[0, h] = b
            bvar[1, h] = b + left
            bvar[2, h] = b + right
            bvar[3, h] = b + left + right

    def head_norm(x, g):
        x2 = x * x
        s_lo = jnp.sum(jnp.where(lo, x2, 0.0), axis=-1, keepdims=True)
        s_hi = jnp.sum(jnp.where(lo, 0.0, x2), axis=-1, keepdims=True)
        ms = jnp.where(lo, s_lo, s_hi) * (1.0 / ATT_DH)
        return x * lax.rsqrt(ms + EPS) * g

    def norm_block(src):
        even, odd = [], []
        for s in range(n_slab):
            xn = head_norm(src[:, s * LANES:(s + 1) * LANES].astype(F32), gk)
            even.append(jnp.where(lo, xn, 0.0).astype(BF16))
            odd.append(jnp.where(lo, 0.0, xn).astype(BF16))
        return jnp.concatenate(even, axis=-1), jnp.concatenate(odd, axis=-1)

    lo_wide = lax.broadcasted_iota(jnp.int32, (1, ATT_W), 1) % LANES < ATT_DH

    def split_heads(v):
        zero = jnp.zeros_like(v)
        return jnp.where(lo_wide, v, zero), jnp.where(lo_wide, zero, v)

    def fill_main(it, carry):
        rho = it // nb
        blk = it % nb
        kall[0, rho, blk + 1], kall[1, rho, blk + 1] = norm_block(km_ref[rho, blk])
        vall[0, rho, blk + 1], vall[1, rho, blk + 1] = split_heads(vm_ref[rho, blk])
        qsrc = q_ref[rho, blk]
        qall[rho, blk] = jnp.concatenate(
            [head_norm(qsrc[:, s * LANES:(s + 1) * LANES].astype(F32), gq).astype(BF16) for s in range(n_slab)], axis=-1)
        return carry

    lax.fori_loop(0, dil * nb, fill_main, 0, unroll=4)

    def fill_halo(rho, carry):
        kall[0, rho, 0], kall[1, rho, 0] = norm_block(kp_ref[rho, 0])
        kall[0, rho, nb + 1], kall[1, rho, nb + 1] = norm_block(kn_ref[rho, 0])
        vall[0, rho, 0], vall[1, rho, 0] = split_heads(vp_ref[rho, 0])
        vall[0, rho, nb + 1], vall[1, rho, nb + 1] = split_heads(vn_ref[rho, 0])
        return carry

    lax.fori_loop(0, dil, fill_halo, 0)

    ones_even = jnp.broadcast_to(jnp.where(lo, 1.0, 0.0).astype(BF16), (KW, LANES))
    ones_odd = jnp.broadcast_to(jnp.where(lo, 0.0, 1.0).astype(BF16), (KW, LANES))

    def body(it, carry):
        rho = it // nqb
        qb = it % nqb
        var = (jnp.where((qb == 0) & (is_first == 1), 1, 0)
               + jnp.where((qb == nqb - 1) & (is_last == 1), 2, 0))
        start = rho + qb * (QB * dil)
        rows = pl.ds(start, QB) if dil == 1 else pl.ds(start, QB, stride=dil)
        for s in range(n_slab):
            sl = slice(s * LANES, (s + 1) * LANES)
            qn = qall[rho, pl.ds(2 * qb, 2), :, sl].reshape(QB, LANES)
            es, ms = [], []
            for hh in range(2):
                kw = kall[hh, rho, pl.ds(2 * qb, 4), :, sl].reshape(KW, LANES)
                sc = lax.dot_general(qn, kw, (((1,), (1,)), ((), ())), preferred_element_type=F32)
                sc = sc + bvar[var, 2 * s + hh]
                m = jnp.max(sc, axis=-1, keepdims=True)
                es.append(jnp.exp2(sc - m).astype(BF16))
                ms.append(m)
            v_even = vall[0, rho, pl.ds(2 * qb, 4), :, sl].reshape(KW, LANES)
            v_odd = vall[1, rho, pl.ds(2 * qb, 4), :, sl].reshape(KW, LANES)
            rhs = jnp.concatenate([jnp.concatenate([v_even, ones_even], axis=1),
                                   jnp.concatenate([v_odd, ones_odd], axis=1)], axis=0)
            res = jnp.dot(jnp.concatenate(es, axis=1), rhs, preferred_element_type=F32)
            den = res[:, LANES:]
            oacc[s, rows, :] = res[:, :LANES] * (1.0 / den)
            lse_ref[s, rows, :] = (jnp.where(lo, ms[0], ms[1]) + jnp.log2(den)) * LN2
        return carry

    lax.fori_loop(0, dil * nqb, body, 0, unroll=2)

    for s in range(n_slab):
        o_ref[:, s * LANES:(s + 1) * LANES] = oacc[s].astype(BF16)


def _attention_group(z, bias_g, gq, gk, first, last, gi, dil):
    nb = SEG // dil // ATT_BLOCK
    z5 = z.reshape(N_SEG, dil, nb, ATT_BLOCK, N_IN)
    cq = (COL_ATT + 3 * gi * ATT_W) // ATT_W
    ck, cv = cq + 1, cq + 2
    main = lambda cb: pl.BlockSpec((None, dil, nb, ATT_BLOCK, ATT_W), lambda c, f, l: (c, 0, 0, 0, cb))
    prev = lambda cb: pl.BlockSpec((None, dil, 1, ATT_BLOCK, ATT_W),
                                   lambda c, f, l: (c - 1 + f[c], 0, nb - 1, 0, cb), pipeline_mode=pl.Buffered(1))
    nxt = lambda cb: pl.BlockSpec((None, dil, 1, ATT_BLOCK, ATT_W),
                                  lambda c, f, l: (c + 1 - l[c], 0, 0, 0, cb), pipeline_mode=pl.Buffered(1))
    grid_spec = pltpu.PrefetchScalarGridSpec(
        num_scalar_prefetch=2,
        grid=(N_SEG,),
        in_specs=[
            main(cq), prev(ck), main(ck), nxt(ck), prev(cv), main(cv), nxt(cv),
            pl.BlockSpec((ATT_HEADS, QB, KW), lambda c, f, l: (0, 0, 0)),
            pl.BlockSpec((1, LANES), lambda c, f, l: (0, 0)),
            pl.BlockSpec((1, LANES), lambda c, f, l: (0, 0)),
        ],
        out_specs=[
            pl.BlockSpec((SEG, ATT_W), lambda c, f, l: (c, 0)),
            pl.BlockSpec((ATT_W // LANES, SEG, LANES), lambda c, f, l: (0, c, 0)),
        ],
        scratch_shapes=[
            pltpu.VMEM((2, dil, nb + 2, ATT_BLOCK, ATT_W), BF16),
            pltpu.VMEM((2, dil, nb + 2, ATT_BLOCK, ATT_W), BF16),
            pltpu.VMEM((dil, nb, ATT_BLOCK, ATT_W), BF16),
            pltpu.VMEM((ATT_W // LANES, SEG, LANES), F32),
            pltpu.VMEM((4, ATT_HEADS, QB, KW), F32),
        ],
    )
    return pl.pallas_call(
        functools.partial(_attention_kernel, dil=dil),
        grid_spec=grid_spec,
        out_shape=[jax.ShapeDtypeStruct((T_ALL, ATT_W), BF16), jax.ShapeDtypeStruct((ATT_W // LANES, T_ALL, LANES), F32)],
        compiler_params=_cparams(("arbitrary",), ATT_VMEM_LIMIT),
        name=f"attention_d{dil}",
    )(first, last, z5, z5, z5, z5, z5, z5, z5, bias_g, gq, gk)


def _merge_kernel(yret_ref, o0_ref, o1_ref, o2_ref, l0_ref, l1_ref, l2_ref,
                  gret_a_ref, gret_b_ref, gatt_a_ref, gatt_b_ref, xp_ref, xs_ref,
                  wret_ref, watt_ref, wout_ref, nffn_ref, wr_ref, wrhi_ref, br_ref,
                  x1_ref, hpa_ref, hpb_ref, idx_ref, gate_ref, rank_ref, cnt_ref, carry_ref):
    i = pl.program_id(0)
    tm = MERGE_TM

    @pl.when(i == 0)
    def _():
        carry_ref[...] = jnp.zeros_like(carry_ref)

    l0, l1, l2 = [jnp.concatenate([r[s] for s in range(ATT_W // LANES)], axis=-1) for r in (l0_ref, l1_ref, l2_ref)]
    lm = jnp.maximum(jnp.maximum(l0, l1), l2)
    e0, e1, e2 = jnp.exp(l0 - lm), jnp.exp(l1 - lm), jnp.exp(l2 - lm)
    inv = 1.0 / (e0 + e1 + e2)
    y_att = ((e0 * inv) * o0_ref[...].astype(F32) + (e1 * inv) * o1_ref[...].astype(F32)
             + (e2 * inv) * o2_ref[...].astype(F32))

    p_ret = jnp.dot(yret_ref[...], wret_ref[...], preferred_element_type=F32)
    p_att = jnp.dot(y_att.astype(BF16), watt_ref[...], preferred_element_type=F32)
    g_ret = jnp.concatenate([gret_a_ref[...], gret_b_ref[...]], axis=-1).astype(F32)
    g_att = jnp.concatenate([gatt_a_ref[...], gatt_b_ref[...]], axis=-1).astype(F32)
    merged = _sigmoid(g_ret) * p_ret + _sigmoid(g_att) * p_att
    x_in = jnp.where(i < T_PROMPT // tm, xp_ref[...], xs_ref[...])
    x1 = x_in + jnp.dot(merged.astype(BF16), wout_ref[...], preferred_element_type=F32)
    x1_ref[...] = x1

    ms = jnp.mean(x1 * x1, axis=-1, keepdims=True)
    h2 = x1 * lax.rsqrt(ms + EPS) * nffn_ref[...]
    hpa_ref[...], hpb_ref[...] = _pack_row_halves(h2)

    h_hi = h2.astype(BF16)
    h_lo = (h2 - h_hi.astype(F32)).astype(BF16)
    p1 = jnp.dot(h_hi, wr_ref[...], preferred_element_type=F32)
    p2 = jnp.dot(h_lo, wrhi_ref[...], preferred_element_type=F32)
    logits = p1 + pltpu.roll(p1, LANES - N_EXPERTS, axis=1) + p2 + br_ref[...]
    lane = lax.broadcasted_iota(jnp.int32, (tm, LANES), 1)
    lane_f = lane.astype(F32)
    work = logits
    vals, idxs = [], []
    for _ in range(TOP_K):
        m = jnp.max(work, axis=-1, keepdims=True)
        ix = jnp.min(jnp.where(work == m, lane_f, float(LANES)), axis=-1, keepdims=True)
        vals.append(m)
        idxs.append(ix)
        work = jnp.where(lane_f == ix, -3e38, work)
    es = [jnp.exp(v - vals[0]) for v in vals]
    den = es[0] + es[1] + es[2] + es[3]
    onehot = jnp.zeros((tm, LANES), F32)
    for ix in idxs:
        onehot = onehot + jnp.where(lane_f == ix, 1.0, 0.0)
    row = lax.broadcasted_iota(jnp.int32, (tm, tm), 0)
    colm = lax.broadcasted_iota(jnp.int32, (tm, tm), 1)
    tri = jnp.where(colm < row, 1.0, 0.0).astype(BF16)
    before = jnp.dot(tri, onehot.astype(BF16), preferred_element_type=F32) + carry_ref[...]
    idx_out = jnp.zeros((tm, LANES), F32)
    gate_out = jnp.zeros((tm, LANES), F32)
    rank_out = jnp.zeros((tm, LANES), F32)
    for k in range(TOP_K):
        rk = jnp.sum(jnp.where(lane_f == idxs[k], before, 0.0), axis=-1, keepdims=True)
        sel = lane == k
        idx_out = jnp.where(sel, idxs[k], idx_out)
        gate_out = jnp.where(sel, es[k] / den, gate_out)
        rank_out = jnp.where(sel, rk, rank_out)
    idx_ref[...] = idx_out.astype(jnp.int32)
    gate_ref[...] = gate_out
    rank_ref[...] = rank_out.astype(jnp.int32)
    total = carry_ref[...] + jnp.sum(onehot, axis=0, keepdims=True)
    carry_ref[...] = total
    cnt_ref[...] = jnp.broadcast_to(total, cnt_ref.shape)


def _merge(y_ret, o_list, lse_list, z, x_p, x_s, w_ret, w_att, w_out, n_ffn, w_router, w_router_hi, b_router):
    tm = MERGE_TM
    n_p = T_PROMPT // tm
    row = lambda w: pl.BlockSpec((tm, w), lambda i: (i, 0))
    full = lambda a: pl.BlockSpec(a.shape, lambda i: (0,) * a.ndim)
    zcol = lambda col: pl.BlockSpec((tm, COL_BLK), lambda i: (i, col // COL_BLK))
    lse_spec = pl.BlockSpec((ATT_W // LANES, tm, LANES), lambda i: (0, i, 0))
    return pl.pallas_call(
        _merge_kernel,
        grid=(T_ALL // tm,),
        in_specs=[row(RET_V_W), row(ATT_W), row(ATT_W), row(ATT_W), lse_spec, lse_spec, lse_spec,
                  zcol(COL_GATE_RET), zcol(COL_GATE_RET + COL_BLK), zcol(COL_GATE_ATT),
                  zcol(COL_GATE_ATT + COL_BLK),
                  pl.BlockSpec((tm, D_MODEL), lambda i: (jnp.minimum(i, n_p - 1), 0)),
                  pl.BlockSpec((tm, D_MODEL), lambda i: (jnp.maximum(i - n_p, 0), 0)),
                  full(w_ret), full(w_att), full(w_out), full(n_ffn), full(w_router), full(w_router_hi),
                  full(b_router)],
        out_specs=[row(D_MODEL), row(HALF_W), row(HALF_W), row(LANES), row(LANES), row(LANES),
                   pl.BlockSpec((8, LANES), lambda i: (0, 0))],
        out_shape=[jax.ShapeDtypeStruct((T_ALL, D_MODEL), F32),
                   jax.ShapeDtypeStruct((T_ALL, HALF_W), jnp.uint32),
                   jax.ShapeDtypeStruct((T_ALL, HALF_W), jnp.uint32),
                   jax.ShapeDtypeStruct((T_ALL, LANES), jnp.int32),
                   jax.ShapeDtypeStruct((T_ALL, LANES), F32),
                   jax.ShapeDtypeStruct((T_ALL, LANES), jnp.int32),
                   jax.ShapeDtypeStruct((8, LANES), F32)],
        scratch_shapes=[pltpu.VMEM((1, LANES), F32)],
        compiler_params=_cparams(("arbitrary",)),
        name="merge_router",
    )(y_ret, *o_list, *lse_list, z, z, z, z, x_p, x_s, w_ret, w_att, w_out, n_ffn, w_router, w_router_hi, b_router)


def _sc_mesh():
    return plsc.VectorSubcoreMesh(core_axis_name="core", subcore_axis_name="subcore")


def _sc_scatter_rows(x, idx_kmajor, n_out):
    n_rows, width = x.shape

    @pl.kernel(out_type=jax.ShapeDtypeStruct((n_out, width), x.dtype), mesh=_sc_mesh(), scratch_types=[])
    def scatter(x_hbm, i_hbm, o_hbm):
        def body(x_vmem, i_vmem):
            for k in range(TOP_K):
                pltpu.sync_copy(x_vmem, o_hbm.at[i_vmem.at[k]])

        pltpu.emit_pipeline(
            body,
            grid=(n_rows // SC_WINDOW,),
            in_specs=[pl.BlockSpec((SC_WINDOW, width), lambda i: (i, 0)),
                      pl.BlockSpec((TOP_K, SC_WINDOW), lambda i: (0, i))],
            out_specs=[],
            core_axis_name=("core", "subcore"),
            dimension_semantics=(pltpu.PARALLEL,),
        )(x_hbm, i_hbm)

    return scatter(x, idx_kmajor)


def _sc_gather_rows(data, idx):
    n_idx = idx.shape[0]
    width = data.shape[1]

    @pl.kernel(out_type=jax.ShapeDtypeStruct((n_idx, width), data.dtype), mesh=_sc_mesh(), scratch_types=[])
    def gather(x_hbm, i_hbm, o_hbm):
        def body(i_vmem, o_vmem):
            pltpu.sync_copy(x_hbm.at[i_vmem.at[0]], o_vmem)

        pltpu.emit_pipeline(
            body,
            grid=(n_idx // SC_WINDOW,),
            in_specs=[pl.BlockSpec((1, SC_WINDOW), lambda i: (0, i))],
            out_specs=[pl.BlockSpec((SC_WINDOW, width), lambda i: (i, 0))],
            core_axis_name=("core", "subcore"),
            dimension_semantics=(pltpu.PARALLEL,),
        )(i_hbm, o_hbm)

    return gather(data, idx.reshape(1, n_idx))


def _expert_kernel(be_ref, nused_ref, nvalid_ref, xa_ref, xb_ref, wg_ref, bg_ref, wu_ref, bu_ref, wd_ref, bd_ref,
                   ya_ref, yb_ref, wbf_ref):
    b = pl.program_id(0)
    active = b < nused_ref[0]
    new_expert = (b == 0) | (be_ref[b] != be_ref[jnp.maximum(b - 1, 0)])

    @pl.when(active & new_expert)
    def _():
        rows = 128
        for wi, w_ref in enumerate((wg_ref, wu_ref, wd_ref)):
            for r in range(0, D_MODEL, rows):
                wbf_ref[wi, r:r + rows, :] = w_ref[0, r:r + rows, :].astype(BF16)

    @pl.when(active)
    def _():
        valid = lax.broadcasted_iota(jnp.int32, (MOE_BM, HALF_W), 0) < nvalid_ref[b]
        zero = jnp.zeros((MOE_BM, HALF_W), jnp.uint32)
        x = _unpack_row_halves(jnp.where(valid, xa_ref[...], zero), jnp.where(valid, xb_ref[...], zero)).astype(BF16)
        g = jnp.dot(x, wbf_ref[0], preferred_element_type=F32) + bg_ref[0]
        u = jnp.dot(x, wbf_ref[1], preferred_element_type=F32) + bu_ref[0]
        g = jnp.minimum(g, SWIGLU_LIMIT)
        u = jnp.clip(u, -SWIGLU_LIMIT, SWIGLU_LIMIT)
        glu = g * jax.nn.sigmoid(SWIGLU_ALPHA * g)
        act = ((u + 1.0) * glu).astype(BF16)
        y = jnp.dot(act, wbf_ref[2], preferred_element_type=F32) + bd_ref[0]
        ya_ref[...], yb_ref[...] = _pack_row_halves(y)

    @pl.when(jnp.logical_not(active))
    def _():
        ya_ref[...] = jnp.zeros_like(ya_ref)
        yb_ref[...] = jnp.zeros_like(yb_ref)


def _experts(block_expert, n_used, n_valid, xs_a, xs_b, wg, bg, wu, bu, wd, bd):
    assert D_FF == D_MODEL
    blk = lambda b, be, nu, nv: (jnp.minimum(b, nu[0] - 1), 0)
    wsp = lambda: pl.BlockSpec((1, D_MODEL, D_FF), lambda b, be, nu, nv: (be[b], 0, 0))
    bsp = lambda: pl.BlockSpec((1, 1, D_FF), lambda b, be, nu, nv: (be[b], 0, 0))
    xsp = lambda: pl.BlockSpec((MOE_BM, HALF_W), blk)
    ysp = lambda: pl.BlockSpec((MOE_BM, HALF_W), lambda b, be, nu, nv: (b, 0))
    slot_arr = jax.ShapeDtypeStruct((N_SLOTS, HALF_W), jnp.uint32)
    grid_spec = pltpu.PrefetchScalarGridSpec(
        num_scalar_prefetch=3,
        grid=(N_SLOT_BLOCKS,),
        in_specs=[xsp(), xsp(), wsp(), bsp(), wsp(), bsp(), wsp(), bsp()],
        out_specs=[ysp(), ysp()],
        scratch_shapes=[pltpu.VMEM((3, D_MODEL, D_FF), BF16)],
    )
    return pl.pallas_call(
        _expert_kernel,
        grid_spec=grid_spec,
        out_shape=[slot_arr, slot_arr],
        compiler_params=_cparams(("arbitrary",)),
        name="experts",
    )(block_expert, n_used, n_valid, xs_a, xs_b, wg, bg, wu, bu, wd, bd)


def _final_kernel(x1_ref, yga_ref, ygb_ref, gate_ref, p_ref, nple_ref, wpg_ref, wpp_ref, out_ref):
    x2 = x1_ref[...]
    gates = gate_ref[...]
    for k in range(TOP_K):
        x2 = x2 + gates[:, k:k + 1] * _unpack_row_halves(yga_ref[k], ygb_ref[k])
    ms = jnp.mean(x2 * x2, axis=-1, keepdims=True)
    h3 = (x2 * lax.rsqrt(ms + EPS) * nple_ref[...]).astype(BF16)
    gate = jax.nn.sigmoid(jnp.dot(h3, wpg_ref[...], preferred_element_type=F32))
    proj = jnp.dot(p_ref[...].astype(BF16), wpp_ref[...], preferred_element_type=F32)
    out_ref[...] = x2 + gate * proj


def _final(x1, yg_a, yg_b, gates, p, n_ple, w_pg, w_pp, row0, n_rows):
    tm = FINAL_TM
    off = row0 // tm
    full = lambda a: pl.BlockSpec(a.shape, lambda i: (0,) * a.ndim)
    return pl.pallas_call(
        _final_kernel,
        grid=(n_rows // tm,),
        in_specs=[pl.BlockSpec((tm, D_MODEL), lambda i: (i + off, 0)),
                  pl.BlockSpec((TOP_K, tm, HALF_W), lambda i: (0, i + off, 0)),
                  pl.BlockSpec((TOP_K, tm, HALF_W), lambda i: (0, i + off, 0)),
                  pl.BlockSpec((tm, LANES), lambda i: (i + off, 0)),
                  pl.BlockSpec((tm, PLE_DIM), lambda i: (i, 0)),
                  full(n_ple), full(w_pg), full(w_pp)],
        out_specs=pl.BlockSpec((tm, D_MODEL), lambda i: (i, 0)),
        out_shape=jax.ShapeDtypeStruct((n_rows, D_MODEL), F32),
        compiler_params=_cparams(("arbitrary",)),
        name="final_ple",
    )(x1, yg_a, yg_b, gates, p, n_ple, w_pg, w_pp)


def _rope_tables():
    half = RET_DK // 2
    freq = ROPE_THETA ** (-jnp.arange(half, dtype=F32) / half)
    ang = jnp.arange(SAMPLE_SEQ, dtype=F32)[:, None] * freq[None, :]
    cos, sin = jnp.cos(ang), jnp.sin(ang)
    return jnp.concatenate([cos, cos], axis=-1), jnp.concatenate([-sin, sin], axis=-1)


def _retention_tables(decay_logit):
    lg = jax.nn.log_sigmoid(decay_logit.astype(F32))
    c = RET_CHUNK
    idx = jnp.arange(c, dtype=F32)
    diff = idx[:, None] - idx[None, :]
    lf = lg[0][:, None, None]
    lb = lg[1][:, None, None]
    mask = jnp.where(diff[None] >= 0, jnp.exp(lf * jnp.maximum(diff, 0.0)[None]),
                     jnp.exp(lb * jnp.maximum(-diff, 0.0)[None]))
    kdec_f = jnp.exp(lg[0][:, None] * (c - 1.0 - idx)[None, :])
    qdec_f = jnp.exp(lg[0][:, None] * (idx + 1.0)[None, :])
    kdec_b = jnp.exp(lg[1][:, None] * idx[None, :])
    qdec_b = jnp.exp(lg[1][:, None] * (c - idx)[None, :])
    dec = jnp.stack([kdec_f, qdec_f, kdec_b, qdec_b], axis=1)[..., None]
    cdec = jnp.exp(lg * c).T
    cdec = jnp.broadcast_to(cdec[:, :, None, None], (RET_HEADS, 2, 1, RET_DV))
    return mask, dec, cdec


def _t5_bucket(rel):
    half = T5_BUCKETS // 2
    exact = half // 2
    n = np.abs(rel)
    ratio = np.log(np.maximum(n, 1).astype(np.float32) / np.float32(exact)) / np.float32(math.log(T5_MAX_DIST / exact))
    large = exact + (ratio * np.float32(half - exact)).astype(np.int32)
    large = np.minimum(large, half - 1)
    return np.where(rel > 0, half, 0) + np.where(n < exact, n, large)


def _attention_bias(rel_bias, gi, dil, radius):
    qi = np.arange(QB)
    ki = np.arange(KW) - ATT_BLOCK
    rel = ki[None, :] - qi[:, None]
    onehot = jnp.asarray(_t5_bucket(rel * dil)[..., None] == np.arange(T5_BUCKETS), F32)
    tab = rel_bias[:, gi * ATT_HEADS:(gi + 1) * ATT_HEADS].astype(F32)
    bias = jnp.einsum('qkb,bh->hqk', onehot, tab, precision=lax.Precision.HIGHEST)
    return jnp.where(jnp.asarray(np.abs(rel) <= radius)[None], bias * LOG2E, NEG)


def _seq_edge_flags():
    first = np.zeros((N_SEG,), np.int32)
    last = np.zeros((N_SEG,), np.int32)
    first[:N_PROMPT_SEG] = 1
    last[:N_PROMPT_SEG] = 1
    first[N_PROMPT_SEG] = 1
    last[N_SEG - 1] = 1
    return jnp.asarray(first), jnp.asarray(last)


def _pad_lanes(a, value=0.0):
    return jnp.pad(a, ((0, 0), (0, LANES - a.shape[-1])), constant_values=value)


def kernel(x_prompt, x_sample, p_prompt, p_sample, norm_mix_g, w_in, ret_decay_logit, ret_gn_g,
           att_q_norm_g, att_k_norm_g, rel_bias, w_ret_proj, w_att_proj, w_out, norm_ffn_g,
           w_router, b_router, w_gate, b_gate, w_up, b_up, w_down, b_down,
           norm_ple_g, w_ple_gate, w_ple_proj):
    assert norm_mix_g.shape[0] == 1, "one layer"
    x_p = x_prompt.reshape(T_PROMPT, D_MODEL)
    x_s = x_sample.reshape(SAMPLE_SEQ, D_MODEL)

    cos_t, sin_t = _rope_tables()
    z = _in_proj(x_p, x_s, norm_mix_g.astype(F32), w_in[0].astype(BF16), cos_t, sin_t)

    ret_mask, ret_dec, ret_cdec = _retention_tables(ret_decay_logit[0])
    y_ret = _retention(z, ret_mask, ret_dec, ret_cdec, ret_gn_g[0].reshape(RET_HEADS, 1, RET_DV).astype(F32))

    first, last = _seq_edge_flags()
    o_list, lse_list = [], []
    for gi, (window, dil) in enumerate(ATT_GROUPS):
        bias_g = _attention_bias(rel_bias, gi, dil, window // (2 * dil))
        gq = jnp.tile(att_q_norm_g[0, gi].astype(F32) * (ATT_DH ** -0.5 * LOG2E), LANES // ATT_DH)[None, :]
        gk = jnp.tile(att_k_norm_g[0, gi].astype(F32), LANES // ATT_DH)[None, :]
        o_g, lse_g = _attention_group(z, bias_g, gq, gk, first, last, gi, dil)
        o_list.append(o_g)
        lse_list.append(lse_g)

    w_r = w_router[0].astype(F32)
    w_r_hi = w_r.astype(BF16)
    w_r_lo = (w_r - w_r_hi.astype(F32)).astype(BF16)
    w_router_cat = _pad_lanes(jnp.concatenate([w_r_hi, w_r_lo], axis=1))
    w_router_hi = _pad_lanes(w_r_hi)
    b_router_p = _pad_lanes(b_router.astype(F32), NEG)
    x1, hp_a, hp_b, idx, gates, rank, cnt = _merge(
        y_ret, o_list, lse_list, z, x_p, x_s, w_ret_proj[0].astype(BF16), w_att_proj[0].astype(BF16),
        w_out[0].astype(BF16), norm_ffn_g.astype(F32), w_router_cat, w_router_hi, b_router_p)

    counts = cnt[0, :N_EXPERTS].astype(jnp.int32)
    padded = (counts + MOE_BM - 1) // MOE_BM * MOE_BM
    pad_end = jnp.cumsum(padded)
    pad_start = pad_end - padded
    expert_ids = jnp.arange(N_EXPERTS, dtype=jnp.int32)
    top_idx = idx[:, :TOP_K]
    start_of = jnp.sum(jnp.where(top_idx[:, :, None] == expert_ids, pad_start, 0), axis=-1)
    dest_kmajor = (start_of + rank[:, :TOP_K]).T
    n_used = (pad_end[-1] // MOE_BM).astype(jnp.int32).reshape(1)
    blk_row0 = jnp.arange(N_SLOT_BLOCKS, dtype=jnp.int32) * MOE_BM
    block_expert = jnp.minimum(jnp.sum((pad_end[None, :] <= blk_row0[:, None]).astype(jnp.int32), axis=1),
                               N_EXPERTS - 1).astype(jnp.int32)
    slot_end = jnp.sum(jnp.where(block_expert[:, None] == expert_ids, pad_start + counts, 0), axis=-1)
    n_valid = jnp.clip(slot_end - blk_row0, 0, MOE_BM).astype(jnp.int32)

    xs_a = _sc_scatter_rows(hp_a, dest_kmajor, N_SLOTS)
    xs_b = _sc_scatter_rows(hp_b, dest_kmajor, N_SLOTS)
    ys_a, ys_b = _experts(block_expert, n_used, n_valid, xs_a, xs_b,
                          w_gate[0], b_gate[0].reshape(N_EXPERTS, 1, D_FF).astype(F32),
                          w_up[0], b_up[0].reshape(N_EXPERTS, 1, D_FF).astype(F32),
                          w_down[0], b_down[0].reshape(N_EXPERTS, 1, D_MODEL).astype(F32))
    dest_flat = dest_kmajor.reshape(-1)
    yg_a = _sc_gather_rows(ys_a, dest_flat).reshape(TOP_K, T_ALL, HALF_W)
    yg_b = _sc_gather_rows(ys_b, dest_flat).reshape(TOP_K, T_ALL, HALF_W)

    n_ple = norm_ple_g.astype(F32)
    w_pg = w_ple_gate[0].astype(BF16)
    w_pp = w_ple_proj[0].astype(BF16)
    y_p = _final(x1, yg_a, yg_b, gates, p_prompt[0].reshape(T_PROMPT, PLE_DIM), n_ple, w_pg, w_pp, 0, T_PROMPT)
    y_s = _final(x1, yg_a, yg_b, gates, p_sample[0].reshape(SAMPLE_SEQ, PLE_DIM), n_ple, w_pg, w_pp, T_PROMPT, SAMPLE_SEQ)
    return (y_p.reshape(x_prompt.shape), y_s.reshape(x_sample.shape))
```

```python
import functools
import math

import jax
import jax.numpy as jnp
import numpy as np
from jax import lax
from jax.experimental import pallas as pl
from jax.experimental.pallas import tpu as pltpu
from jax.experimental.pallas import tpu_sc as plsc

F32 = jnp.float32
BF16 = jnp.bfloat16

D_MODEL = 1024
N_PROMPT_SEQ = 8
PROMPT_SEQ = 2048
SAMPLE_SEQ = 16384
T_PROMPT = N_PROMPT_SEQ * PROMPT_SEQ
T_ALL = T_PROMPT + SAMPLE_SEQ

RET_HEADS = 4
RET_DK = 128
RET_DV = 256
RET_CHUNK = 128
ROPE_THETA = 10000.0
ATT_GROUPS = ((128, 1), (512, 4), (2048, 16))
N_GROUPS = 3
ATT_HEADS = 8
ATT_DH = 64
ATT_BLOCK = 64
ATT_W = ATT_HEADS * ATT_DH
T5_BUCKETS = 32
T5_MAX_DIST = 1024
N_EXPERTS = 32
TOP_K = 4
D_FF = 1024
SWIGLU_ALPHA = 1.702
SWIGLU_LIMIT = 7.0
PLE_DIM = 256
EPS = 1e-6

RET_QK_W = RET_HEADS * RET_DK
RET_V_W = RET_HEADS * RET_DV
N_IN = 2 * RET_QK_W + 2 * RET_V_W + 3 * N_GROUPS * ATT_W + 2 * D_MODEL

COL_RQ = 0
COL_RK = RET_QK_W
COL_RV = 2 * RET_QK_W
COL_RG = COL_RV + RET_V_W
COL_ATT = COL_RG + RET_V_W
COL_GATE_RET = COL_ATT + 3 * N_GROUPS * ATT_W
COL_GATE_ATT = COL_GATE_RET + D_MODEL

LANES = 128
VMEM_LIMIT = 56 * 1024 * 1024
ATT_VMEM_LIMIT = 58 * 1024 * 1024

SEG = 2048
N_SEG = T_ALL // SEG
N_PROMPT_SEG = T_PROMPT // SEG
COL_BLK = 512
N_COL_BLK = N_IN // COL_BLK
QB = 128
KW = 256
NEG = -1e30
LOG2E = math.log2(math.e)
LN2 = math.log(2.0)
MERGE_TM = 512
FINAL_TM = 512
MOE_BM = 512
T_GROUP = T_PROMPT
assert SAMPLE_SEQ == T_GROUP
N_SLOT_BLOCKS = T_GROUP * TOP_K // MOE_BM + N_EXPERTS
N_SLOTS = N_SLOT_BLOCKS * MOE_BM
HALF_W = D_MODEL // 4
SC_WINDOW = 128


def _cparams(sem, vmem=VMEM_LIMIT):
    return pltpu.CompilerParams(dimension_semantics=sem, vmem_limit_bytes=vmem)


def _sigmoid(x):
    return 0.5 * jnp.tanh(0.5 * x) + 0.5


def _pack_bf16_pair(x):
    w = x.shape[-1] // 2
    hi = pltpu.bitcast(x[:, :w].astype(BF16).astype(F32), jnp.uint32)
    lo = pltpu.bitcast(x[:, w:].astype(BF16).astype(F32), jnp.uint32)
    return hi | (lo >> 16)


def _unpack_bf16_pair(p):
    hi = pltpu.bitcast(p & jnp.uint32(0xFFFF0000), F32)
    lo = pltpu.bitcast(p << 16, F32)
    return jnp.concatenate([hi, lo], axis=-1)


def _pack_row_halves(x):
    half = x.shape[-1] // 2
    return _pack_bf16_pair(x[:, :half]), _pack_bf16_pair(x[:, half:])


def _unpack_row_halves(pa, pb):
    return jnp.concatenate([_unpack_bf16_pair(pa), _unpack_bf16_pair(pb)], axis=-1)


def _in_proj_kernel(xp_ref, xs_ref, g_ref, w_ref, cos_ref, sin_ref, z_ref, h_ref, p_ref, p2_ref):
    i = pl.program_id(0)
    j = pl.program_id(1)

    def norm_into_h(x_ref):
        xf = x_ref[...]
        ms = jnp.mean(xf * xf, axis=-1, keepdims=True)
        h_ref[...] = (xf * lax.rsqrt(ms + EPS) * g_ref[...]).astype(BF16)

    @pl.when((j == 0) & (i < N_PROMPT_SEG))
    def _():
        norm_into_h(xp_ref)

    @pl.when((j == 0) & (i >= N_PROMPT_SEG))
    def _():
        norm_into_h(xs_ref)

    n_slab = COL_BLK // LANES

    def project():
        return jnp.dot(h_ref[...], w_ref[...], preferred_element_type=F32)

    is_rope = j < (COL_RV // COL_BLK)
    att0 = COL_ATT // COL_BLK
    is_d4 = (j >= att0 + 3) & (j < att0 + 6)
    is_d16 = (j >= att0 + 6) & (j < att0 + 9)

    @pl.when(is_rope)
    def _():
        acc = project()
        scale = jnp.where(j == COL_RK // COL_BLK, RET_DK ** -0.5, 1.0).astype(F32)
        c = cos_ref[...]
        sn = sin_ref[...]
        for s in range(n_slab):
            xs = acc[:, s * LANES:(s + 1) * LANES]
            r = xs * c + pltpu.roll(xs, RET_DK // 2, axis=1) * sn
            z_ref[:, s * LANES:(s + 1) * LANES] = (r * scale).astype(BF16)

    @pl.when(is_d4)
    def _():
        acc = project()
        for s in range(n_slab):
            p_ref[s] = acc[:, s * LANES:(s + 1) * LANES]
        rows = SEG // 4
        for rho in range(4):
            for s in range(n_slab):
                piece = p_ref[s, pl.ds(rho, rows, stride=4), :]
                z_ref[rho * rows:(rho + 1) * rows, s * LANES:(s + 1) * LANES] = piece.astype(BF16)

    @pl.when(is_d16)
    def _():
        acc = project()
        for s in range(n_slab):
            p_ref[s] = acc[:, s * LANES:(s + 1) * LANES]
        quarter = SEG // 4
        rows = SEG // 16
        for r4 in range(4):
            for s in range(n_slab):
                p2_ref[s, r4 * quarter:(r4 + 1) * quarter, :] = p_ref[s, pl.ds(r4, quarter, stride=4), :]
        for r4 in range(4):
            for hi in range(4):
                rho = 4 * hi + r4
                for s in range(n_slab):
                    piece = p2_ref[s, pl.ds(r4 * quarter + hi, rows, stride=4), :]
                    z_ref[rho * rows:(rho + 1) * rows, s * LANES:(s + 1) * LANES] = piece.astype(BF16)

    @pl.when(jnp.logical_not(is_rope | is_d4 | is_d16))
    def _():
        z_ref[...] = project().astype(BF16)


def _in_proj(x_p, x_s, norm_g, w_in_bf, cos_t, sin_t):
    def pos_blk(i, j):
        return (jnp.maximum(i - N_PROMPT_SEG, 0), 0)

    return pl.pallas_call(
        _in_proj_kernel,
        grid=(N_SEG, N_COL_BLK),
        in_specs=[
            pl.BlockSpec((SEG, D_MODEL), lambda i, j: (jnp.minimum(i, N_PROMPT_SEG - 1), 0),
                         pipeline_mode=pl.Buffered(1)),
            pl.BlockSpec((SEG, D_MODEL), pos_blk, pipeline_mode=pl.Buffered(1)),
            pl.BlockSpec((1, D_MODEL), lambda i, j: (0, 0)),
            pl.BlockSpec((D_MODEL, COL_BLK), lambda i, j: (0, j)),
            pl.BlockSpec((SEG, LANES), pos_blk),
            pl.BlockSpec((SEG, LANES), pos_blk),
        ],
        out_specs=pl.BlockSpec((SEG, COL_BLK), lambda i, j: (i, j)),
        out_shape=jax.ShapeDtypeStruct((T_ALL, N_IN), BF16),
        scratch_shapes=[
            pltpu.VMEM((SEG, D_MODEL), BF16),
            pltpu.VMEM((COL_BLK // LANES, SEG, LANES), F32),
            pltpu.VMEM((COL_BLK // LANES, SEG, LANES), F32),
        ],
        compiler_params=_cparams(("arbitrary", "arbitrary")),
        name="in_proj",
    )(x_p, x_s, norm_g, w_in_bf, cos_t, sin_t)


RET_CHUNKS_PER_SEG = SEG // RET_CHUNK
RET_MAX_CHUNKS = SAMPLE_SEQ // RET_CHUNK
RET_GROUP = 4


def _retention_kernel(seg_ref, phase_ref, reset_ref, cbase_ref,
                      q_ref, k_ref, v_ref, g_ref, mask_ref, dec_ref, cdec_ref, gn_ref,
                      y_ref, sb_ref, sf_ref, sr_ref):
    step = pl.program_id(1)
    phase = phase_ref[step]
    reset = reset_ref[step]
    cbase = cbase_ref[step]
    kdec_f = dec_ref[0, 0]
    qdec_f = dec_ref[0, 1]
    kdec_b = dec_ref[0, 2]
    qdec_b = dec_ref[0, 3]
    cd_f = cdec_ref[0, 0]
    cd_b = cdec_ref[0, 1]

    def kv_outer(kd, v):
        return lax.dot_general(kd, v, (((0,), (0,)), ((), ())), preferred_element_type=F32)

    @pl.when((phase == 0) & (reset == 1))
    def _():
        sr_ref[...] = jnp.zeros_like(sr_ref)

    @pl.when((phase == 1) & (reset == 1))
    def _():
        sf_ref[...] = jnp.zeros_like(sf_ref)

    n_groups = RET_CHUNKS_PER_SEG // RET_GROUP

    def chunk_rows(c):
        return pl.ds(pl.multiple_of(c * RET_CHUNK, RET_CHUNK), RET_CHUNK)

    @pl.when(phase == 0)
    def _():
        def body(it, carry):
            top = RET_CHUNKS_PER_SEG - 1 - it * RET_GROUP
            kvs = []
            for j in range(RET_GROUP):
                rows = chunk_rows(top - j)
                kd = (k_ref[rows, :].astype(F32) * kdec_b).astype(BF16)
                kvs.append(kv_outer(kd, v_ref[rows, :]))
            state = sr_ref[...]
            for j in range(RET_GROUP):
                sb_ref[cbase + top - j] = state.astype(BF16)
                state = cd_b * state + kvs[j]
            sr_ref[...] = state
            return carry

        lax.fori_loop(0, n_groups, body, 0)

    @pl.when(phase == 1)
    def _():
        msk = mask_ref[0]
        gn = gn_ref[0]

        def body(it, carry):
            c0 = it * RET_GROUP
            lhs, vs, kvs = [], [], []
            for j in range(RET_GROUP):
                rows = chunk_rows(c0 + j)
                qb = q_ref[rows, :]
                kb = k_ref[rows, :]
                v = v_ref[rows, :]
                q = qb.astype(F32)
                s = lax.dot_general(qb, kb, (((1,), (1,)), ((), ())), preferred_element_type=F32)
                lhs.append(jnp.concatenate(
                    [(s * msk).astype(BF16), (q * qdec_f).astype(BF16), (q * qdec_b).astype(BF16)], axis=-1))
                vs.append(v)
                kvs.append(kv_outer((kb.astype(F32) * kdec_f).astype(BF16), v))
            state = sf_ref[...]
            for j in range(RET_GROUP):
                c = c0 + j
                rhs = jnp.concatenate([vs[j], state.astype(BF16), sb_ref[cbase + c]], axis=0)
                o = jnp.dot(lhs[j], rhs, preferred_element_type=F32)
                state = cd_f * state + kvs[j]
                mu = jnp.mean(o, axis=-1, keepdims=True)
                oc = o - mu
                var = jnp.mean(oc * oc, axis=-1, keepdims=True)
                on = oc * lax.rsqrt(var + EPS) * gn
                rows = chunk_rows(c)
                gate = g_ref[rows, :].astype(F32)
                y_ref[rows, :] = (gate * jax.nn.sigmoid(gate) * on).astype(BF16)
            sf_ref[...] = state
            return carry

        lax.fori_loop(0, n_groups, body, 0)


def _retention_schedule():
    seg, phase, reset, cbase = [], [], [], []
    for p in range(N_PROMPT_SEG):
        for ph in (0, 1):
            seg.append(p); phase.append(ph); reset.append(1); cbase.append(0)
    n_s = N_SEG - N_PROMPT_SEG
    for i in range(n_s):
        t = n_s - 1 - i
        seg.append(N_PROMPT_SEG + t); phase.append(0); reset.append(int(i == 0)); cbase.append(t * RET_CHUNKS_PER_SEG)
    for t in range(n_s):
        seg.append(N_PROMPT_SEG + t); phase.append(1); reset.append(int(t == 0)); cbase.append(t * RET_CHUNKS_PER_SEG)
    hold = list(seg)
    for i in range(len(seg)):
        if phase[i] == 0:
            nxt = next(j for j in range(i + 1, len(seg)) if phase[j] == 1)
            hold[i] = seg[nxt]
    arr = lambda a: jnp.asarray(np.asarray(a, np.int32))
    return arr(seg), arr(phase), arr(reset), arr(cbase), arr(hold)


def _retention(z, ret_mask, ret_dec, ret_cdec, gn_g):
    seg, phase, reset, cbase, hold = _retention_schedule()
    n_steps = int(seg.shape[0])
    qk_blk = lambda col: (lambda h, s, seg_r, ph_r, rs_r, cb_r, hold_r: (seg_r[s], col // RET_DK + h))
    hold_blk = lambda col, w: (lambda h, s, seg_r, ph_r, rs_r, cb_r, hold_r: (hold_r[s], col // w + h))
    v_blk = lambda h, s, seg_r, ph_r, rs_r, cb_r, hold_r: (seg_r[s], COL_RV // RET_DV + h)
    per_head = lambda h, s, *_: (h, 0, 0)
    per_head4 = lambda h, s, *_: (h, 0, 0, 0)
    grid_spec = pltpu.PrefetchScalarGridSpec(
        num_scalar_prefetch=5,
        grid=(RET_HEADS, n_steps),
        in_specs=[
            pl.BlockSpec((SEG, RET_DK), hold_blk(COL_RQ, RET_DK)),
            pl.BlockSpec((SEG, RET_DK), qk_blk(COL_RK)),
            pl.BlockSpec((SEG, RET_DV), v_blk),
            pl.BlockSpec((SEG, RET_DV), hold_blk(COL_RG, RET_DV)),
            pl.BlockSpec((1, RET_CHUNK, RET_CHUNK), per_head),
            pl.BlockSpec((1, 4, RET_CHUNK, 1), per_head4),
            pl.BlockSpec((1, 2, 1, RET_DV), per_head4),
            pl.BlockSpec((1, 1, RET_DV), per_head),
        ],
        out_specs=pl.BlockSpec((SEG, RET_DV), lambda h, s, seg_r, ph_r, rs_r, cb_r, hold_r: (hold_r[s], h)),
        scratch_shapes=[
            pltpu.VMEM((RET_MAX_CHUNKS, RET_DK, RET_DV), BF16),
            pltpu.VMEM((RET_DK, RET_DV), F32),
            pltpu.VMEM((RET_DK, RET_DV), F32),
        ],
    )

    def kernel(seg_r, ph_r, rs_r, cb_r, hold_r, *refs):
        _retention_kernel(seg_r, ph_r, rs_r, cb_r, *refs)

    return pl.pallas_call(
        kernel,
        grid_spec=grid_spec,
        out_shape=jax.ShapeDtypeStruct((T_ALL, RET_V_W), BF16),
        compiler_params=_cparams(("arbitrary", "arbitrary")),
        name="retention",
    )(seg, phase, reset, cbase, hold, z, z, z, z, ret_mask, ret_dec, ret_cdec, gn_g)


def _attention_kernel(first_ref, last_ref, q_ref, kp_ref, km_ref, kn_ref, vp_ref, vm_ref, vn_ref,
                      bias_ref, gq_ref, gk_ref, o_ref, lse_ref, kall, vall, qall, oacc, bvar, *, dil):
    nb = SEG // dil // ATT_BLOCK
    nqb = nb // 2
    n_slab = ATT_W // LANES
    c = pl.program_id(0)
    is_first = first_ref[c]
    is_last = last_ref[c]
    lane = lax.broadcasted_iota(jnp.int32, (1, LANES), 1)
    lo = lane < ATT_DH
    gq = gq_ref[...]
    gk = gk_ref[...]

    @pl.when(c == 0)
    def _():
        col = lax.broadcasted_iota(jnp.int32, (1, KW), 1)
        left = jnp.where(col < ATT_BLOCK, NEG, 0.0).astype(F32)
        right = jnp.where(col >= KW - ATT_BLOCK, NEG, 0.0).astype(F32)
        for h in range(ATT_HEADS):
            b = bias_ref[h]
            bvar[0, h] = b
            bvar[1, h] = b + left
            bvar[2, h] = b + right
            bvar[3, h] = b + left + right

    li = lax.broadcasted_iota(jnp.int32, (LANES, LANES), 0) < ATT_DH
    lj = lax.broadcasted_iota(jnp.int32, (LANES, LANES), 1) < ATT_DH
    same_head = jnp.where(li == lj, 1.0, 0.0).astype(BF16)

    def head_norm_slabs(src, g):
        xs = [src[:, s * LANES:(s + 1) * LANES].astype(F32) for s in range(n_slab)]
        x2 = jnp.concatenate([x * x for x in xs], axis=0)
        hi = x2.astype(BF16)
        lo_part = (x2 - hi.astype(F32)).astype(BF16)
        ss = (jnp.dot(hi, same_head, preferred_element_type=F32)
              + jnp.dot(lo_part, same_head, preferred_element_type=F32))
        inv = lax.rsqrt(ss * (1.0 / ATT_DH) + EPS)
        return [xs[s] * inv[s * ATT_BLOCK:(s + 1) * ATT_BLOCK] * g for s in range(n_slab)]

    def norm_block(src):
        even, odd = [], []
        for xn in head_norm_slabs(src, gk):
            even.append(jnp.where(lo, xn, 0.0).astype(BF16))
            odd.append(jnp.where(lo, 0.0, xn).astype(BF16))
        return jnp.concatenate(even, axis=-1), jnp.concatenate(odd, axis=-1)

    lo_wide = lax.broadcasted_iota(jnp.int32, (1, ATT_W), 1) % LANES < ATT_DH

    def split_heads(v):
        zero = jnp.zeros_like(v)
        return jnp.where(lo_wide, v, zero), jnp.where(lo_wide, zero, v)

    def fill_main(it, carry):
        rho = it // nb
        blk = it % nb
        kall[0, rho, blk + 1], kall[1, rho, blk + 1] = norm_block(km_ref[rho, blk])
        vall[0, rho, blk + 1], vall[1, rho, blk + 1] = split_heads(vm_ref[rho, blk])
        qall[rho, blk] = jnp.concatenate([xn.astype(BF16) for xn in head_norm_slabs(q_ref[rho, blk], gq)], axis=-1)
        return carry

    lax.fori_loop(0, dil * nb, fill_main, 0, unroll=4)

    def fill_halo(rho, carry):
        kall[0, rho, 0], kall[1, rho, 0] = norm_block(kp_ref[rho, 0])
        kall[0, rho, nb + 1], kall[1, rho, nb + 1] = norm_block(kn_ref[rho, 0])
        vall[0, rho, 0], vall[1, rho, 0] = split_heads(vp_ref[rho, 0])
        vall[0, rho, nb + 1], vall[1, rho, nb + 1] = split_heads(vn_ref[rho, 0])
        return carry

    lax.fori_loop(0, dil, fill_halo, 0)

    ones_even = jnp.broadcast_to(jnp.where(lo, 1.0, 0.0).astype(BF16), (KW, LANES))
    ones_odd = jnp.broadcast_to(jnp.where(lo, 0.0, 1.0).astype(BF16), (KW, LANES))

    def body(it, carry):
        rho = it // nqb
        qb = it % nqb
        var = (jnp.where((qb == 0) & (is_first == 1), 1, 0)
               + jnp.where((qb == nqb - 1) & (is_last == 1), 2, 0))
        start = rho + qb * (QB * dil)
        rows = pl.ds(start, QB) if dil == 1 else pl.ds(start, QB, stride=dil)
        for s in range(n_slab):
            sl = slice(s * LANES, (s + 1) * LANES)
            qn = qall[rho, pl.ds(2 * qb, 2), :, sl].reshape(QB, LANES)
            es, ms = [], []
            for hh in range(2):
                kw = kall[hh, rho, pl.ds(2 * qb, 4), :, sl].reshape(KW, LANES)
                sc = lax.dot_general(qn, kw, (((1,), (1,)), ((), ())), preferred_element_type=F32)
                sc = sc + bvar[var, 2 * s + hh]
                m = jnp.max(sc, axis=-1, keepdims=True)
                es.append(jnp.exp2(sc - m).astype(BF16))
                ms.append(m)
            v_even = vall[0, rho, pl.ds(2 * qb, 4), :, sl].reshape(KW, LANES)
            v_odd = vall[1, rho, pl.ds(2 * qb, 4), :, sl].reshape(KW, LANES)
            rhs = jnp.concatenate([jnp.concatenate([v_even, ones_even], axis=1),
                                   jnp.concatenate([v_odd, ones_odd], axis=1)], axis=0)
            res = jnp.dot(jnp.concatenate(es, axis=1), rhs, preferred_element_type=F32)
            den = res[:, LANES:]
            oacc[s, rows, :] = res[:, :LANES] * (1.0 / den)
            lse_ref[s, rows, :] = (jnp.where(lo, ms[0], ms[1]) + jnp.log2(den)) * LN2
        return carry

    lax.fori_loop(0, dil * nqb, body, 0, unroll=2)

    for s in range(n_slab):
        o_ref[:, s * LANES:(s + 1) * LANES] = oacc[s].astype(BF16)


def _attention_group(z, bias_g, gq, gk, first, last, gi, dil):
    nb = SEG // dil // ATT_BLOCK
    z5 = z.reshape(N_SEG, dil, nb, ATT_BLOCK, N_IN)
    cq = (COL_ATT + 3 * gi * ATT_W) // ATT_W
    ck, cv = cq + 1, cq + 2
    main = lambda cb: pl.BlockSpec((None, dil, nb, ATT_BLOCK, ATT_W), lambda c, f, l: (c, 0, 0, 0, cb))
    prev = lambda cb: pl.BlockSpec((None, dil, 1, ATT_BLOCK, ATT_W),
                                   lambda c, f, l: (c - 1 + f[c], 0, nb - 1, 0, cb), pipeline_mode=pl.Buffered(1))
    nxt = lambda cb: pl.BlockSpec((None, dil, 1, ATT_BLOCK, ATT_W),
                                  lambda c, f, l: (c + 1 - l[c], 0, 0, 0, cb), pipeline_mode=pl.Buffered(1))
    grid_spec = pltpu.PrefetchScalarGridSpec(
        num_scalar_prefetch=2,
        grid=(N_SEG,),
        in_specs=[
            main(cq), prev(ck), main(ck), nxt(ck), prev(cv), main(cv), nxt(cv),
            pl.BlockSpec((ATT_HEADS, QB, KW), lambda c, f, l: (0, 0, 0)),
            pl.BlockSpec((1, LANES), lambda c, f, l: (0, 0)),
            pl.BlockSpec((1, LANES), lambda c, f, l: (0, 0)),
        ],
        out_specs=[
            pl.BlockSpec((SEG, ATT_W), lambda c, f, l: (c, 0)),
            pl.BlockSpec((ATT_W // LANES, SEG, LANES), lambda c, f, l: (0, c, 0)),
        ],
        scratch_shapes=[
            pltpu.VMEM((2, dil, nb + 2, ATT_BLOCK, ATT_W), BF16),
            pltpu.VMEM((2, dil, nb + 2, ATT_BLOCK, ATT_W), BF16),
            pltpu.VMEM((dil, nb, ATT_BLOCK, ATT_W), BF16),
            pltpu.VMEM((ATT_W // LANES, SEG, LANES), F32),
            pltpu.VMEM((4, ATT_HEADS, QB, KW), F32),
        ],
    )
    return pl.pallas_call(
        functools.partial(_attention_kernel, dil=dil),
        grid_spec=grid_spec,
        out_shape=[jax.ShapeDtypeStruct((T_ALL, ATT_W), BF16), jax.ShapeDtypeStruct((ATT_W // LANES, T_ALL, LANES), F32)],
        compiler_params=_cparams(("arbitrary",), ATT_VMEM_LIMIT),
        name=f"attention_d{dil}",
    )(first, last, z5, z5, z5, z5, z5, z5, z5, bias_g, gq, gk)


def _merge_kernel(yret_ref, o0_ref, o1_ref, o2_ref, l0_ref, l1_ref, l2_ref,
                  gret_a_ref, gret_b_ref, gatt_a_ref, gatt_b_ref, x_ref,
                  wret_ref, watt_ref, wout_ref, nffn_ref, wr_ref, wrhi_ref, br_ref,
                  x1_ref, hpa_ref, hpb_ref, idx_ref, gate_ref, rank_ref, cnt_ref, carry_ref):
    i = pl.program_id(0)
    tm = MERGE_TM

    @pl.when(i == 0)
    def _():
        carry_ref[...] = jnp.zeros_like(carry_ref)

    l0, l1, l2 = [jnp.concatenate([r[s] for s in range(ATT_W // LANES)], axis=-1) for r in (l0_ref, l1_ref, l2_ref)]
    lm = jnp.maximum(jnp.maximum(l0, l1), l2)
    e0, e1, e2 = jnp.exp(l0 - lm), jnp.exp(l1 - lm), jnp.exp(l2 - lm)
    inv = 1.0 / (e0 + e1 + e2)
    y_att = ((e0 * inv) * o0_ref[...].astype(F32) + (e1 * inv) * o1_ref[...].astype(F32)
             + (e2 * inv) * o2_ref[...].astype(F32))

    p_ret = jnp.dot(yret_ref[...], wret_ref[...], preferred_element_type=F32)
    p_att = jnp.dot(y_att.astype(BF16), watt_ref[...], preferred_element_type=F32)
    g_ret = jnp.concatenate([gret_a_ref[...], gret_b_ref[...]], axis=-1).astype(F32)
    g_att = jnp.concatenate([gatt_a_ref[...], gatt_b_ref[...]], axis=-1).astype(F32)
    merged = _sigmoid(g_ret) * p_ret + _sigmoid(g_att) * p_att
    x1 = x_ref[...] + jnp.dot(merged.astype(BF16), wout_ref[...], preferred_element_type=F32)
    x1_ref[...] = x1

    ms = jnp.mean(x1 * x1, axis=-1, keepdims=True)
    h2 = x1 * lax.rsqrt(ms + EPS) * nffn_ref[...]
    hpa_ref[...], hpb_ref[...] = _pack_row_halves(h2)

    h_hi = h2.astype(BF16)
    h_lo = (h2 - h_hi.astype(F32)).astype(BF16)
    p1 = jnp.dot(h_hi, wr_ref[...], preferred_element_type=F32)
    p2 = jnp.dot(h_lo, wrhi_ref[...], preferred_element_type=F32)
    logits = p1 + pltpu.roll(p1, LANES - N_EXPERTS, axis=1) + p2 + br_ref[...]
    lane = lax.broadcasted_iota(jnp.int32, (tm, LANES), 1)
    lane_f = lane.astype(F32)
    work = logits
    vals, idxs = [], []
    for _ in range(TOP_K):
        m = jnp.max(work, axis=-1, keepdims=True)
        ix = jnp.min(jnp.where(work == m, lane_f, float(LANES)), axis=-1, keepdims=True)
        vals.append(m)
        idxs.append(ix)
        work = jnp.where(lane_f == ix, -3e38, work)
    es = [jnp.exp(v - vals[0]) for v in vals]
    den = es[0] + es[1] + es[2] + es[3]
    onehot = jnp.zeros((tm, LANES), F32)
    for ix in idxs:
        onehot = onehot + jnp.where(lane_f == ix, 1.0, 0.0)
    row = lax.broadcasted_iota(jnp.int32, (tm, tm), 0)
    colm = lax.broadcasted_iota(jnp.int32, (tm, tm), 1)
    tri = jnp.where(colm < row, 1.0, 0.0).astype(BF16)
    before = jnp.dot(tri, onehot.astype(BF16), preferred_element_type=F32) + carry_ref[...]
    idx_out = jnp.zeros((tm, LANES), F32)
    gate_out = jnp.zeros((tm, LANES), F32)
    rank_out = jnp.zeros((tm, LANES), F32)
    for k in range(TOP_K):
        rk = jnp.sum(jnp.where(lane_f == idxs[k], before, 0.0), axis=-1, keepdims=True)
        sel = lane == k
        idx_out = jnp.where(sel, idxs[k], idx_out)
        gate_out = jnp.where(sel, es[k] / den, gate_out)
        rank_out = jnp.where(sel, rk, rank_out)
    idx_ref[...] = idx_out.astype(jnp.int32)
    gate_ref[...] = gate_out
    rank_ref[...] = rank_out.astype(jnp.int32)
    total = carry_ref[...] + jnp.sum(onehot, axis=0, keepdims=True)
    carry_ref[...] = total
    cnt_ref[...] = jnp.broadcast_to(total, cnt_ref.shape)


def _merge(y_ret, o_list, lse_list, z, x_g, row0, w_ret, w_att, w_out, n_ffn, w_router, w_router_hi, b_router):
    tm = MERGE_TM
    off = row0 // tm
    row = lambda w: pl.BlockSpec((tm, w), lambda i: (i + off, 0))
    out_row = lambda w: pl.BlockSpec((tm, w), lambda i: (i, 0))
    full = lambda a: pl.BlockSpec(a.shape, lambda i: (0,) * a.ndim)
    zcol = lambda col: pl.BlockSpec((tm, COL_BLK), lambda i: (i + off, col // COL_BLK))
    lse_spec = pl.BlockSpec((ATT_W // LANES, tm, LANES), lambda i: (0, i + off, 0))
    return pl.pallas_call(
        _merge_kernel,
        grid=(T_GROUP // tm,),
        in_specs=[row(RET_V_W), row(ATT_W), row(ATT_W), row(ATT_W), lse_spec, lse_spec, lse_spec,
                  zcol(COL_GATE_RET), zcol(COL_GATE_RET + COL_BLK), zcol(COL_GATE_ATT),
                  zcol(COL_GATE_ATT + COL_BLK), out_row(D_MODEL),
                  full(w_ret), full(w_att), full(w_out), full(n_ffn), full(w_router), full(w_router_hi),
                  full(b_router)],
        out_specs=[out_row(D_MODEL), out_row(HALF_W), out_row(HALF_W), out_row(LANES), out_row(LANES), out_row(LANES),
                   pl.BlockSpec((8, LANES), lambda i: (0, 0))],
        out_shape=[jax.ShapeDtypeStruct((T_GROUP, D_MODEL), F32),
                   jax.ShapeDtypeStruct((T_GROUP, HALF_W), jnp.uint32),
                   jax.ShapeDtypeStruct((T_GROUP, HALF_W), jnp.uint32),
                   jax.ShapeDtypeStruct((T_GROUP, LANES), jnp.int32),
                   jax.ShapeDtypeStruct((T_GROUP, LANES), F32),
                   jax.ShapeDtypeStruct((T_GROUP, LANES), jnp.int32),
                   jax.ShapeDtypeStruct((8, LANES), F32)],
        scratch_shapes=[pltpu.VMEM((1, LANES), F32)],
        compiler_params=_cparams(("arbitrary",)),
        name="merge_router",
    )(y_ret, *o_list, *lse_list, z, z, z, z, x_g, w_ret, w_att, w_out, n_ffn, w_router, w_router_hi, b_router)


def _sc_mesh():
    return plsc.VectorSubcoreMesh(core_axis_name="core", subcore_axis_name="subcore")


def _sc_scatter_rows(x, idx_kmajor, n_out):
    n_rows, width = x.shape

    @pl.kernel(out_type=jax.ShapeDtypeStruct((n_out, width), x.dtype), mesh=_sc_mesh(), scratch_types=[])
    def scatter(x_hbm, i_hbm, o_hbm):
        def body(x_vmem, i_vmem):
            for k in range(TOP_K):
                pltpu.sync_copy(x_vmem, o_hbm.at[i_vmem.at[k]])

        pltpu.emit_pipeline(
            body,
            grid=(n_rows // SC_WINDOW,),
            in_specs=[pl.BlockSpec((SC_WINDOW, width), lambda i: (i, 0)),
                      pl.BlockSpec((TOP_K, SC_WINDOW), lambda i: (0, i))],
            out_specs=[],
            core_axis_name=("core", "subcore"),
            dimension_semantics=(pltpu.PARALLEL,),
        )(x_hbm, i_hbm)

    return scatter(x, idx_kmajor)


def _sc_gather_rows(data, idx):
    n_idx = idx.shape[0]
    width = data.shape[1]

    @pl.kernel(out_type=jax.ShapeDtypeStruct((n_idx, width), data.dtype), mesh=_sc_mesh(), scratch_types=[])
    def gather(x_hbm, i_hbm, o_hbm):
        def body(i_vmem, o_vmem):
            pltpu.sync_copy(x_hbm.at[i_vmem.at[0]], o_vmem)

        pltpu.emit_pipeline(
            body,
            grid=(n_idx // SC_WINDOW,),
            in_specs=[pl.BlockSpec((1, SC_WINDOW), lambda i: (0, i))],
            out_specs=[pl.BlockSpec((SC_WINDOW, width), lambda i: (i, 0))],
            core_axis_name=("core", "subcore"),
            dimension_semantics=(pltpu.PARALLEL,),
        )(i_hbm, o_hbm)

    return gather(data, idx.reshape(1, n_idx))


def _expert_kernel(be_ref, nused_ref, nvalid_ref, xa_ref, xb_ref, wg_ref, bg_ref, wu_ref, bu_ref, wd_ref, bd_ref,
                   ya_ref, yb_ref, wbf_ref):
    b = pl.program_id(0)
    active = b < nused_ref[0]
    new_expert = (b == 0) | (be_ref[b] != be_ref[jnp.maximum(b - 1, 0)])

    @pl.when(active & new_expert)
    def _():
        rows = 128
        for wi, w_ref in enumerate((wg_ref, wu_ref, wd_ref)):
            for r in range(0, D_MODEL, rows):
                wbf_ref[wi, r:r + rows, :] = w_ref[0, r:r + rows, :].astype(BF16)

    @pl.when(active)
    def _():
        valid = lax.broadcasted_iota(jnp.int32, (MOE_BM, HALF_W), 0) < nvalid_ref[b]
        zero = jnp.zeros((MOE_BM, HALF_W), jnp.uint32)
        x = _unpack_row_halves(jnp.where(valid, xa_ref[...], zero), jnp.where(valid, xb_ref[...], zero)).astype(BF16)
        g = jnp.dot(x, wbf_ref[0], preferred_element_type=F32) + bg_ref[0]
        u = jnp.dot(x, wbf_ref[1], preferred_element_type=F32) + bu_ref[0]
        g = jnp.minimum(g, SWIGLU_LIMIT)
        u = jnp.clip(u, -SWIGLU_LIMIT, SWIGLU_LIMIT)
        glu = g * jax.nn.sigmoid(SWIGLU_ALPHA * g)
        act = ((u + 1.0) * glu).astype(BF16)
        y = jnp.dot(act, wbf_ref[2], preferred_element_type=F32) + bd_ref[0]
        ya_ref[...], yb_ref[...] = _pack_row_halves(y)

    @pl.when(jnp.logical_not(active))
    def _():
        ya_ref[...] = jnp.zeros_like(ya_ref)
        yb_ref[...] = jnp.zeros_like(yb_ref)


def _experts(block_expert, n_used, n_valid, xs_a, xs_b, wg, bg, wu, bu, wd, bd):
    assert D_FF == D_MODEL
    blk = lambda b, be, nu, nv: (jnp.minimum(b, nu[0] - 1), 0)
    wsp = lambda: pl.BlockSpec((1, D_MODEL, D_FF), lambda b, be, nu, nv: (be[b], 0, 0))
    bsp = lambda: pl.BlockSpec((1, 1, D_FF), lambda b, be, nu, nv: (be[b], 0, 0))
    xsp = lambda: pl.BlockSpec((MOE_BM, HALF_W), blk)
    ysp = lambda: pl.BlockSpec((MOE_BM, HALF_W), lambda b, be, nu, nv: (b, 0))
    slot_arr = jax.ShapeDtypeStruct((N_SLOTS, HALF_W), jnp.uint32)
    grid_spec = pltpu.PrefetchScalarGridSpec(
        num_scalar_prefetch=3,
        grid=(N_SLOT_BLOCKS,),
        in_specs=[xsp(), xsp(), wsp(), bsp(), wsp(), bsp(), wsp(), bsp()],
        out_specs=[ysp(), ysp()],
        scratch_shapes=[pltpu.VMEM((3, D_MODEL, D_FF), BF16)],
    )
    return pl.pallas_call(
        _expert_kernel,
        grid_spec=grid_spec,
        out_shape=[slot_arr, slot_arr],
        compiler_params=_cparams(("arbitrary",)),
        name="experts",
    )(block_expert, n_used, n_valid, xs_a, xs_b, wg, bg, wu, bu, wd, bd)


def _final_kernel(x1_ref, yga_ref, ygb_ref, gate_ref, p_ref, nple_ref, wpg_ref, wpp_ref, out_ref):
    x2 = x1_ref[...]
    gates = gate_ref[...]
    for k in range(TOP_K):
        x2 = x2 + gates[:, k:k + 1] * _unpack_row_halves(yga_ref[k], ygb_ref[k])
    ms = jnp.mean(x2 * x2, axis=-1, keepdims=True)
    h3 = (x2 * lax.rsqrt(ms + EPS) * nple_ref[...]).astype(BF16)
    gate = jax.nn.sigmoid(jnp.dot(h3, wpg_ref[...], preferred_element_type=F32))
    proj = jnp.dot(p_ref[...].astype(BF16), wpp_ref[...], preferred_element_type=F32)
    out_ref[...] = x2 + gate * proj


def _final(x1, yg_a, yg_b, gates, p, n_ple, w_pg, w_pp):
    tm = FINAL_TM
    n_rows = x1.shape[0]
    full = lambda a: pl.BlockSpec(a.shape, lambda i: (0,) * a.ndim)
    return pl.pallas_call(
        _final_kernel,
        grid=(n_rows // tm,),
        in_specs=[pl.BlockSpec((tm, D_MODEL), lambda i: (i, 0)),
                  pl.BlockSpec((TOP_K, tm, HALF_W), lambda i: (0, i, 0)),
                  pl.BlockSpec((TOP_K, tm, HALF_W), lambda i: (0, i, 0)),
                  pl.BlockSpec((tm, LANES), lambda i: (i, 0)),
                  pl.BlockSpec((tm, PLE_DIM), lambda i: (i, 0)),
                  full(n_ple), full(w_pg), full(w_pp)],
        out_specs=pl.BlockSpec((tm, D_MODEL), lambda i: (i, 0)),
        out_shape=jax.ShapeDtypeStruct((n_rows, D_MODEL), F32),
        compiler_params=_cparams(("arbitrary",)),
        name="final_ple",
    )(x1, yg_a, yg_b, gates, p, n_ple, w_pg, w_pp)


def _rope_tables():
    half = RET_DK // 2
    freq = ROPE_THETA ** (-jnp.arange(half, dtype=F32) / half)
    ang = jnp.arange(SAMPLE_SEQ, dtype=F32)[:, None] * freq[None, :]
    cos, sin = jnp.cos(ang), jnp.sin(ang)
    return jnp.concatenate([cos, cos], axis=-1), jnp.concatenate([-sin, sin], axis=-1)


def _retention_tables(decay_logit):
    lg = jax.nn.log_sigmoid(decay_logit.astype(F32))
    c = RET_CHUNK
    idx = jnp.arange(c, dtype=F32)
    diff = idx[:, None] - idx[None, :]
    lf = lg[0][:, None, None]
    lb = lg[1][:, None, None]
    mask = jnp.where(diff[None] >= 0, jnp.exp(lf * jnp.maximum(diff, 0.0)[None]),
                     jnp.exp(lb * jnp.maximum(-diff, 0.0)[None]))
    kdec_f = jnp.exp(lg[0][:, None] * (c - 1.0 - idx)[None, :])
    qdec_f = jnp.exp(lg[0][:, None] * (idx + 1.0)[None, :])
    kdec_b = jnp.exp(lg[1][:, None] * idx[None, :])
    qdec_b = jnp.exp(lg[1][:, None] * (c - idx)[None, :])
    dec = jnp.stack([kdec_f, qdec_f, kdec_b, qdec_b], axis=1)[..., None]
    cdec = jnp.exp(lg * c).T
    cdec = jnp.broadcast_to(cdec[:, :, None, None], (RET_HEADS, 2, 1, RET_DV))
    return mask, dec, cdec


def _t5_bucket(rel):
    half = T5_BUCKETS // 2
    exact = half // 2
    n = np.abs(rel)
    ratio = np.log(np.maximum(n, 1).astype(np.float32) / np.float32(exact)) / np.float32(math.log(T5_MAX_DIST / exact))
    large = exact + (ratio * np.float32(half - exact)).astype(np.int32)
    large = np.minimum(large, half - 1)
    return np.where(rel > 0, half, 0) + np.where(n < exact, n, large)


def _attention_bias(rel_bias, gi, dil, radius):
    qi = np.arange(QB)
    ki = np.arange(KW) - ATT_BLOCK
    rel = ki[None, :] - qi[:, None]
    onehot = jnp.asarray(_t5_bucket(rel * dil)[..., None] == np.arange(T5_BUCKETS), F32)
    tab = rel_bias[:, gi * ATT_HEADS:(gi + 1) * ATT_HEADS].astype(F32)
    bias = jnp.einsum('qkb,bh->hqk', onehot, tab, precision=lax.Precision.HIGHEST)
    return jnp.where(jnp.asarray(np.abs(rel) <= radius)[None], bias * LOG2E, NEG)


def _seq_edge_flags():
    first = np.zeros((N_SEG,), np.int32)
    last = np.zeros((N_SEG,), np.int32)
    first[:N_PROMPT_SEG] = 1
    last[:N_PROMPT_SEG] = 1
    first[N_PROMPT_SEG] = 1
    last[N_SEG - 1] = 1
    return jnp.asarray(first), jnp.asarray(last)


def _pad_lanes(a, value=0.0):
    return jnp.pad(a, ((0, 0), (0, LANES - a.shape[-1])), constant_values=value)


def kernel(x_prompt, x_sample, p_prompt, p_sample, norm_mix_g, w_in, ret_decay_logit, ret_gn_g,
           att_q_norm_g, att_k_norm_g, rel_bias, w_ret_proj, w_att_proj, w_out, norm_ffn_g,
           w_router, b_router, w_gate, b_gate, w_up, b_up, w_down, b_down,
           norm_ple_g, w_ple_gate, w_ple_proj):
    assert norm_mix_g.shape[0] == 1, "one layer"
    x_p = x_prompt.reshape(T_PROMPT, D_MODEL)
    x_s = x_sample.reshape(SAMPLE_SEQ, D_MODEL)

    cos_t, sin_t = _rope_tables()
    z = _in_proj(x_p, x_s, norm_mix_g.astype(F32), w_in[0].astype(BF16), cos_t, sin_t)

    ret_mask, ret_dec, ret_cdec = _retention_tables(ret_decay_logit[0])
    y_ret = _retention(z, ret_mask, ret_dec, ret_cdec, ret_gn_g[0].reshape(RET_HEADS, 1, RET_DV).astype(F32))

    first, last = _seq_edge_flags()
    o_list, lse_list = [], []
    for gi, (window, dil) in enumerate(ATT_GROUPS):
        bias_g = _attention_bias(rel_bias, gi, dil, window // (2 * dil))
        gq = jnp.tile(att_q_norm_g[0, gi].astype(F32) * (ATT_DH ** -0.5 * LOG2E), LANES // ATT_DH)[None, :]
        gk = jnp.tile(att_k_norm_g[0, gi].astype(F32), LANES // ATT_DH)[None, :]
        o_g, lse_g = _attention_group(z, bias_g, gq, gk, first, last, gi, dil)
        o_list.append(o_g)
        lse_list.append(lse_g)

    w_r = w_router[0].astype(F32)
    w_r_hi = w_r.astype(BF16)
    w_r_lo = (w_r - w_r_hi.astype(F32)).astype(BF16)
    w_router_cat = _pad_lanes(jnp.concatenate([w_r_hi, w_r_lo], axis=1))
    w_router_hi = _pad_lanes(w_r_hi)
    b_router_p = _pad_lanes(b_router.astype(F32), NEG)
    w_ret_bf, w_att_bf, w_out_bf = w_ret_proj[0].astype(BF16), w_att_proj[0].astype(BF16), w_out[0].astype(BF16)
    n_ffn = norm_ffn_g.astype(F32)
    b_g = b_gate[0].reshape(N_EXPERTS, 1, D_FF).astype(F32)
    b_u = b_up[0].reshape(N_EXPERTS, 1, D_FF).astype(F32)
    b_d = b_down[0].reshape(N_EXPERTS, 1, D_MODEL).astype(F32)
    n_ple = norm_ple_g.astype(F32)
    w_pg = w_ple_gate[0].astype(BF16)
    w_pp = w_ple_proj[0].astype(BF16)
    expert_ids = jnp.arange(N_EXPERTS, dtype=jnp.int32)
    blk_row0 = jnp.arange(N_SLOT_BLOCKS, dtype=jnp.int32) * MOE_BM

    routed = []
    for x_g, row0 in ((x_p, 0), (x_s, T_PROMPT)):
        x1, hp_a, hp_b, idx, gates, rank, cnt = _merge(
            y_ret, o_list, lse_list, z, x_g, row0, w_ret_bf, w_att_bf, w_out_bf, n_ffn,
            w_router_cat, w_router_hi, b_router_p)
        counts = cnt[0, :N_EXPERTS].astype(jnp.int32)
        padded = (counts + MOE_BM - 1) // MOE_BM * MOE_BM
        pad_end = jnp.cumsum(padded)
        pad_start = pad_end - padded
        top_idx = idx[:, :TOP_K]
        start_of = jnp.sum(jnp.where(top_idx[:, :, None] == expert_ids, pad_start, 0), axis=-1)
        dest_kmajor = (start_of + rank[:, :TOP_K]).T
        n_used = (pad_end[-1] // MOE_BM).astype(jnp.int32).reshape(1)
        block_expert = jnp.minimum(jnp.sum((pad_end[None, :] <= blk_row0[:, None]).astype(jnp.int32), axis=1),
                                   N_EXPERTS - 1).astype(jnp.int32)
        slot_end = jnp.sum(jnp.where(block_expert[:, None] == expert_ids, pad_start + counts, 0), axis=-1)
        n_valid = jnp.clip(slot_end - blk_row0, 0, MOE_BM).astype(jnp.int32)
        xs_a = _sc_scatter_rows(hp_a, dest_kmajor, N_SLOTS)
        xs_b = _sc_scatter_rows(hp_b, dest_kmajor, N_SLOTS)
        routed.append((x1, gates, dest_kmajor, block_expert, n_used, n_valid, xs_a, xs_b))

    outs = []
    for (x1, gates, dest_kmajor, block_expert, n_used, n_valid, xs_a, xs_b), p_g in zip(
            routed, (p_prompt[0].reshape(T_PROMPT, PLE_DIM), p_sample[0].reshape(SAMPLE_SEQ, PLE_DIM))):
        ys_a, ys_b = _experts(block_expert, n_used, n_valid, xs_a, xs_b, w_gate[0], b_g, w_up[0], b_u, w_down[0], b_d)
        dest_flat = dest_kmajor.reshape(-1)
        yg_a = _sc_gather_rows(ys_a, dest_flat).reshape(TOP_K, T_GROUP, HALF_W)
        yg_b = _sc_gather_rows(ys_b, dest_flat).reshape(TOP_K, T_GROUP, HALF_W)
        outs.append(_final(x1, yg_a, yg_b, gates, p_g, n_ple, w_pg, w_pp))
    y_p, y_s = outs
    return (y_p.reshape(x_prompt.shape), y_s.reshape(x_sample.shape))
```

```python
import functools
import math

import jax
import jax.numpy as jnp
import numpy as np
from jax import lax
from jax.experimental import pallas as pl
from jax.experimental.pallas import tpu as pltpu
from jax.experimental.pallas import tpu_sc as plsc

F32 = jnp.float32
BF16 = jnp.bfloat16

D_MODEL = 1024
N_PROMPT_SEQ = 8
PROMPT_SEQ = 2048
SAMPLE_SEQ = 16384
T_PROMPT = N_PROMPT_SEQ * PROMPT_SEQ
T_ALL = T_PROMPT + SAMPLE_SEQ

RET_HEADS = 4
RET_DK = 128
RET_DV = 256
RET_CHUNK = 128
ROPE_THETA = 10000.0
ATT_GROUPS = ((128, 1), (512, 4), (2048, 16))
N_GROUPS = 3
ATT_HEADS = 8
ATT_DH = 64
ATT_BLOCK = 64
ATT_W = ATT_HEADS * ATT_DH
T5_BUCKETS = 32
T5_MAX_DIST = 1024
N_EXPERTS = 32
TOP_K = 4
D_FF = 1024
SWIGLU_ALPHA = 1.702
SWIGLU_LIMIT = 7.0
PLE_DIM = 256
EPS = 1e-6

RET_QK_W = RET_HEADS * RET_DK
RET_V_W = RET_HEADS * RET_DV
N_IN = 2 * RET_QK_W + 2 * RET_V_W + 3 * N_GROUPS * ATT_W + 2 * D_MODEL

COL_RQ = 0
COL_RK = RET_QK_W
COL_RV = 2 * RET_QK_W
COL_RG = COL_RV + RET_V_W
COL_ATT = COL_RG + RET_V_W
COL_GATE_RET = COL_ATT + 3 * N_GROUPS * ATT_W
COL_GATE_ATT = COL_GATE_RET + D_MODEL

LANES = 128
VMEM_LIMIT = 56 * 1024 * 1024
ATT_VMEM_LIMIT = 58 * 1024 * 1024

SEG = 2048
N_SEG = T_ALL // SEG
N_PROMPT_SEG = T_PROMPT // SEG
COL_BLK = 512
N_COL_BLK = N_IN // COL_BLK
QB = 128
KW = 256
NEG = -1e30
LOG2E = math.log2(math.e)
LN2 = math.log(2.0)
MERGE_TM = 512
FINAL_TM = 512
MOE_BM = 512
N_SLOT_BLOCKS = T_ALL * TOP_K // MOE_BM + N_EXPERTS
N_SLOTS = N_SLOT_BLOCKS * MOE_BM
HALF_W = D_MODEL // 4
SC_WINDOW = 128


def _cparams(sem, vmem=VMEM_LIMIT):
    return pltpu.CompilerParams(dimension_semantics=sem, vmem_limit_bytes=vmem)


def _sigmoid(x):
    return 0.5 * jnp.tanh(0.5 * x) + 0.5


def _pack_bf16_pair(x):
    w = x.shape[-1] // 2
    hi = pltpu.bitcast(x[:, :w].astype(BF16).astype(F32), jnp.uint32)
    lo = pltpu.bitcast(x[:, w:].astype(BF16).astype(F32), jnp.uint32)
    return hi | (lo >> 16)


def _unpack_bf16_pair(p):
    hi = pltpu.bitcast(p & jnp.uint32(0xFFFF0000), F32)
    lo = pltpu.bitcast(p << 16, F32)
    return jnp.concatenate([hi, lo], axis=-1)


def _pack_row_halves(x):
    half = x.shape[-1] // 2
    return _pack_bf16_pair(x[:, :half]), _pack_bf16_pair(x[:, half:])


def _unpack_row_halves(pa, pb):
    return jnp.concatenate([_unpack_bf16_pair(pa), _unpack_bf16_pair(pb)], axis=-1)


def _in_proj_kernel(xp_ref, xs_ref, g_ref, w_ref, cos_ref, sin_ref, z_ref, h_ref, p_ref, p2_ref):
    i = pl.program_id(0)
    j = pl.program_id(1)

    def norm_into_h(x_ref):
        xf = x_ref[...]
        ms = jnp.mean(xf * xf, axis=-1, keepdims=True)
        h_ref[...] = (xf * lax.rsqrt(ms + EPS) * g_ref[...]).astype(BF16)

    @pl.when((j == 0) & (i < N_PROMPT_SEG))
    def _():
        norm_into_h(xp_ref)

    @pl.when((j == 0) & (i >= N_PROMPT_SEG))
    def _():
        norm_into_h(xs_ref)

    n_slab = COL_BLK // LANES

    def project():
        return jnp.dot(h_ref[...], w_ref[...], preferred_element_type=F32)

    is_rope = j < (COL_RV // COL_BLK)
    att0 = COL_ATT // COL_BLK
    is_d4 = (j >= att0 + 3) & (j < att0 + 6)
    is_d16 = (j >= att0 + 6) & (j < att0 + 9)

    @pl.when(is_rope)
    def _():
        acc = project()
        scale = jnp.where(j == COL_RK // COL_BLK, RET_DK ** -0.5, 1.0).astype(F32)
        c = cos_ref[...]
        sn = sin_ref[...]
        for s in range(n_slab):
            xs = acc[:, s * LANES:(s + 1) * LANES]
            r = xs * c + pltpu.roll(xs, RET_DK // 2, axis=1) * sn
            z_ref[:, s * LANES:(s + 1) * LANES] = (r * scale).astype(BF16)

    @pl.when(is_d4)
    def _():
        acc = project()
        for s in range(n_slab):
            p_ref[s] = acc[:, s * LANES:(s + 1) * LANES]
        rows = SEG // 4
        for rho in range(4):
            for s in range(n_slab):
                piece = p_ref[s, pl.ds(rho, rows, stride=4), :]
                z_ref[rho * rows:(rho + 1) * rows, s * LANES:(s + 1) * LANES] = piece.astype(BF16)

    @pl.when(is_d16)
    def _():
        acc = project()
        for s in range(n_slab):
            p_ref[s] = acc[:, s * LANES:(s + 1) * LANES]
        quarter = SEG // 4
        rows = SEG // 16
        for r4 in range(4):
            for s in range(n_slab):
                p2_ref[s, r4 * quarter:(r4 + 1) * quarter, :] = p_ref[s, pl.ds(r4, quarter, stride=4), :]
        for r4 in range(4):
            for hi in range(4):
                rho = 4 * hi + r4
                for s in range(n_slab):
                    piece = p2_ref[s, pl.ds(r4 * quarter + hi, rows, stride=4), :]
                    z_ref[rho * rows:(rho + 1) * rows, s * LANES:(s + 1) * LANES] = piece.astype(BF16)

    @pl.when(jnp.logical_not(is_rope | is_d4 | is_d16))
    def _():
        z_ref[...] = project().astype(BF16)


def _in_proj(x_p, x_s, norm_g, w_in_bf, cos_t, sin_t):
    def pos_blk(i, j):
        return (jnp.maximum(i - N_PROMPT_SEG, 0), 0)

    return pl.pallas_call(
        _in_proj_kernel,
        grid=(N_SEG, N_COL_BLK),
        in_specs=[
            pl.BlockSpec((SEG, D_MODEL), lambda i, j: (jnp.minimum(i, N_PROMPT_SEG - 1), 0),
                         pipeline_mode=pl.Buffered(1)),
            pl.BlockSpec((SEG, D_MODEL), pos_blk, pipeline_mode=pl.Buffered(1)),
            pl.BlockSpec((1, D_MODEL), lambda i, j: (0, 0)),
            pl.BlockSpec((D_MODEL, COL_BLK), lambda i, j: (0, j)),
            pl.BlockSpec((SEG, LANES), pos_blk),
            pl.BlockSpec((SEG, LANES), pos_blk),
        ],
        out_specs=pl.BlockSpec((SEG, COL_BLK), lambda i, j: (i, j)),
        out_shape=jax.ShapeDtypeStruct((T_ALL, N_IN), BF16),
        scratch_shapes=[
            pltpu.VMEM((SEG, D_MODEL), BF16),
            pltpu.VMEM((COL_BLK // LANES, SEG, LANES), F32),
            pltpu.VMEM((COL_BLK // LANES, SEG, LANES), F32),
        ],
        compiler_params=_cparams(("arbitrary", "arbitrary")),
        name="in_proj",
    )(x_p, x_s, norm_g, w_in_bf, cos_t, sin_t)


RET_CHUNKS_PER_SEG = SEG // RET_CHUNK
RET_MAX_CHUNKS = SAMPLE_SEQ // RET_CHUNK
RET_GROUP = 8


def _retention_kernel(seg_ref, phase_ref, reset_ref, cbase_ref,
                      q_ref, k_ref, v_ref, g_ref, mask_ref, dec_ref, cdec_ref, gn_ref,
                      y_ref, sb_ref, sf_ref, sr_ref):
    step = pl.program_id(1)
    phase = phase_ref[step]
    reset = reset_ref[step]
    cbase = cbase_ref[step]
    kdec_f = dec_ref[0, 0]
    qdec_f = dec_ref[0, 1]
    kdec_b = dec_ref[0, 2]
    qdec_b = dec_ref[0, 3]
    cd_f = cdec_ref[0, 0]
    cd_b = cdec_ref[0, 1]

    def kv_outer(kd, v):
        return lax.dot_general(kd, v, (((0,), (0,)), ((), ())), preferred_element_type=F32)

    @pl.when((phase == 0) & (reset == 1))
    def _():
        sr_ref[...] = jnp.zeros_like(sr_ref)

    @pl.when((phase == 1) & (reset == 1))
    def _():
        sf_ref[...] = jnp.zeros_like(sf_ref)

    n_groups = RET_CHUNKS_PER_SEG // RET_GROUP

    def chunk_rows(c):
        return pl.ds(pl.multiple_of(c * RET_CHUNK, RET_CHUNK), RET_CHUNK)

    @pl.when(phase == 0)
    def _():
        def body(it, carry):
            top = RET_CHUNKS_PER_SEG - 1 - it * RET_GROUP
            kvs = []
            for j in range(RET_GROUP):
                rows = chunk_rows(top - j)
                kd = (k_ref[rows, :].astype(F32) * kdec_b).astype(BF16)
                kvs.append(kv_outer(kd, v_ref[rows, :]))
            state = sr_ref[...]
            for j in range(RET_GROUP):
                sb_ref[cbase + top - j] = state.astype(BF16)
                state = cd_b * state + kvs[j]
            sr_ref[...] = state
            return carry

        lax.fori_loop(0, n_groups, body, 0)

    @pl.when(phase == 1)
    def _():
        msk = mask_ref[0]
        gn = gn_ref[0]

        def body(it, carry):
            c0 = it * RET_GROUP
            lhs, vs, kvs = [], [], []
            for j in range(RET_GROUP):
                rows = chunk_rows(c0 + j)
                qb = q_ref[rows, :]
                kb = k_ref[rows, :]
                v = v_ref[rows, :]
                q = qb.astype(F32)
                s = lax.dot_general(qb, kb, (((1,), (1,)), ((), ())), preferred_element_type=F32)
                lhs.append(jnp.concatenate(
                    [(s * msk).astype(BF16), (q * qdec_f).astype(BF16), (q * qdec_b).astype(BF16)], axis=-1))
                vs.append(v)
                kvs.append(kv_outer((kb.astype(F32) * kdec_f).astype(BF16), v))
            state = sf_ref[...]
            for j in range(RET_GROUP):
                c = c0 + j
                rhs = jnp.concatenate([vs[j], state.astype(BF16), sb_ref[cbase + c]], axis=0)
                o = jnp.dot(lhs[j], rhs, preferred_element_type=F32)
                state = cd_f * state + kvs[j]
                mu = jnp.mean(o, axis=-1, keepdims=True)
                oc = o - mu
                var = jnp.mean(oc * oc, axis=-1, keepdims=True)
                on = oc * lax.rsqrt(var + EPS) * gn
                rows = chunk_rows(c)
                gate = g_ref[rows, :].astype(F32)
                y_ref[rows, :] = (gate * jax.nn.sigmoid(gate) * on).astype(BF16)
            sf_ref[...] = state
            return carry

        lax.fori_loop(0, n_groups, body, 0)


def _retention_schedule():
    seg, phase, reset, cbase = [], [], [], []
    for p in range(N_PROMPT_SEG):
        for ph in (0, 1):
            seg.append(p); phase.append(ph); reset.append(1); cbase.append(0)
    n_s = N_SEG - N_PROMPT_SEG
    for i in range(n_s):
        t = n_s - 1 - i
        seg.append(N_PROMPT_SEG + t); phase.append(0); reset.append(int(i == 0)); cbase.append(t * RET_CHUNKS_PER_SEG)
    for t in range(n_s):
        seg.append(N_PROMPT_SEG + t); phase.append(1); reset.append(int(t == 0)); cbase.append(t * RET_CHUNKS_PER_SEG)
    hold = list(seg)
    for i in range(len(seg)):
        if phase[i] == 0:
            nxt = next(j for j in range(i + 1, len(seg)) if phase[j] == 1)
            hold[i] = seg[nxt]
    arr = lambda a: jnp.asarray(np.asarray(a, np.int32))
    return arr(seg), arr(phase), arr(reset), arr(cbase), arr(hold)


def _retention(z, ret_mask, ret_dec, ret_cdec, gn_g):
    seg, phase, reset, cbase, hold = _retention_schedule()
    n_steps = int(seg.shape[0])
    qk_blk = lambda col: (lambda h, s, seg_r, ph_r, rs_r, cb_r, hold_r: (seg_r[s], col // RET_DK + h))
    hold_blk = lambda col, w: (lambda h, s, seg_r, ph_r, rs_r, cb_r, hold_r: (hold_r[s], col // w + h))
    v_blk = lambda h, s, seg_r, ph_r, rs_r, cb_r, hold_r: (seg_r[s], COL_RV // RET_DV + h)
    per_head = lambda h, s, *_: (h, 0, 0)
    per_head4 = lambda h, s, *_: (h, 0, 0, 0)
    grid_spec = pltpu.PrefetchScalarGridSpec(
        num_scalar_prefetch=5,
        grid=(RET_HEADS, n_steps),
        in_specs=[
            pl.BlockSpec((SEG, RET_DK), hold_blk(COL_RQ, RET_DK)),
            pl.BlockSpec((SEG, RET_DK), qk_blk(COL_RK)),
            pl.BlockSpec((SEG, RET_DV), v_blk),
            pl.BlockSpec((SEG, RET_DV), hold_blk(COL_RG, RET_DV)),
            pl.BlockSpec((1, RET_CHUNK, RET_CHUNK), per_head),
            pl.BlockSpec((1, 4, RET_CHUNK, 1), per_head4),
            pl.BlockSpec((1, 2, 1, RET_DV), per_head4),
            pl.BlockSpec((1, 1, RET_DV), per_head),
        ],
        out_specs=pl.BlockSpec((SEG, RET_DV), lambda h, s, seg_r, ph_r, rs_r, cb_r, hold_r: (hold_r[s], h)),
        scratch_shapes=[
            pltpu.VMEM((RET_MAX_CHUNKS, RET_DK, RET_DV), BF16),
            pltpu.VMEM((RET_DK, RET_DV), F32),
            pltpu.VMEM((RET_DK, RET_DV), F32),
        ],
    )

    def kernel(seg_r, ph_r, rs_r, cb_r, hold_r, *refs):
        _retention_kernel(seg_r, ph_r, rs_r, cb_r, *refs)

    return pl.pallas_call(
        kernel,
        grid_spec=grid_spec,
        out_shape=jax.ShapeDtypeStruct((T_ALL, RET_V_W), BF16),
        compiler_params=_cparams(("arbitrary", "arbitrary")),
        name="retention",
    )(seg, phase, reset, cbase, hold, z, z, z, z, ret_mask, ret_dec, ret_cdec, gn_g)


def _attention_kernel(first_ref, last_ref, q_ref, kp_ref, km_ref, kn_ref, vp_ref, vm_ref, vn_ref,
                      bias_ref, gq_ref, gk_ref, o_ref, lse_ref, kall, vall, qall, oacc, bvar, *, dil):
    nb = SEG // dil // ATT_BLOCK
    nqb = nb // 2
    n_slab = ATT_W // LANES
    c = pl.program_id(0)
    is_first = first_ref[c]
    is_last = last_ref[c]
    lane = lax.broadcasted_iota(jnp.int32, (1, LANES), 1)
    lo = lane < ATT_DH
    gq = gq_ref[...]
    gk = gk_ref[...]

    @pl.when(c == 0)
    def _():
        col = lax.broadcasted_iota(jnp.int32, (1, KW), 1)
        left = jnp.where(col < ATT_BLOCK, NEG, 0.0).astype(F32)
        right = jnp.where(col >= KW - ATT_BLOCK, NEG, 0.0).astype(F32)
        for h in range(ATT_HEADS):
            b = bias_ref[h]
            bvar[0, h] = b
            bvar[1, h] = b + left
            bvar[2, h] = b + right
            bvar[3, h] = b + left + right

    def head_norm(x, g):
        x2 = x * x
        s_lo = jnp.sum(jnp.where(lo, x2, 0.0), axis=-1, keepdims=True)
        s_hi = jnp.sum(jnp.where(lo, 0.0, x2), axis=-1, keepdims=True)
        ms = jnp.where(lo, s_lo, s_hi) * (1.0 / ATT_DH)
        return x * lax.rsqrt(ms + EPS) * g

    def norm_block(src):
        even, odd = [], []
        for s in range(n_slab):
            xn = head_norm(src[:, s * LANES:(s + 1) * LANES].astype(F32), gk)
            even.append(jnp.where(lo, xn, 0.0).astype(BF16))
            odd.append(jnp.where(lo, 0.0, xn).astype(BF16))
        return jnp.concatenate(even, axis=-1), jnp.concatenate(odd, axis=-1)

    lo_wide = lax.broadcasted_iota(jnp.int32, (1, ATT_W), 1) % LANES < ATT_DH

    def split_heads(v):
        zero = jnp.zeros_like(v)
        return jnp.where(lo_wide, v, zero), jnp.where(lo_wide, zero, v)

    def fill_main(it, carry):
        rho = it // nb
        blk = it % nb
        kall[0, rho, blk + 1], kall[1, rho, blk + 1] = norm_block(km_ref[rho, blk])
        vall[0, rho, blk + 1], vall[1, rho, blk + 1] = split_heads(vm_ref[rho, blk])
        qsrc = q_ref[rho, blk]
        qall[rho, blk] = jnp.concatenate(
            [head_norm(qsrc[:, s * LANES:(s + 1) * LANES].astype(F32), gq).astype(BF16) for s in range(n_slab)], axis=-1)
        return carry

    lax.fori_loop(0, dil * nb, fill_main, 0, unroll=4)

    def fill_halo(rho, carry):
        kall[0, rho, 0], kall[1, rho, 0] = norm_block(kp_ref[rho, 0])
        kall[0, rho, nb + 1], kall[1, rho, nb + 1] = norm_block(kn_ref[rho, 0])
        vall[0, rho, 0], vall[1, rho, 0] = split_heads(vp_ref[rho, 0])
        vall[0, rho, nb + 1], vall[1, rho, nb + 1] = split_heads(vn_ref[rho, 0])
        return carry

    lax.fori_loop(0, dil, fill_halo, 0)

    ones_even = jnp.broadcast_to(jnp.where(lo, 1.0, 0.0).astype(BF16), (KW, LANES))
    ones_odd = jnp.broadcast_to(jnp.where(lo, 0.0, 1.0).astype(BF16), (KW, LANES))

    def body(it, carry):
        rho = it // nqb
        qb = it % nqb
        var = (jnp.where((qb == 0) & (is_first == 1), 1, 0)
               + jnp.where((qb == nqb - 1) & (is_last == 1), 2, 0))
        start = rho + qb * (QB * dil)
        rows = pl.ds(start, QB) if dil == 1 else pl.ds(start, QB, stride=dil)
        for s in range(n_slab):
            sl = slice(s * LANES, (s + 1) * LANES)
            qn = qall[rho, pl.ds(2 * qb, 2), :, sl].reshape(QB, LANES)
            es, ms = [], []
            for hh in range(2):
                kw = kall[hh, rho, pl.ds(2 * qb, 4), :, sl].reshape(KW, LANES)
                sc = lax.dot_general(qn, kw, (((1,), (1,)), ((), ())), preferred_element_type=F32)
                sc = sc + bvar[var, 2 * s + hh]
                m = jnp.max(sc, axis=-1, keepdims=True)
                es.append(jnp.exp2(sc - m).astype(BF16))
                ms.append(m)
            v_even = vall[0, rho, pl.ds(2 * qb, 4), :, sl].reshape(KW, LANES)
            v_odd = vall[1, rho, pl.ds(2 * qb, 4), :, sl].reshape(KW, LANES)
            rhs = jnp.concatenate([jnp.concatenate([v_even, ones_even], axis=1),
                                   jnp.concatenate([v_odd, ones_odd], axis=1)], axis=0)
            res = jnp.dot(jnp.concatenate(es, axis=1), rhs, preferred_element_type=F32)
            den = res[:, LANES:]
            oacc[s, rows, :] = res[:, :LANES] * (1.0 / den)
            lse_ref[s, rows, :] = (jnp.where(lo, ms[0], ms[1]) + jnp.log2(den)) * LN2
        return carry

    lax.fori_loop(0, dil * nqb, body, 0, unroll=4)

    for s in range(n_slab):
        o_ref[:, s * LANES:(s + 1) * LANES] = oacc[s].astype(BF16)


def _attention_group(z, bias_g, gq, gk, first, last, gi, dil):
    nb = SEG // dil // ATT_BLOCK
    z5 = z.reshape(N_SEG, dil, nb, ATT_BLOCK, N_IN)
    cq = (COL_ATT + 3 * gi * ATT_W) // ATT_W
    ck, cv = cq + 1, cq + 2
    main = lambda cb: pl.BlockSpec((None, dil, nb, ATT_BLOCK, ATT_W), lambda c, f, l: (c, 0, 0, 0, cb))
    prev = lambda cb: pl.BlockSpec((None, dil, 1, ATT_BLOCK, ATT_W),
                                   lambda c, f, l: (c - 1 + f[c], 0, nb - 1, 0, cb), pipeline_mode=pl.Buffered(1))
    nxt = lambda cb: pl.BlockSpec((None, dil, 1, ATT_BLOCK, ATT_W),
                                  lambda c, f, l: (c + 1 - l[c], 0, 0, 0, cb), pipeline_mode=pl.Buffered(1))
    grid_spec = pltpu.PrefetchScalarGridSpec(
        num_scalar_prefetch=2,
        grid=(N_SEG,),
        in_specs=[
            main(cq), prev(ck), main(ck), nxt(ck), prev(cv), main(cv), nxt(cv),
            pl.BlockSpec((ATT_HEADS, QB, KW), lambda c, f, l: (0, 0, 0)),
            pl.BlockSpec((1, LANES), lambda c, f, l: (0, 0)),
            pl.BlockSpec((1, LANES), lambda c, f, l: (0, 0)),
        ],
        out_specs=[
            pl.BlockSpec((SEG, ATT_W), lambda c, f, l: (c, 0)),
            pl.BlockSpec((ATT_W // LANES, SEG, LANES), lambda c, f, l: (0, c, 0)),
        ],
        scratch_shapes=[
            pltpu.VMEM((2, dil, nb + 2, ATT_BLOCK, ATT_W), BF16),
            pltpu.VMEM((2, dil, nb + 2, ATT_BLOCK, ATT_W), BF16),
            pltpu.VMEM((dil, nb, ATT_BLOCK, ATT_W), BF16),
            pltpu.VMEM((ATT_W // LANES, SEG, LANES), F32),
            pltpu.VMEM((4, ATT_HEADS, QB, KW), F32),
        ],
    )
    return pl.pallas_call(
        functools.partial(_attention_kernel, dil=dil),
        grid_spec=grid_spec,
        out_shape=[jax.ShapeDtypeStruct((T_ALL, ATT_W), BF16), jax.ShapeDtypeStruct((ATT_W // LANES, T_ALL, LANES), F32)],
        compiler_params=_cparams(("arbitrary",), ATT_VMEM_LIMIT),
        name=f"attention_d{dil}",
    )(first, last, z5, z5, z5, z5, z5, z5, z5, bias_g, gq, gk)


def _merge_kernel(yret_ref, o0_ref, o1_ref, o2_ref, l0_ref, l1_ref, l2_ref,
                  gret_a_ref, gret_b_ref, gatt_a_ref, gatt_b_ref, xp_ref, xs_ref,
                  wret_ref, watt_ref, wout_ref, nffn_ref, wr_ref, wrhi_ref, br_ref,
                  x1_ref, hpa_ref, hpb_ref, idx_ref, gate_ref, rank_ref, cnt_ref, carry_ref):
    i = pl.program_id(0)
    tm = MERGE_TM

    @pl.when(i == 0)
    def _():
        carry_ref[...] = jnp.zeros_like(carry_ref)

    l0, l1, l2 = [jnp.concatenate([r[s] for s in range(ATT_W // LANES)], axis=-1) for r in (l0_ref, l1_ref, l2_ref)]
    lm = jnp.maximum(jnp.maximum(l0, l1), l2)
    e0, e1, e2 = jnp.exp(l0 - lm), jnp.exp(l1 - lm), jnp.exp(l2 - lm)
    inv = 1.0 / (e0 + e1 + e2)
    y_att = ((e0 * inv) * o0_ref[...].astype(F32) + (e1 * inv) * o1_ref[...].astype(F32)
             + (e2 * inv) * o2_ref[...].astype(F32))

    p_ret = jnp.dot(yret_ref[...], wret_ref[...], preferred_element_type=F32)
    p_att = jnp.dot(y_att.astype(BF16), watt_ref[...], preferred_element_type=F32)
    g_ret = jnp.concatenate([gret_a_ref[...], gret_b_ref[...]], axis=-1).astype(F32)
    g_att = jnp.concatenate([gatt_a_ref[...], gatt_b_ref[...]], axis=-1).astype(F32)
    merged = _sigmoid(g_ret) * p_ret + _sigmoid(g_att) * p_att
    x_in = jnp.where(i < T_PROMPT // tm, xp_ref[...], xs_ref[...])
    x1 = x_in + jnp.dot(merged.astype(BF16), wout_ref[...], preferred_element_type=F32)
    x1_ref[...] = x1

    ms = jnp.mean(x1 * x1, axis=-1, keepdims=True)
    h2 = x1 * lax.rsqrt(ms + EPS) * nffn_ref[...]
    hpa_ref[...], hpb_ref[...] = _pack_row_halves(h2)

    h_hi = h2.astype(BF16)
    h_lo = (h2 - h_hi.astype(F32)).astype(BF16)
    p1 = jnp.dot(h_hi, wr_ref[...], preferred_element_type=F32)
    p2 = jnp.dot(h_lo, wrhi_ref[...], preferred_element_type=F32)
    logits = p1 + pltpu.roll(p1, LANES - N_EXPERTS, axis=1) + p2 + br_ref[...]
    lane = lax.broadcasted_iota(jnp.int32, (tm, LANES), 1)
    lane_f = lane.astype(F32)
    work = logits
    vals, idxs = [], []
    for _ in range(TOP_K):
        m = jnp.max(work, axis=-1, keepdims=True)
        ix = jnp.min(jnp.where(work == m, lane_f, float(LANES)), axis=-1, keepdims=True)
        vals.append(m)
        idxs.append(ix)
        work = jnp.where(lane_f == ix, -3e38, work)
    es = [jnp.exp(v - vals[0]) for v in vals]
    den = es[0] + es[1] + es[2] + es[3]
    onehot = jnp.zeros((tm, LANES), F32)
    for ix in idxs:
        onehot = onehot + jnp.where(lane_f == ix, 1.0, 0.0)
    row = lax.broadcasted_iota(jnp.int32, (tm, tm), 0)
    colm = lax.broadcasted_iota(jnp.int32, (tm, tm), 1)
    tri = jnp.where(colm < row, 1.0, 0.0).astype(BF16)
    before = jnp.dot(tri, onehot.astype(BF16), preferred_element_type=F32) + carry_ref[...]
    idx_out = jnp.zeros((tm, LANES), F32)
    gate_out = jnp.zeros((tm, LANES), F32)
    rank_out = jnp.zeros((tm, LANES), F32)
    for k in range(TOP_K):
        rk = jnp.sum(jnp.where(lane_f == idxs[k], before, 0.0), axis=-1, keepdims=True)
        sel = lane == k
        idx_out = jnp.where(sel, idxs[k], idx_out)
        gate_out = jnp.where(sel, es[k] / den, gate_out)
        rank_out = jnp.where(sel, rk, rank_out)
    idx_ref[...] = idx_out.astype(jnp.int32)
    gate_ref[...] = gate_out
    rank_ref[...] = rank_out.astype(jnp.int32)
    total = carry_ref[...] + jnp.sum(onehot, axis=0, keepdims=True)
    carry_ref[...] = total
    cnt_ref[...] = jnp.broadcast_to(total, cnt_ref.shape)


def _merge(y_ret, o_list, lse_list, z, x_p, x_s, w_ret, w_att, w_out, n_ffn, w_router, w_router_hi, b_router):
    tm = MERGE_TM
    n_p = T_PROMPT // tm
    row = lambda w: pl.BlockSpec((tm, w), lambda i: (i, 0))
    full = lambda a: pl.BlockSpec(a.shape, lambda i: (0,) * a.ndim)
    zcol = lambda col: pl.BlockSpec((tm, COL_BLK), lambda i: (i, col // COL_BLK))
    lse_spec = pl.BlockSpec((ATT_W // LANES, tm, LANES), lambda i: (0, i, 0))
    return pl.pallas_call(
        _merge_kernel,
        grid=(T_ALL // tm,),
        in_specs=[row(RET_V_W), row(ATT_W), row(ATT_W), row(ATT_W), lse_spec, lse_spec, lse_spec,
                  zcol(COL_GATE_RET), zcol(COL_GATE_RET + COL_BLK), zcol(COL_GATE_ATT),
                  zcol(COL_GATE_ATT + COL_BLK),
                  pl.BlockSpec((tm, D_MODEL), lambda i: (jnp.minimum(i, n_p - 1), 0)),
                  pl.BlockSpec((tm, D_MODEL), lambda i: (jnp.maximum(i - n_p, 0), 0)),
                  full(w_ret), full(w_att), full(w_out), full(n_ffn), full(w_router), full(w_router_hi),
                  full(b_router)],
        out_specs=[row(D_MODEL), row(HALF_W), row(HALF_W), row(LANES), row(LANES), row(LANES),
                   pl.BlockSpec((8, LANES), lambda i: (0, 0))],
        out_shape=[jax.ShapeDtypeStruct((T_ALL, D_MODEL), F32),
                   jax.ShapeDtypeStruct((T_ALL, HALF_W), jnp.uint32),
                   jax.ShapeDtypeStruct((T_ALL, HALF_W), jnp.uint32),
                   jax.ShapeDtypeStruct((T_ALL, LANES), jnp.int32),
                   jax.ShapeDtypeStruct((T_ALL, LANES), F32),
                   jax.ShapeDtypeStruct((T_ALL, LANES), jnp.int32),
                   jax.ShapeDtypeStruct((8, LANES), F32)],
        scratch_shapes=[pltpu.VMEM((1, LANES), F32)],
        compiler_params=_cparams(("arbitrary",)),
        name="merge_router",
    )(y_ret, *o_list, *lse_list, z, z, z, z, x_p, x_s, w_ret, w_att, w_out, n_ffn, w_router, w_router_hi, b_router)


def _sc_mesh():
    return plsc.VectorSubcoreMesh(core_axis_name="core", subcore_axis_name="subcore")


def _sc_scatter_rows(x, idx_kmajor, n_out):
    n_rows, width = x.shape

    @pl.kernel(out_type=jax.ShapeDtypeStruct((n_out, width), x.dtype), mesh=_sc_mesh(), scratch_types=[])
    def scatter(x_hbm, i_hbm, o_hbm):
        def body(x_vmem, i_vmem):
            for k in range(TOP_K):
                pltpu.sync_copy(x_vmem, o_hbm.at[i_vmem.at[k]])

        pltpu.emit_pipeline(
            body,
            grid=(n_rows // SC_WINDOW,),
            in_specs=[pl.BlockSpec((SC_WINDOW, width), lambda i: (i, 0)),
                      pl.BlockSpec((TOP_K, SC_WINDOW), lambda i: (0, i))],
            out_specs=[],
            core_axis_name=("core", "subcore"),
            dimension_semantics=(pltpu.PARALLEL,),
        )(x_hbm, i_hbm)

    return scatter(x, idx_kmajor)


def _sc_gather_rows(data, idx):
    n_idx = idx.shape[0]
    width = data.shape[1]

    @pl.kernel(out_type=jax.ShapeDtypeStruct((n_idx, width), data.dtype), mesh=_sc_mesh(), scratch_types=[])
    def gather(x_hbm, i_hbm, o_hbm):
        def body(i_vmem, o_vmem):
            pltpu.sync_copy(x_hbm.at[i_vmem.at[0]], o_vmem)

        pltpu.emit_pipeline(
            body,
            grid=(n_idx // SC_WINDOW,),
            in_specs=[pl.BlockSpec((1, SC_WINDOW), lambda i: (0, i))],
            out_specs=[pl.BlockSpec((SC_WINDOW, width), lambda i: (i, 0))],
            core_axis_name=("core", "subcore"),
            dimension_semantics=(pltpu.PARALLEL,),
        )(i_hbm, o_hbm)

    return gather(data, idx.reshape(1, n_idx))


def _expert_kernel(be_ref, nused_ref, nvalid_ref, xa_ref, xb_ref, wg_ref, bg_ref, wu_ref, bu_ref, wd_ref, bd_ref,
                   ya_ref, yb_ref, wbf_ref):
    b = pl.program_id(0)
    active = b < nused_ref[0]
    new_expert = (b == 0) | (be_ref[b] != be_ref[jnp.maximum(b - 1, 0)])

    @pl.when(active & new_expert)
    def _():
        rows = 128
        for wi, w_ref in enumerate((wg_ref, wu_ref, wd_ref)):
            for r in range(0, D_MODEL, rows):
                wbf_ref[wi, r:r + rows, :] = w_ref[0, r:r + rows, :].astype(BF16)

    @pl.when(active)
    def _():
        valid = lax.broadcasted_iota(jnp.int32, (MOE_BM, HALF_W), 0) < nvalid_ref[b]
        zero = jnp.zeros((MOE_BM, HALF_W), jnp.uint32)
        x = _unpack_row_halves(jnp.where(valid, xa_ref[...], zero), jnp.where(valid, xb_ref[...], zero)).astype(BF16)
        g = jnp.dot(x, wbf_ref[0], preferred_element_type=F32) + bg_ref[0]
        u = jnp.dot(x, wbf_ref[1], preferred_element_type=F32) + bu_ref[0]
        g = jnp.minimum(g, SWIGLU_LIMIT)
        u = jnp.clip(u, -SWIGLU_LIMIT, SWIGLU_LIMIT)
        glu = g * jax.nn.sigmoid(SWIGLU_ALPHA * g)
        act = ((u + 1.0) * glu).astype(BF16)
        y = jnp.dot(act, wbf_ref[2], preferred_element_type=F32) + bd_ref[0]
        ya_ref[...], yb_ref[...] = _pack_row_halves(y)

    @pl.when(jnp.logical_not(active))
    def _():
        ya_ref[...] = jnp.zeros_like(ya_ref)
        yb_ref[...] = jnp.zeros_like(yb_ref)


def _experts(block_expert, n_used, n_valid, xs_a, xs_b, wg, bg, wu, bu, wd, bd):
    assert D_FF == D_MODEL
    blk = lambda b, be, nu, nv: (jnp.minimum(b, nu[0] - 1), 0)
    wsp = lambda: pl.BlockSpec((1, D_MODEL, D_FF), lambda b, be, nu, nv: (be[b], 0, 0))
    bsp = lambda: pl.BlockSpec((1, 1, D_FF), lambda b, be, nu, nv: (be[b], 0, 0))
    xsp = lambda: pl.BlockSpec((MOE_BM, HALF_W), blk)
    ysp = lambda: pl.BlockSpec((MOE_BM, HALF_W), lambda b, be, nu, nv: (b, 0))
    slot_arr = jax.ShapeDtypeStruct((N_SLOTS, HALF_W), jnp.uint32)
    grid_spec = pltpu.PrefetchScalarGridSpec(
        num_scalar_prefetch=3,
        grid=(N_SLOT_BLOCKS,),
        in_specs=[xsp(), xsp(), wsp(), bsp(), wsp(), bsp(), wsp(), bsp()],
        out_specs=[ysp(), ysp()],
        scratch_shapes=[pltpu.VMEM((3, D_MODEL, D_FF), BF16)],
    )
    return pl.pallas_call(
        _expert_kernel,
        grid_spec=grid_spec,
        out_shape=[slot_arr, slot_arr],
        compiler_params=_cparams(("arbitrary",)),
        name="experts",
    )(block_expert, n_used, n_valid, xs_a, xs_b, wg, bg, wu, bu, wd, bd)


def _final_kernel(x1_ref, yga_ref, ygb_ref, gate_ref, p_ref, nple_ref, wpg_ref, wpp_ref, out_ref):
    x2 = x1_ref[...]
    gates = gate_ref[...]
    for k in range(TOP_K):
        x2 = x2 + gates[:, k:k + 1] * _unpack_row_halves(yga_ref[k], ygb_ref[k])
    ms = jnp.mean(x2 * x2, axis=-1, keepdims=True)
    h3 = (x2 * lax.rsqrt(ms + EPS) * nple_ref[...]).astype(BF16)
    gate = jax.nn.sigmoid(jnp.dot(h3, wpg_ref[...], preferred_element_type=F32))
    proj = jnp.dot(p_ref[...].astype(BF16), wpp_ref[...], preferred_element_type=F32)
    out_ref[...] = x2 + gate * proj


def _final(x1, yg_a, yg_b, gates, p, n_ple, w_pg, w_pp, row0, n_rows):
    tm = FINAL_TM
    off = row0 // tm
    full = lambda a: pl.BlockSpec(a.shape, lambda i: (0,) * a.ndim)
    return pl.pallas_call(
        _final_kernel,
        grid=(n_rows // tm,),
        in_specs=[pl.BlockSpec((tm, D_MODEL), lambda i: (i + off, 0)),
                  pl.BlockSpec((TOP_K, tm, HALF_W), lambda i: (0, i + off, 0)),
                  pl.BlockSpec((TOP_K, tm, HALF_W), lambda i: (0, i + off, 0)),
                  pl.BlockSpec((tm, LANES), lambda i: (i + off, 0)),
                  pl.BlockSpec((tm, PLE_DIM), lambda i: (i, 0)),
                  full(n_ple), full(w_pg), full(w_pp)],
        out_specs=pl.BlockSpec((tm, D_MODEL), lambda i: (i, 0)),
        out_shape=jax.ShapeDtypeStruct((n_rows, D_MODEL), F32),
        compiler_params=_cparams(("arbitrary",)),
        name="final_ple",
    )(x1, yg_a, yg_b, gates, p, n_ple, w_pg, w_pp)


def _rope_tables():
    half = RET_DK // 2
    freq = ROPE_THETA ** (-jnp.arange(half, dtype=F32) / half)
    ang = jnp.arange(SAMPLE_SEQ, dtype=F32)[:, None] * freq[None, :]
    cos, sin = jnp.cos(ang), jnp.sin(ang)
    return jnp.concatenate([cos, cos], axis=-1), jnp.concatenate([-sin, sin], axis=-1)


def _retention_tables(decay_logit):
    lg = jax.nn.log_sigmoid(decay_logit.astype(F32))
    c = RET_CHUNK
    idx = jnp.arange(c, dtype=F32)
    diff = idx[:, None] - idx[None, :]
    lf = lg[0][:, None, None]
    lb = lg[1][:, None, None]
    mask = jnp.where(diff[None] >= 0, jnp.exp(lf * jnp.maximum(diff, 0.0)[None]),
                     jnp.exp(lb * jnp.maximum(-diff, 0.0)[None]))
    kdec_f = jnp.exp(lg[0][:, None] * (c - 1.0 - idx)[None, :])
    qdec_f = jnp.exp(lg[0][:, None] * (idx + 1.0)[None, :])
    kdec_b = jnp.exp(lg[1][:, None] * idx[None, :])
    qdec_b = jnp.exp(lg[1][:, None] * (c - idx)[None, :])
    dec = jnp.stack([kdec_f, qdec_f, kdec_b, qdec_b], axis=1)[..., None]
    cdec = jnp.exp(lg * c).T
    cdec = jnp.broadcast_to(cdec[:, :, None, None], (RET_HEADS, 2, 1, RET_DV))
    return mask, dec, cdec


def _t5_bucket(rel):
    half = T5_BUCKETS // 2
    exact = half // 2
    n = np.abs(rel)
    ratio = np.log(np.maximum(n, 1).astype(np.float32) / np.float32(exact)) / np.float32(math.log(T5_MAX_DIST / exact))
    large = exact + (ratio * np.float32(half - exact)).astype(np.int32)
    large = np.minimum(large, half - 1)
    return np.where(rel > 0, half, 0) + np.where(n < exact, n, large)


def _attention_bias(rel_bias, gi, dil, radius):
    qi = np.arange(QB)
    ki = np.arange(KW) - ATT_BLOCK
    rel = ki[None, :] - qi[:, None]
    onehot = jnp.asarray(_t5_bucket(rel * dil)[..., None] == np.arange(T5_BUCKETS), F32)
    tab = rel_bias[:, gi * ATT_HEADS:(gi + 1) * ATT_HEADS].astype(F32)
    bias = jnp.einsum('qkb,bh->hqk', onehot, tab, precision=lax.Precision.HIGHEST)
    return jnp.where(jnp.asarray(np.abs(rel) <= radius)[None], bias * LOG2E, NEG)


def _seq_edge_flags():
    first = np.zeros((N_SEG,), np.int32)
    last = np.zeros((N_SEG,), np.int32)
    first[:N_PROMPT_SEG] = 1
    last[:N_PROMPT_SEG] = 1
    first[N_PROMPT_SEG] = 1
    last[N_SEG - 1] = 1
    return jnp.asarray(first), jnp.asarray(last)


def _pad_lanes(a, value=0.0):
    return jnp.pad(a, ((0, 0), (0, LANES - a.shape[-1])), constant_values=value)


def kernel(x_prompt, x_sample, p_prompt, p_sample, norm_mix_g, w_in, ret_decay_logit, ret_gn_g,
           att_q_norm_g, att_k_norm_g, rel_bias, w_ret_proj, w_att_proj, w_out, norm_ffn_g,
           w_router, b_router, w_gate, b_gate, w_up, b_up, w_down, b_down,
           norm_ple_g, w_ple_gate, w_ple_proj):
    assert norm_mix_g.shape[0] == 1, "one layer"
    x_p = x_prompt.reshape(T_PROMPT, D_MODEL)
    x_s = x_sample.reshape(SAMPLE_SEQ, D_MODEL)

    cos_t, sin_t = _rope_tables()
    z = _in_proj(x_p, x_s, norm_mix_g.astype(F32), w_in[0].astype(BF16), cos_t, sin_t)

    ret_mask, ret_dec, ret_cdec = _retention_tables(ret_decay_logit[0])
    y_ret = _retention(z, ret_mask, ret_dec, ret_cdec, ret_gn_g[0].reshape(RET_HEADS, 1, RET_DV).astype(F32))

    first, last = _seq_edge_flags()
    o_list, lse_list = [], []
    for gi, (window, dil) in enumerate(ATT_GROUPS):
        bias_g = _attention_bias(rel_bias, gi, dil, window // (2 * dil))
        gq = jnp.tile(att_q_norm_g[0, gi].astype(F32) * (ATT_DH ** -0.5 * LOG2E), LANES // ATT_DH)[None, :]
        gk = jnp.tile(att_k_norm_g[0, gi].astype(F32), LANES // ATT_DH)[None, :]
        o_g, lse_g = _attention_group(z, bias_g, gq, gk, first, last, gi, dil)
        o_list.append(o_g)
        lse_list.append(lse_g)

    w_r = w_router[0].astype(F32)
    w_r_hi = w_r.astype(BF16)
    w_r_lo = (w_r - w_r_hi.astype(F32)).astype(BF16)
    w_router_cat = _pad_lanes(jnp.concatenate([w_r_hi, w_r_lo], axis=1))
    w_router_hi = _pad_lanes(w_r_hi)
    b_router_p = _pad_lanes(b_router.astype(F32), NEG)
    x1, hp_a, hp_b, idx, gates, rank, cnt = _merge(
        y_ret, o_list, lse_list, z, x_p, x_s, w_ret_proj[0].astype(BF16), w_att_proj[0].astype(BF16),
        w_out[0].astype(BF16), norm_ffn_g.astype(F32), w_router_cat, w_router_hi, b_router_p)

    counts = cnt[0, :N_EXPERTS].astype(jnp.int32)
    padded = (counts + MOE_BM - 1) // MOE_BM * MOE_BM
    pad_end = jnp.cumsum(padded)
    pad_start = pad_end - padded
    expert_ids = jnp.arange(N_EXPERTS, dtype=jnp.int32)
    top_idx = idx[:, :TOP_K]
    start_of = jnp.sum(jnp.where(top_idx[:, :, None] == expert_ids, pad_start, 0), axis=-1)
    dest_kmajor = (start_of + rank[:, :TOP_K]).T
    n_used = (pad_end[-1] // MOE_BM).astype(jnp.int32).reshape(1)
    blk_row0 = jnp.arange(N_SLOT_BLOCKS, dtype=jnp.int32) * MOE_BM
    block_expert = jnp.minimum(jnp.sum((pad_end[None, :] <= blk_row0[:, None]).astype(jnp.int32), axis=1),
                               N_EXPERTS - 1).astype(jnp.int32)
    slot_end = jnp.sum(jnp.where(block_expert[:, None] == expert_ids, pad_start + counts, 0), axis=-1)
    n_valid = jnp.clip(slot_end - blk_row0, 0, MOE_BM).astype(jnp.int32)

    xs_a = _sc_scatter_rows(hp_a, dest_kmajor, N_SLOTS)
    xs_b = _sc_scatter_rows(hp_b, dest_kmajor, N_SLOTS)
    ys_a, ys_b = _experts(block_expert, n_used, n_valid, xs_a, xs_b,
                          w_gate[0], b_gate[0].reshape(N_EXPERTS, 1, D_FF).astype(F32),
                          w_up[0], b_up[0].reshape(N_EXPERTS, 1, D_FF).astype(F32),
                          w_down[0], b_down[0].reshape(N_EXPERTS, 1, D_MODEL).astype(F32))
    dest_flat = dest_kmajor.reshape(-1)
    yg_a = _sc_gather_rows(ys_a, dest_flat).reshape(TOP_K, T_ALL, HALF_W)
    yg_b = _sc_gather_rows(ys_b, dest_flat).reshape(TOP_K, T_ALL, HALF_W)

    n_ple = norm_ple_g.astype(F32)
    w_pg = w_ple_gate[0].astype(BF16)
    w_pp = w_ple_proj[0].astype(BF16)
    y_p = _final(x1, yg_a, yg_b, gates, p_prompt[0].reshape(T_PROMPT, PLE_DIM), n_ple, w_pg, w_pp, 0, T_PROMPT)
    y_s = _final(x1, yg_a, yg_b, gates, p_sample[0].reshape(SAMPLE_SEQ, PLE_DIM), n_ple, w_pg, w_pp, T_PROMPT, SAMPLE_SEQ)
    return (y_p.reshape(x_prompt.shape), y_s.reshape(x_sample.shape))
```

```python
import functools
import math

import jax
import jax.numpy as jnp
import numpy as np
from jax import lax
from jax.experimental import pallas as pl
from jax.experimental.pallas import tpu as pltpu
from jax.experimental.pallas import tpu_sc as plsc

F32 = jnp.float32
BF16 = jnp.bfloat16

D_MODEL = 1024
N_PROMPT_SEQ = 8
PROMPT_SEQ = 2048
SAMPLE_SEQ = 16384
T_PROMPT = N_PROMPT_SEQ * PROMPT_SEQ
T_ALL = T_PROMPT + SAMPLE_SEQ

RET_HEADS = 4
RET_DK = 128
RET_DV = 256
RET_CHUNK = 128
ROPE_THETA = 10000.0
ATT_GROUPS = ((128, 1), (512, 4), (2048, 16))
N_GROUPS = 3
ATT_HEADS = 8
ATT_DH = 64
ATT_BLOCK = 64
ATT_W = ATT_HEADS * ATT_DH
T5_BUCKETS = 32
T5_MAX_DIST = 1024
N_EXPERTS = 32
TOP_K = 4
D_FF = 1024
SWIGLU_ALPHA = 1.702
SWIGLU_LIMIT = 7.0
PLE_DIM = 256
EPS = 1e-6

RET_QK_W = RET_HEADS * RET_DK
RET_V_W = RET_HEADS * RET_DV
N_IN = 2 * RET_QK_W + 2 * RET_V_W + 3 * N_GROUPS * ATT_W + 2 * D_MODEL

COL_RQ = 0
COL_RK = RET_QK_W
COL_RV = 2 * RET_QK_W
COL_RG = COL_RV + RET_V_W
COL_ATT = COL_RG + RET_V_W
COL_GATE_RET = COL_ATT + 3 * N_GROUPS * ATT_W
COL_GATE_ATT = COL_GATE_RET + D_MODEL

LANES = 128
VMEM_LIMIT = 56 * 1024 * 1024
ATT_VMEM_LIMIT = 58 * 1024 * 1024

SEG = 2048
N_SEG = T_ALL // SEG
N_PROMPT_SEG = T_PROMPT // SEG
COL_BLK = 512
N_COL_BLK = N_IN // COL_BLK
QB = 128
KW = 256
NEG = -1e30
LOG2E = math.log2(math.e)
LN2 = math.log(2.0)
MERGE_TM = 512
MERGE_SUB = 512
FINAL_TM = 512
MOE_BM = 512
N_SLOT_BLOCKS = T_ALL * TOP_K // MOE_BM + N_EXPERTS
N_SLOTS = N_SLOT_BLOCKS * MOE_BM
HALF_W = D_MODEL // 4
SC_WINDOW = 128


def _cparams(sem, vmem=VMEM_LIMIT):
    return pltpu.CompilerParams(dimension_semantics=sem, vmem_limit_bytes=vmem)


def _sigmoid(x):
    return 0.5 * jnp.tanh(0.5 * x) + 0.5


def _pack_bf16_pair(x):
    w = x.shape[-1] // 2
    hi = pltpu.bitcast(x[:, :w].astype(BF16).astype(F32), jnp.uint32)
    lo = pltpu.bitcast(x[:, w:].astype(BF16).astype(F32), jnp.uint32)
    return hi | (lo >> 16)


def _unpack_bf16_pair(p):
    hi = pltpu.bitcast(p & jnp.uint32(0xFFFF0000), F32)
    lo = pltpu.bitcast(p << 16, F32)
    return jnp.concatenate([hi, lo], axis=-1)


def _pack_row_halves(x):
    half = x.shape[-1] // 2
    return _pack_bf16_pair(x[:, :half]), _pack_bf16_pair(x[:, half:])


def _unpack_row_halves(pa, pb):
    return jnp.concatenate([_unpack_bf16_pair(pa), _unpack_bf16_pair(pb)], axis=-1)


def _in_proj_kernel(xp_ref, xs_ref, g_ref, w_ref, cos_ref, sin_ref, z_ref, h_ref, p_ref, p2_ref):
    i = pl.program_id(0)
    j = pl.program_id(1)

    def norm_into_h(x_ref):
        xf = x_ref[...]
        ms = jnp.mean(xf * xf, axis=-1, keepdims=True)
        h_ref[...] = (xf * lax.rsqrt(ms + EPS) * g_ref[...]).astype(BF16)

    @pl.when((j == 0) & (i < N_PROMPT_SEG))
    def _():
        norm_into_h(xp_ref)

    @pl.when((j == 0) & (i >= N_PROMPT_SEG))
    def _():
        norm_into_h(xs_ref)

    n_slab = COL_BLK // LANES

    def project():
        return jnp.dot(h_ref[...], w_ref[...], preferred_element_type=F32)

    is_rope = j < (COL_RV // COL_BLK)
    att0 = COL_ATT // COL_BLK
    is_d4 = (j >= att0 + 3) & (j < att0 + 6)
    is_d16 = (j >= att0 + 6) & (j < att0 + 9)

    @pl.when(is_rope)
    def _():
        acc = project()
        scale = jnp.where(j == COL_RK // COL_BLK, RET_DK ** -0.5, 1.0).astype(F32)
        c = cos_ref[...]
        sn = sin_ref[...]
        for s in range(n_slab):
            xs = acc[:, s * LANES:(s + 1) * LANES]
            r = xs * c + pltpu.roll(xs, RET_DK // 2, axis=1) * sn
            z_ref[:, s * LANES:(s + 1) * LANES] = (r * scale).astype(BF16)

    @pl.when(is_d4)
    def _():
        acc = project()
        for s in range(n_slab):
            p_ref[s] = acc[:, s * LANES:(s + 1) * LANES]
        rows = SEG // 4
        for rho in range(4):
            for s in range(n_slab):
                piece = p_ref[s, pl.ds(rho, rows, stride=4), :]
                z_ref[rho * rows:(rho + 1) * rows, s * LANES:(s + 1) * LANES] = piece.astype(BF16)

    @pl.when(is_d16)
    def _():
        acc = project()
        for s in range(n_slab):
            p_ref[s] = acc[:, s * LANES:(s + 1) * LANES]
        quarter = SEG // 4
        rows = SEG // 16
        for r4 in range(4):
            for s in range(n_slab):
                p2_ref[s, r4 * quarter:(r4 + 1) * quarter, :] = p_ref[s, pl.ds(r4, quarter, stride=4), :]
        for r4 in range(4):
            for hi in range(4):
                rho = 4 * hi + r4
                for s in range(n_slab):
                    piece = p2_ref[s, pl.ds(r4 * quarter + hi, rows, stride=4), :]
                    z_ref[rho * rows:(rho + 1) * rows, s * LANES:(s + 1) * LANES] = piece.astype(BF16)

    @pl.when(jnp.logical_not(is_rope | is_d4 | is_d16))
    def _():
        z_ref[...] = project().astype(BF16)


def _in_proj(x_p, x_s, norm_g, w_in_bf, cos_t, sin_t):
    def pos_blk(i, j):
        return (jnp.maximum(i - N_PROMPT_SEG, 0), 0)

    return pl.pallas_call(
        _in_proj_kernel,
        grid=(N_SEG, N_COL_BLK),
        in_specs=[
            pl.BlockSpec((SEG, D_MODEL), lambda i, j: (jnp.minimum(i, N_PROMPT_SEG - 1), 0),
                         pipeline_mode=pl.Buffered(1)),
            pl.BlockSpec((SEG, D_MODEL), pos_blk, pipeline_mode=pl.Buffered(1)),
            pl.BlockSpec((1, D_MODEL), lambda i, j: (0, 0)),
            pl.BlockSpec((D_MODEL, COL_BLK), lambda i, j: (0, j)),
            pl.BlockSpec((SEG, LANES), pos_blk),
            pl.BlockSpec((SEG, LANES), pos_blk),
        ],
        out_specs=pl.BlockSpec((SEG, COL_BLK), lambda i, j: (i, j)),
        out_shape=jax.ShapeDtypeStruct((T_ALL, N_IN), BF16),
        scratch_shapes=[
            pltpu.VMEM((SEG, D_MODEL), BF16),
            pltpu.VMEM((COL_BLK // LANES, SEG, LANES), F32),
            pltpu.VMEM((COL_BLK // LANES, SEG, LANES), F32),
        ],
        compiler_params=_cparams(("arbitrary", "arbitrary")),
        name="in_proj",
    )(x_p, x_s, norm_g, w_in_bf, cos_t, sin_t)


RET_CHUNKS_PER_SEG = SEG // RET_CHUNK
RET_MAX_CHUNKS = SAMPLE_SEQ // RET_CHUNK
RET_GROUP = 16


def _retention_kernel(seg_ref, phase_ref, reset_ref, cbase_ref,
                      q_ref, k_ref, v_ref, g_ref, mask_ref, dec_ref, cdec_ref, gn_ref,
                      y_ref, sb_ref, sf_ref, sr_ref):
    step = pl.program_id(1)
    phase = phase_ref[step]
    reset = reset_ref[step]
    cbase = cbase_ref[step]
    kdec_f = dec_ref[0, 0]
    qdec_f = dec_ref[0, 1]
    kdec_b = dec_ref[0, 2]
    qdec_b = dec_ref[0, 3]
    cd_f = cdec_ref[0, 0]
    cd_b = cdec_ref[0, 1]

    def kv_outer(kd, v):
        return lax.dot_general(kd, v, (((0,), (0,)), ((), ())), preferred_element_type=F32)

    @pl.when((phase == 0) & (reset == 1))
    def _():
        sr_ref[...] = jnp.zeros_like(sr_ref)

    @pl.when((phase == 1) & (reset == 1))
    def _():
        sf_ref[...] = jnp.zeros_like(sf_ref)

    n_groups = RET_CHUNKS_PER_SEG // RET_GROUP

    def chunk_rows(c):
        return pl.ds(pl.multiple_of(c * RET_CHUNK, RET_CHUNK), RET_CHUNK)

    @pl.when(phase == 0)
    def _():
        def body(it, carry):
            top = RET_CHUNKS_PER_SEG - 1 - it * RET_GROUP
            kvs = []
            for j in range(RET_GROUP):
                rows = chunk_rows(top - j)
                kd = (k_ref[rows, :].astype(F32) * kdec_b).astype(BF16)
                kvs.append(kv_outer(kd, v_ref[rows, :]))
            state = sr_ref[...]
            for j in range(RET_GROUP):
                sb_ref[cbase + top - j] = state.astype(BF16)
                state = cd_b * state + kvs[j]
            sr_ref[...] = state
            return carry

        lax.fori_loop(0, n_groups, body, 0)

    @pl.when(phase == 1)
    def _():
        msk = mask_ref[0]
        gn = gn_ref[0]

        def body(it, carry):
            c0 = it * RET_GROUP
            lhs, vs, kvs = [], [], []
            for j in range(RET_GROUP):
                rows = chunk_rows(c0 + j)
                qb = q_ref[rows, :]
                kb = k_ref[rows, :]
                v = v_ref[rows, :]
                q = qb.astype(F32)
                s = lax.dot_general(qb, kb, (((1,), (1,)), ((), ())), preferred_element_type=F32)
                lhs.append(jnp.concatenate(
                    [(s * msk).astype(BF16), (q * qdec_f).astype(BF16), (q * qdec_b).astype(BF16)], axis=-1))
                vs.append(v)
                kvs.append(kv_outer((kb.astype(F32) * kdec_f).astype(BF16), v))
            state = sf_ref[...]
            for j in range(RET_GROUP):
                c = c0 + j
                rhs = jnp.concatenate([vs[j], state.astype(BF16), sb_ref[cbase + c]], axis=0)
                o = jnp.dot(lhs[j], rhs, preferred_element_type=F32)
                state = cd_f * state + kvs[j]
                mu = jnp.mean(o, axis=-1, keepdims=True)
                oc = o - mu
                var = jnp.mean(oc * oc, axis=-1, keepdims=True)
                on = oc * lax.rsqrt(var + EPS) * gn
                rows = chunk_rows(c)
                gate = g_ref[rows, :].astype(F32)
                y_ref[rows, :] = (gate * jax.nn.sigmoid(gate) * on).astype(BF16)
            sf_ref[...] = state
            return carry

        lax.fori_loop(0, n_groups, body, 0)


def _retention_schedule():
    seg, phase, reset, cbase = [], [], [], []
    for p in range(N_PROMPT_SEG):
        for ph in (0, 1):
            seg.append(p); phase.append(ph); reset.append(1); cbase.append(0)
    n_s = N_SEG - N_PROMPT_SEG
    for i in range(n_s):
        t = n_s - 1 - i
        seg.append(N_PROMPT_SEG + t); phase.append(0); reset.append(int(i == 0)); cbase.append(t * RET_CHUNKS_PER_SEG)
    for t in range(n_s):
        seg.append(N_PROMPT_SEG + t); phase.append(1); reset.append(int(t == 0)); cbase.append(t * RET_CHUNKS_PER_SEG)
    hold = list(seg)
    for i in range(len(seg)):
        if phase[i] == 0:
            nxt = next(j for j in range(i + 1, len(seg)) if phase[j] == 1)
            hold[i] = seg[nxt]
    arr = lambda a: jnp.asarray(np.asarray(a, np.int32))
    return arr(seg), arr(phase), arr(reset), arr(cbase), arr(hold)


def _retention(z, ret_mask, ret_dec, ret_cdec, gn_g):
    seg, phase, reset, cbase, hold = _retention_schedule()
    n_steps = int(seg.shape[0])
    qk_blk = lambda col: (lambda h, s, seg_r, ph_r, rs_r, cb_r, hold_r: (seg_r[s], col // RET_DK + h))
    hold_blk = lambda col, w: (lambda h, s, seg_r, ph_r, rs_r, cb_r, hold_r: (hold_r[s], col // w + h))
    v_blk = lambda h, s, seg_r, ph_r, rs_r, cb_r, hold_r: (seg_r[s], COL_RV // RET_DV + h)
    per_head = lambda h, s, *_: (h, 0, 0)
    per_head4 = lambda h, s, *_: (h, 0, 0, 0)
    grid_spec = pltpu.PrefetchScalarGridSpec(
        num_scalar_prefetch=5,
        grid=(RET_HEADS, n_steps),
        in_specs=[
            pl.BlockSpec((SEG, RET_DK), hold_blk(COL_RQ, RET_DK)),
            pl.BlockSpec((SEG, RET_DK), qk_blk(COL_RK)),
            pl.BlockSpec((SEG, RET_DV), v_blk),
            pl.BlockSpec((SEG, RET_DV), hold_blk(COL_RG, RET_DV)),
            pl.BlockSpec((1, RET_CHUNK, RET_CHUNK), per_head),
            pl.BlockSpec((1, 4, RET_CHUNK, 1), per_head4),
            pl.BlockSpec((1, 2, 1, RET_DV), per_head4),
            pl.BlockSpec((1, 1, RET_DV), per_head),
        ],
        out_specs=pl.BlockSpec((SEG, RET_DV), lambda h, s, seg_r, ph_r, rs_r, cb_r, hold_r: (hold_r[s], h)),
        scratch_shapes=[
            pltpu.VMEM((RET_MAX_CHUNKS, RET_DK, RET_DV), BF16),
            pltpu.VMEM((RET_DK, RET_DV), F32),
            pltpu.VMEM((RET_DK, RET_DV), F32),
        ],
    )

    def kernel(seg_r, ph_r, rs_r, cb_r, hold_r, *refs):
        _retention_kernel(seg_r, ph_r, rs_r, cb_r, *refs)

    return pl.pallas_call(
        kernel,
        grid_spec=grid_spec,
        out_shape=jax.ShapeDtypeStruct((T_ALL, RET_V_W), BF16),
        compiler_params=_cparams(("arbitrary", "arbitrary")),
        name="retention",
    )(seg, phase, reset, cbase, hold, z, z, z, z, ret_mask, ret_dec, ret_cdec, gn_g)


def _attention_kernel(first_ref, last_ref, q_ref, kp_ref, km_ref, kn_ref, vp_ref, vm_ref, vn_ref,
                      bias_ref, gq_ref, gk_ref, o_ref, lse_ref, kall, vall, qall, oacc, bvar, *, dil):
    nb = SEG // dil // ATT_BLOCK
    nqb = nb // 2
    n_slab = ATT_W // LANES
    c = pl.program_id(0)
    is_first = first_ref[c]
    is_last = last_ref[c]
    lane = lax.broadcasted_iota(jnp.int32, (1, LANES), 1)
    lo = lane < ATT_DH
    gq = gq_ref[...]
    gk = gk_ref[...]

    @pl.when(c == 0)
    def _():
        col = lax.broadcasted_iota(jnp.int32, (1, KW), 1)
        left = jnp.where(col < ATT_BLOCK, NEG, 0.0).astype(F32)
        right = jnp.where(col >= KW - ATT_BLOCK, NEG, 0.0).astype(F32)
        for h in range(ATT_HEADS):
            b = bias_ref[h]
            bvar[0, h] = b
            bvar[1, h] = b + left
            bvar[2, h] = b + right
            bvar[3, h] = b + left + right

    def head_norm(x, g):
        x2 = x * x
        s_lo = jnp.sum(jnp.where(lo, x2, 0.0), axis=-1, keepdims=True)
        s_hi = jnp.sum(jnp.where(lo, 0.0, x2), axis=-1, keepdims=True)
        ms = jnp.where(lo, s_lo, s_hi) * (1.0 / ATT_DH)
        return x * lax.rsqrt(ms + EPS) * g

    def norm_block(src):
        even, odd = [], []
        for s in range(n_slab):
            xn = head_norm(src[:, s * LANES:(s + 1) * LANES].astype(F32), gk)
            even.append(jnp.where(lo, xn, 0.0).astype(BF16))
            odd.append(jnp.where(lo, 0.0, xn).astype(BF16))
        return jnp.concatenate(even, axis=-1), jnp.concatenate(odd, axis=-1)

    lo_wide = lax.broadcasted_iota(jnp.int32, (1, ATT_W), 1) % LANES < ATT_DH

    def split_heads(v):
        zero = jnp.zeros_like(v)
        return jnp.where(lo_wide, v, zero), jnp.where(lo_wide, zero, v)

    def fill_main(it, carry):
        rho = it // nb
        blk = it % nb
        kall[0, rho, blk + 1], kall[1, rho, blk + 1] = norm_block(km_ref[rho, blk])
        vall[0, rho, blk + 1], vall[1, rho, blk + 1] = split_heads(vm_ref[rho, blk])
        qsrc = q_ref[rho, blk]
        qall[rho, blk] = jnp.concatenate(
            [head_norm(qsrc[:, s * LANES:(s + 1) * LANES].astype(F32), gq).astype(BF16) for s in range(n_slab)], axis=-1)
        return carry

    lax.fori_loop(0, dil * nb, fill_main, 0, unroll=4)

    def fill_halo(rho, carry):
        kall[0, rho, 0], kall[1, rho, 0] = norm_block(kp_ref[rho, 0])
        kall[0, rho, nb + 1], kall[1, rho, nb + 1] = norm_block(kn_ref[rho, 0])
        vall[0, rho, 0], vall[1, rho, 0] = split_heads(vp_ref[rho, 0])
        vall[0, rho, nb + 1], vall[1, rho, nb + 1] = split_heads(vn_ref[rho, 0])
        return carry

    lax.fori_loop(0, dil, fill_halo, 0)

    ones_even = jnp.broadcast_to(jnp.where(lo, 1.0, 0.0).astype(BF16), (KW, LANES))
    ones_odd = jnp.broadcast_to(jnp.where(lo, 0.0, 1.0).astype(BF16), (KW, LANES))

    def body(it, carry):
        rho = it // nqb
        qb = it % nqb
        var = (jnp.where((qb == 0) & (is_first == 1), 1, 0)
               + jnp.where((qb == nqb - 1) & (is_last == 1), 2, 0))
        start = rho + qb * (QB * dil)
        rows = pl.ds(start, QB) if dil == 1 else pl.ds(start, QB, stride=dil)
        for s in range(n_slab):
            sl = slice(s * LANES, (s + 1) * LANES)
            qn = qall[rho, pl.ds(2 * qb, 2), :, sl].reshape(QB, LANES)
            es, ms = [], []
            for hh in range(2):
                kw = kall[hh, rho, pl.ds(2 * qb, 4), :, sl].reshape(KW, LANES)
                sc = lax.dot_general(qn, kw, (((1,), (1,)), ((), ())), preferred_element_type=F32)
                sc = sc + bvar[var, 2 * s + hh]
                m = jnp.max(sc, axis=-1, keepdims=True)
                es.append(jnp.exp2(sc - m).astype(BF16))
                ms.append(m)
            v_even = vall[0, rho, pl.ds(2 * qb, 4), :, sl].reshape(KW, LANES)
            v_odd = vall[1, rho, pl.ds(2 * qb, 4), :, sl].reshape(KW, LANES)
            rhs = jnp.concatenate([jnp.concatenate([v_even, ones_even], axis=1),
                                   jnp.concatenate([v_odd, ones_odd], axis=1)], axis=0)
            res = jnp.dot(jnp.concatenate(es, axis=1), rhs, preferred_element_type=F32)
            den = res[:, LANES:]
            oacc[s, rows, :] = res[:, :LANES] * (1.0 / den)
            lse_ref[s, rows, :] = (jnp.where(lo, ms[0], ms[1]) + jnp.log2(den)) * LN2
        return carry

    lax.fori_loop(0, dil * nqb, body, 0, unroll=8)

    for s in range(n_slab):
        o_ref[:, s * LANES:(s + 1) * LANES] = oacc[s].astype(BF16)


def _attention_group(z, bias_g, gq, gk, first, last, gi, dil):
    nb = SEG // dil // ATT_BLOCK
    z5 = z.reshape(N_SEG, dil, nb, ATT_BLOCK, N_IN)
    cq = (COL_ATT + 3 * gi * ATT_W) // ATT_W
    ck, cv = cq + 1, cq + 2
    main = lambda cb: pl.BlockSpec((None, dil, nb, ATT_BLOCK, ATT_W), lambda c, f, l: (c, 0, 0, 0, cb))
    prev = lambda cb: pl.BlockSpec((None, dil, 1, ATT_BLOCK, ATT_W),
                                   lambda c, f, l: (c - 1 + f[c], 0, nb - 1, 0, cb), pipeline_mode=pl.Buffered(1))
    nxt = lambda cb: pl.BlockSpec((None, dil, 1, ATT_BLOCK, ATT_W),
                                  lambda c, f, l: (c + 1 - l[c], 0, 0, 0, cb), pipeline_mode=pl.Buffered(1))
    grid_spec = pltpu.PrefetchScalarGridSpec(
        num_scalar_prefetch=2,
        grid=(N_SEG,),
        in_specs=[
            main(cq), prev(ck), main(ck), nxt(ck), prev(cv), main(cv), nxt(cv),
            pl.BlockSpec((ATT_HEADS, QB, KW), lambda c, f, l: (0, 0, 0)),
            pl.BlockSpec((1, LANES), lambda c, f, l: (0, 0)),
            pl.BlockSpec((1, LANES), lambda c, f, l: (0, 0)),
        ],
        out_specs=[
            pl.BlockSpec((SEG, ATT_W), lambda c, f, l: (c, 0)),
            pl.BlockSpec((ATT_W // LANES, SEG, LANES), lambda c, f, l: (0, c, 0)),
        ],
        scratch_shapes=[
            pltpu.VMEM((2, dil, nb + 2, ATT_BLOCK, ATT_W), BF16),
            pltpu.VMEM((2, dil, nb + 2, ATT_BLOCK, ATT_W), BF16),
            pltpu.VMEM((dil, nb, ATT_BLOCK, ATT_W), BF16),
            pltpu.VMEM((ATT_W // LANES, SEG, LANES), F32),
            pltpu.VMEM((4, ATT_HEADS, QB, KW), F32),
        ],
    )
    return pl.pallas_call(
        functools.partial(_attention_kernel, dil=dil),
        grid_spec=grid_spec,
        out_shape=[jax.ShapeDtypeStruct((T_ALL, ATT_W), BF16), jax.ShapeDtypeStruct((ATT_W // LANES, T_ALL, LANES), F32)],
        compiler_params=_cparams(("arbitrary",), ATT_VMEM_LIMIT),
        name=f"attention_d{dil}",
    )(first, last, z5, z5, z5, z5, z5, z5, z5, bias_g, gq, gk)


def _merge_kernel(yret_ref, o0_ref, o1_ref, o2_ref, l0_ref, l1_ref, l2_ref,
                  gret_a_ref, gret_b_ref, gatt_a_ref, gatt_b_ref, xp_ref, xs_ref,
                  wret_ref, watt_ref, wout_ref, nffn_ref, wr_ref, wrhi_ref, br_ref,
                  x1_ref, hpa_ref, hpb_ref, idx_ref, gate_ref, rank_ref, cnt_ref, carry_ref):
    i = pl.program_id(0)
    tm = MERGE_TM
    sub = MERGE_SUB

    @pl.when(i == 0)
    def _():
        carry_ref[...] = jnp.zeros_like(carry_ref)

    is_prompt = i < T_PROMPT // tm
    lane = lax.broadcasted_iota(jnp.int32, (sub, LANES), 1)
    lane_f = lane.astype(F32)
    tri = jnp.where(lax.broadcasted_iota(jnp.int32, (sub, sub), 1) < lax.broadcasted_iota(jnp.int32, (sub, sub), 0),
                    1.0, 0.0).astype(BF16)
    carry = carry_ref[...]

    for r0 in range(0, tm, sub):
        rows = slice(r0, r0 + sub)
        l0, l1, l2 = [jnp.concatenate([r[s, rows, :] for s in range(ATT_W // LANES)], axis=-1)
                      for r in (l0_ref, l1_ref, l2_ref)]
        lm = jnp.maximum(jnp.maximum(l0, l1), l2)
        e0, e1, e2 = jnp.exp(l0 - lm), jnp.exp(l1 - lm), jnp.exp(l2 - lm)
        inv = 1.0 / (e0 + e1 + e2)
        y_att = ((e0 * inv) * o0_ref[rows, :].astype(F32) + (e1 * inv) * o1_ref[rows, :].astype(F32)
                 + (e2 * inv) * o2_ref[rows, :].astype(F32))

        p_ret = jnp.dot(yret_ref[rows, :], wret_ref[...], preferred_element_type=F32)
        p_att = jnp.dot(y_att.astype(BF16), watt_ref[...], preferred_element_type=F32)
        g_ret = jnp.concatenate([gret_a_ref[rows, :], gret_b_ref[rows, :]], axis=-1).astype(F32)
        g_att = jnp.concatenate([gatt_a_ref[rows, :], gatt_b_ref[rows, :]], axis=-1).astype(F32)
        merged = _sigmoid(g_ret) * p_ret + _sigmoid(g_att) * p_att
        x_in = jnp.where(is_prompt, xp_ref[rows, :], xs_ref[rows, :])
        x1 = x_in + jnp.dot(merged.astype(BF16), wout_ref[...], preferred_element_type=F32)
        x1_ref[rows, :] = x1

        ms = jnp.mean(x1 * x1, axis=-1, keepdims=True)
        h2 = x1 * lax.rsqrt(ms + EPS) * nffn_ref[...]
        hpa_ref[rows, :], hpb_ref[rows, :] = _pack_row_halves(h2)

        h_hi = h2.astype(BF16)
        h_lo = (h2 - h_hi.astype(F32)).astype(BF16)
        p1 = jnp.dot(h_hi, wr_ref[...], preferred_element_type=F32)
        p2 = jnp.dot(h_lo, wrhi_ref[...], preferred_element_type=F32)
        work = p1 + pltpu.roll(p1, LANES - N_EXPERTS, axis=1) + p2 + br_ref[...]
        vals, idxs = [], []
        for _ in range(TOP_K):
            m = jnp.max(work, axis=-1, keepdims=True)
            ix = jnp.min(jnp.where(work == m, lane_f, float(LANES)), axis=-1, keepdims=True)
            vals.append(m)
            idxs.append(ix)
            work = jnp.where(lane_f == ix, -3e38, work)
        es = [jnp.exp(v - vals[0]) for v in vals]
        den = es[0] + es[1] + es[2] + es[3]
        onehot = jnp.zeros((sub, LANES), F32)
        for ix in idxs:
            onehot = onehot + jnp.where(lane_f == ix, 1.0, 0.0)
        before = jnp.dot(tri, onehot.astype(BF16), preferred_element_type=F32) + carry
        idx_out = jnp.zeros((sub, LANES), F32)
        gate_out = jnp.zeros((sub, LANES), F32)
        rank_out = jnp.zeros((sub, LANES), F32)
        for k in range(TOP_K):
            rk = jnp.sum(jnp.where(lane_f == idxs[k], before, 0.0), axis=-1, keepdims=True)
            sel = lane == k
            idx_out = jnp.where(sel, idxs[k], idx_out)
            gate_out = jnp.where(sel, es[k] / den, gate_out)
            rank_out = jnp.where(sel, rk, rank_out)
        idx_ref[rows, :] = idx_out.astype(jnp.int32)
        gate_ref[rows, :] = gate_out
        rank_ref[rows, :] = rank_out.astype(jnp.int32)
        carry = carry + jnp.sum(onehot, axis=0, keepdims=True)

    carry_ref[...] = carry
    cnt_ref[...] = jnp.broadcast_to(carry, cnt_ref.shape)


def _merge(y_ret, o_list, lse_list, z, x_p, x_s, w_ret, w_att, w_out, n_ffn, w_router, w_router_hi, b_router):
    tm = MERGE_TM
    n_p = T_PROMPT // tm
    row = lambda w: pl.BlockSpec((tm, w), lambda i: (i, 0))
    full = lambda a: pl.BlockSpec(a.shape, lambda i: (0,) * a.ndim)
    zcol = lambda col: pl.BlockSpec((tm, COL_BLK), lambda i: (i, col // COL_BLK))
    lse_spec = pl.BlockSpec((ATT_W // LANES, tm, LANES), lambda i: (0, i, 0))
    return pl.pallas_call(
        _merge_kernel,
        grid=(T_ALL // tm,),
        in_specs=[row(RET_V_W), row(ATT_W), row(ATT_W), row(ATT_W), lse_spec, lse_spec, lse_spec,
                  zcol(COL_GATE_RET), zcol(COL_GATE_RET + COL_BLK), zcol(COL_GATE_ATT),
                  zcol(COL_GATE_ATT + COL_BLK),
                  pl.BlockSpec((tm, D_MODEL), lambda i: (jnp.minimum(i, n_p - 1), 0)),
                  pl.BlockSpec((tm, D_MODEL), lambda i: (jnp.maximum(i - n_p, 0), 0)),
                  full(w_ret), full(w_att), full(w_out), full(n_ffn), full(w_router), full(w_router_hi),
                  full(b_router)],
        out_specs=[row(D_MODEL), row(HALF_W), row(HALF_W), row(LANES), row(LANES), row(LANES),
                   pl.BlockSpec((8, LANES), lambda i: (0, 0))],
        out_shape=[jax.ShapeDtypeStruct((T_ALL, D_MODEL), F32),
                   jax.ShapeDtypeStruct((T_ALL, HALF_W), jnp.uint32),
                   jax.ShapeDtypeStruct((T_ALL, HALF_W), jnp.uint32),
                   jax.ShapeDtypeStruct((T_ALL, LANES), jnp.int32),
                   jax.ShapeDtypeStruct((T_ALL, LANES), F32),
                   jax.ShapeDtypeStruct((T_ALL, LANES), jnp.int32),
                   jax.ShapeDtypeStruct((8, LANES), F32)],
        scratch_shapes=[pltpu.VMEM((1, LANES), F32)],
        compiler_params=_cparams(("arbitrary",)),
        name="merge_router",
    )(y_ret, *o_list, *lse_list, z, z, z, z, x_p, x_s, w_ret, w_att, w_out, n_ffn, w_router, w_router_hi, b_router)


def _sc_mesh():
    return plsc.VectorSubcoreMesh(core_axis_name="core", subcore_axis_name="subcore")


def _sc_scatter_rows(x, idx_kmajor, n_out):
    n_rows, width = x.shape

    @pl.kernel(out_type=jax.ShapeDtypeStruct((n_out, width), x.dtype), mesh=_sc_mesh(), scratch_types=[])
    def scatter(x_hbm, i_hbm, o_hbm):
        def body(x_vmem, i_vmem):
            for k in range(TOP_K):
                pltpu.sync_copy(x_vmem, o_hbm.at[i_vmem.at[k]])

        pltpu.emit_pipeline(
            body,
            grid=(n_rows // SC_WINDOW,),
            in_specs=[pl.BlockSpec((SC_WINDOW, width), lambda i: (i, 0)),
                      pl.BlockSpec((TOP_K, SC_WINDOW), lambda i: (0, i))],
            out_specs=[],
            core_axis_name=("core", "subcore"),
            dimension_semantics=(pltpu.PARALLEL,),
        )(x_hbm, i_hbm)

    return scatter(x, idx_kmajor)


def _sc_gather_rows(data, idx):
    n_idx = idx.shape[0]
    width = data.shape[1]

    @pl.kernel(out_type=jax.ShapeDtypeStruct((n_idx, width), data.dtype), mesh=_sc_mesh(), scratch_types=[])
    def gather(x_hbm, i_hbm, o_hbm):
        def body(i_vmem, o_vmem):
            pltpu.sync_copy(x_hbm.at[i_vmem.at[0]], o_vmem)

        pltpu.emit_pipeline(
            body,
            grid=(n_idx // SC_WINDOW,),
            in_specs=[pl.BlockSpec((1, SC_WINDOW), lambda i: (0, i))],
            out_specs=[pl.BlockSpec((SC_WINDOW, width), lambda i: (i, 0))],
            core_axis_name=("core", "subcore"),
            dimension_semantics=(pltpu.PARALLEL,),
        )(i_hbm, o_hbm)

    return gather(data, idx.reshape(1, n_idx))


def _expert_kernel(be_ref, nused_ref, nvalid_ref, xa_ref, xb_ref, wg_ref, bg_ref, wu_ref, bu_ref, wd_ref, bd_ref,
                   ya_ref, yb_ref, wbf_ref):
    b = pl.program_id(0)
    active = b < nused_ref[0]
    new_expert = (b == 0) | (be_ref[b] != be_ref[jnp.maximum(b - 1, 0)])

    @pl.when(active & new_expert)
    def _():
        rows = 128
        for wi, w_ref in enumerate((wg_ref, wu_ref, wd_ref)):
            for r in range(0, D_MODEL, rows):
                wbf_ref[wi, r:r + rows, :] = w_ref[0, r:r + rows, :].astype(BF16)

    @pl.when(active)
    def _():
        valid = lax.broadcasted_iota(jnp.int32, (MOE_BM, HALF_W), 0) < nvalid_ref[b]
        zero = jnp.zeros((MOE_BM, HALF_W), jnp.uint32)
        x = _unpack_row_halves(jnp.where(valid, xa_ref[...], zero), jnp.where(valid, xb_ref[...], zero)).astype(BF16)
        g = jnp.dot(x, wbf_ref[0], preferred_element_type=F32) + bg_ref[0]
        u = jnp.dot(x, wbf_ref[1], preferred_element_type=F32) + bu_ref[0]
        g = jnp.minimum(g, SWIGLU_LIMIT)
        u = jnp.clip(u, -SWIGLU_LIMIT, SWIGLU_LIMIT)
        glu = g * jax.nn.sigmoid(SWIGLU_ALPHA * g)
        act = ((u + 1.0) * glu).astype(BF16)
        y = jnp.dot(act, wbf_ref[2], preferred_element_type=F32) + bd_ref[0]
        ya_ref[...], yb_ref[...] = _pack_row_halves(y)

    @pl.when(jnp.logical_not(active))
    def _():
        ya_ref[...] = jnp.zeros_like(ya_ref)
        yb_ref[...] = jnp.zeros_like(yb_ref)


def _experts(block_expert, n_used, n_valid, xs_a, xs_b, wg, bg, wu, bu, wd, bd):
    assert D_FF == D_MODEL
    blk = lambda b, be, nu, nv: (jnp.minimum(b, nu[0] - 1), 0)
    wsp = lambda: pl.BlockSpec((1, D_MODEL, D_FF), lambda b, be, nu, nv: (be[b], 0, 0))
    bsp = lambda: pl.BlockSpec((1, 1, D_FF), lambda b, be, nu, nv: (be[b], 0, 0))
    xsp = lambda: pl.BlockSpec((MOE_BM, HALF_W), blk)
    ysp = lambda: pl.BlockSpec((MOE_BM, HALF_W), lambda b, be, nu, nv: (b, 0))
    slot_arr = jax.ShapeDtypeStruct((N_SLOTS, HALF_W), jnp.uint32)
    grid_spec = pltpu.PrefetchScalarGridSpec(
        num_scalar_prefetch=3,
        grid=(N_SLOT_BLOCKS,),
        in_specs=[xsp(), xsp(), wsp(), bsp(), wsp(), bsp(), wsp(), bsp()],
        out_specs=[ysp(), ysp()],
        scratch_shapes=[pltpu.VMEM((3, D_MODEL, D_FF), BF16)],
    )
    return pl.pallas_call(
        _expert_kernel,
        grid_spec=grid_spec,
        out_shape=[slot_arr, slot_arr],
        compiler_params=_cparams(("arbitrary",)),
        name="experts",
    )(block_expert, n_used, n_valid, xs_a, xs_b, wg, bg, wu, bu, wd, bd)


def _final_kernel(x1_ref, yga_ref, ygb_ref, gate_ref, p_ref, nple_ref, wpg_ref, wpp_ref, out_ref):
    x2 = x1_ref[...]
    gates = gate_ref[...]
    for k in range(TOP_K):
        x2 = x2 + gates[:, k:k + 1] * _unpack_row_halves(yga_ref[k], ygb_ref[k])
    ms = jnp.mean(x2 * x2, axis=-1, keepdims=True)
    h3 = (x2 * lax.rsqrt(ms + EPS) * nple_ref[...]).astype(BF16)
    gate = jax.nn.sigmoid(jnp.dot(h3, wpg_ref[...], preferred_element_type=F32))
    proj = jnp.dot(p_ref[...].astype(BF16), wpp_ref[...], preferred_element_type=F32)
    out_ref[...] = x2 + gate * proj


def _final(x1, yg_a, yg_b, gates, p, n_ple, w_pg, w_pp, row0, n_rows):
    tm = FINAL_TM
    off = row0 // tm
    full = lambda a: pl.BlockSpec(a.shape, lambda i: (0,) * a.ndim)
    return pl.pallas_call(
        _final_kernel,
        grid=(n_rows // tm,),
        in_specs=[pl.BlockSpec((tm, D_MODEL), lambda i: (i + off, 0)),
                  pl.BlockSpec((TOP_K, tm, HALF_W), lambda i: (0, i + off, 0)),
                  pl.BlockSpec((TOP_K, tm, HALF_W), lambda i: (0, i + off, 0)),
                  pl.BlockSpec((tm, LANES), lambda i: (i + off, 0)),
                  pl.BlockSpec((tm, PLE_DIM), lambda i: (i, 0)),
                  full(n_ple), full(w_pg), full(w_pp)],
        out_specs=pl.BlockSpec((tm, D_MODEL), lambda i: (i, 0)),
        out_shape=jax.ShapeDtypeStruct((n_rows, D_MODEL), F32),
        compiler_params=_cparams(("arbitrary",)),
        name="final_ple",
    )(x1, yg_a, yg_b, gates, p, n_ple, w_pg, w_pp)


def _rope_tables():
    half = RET_DK // 2
    freq = ROPE_THETA ** (-jnp.arange(half, dtype=F32) / half)
    ang = jnp.arange(SAMPLE_SEQ, dtype=F32)[:, None] * freq[None, :]
    cos, sin = jnp.cos(ang), jnp.sin(ang)
    return jnp.concatenate([cos, cos], axis=-1), jnp.concatenate([-sin, sin], axis=-1)


def _retention_tables(decay_logit):
    lg = jax.nn.log_sigmoid(decay_logit.astype(F32))
    c = RET_CHUNK
    idx = jnp.arange(c, dtype=F32)
    diff = idx[:, None] - idx[None, :]
    lf = lg[0][:, None, None]
    lb = lg[1][:, None, None]
    mask = jnp.where(diff[None] >= 0, jnp.exp(lf * jnp.maximum(diff, 0.0)[None]),
                     jnp.exp(lb * jnp.maximum(-diff, 0.0)[None]))
    kdec_f = jnp.exp(lg[0][:, None] * (c - 1.0 - idx)[None, :])
    qdec_f = jnp.exp(lg[0][:, None] * (idx + 1.0)[None, :])
    kdec_b = jnp.exp(lg[1][:, None] * idx[None, :])
    qdec_b = jnp.exp(lg[1][:, None] * (c - idx)[None, :])
    dec = jnp.stack([kdec_f, qdec_f, kdec_b, qdec_b], axis=1)[..., None]
    cdec = jnp.exp(lg * c).T
    cdec = jnp.broadcast_to(cdec[:, :, None, None], (RET_HEADS, 2, 1, RET_DV))
    return mask, dec, cdec


def _t5_bucket(rel):
    half = T5_BUCKETS // 2
    exact = half // 2
    n = np.abs(rel)
    ratio = np.log(np.maximum(n, 1).astype(np.float32) / np.float32(exact)) / np.float32(math.log(T5_MAX_DIST / exact))
    large = exact + (ratio * np.float32(half - exact)).astype(np.int32)
    large = np.minimum(large, half - 1)
    return np.where(rel > 0, half, 0) + np.where(n < exact, n, large)


def _attention_bias(rel_bias, gi, dil, radius):
    qi = np.arange(QB)
    ki = np.arange(KW) - ATT_BLOCK
    rel = ki[None, :] - qi[:, None]
    onehot = jnp.asarray(_t5_bucket(rel * dil)[..., None] == np.arange(T5_BUCKETS), F32)
    tab = rel_bias[:, gi * ATT_HEADS:(gi + 1) * ATT_HEADS].astype(F32)
    bias = jnp.einsum('qkb,bh->hqk', onehot, tab, precision=lax.Precision.HIGHEST)
    return jnp.where(jnp.asarray(np.abs(rel) <= radius)[None], bias * LOG2E, NEG)


def _seq_edge_flags():
    first = np.zeros((N_SEG,), np.int32)
    last = np.zeros((N_SEG,), np.int32)
    first[:N_PROMPT_SEG] = 1
    last[:N_PROMPT_SEG] = 1
    first[N_PROMPT_SEG] = 1
    last[N_SEG - 1] = 1
    return jnp.asarray(first), jnp.asarray(last)


def _pad_lanes(a, value=0.0):
    return jnp.pad(a, ((0, 0), (0, LANES - a.shape[-1])), constant_values=value)


def kernel(x_prompt, x_sample, p_prompt, p_sample, norm_mix_g, w_in, ret_decay_logit, ret_gn_g,
           att_q_norm_g, att_k_norm_g, rel_bias, w_ret_proj, w_att_proj, w_out, norm_ffn_g,
           w_router, b_router, w_gate, b_gate, w_up, b_up, w_down, b_down,
           norm_ple_g, w_ple_gate, w_ple_proj):
    assert norm_mix_g.shape[0] == 1, "one layer"
    x_p = x_prompt.reshape(T_PROMPT, D_MODEL)
    x_s = x_sample.reshape(SAMPLE_SEQ, D_MODEL)

    cos_t, sin_t = _rope_tables()
    z = _in_proj(x_p, x_s, norm_mix_g.astype(F32), w_in[0].astype(BF16), cos_t, sin_t)

    ret_mask, ret_dec, ret_cdec = _retention_tables(ret_decay_logit[0])
    y_ret = _retention(z, ret_mask, ret_dec, ret_cdec, ret_gn_g[0].reshape(RET_HEADS, 1, RET_DV).astype(F32))

    first, last = _seq_edge_flags()
    o_list, lse_list = [], []
    for gi, (window, dil) in enumerate(ATT_GROUPS):
        bias_g = _attention_bias(rel_bias, gi, dil, window // (2 * dil))
        gq = jnp.tile(att_q_norm_g[0, gi].astype(F32) * (ATT_DH ** -0.5 * LOG2E), LANES // ATT_DH)[None, :]
        gk = jnp.tile(att_k_norm_g[0, gi].astype(F32), LANES // ATT_DH)[None, :]
        o_g, lse_g = _attention_group(z, bias_g, gq, gk, first, last, gi, dil)
        o_list.append(o_g)
        lse_list.append(lse_g)

    w_r = w_router[0].astype(F32)
    w_r_hi = w_r.astype(BF16)
    w_r_lo = (w_r - w_r_hi.astype(F32)).astype(BF16)
    w_router_cat = _pad_lanes(jnp.concatenate([w_r_hi, w_r_lo], axis=1))
    w_router_hi = _pad_lanes(w_r_hi)
    b_router_p = _pad_lanes(b_router.astype(F32), NEG)
    x1, hp_a, hp_b, idx, gates, rank, cnt = _merge(
        y_ret, o_list, lse_list, z, x_p, x_s, w_ret_proj[0].astype(BF16), w_att_proj[0].astype(BF16),
        w_out[0].astype(BF16), norm_ffn_g.astype(F32), w_router_cat, w_router_hi, b_router_p)

    counts = cnt[0, :N_EXPERTS].astype(jnp.int32)
    padded = (counts + MOE_BM - 1) // MOE_BM * MOE_BM
    pad_end = jnp.cumsum(padded)
    pad_start = pad_end - padded
    expert_ids = jnp.arange(N_EXPERTS, dtype=jnp.int32)
    top_idx = idx[:, :TOP_K]
    start_of = jnp.sum(jnp.where(top_idx[:, :, None] == expert_ids, pad_start, 0), axis=-1)
    dest_kmajor = (start_of + rank[:, :TOP_K]).T
    n_used = (pad_end[-1] // MOE_BM).astype(jnp.int32).reshape(1)
    blk_row0 = jnp.arange(N_SLOT_BLOCKS, dtype=jnp.int32) * MOE_BM
    block_expert = jnp.minimum(jnp.sum((pad_end[None, :] <= blk_row0[:, None]).astype(jnp.int32), axis=1),
                               N_EXPERTS - 1).astype(jnp.int32)
    slot_end = jnp.sum(jnp.where(block_expert[:, None] == expert_ids, pad_start + counts, 0), axis=-1)
    n_valid = jnp.clip(slot_end - blk_row0, 0, MOE_BM).astype(jnp.int32)

    xs_a = _sc_scatter_rows(hp_a, dest_kmajor, N_SLOTS)
    xs_b = _sc_scatter_rows(hp_b, dest_kmajor, N_SLOTS)
    ys_a, ys_b = _experts(block_expert, n_used, n_valid, xs_a, xs_b,
                          w_gate[0], b_gate[0].reshape(N_EXPERTS, 1, D_FF).astype(F32),
                          w_up[0], b_up[0].reshape(N_EXPERTS, 1, D_FF).astype(F32),
                          w_down[0], b_down[0].reshape(N_EXPERTS, 1, D_MODEL).astype(F32))
    dest_flat = dest_kmajor.reshape(-1)
    yg_a = _sc_gather_rows(ys_a, dest_flat).reshape(TOP_K, T_ALL, HALF_W)
    yg_b = _sc_gather_rows(ys_b, dest_flat).reshape(TOP_K, T_ALL, HALF_W)

    n_ple = norm_ple_g.astype(F32)
    w_pg = w_ple_gate[0].astype(BF16)
    w_pp = w_ple_proj[0].astype(BF16)
    y_p = _final(x1, yg_a, yg_b, gates, p_prompt[0].reshape(T_PROMPT, PLE_DIM), n_ple, w_pg, w_pp, 0, T_PROMPT)
    y_s = _final(x1, yg_a, yg_b, gates, p_sample[0].reshape(SAMPLE_SEQ, PLE_DIM), n_ple, w_pg, w_pp, T_PROMPT, SAMPLE_SEQ)
    return (y_p.reshape(x_prompt.shape), y_s.reshape(x_sample.shape))
```

```python
import functools
import math

import jax
import jax.numpy as jnp
import numpy as np
from jax import lax
from jax.experimental import pallas as pl
from jax.experimental.pallas import tpu as pltpu
from jax.experimental.pallas import tpu_sc as plsc

F32 = jnp.float32
BF16 = jnp.bfloat16

D_MODEL = 1024
N_PROMPT_SEQ = 8
PROMPT_SEQ = 2048
SAMPLE_SEQ = 16384
T_PROMPT = N_PROMPT_SEQ * PROMPT_SEQ
T_ALL = T_PROMPT + SAMPLE_SEQ

RET_HEADS = 4
RET_DK = 128
RET_DV = 256
RET_CHUNK = 128
ROPE_THETA = 10000.0
ATT_GROUPS = ((128, 1), (512, 4), (2048, 16))
N_GROUPS = 3
ATT_HEADS = 8
ATT_DH = 64
ATT_BLOCK = 64
ATT_W = ATT_HEADS * ATT_DH
T5_BUCKETS = 32
T5_MAX_DIST = 1024
N_EXPERTS = 32
TOP_K = 4
D_FF = 1024
SWIGLU_ALPHA = 1.702
SWIGLU_LIMIT = 7.0
PLE_DIM = 256
EPS = 1e-6

RET_QK_W = RET_HEADS * RET_DK
RET_V_W = RET_HEADS * RET_DV
N_IN = 2 * RET_QK_W + 2 * RET_V_W + 3 * N_GROUPS * ATT_W + 2 * D_MODEL

COL_RQ = 0
COL_RK = RET_QK_W
COL_RV = 2 * RET_QK_W
COL_RG = COL_RV + RET_V_W
COL_ATT = COL_RG + RET_V_W
COL_GATE_RET = COL_ATT + 3 * N_GROUPS * ATT_W
COL_GATE_ATT = COL_GATE_RET + D_MODEL

LANES = 128
VMEM_LIMIT = 56 * 1024 * 1024
ATT_VMEM_LIMIT = 58 * 1024 * 1024

SEG = 2048
N_SEG = T_ALL // SEG
N_PROMPT_SEG = T_PROMPT // SEG
COL_BLK = 512
N_COL_BLK = N_IN // COL_BLK
QB = 128
KW = 256
NEG = -1e30
LOG2E = math.log2(math.e)
LN2 = math.log(2.0)
MERGE_TM = 512
MERGE_SUB = 512
FINAL_TM = 512
MOE_BM = 512
N_SLOT_BLOCKS = T_ALL * TOP_K // MOE_BM + N_EXPERTS
N_SLOTS = N_SLOT_BLOCKS * MOE_BM
HALF_W = D_MODEL // 4
SC_WINDOW = 128


def _cparams(sem, vmem=VMEM_LIMIT):
    return pltpu.CompilerParams(dimension_semantics=sem, vmem_limit_bytes=vmem)


def _sigmoid(x):
    return 0.5 * jnp.tanh(0.5 * x) + 0.5


def _pack_bf16_pair(x):
    w = x.shape[-1] // 2
    hi = pltpu.bitcast(x[:, :w].astype(BF16).astype(F32), jnp.uint32)
    lo = pltpu.bitcast(x[:, w:].astype(BF16).astype(F32), jnp.uint32)
    return hi | (lo >> 16)


def _unpack_bf16_pair(p):
    hi = pltpu.bitcast(p & jnp.uint32(0xFFFF0000), F32)
    lo = pltpu.bitcast(p << 16, F32)
    return jnp.concatenate([hi, lo], axis=-1)


def _pack_row_halves(x):
    half = x.shape[-1] // 2
    return _pack_bf16_pair(x[:, :half]), _pack_bf16_pair(x[:, half:])


def _unpack_row_halves(pa, pb):
    return jnp.concatenate([_unpack_bf16_pair(pa), _unpack_bf16_pair(pb)], axis=-1)


def _in_proj_kernel(xp_ref, xs_ref, g_ref, w_ref, cos_ref, sin_ref, z_ref, h_ref, p_ref, p2_ref):
    i = pl.program_id(0)
    j = pl.program_id(1)

    def norm_into_h(x_ref):
        xf = x_ref[...]
        ms = jnp.mean(xf * xf, axis=-1, keepdims=True)
        h_ref[...] = (xf * lax.rsqrt(ms + EPS) * g_ref[...]).astype(BF16)

    @pl.when((j == 0) & (i < N_PROMPT_SEG))
    def _():
        norm_into_h(xp_ref)

    @pl.when((j == 0) & (i >= N_PROMPT_SEG))
    def _():
        norm_into_h(xs_ref)

    n_slab = COL_BLK // LANES

    def project():
        return jnp.dot(h_ref[...], w_ref[...], preferred_element_type=F32)

    is_rope = j < (COL_RV // COL_BLK)
    att0 = COL_ATT // COL_BLK
    is_d4 = (j >= att0 + 3) & (j < att0 + 6)
    is_d16 = (j >= att0 + 6) & (j < att0 + 9)

    @pl.when(is_rope)
    def _():
        acc = project()
        scale = jnp.where(j == COL_RK // COL_BLK, RET_DK ** -0.5, 1.0).astype(F32)
        c = cos_ref[...]
        sn = sin_ref[...]
        for s in range(n_slab):
            xs = acc[:, s * LANES:(s + 1) * LANES]
            r = xs * c + pltpu.roll(xs, RET_DK // 2, axis=1) * sn
            z_ref[:, s * LANES:(s + 1) * LANES] = (r * scale).astype(BF16)

    @pl.when(is_d4)
    def _():
        acc = project()
        for s in range(n_slab):
            p_ref[s] = acc[:, s * LANES:(s + 1) * LANES]
        rows = SEG // 4
        for rho in range(4):
            for s in range(n_slab):
                piece = p_ref[s, pl.ds(rho, rows, stride=4), :]
                z_ref[rho * rows:(rho + 1) * rows, s * LANES:(s + 1) * LANES] = piece.astype(BF16)

    @pl.when(is_d16)
    def _():
        acc = project()
        for s in range(n_slab):
            p_ref[s] = acc[:, s * LANES:(s + 1) * LANES]
        quarter = SEG // 4
        rows = SEG // 16
        for r4 in range(4):
            for s in range(n_slab):
                p2_ref[s, r4 * quarter:(r4 + 1) * quarter, :] = p_ref[s, pl.ds(r4, quarter, stride=4), :]
        for r4 in range(4):
            for hi in range(4):
                rho = 4 * hi + r4
                for s in range(n_slab):
                    piece = p2_ref[s, pl.ds(r4 * quarter + hi, rows, stride=4), :]
                    z_ref[rho * rows:(rho + 1) * rows, s * LANES:(s + 1) * LANES] = piece.astype(BF16)

    @pl.when(jnp.logical_not(is_rope | is_d4 | is_d16))
    def _():
        z_ref[...] = project().astype(BF16)


def _in_proj(x_p, x_s, norm_g, w_in_bf, cos_t, sin_t):
    def pos_blk(i, j):
        return (jnp.maximum(i - N_PROMPT_SEG, 0), 0)

    return pl.pallas_call(
        _in_proj_kernel,
        grid=(N_SEG, N_COL_BLK),
        in_specs=[
            pl.BlockSpec((SEG, D_MODEL), lambda i, j: (jnp.minimum(i, N_PROMPT_SEG - 1), 0),
                         pipeline_mode=pl.Buffered(1)),
            pl.BlockSpec((SEG, D_MODEL), pos_blk, pipeline_mode=pl.Buffered(1)),
            pl.BlockSpec((1, D_MODEL), lambda i, j: (0, 0)),
            pl.BlockSpec((D_MODEL, COL_BLK), lambda i, j: (0, j)),
            pl.BlockSpec((SEG, LANES), pos_blk),
            pl.BlockSpec((SEG, LANES), pos_blk),
        ],
        out_specs=pl.BlockSpec((SEG, COL_BLK), lambda i, j: (i, j)),
        out_shape=jax.ShapeDtypeStruct((T_ALL, N_IN), BF16),
        scratch_shapes=[
            pltpu.VMEM((SEG, D_MODEL), BF16),
            pltpu.VMEM((COL_BLK // LANES, SEG, LANES), F32),
            pltpu.VMEM((COL_BLK // LANES, SEG, LANES), F32),
        ],
        compiler_params=_cparams(("arbitrary", "arbitrary")),
        name="in_proj",
    )(x_p, x_s, norm_g, w_in_bf, cos_t, sin_t)


RET_CHUNKS_PER_SEG = SEG // RET_CHUNK
RET_MAX_CHUNKS = SAMPLE_SEQ // RET_CHUNK
RET_GROUP = 16


def _retention_kernel(seg_ref, phase_ref, reset_ref, cbase_ref,
                      q_ref, k_ref, v_ref, g_ref, mask_ref, dec_ref, cdec_ref, gn_ref,
                      y_ref, sb_ref, sf_ref, sr_ref):
    step = pl.program_id(1)
    phase = phase_ref[step]
    reset = reset_ref[step]
    cbase = cbase_ref[step]
    kdec_f = dec_ref[0, 0]
    qdec_f = dec_ref[0, 1]
    kdec_b = dec_ref[0, 2]
    qdec_b = dec_ref[0, 3]
    cd_f = cdec_ref[0, 0]
    cd_b = cdec_ref[0, 1]

    def kv_outer(kd, v):
        return lax.dot_general(kd, v, (((0,), (0,)), ((), ())), preferred_element_type=F32)

    @pl.when((phase == 0) & (reset == 1))
    def _():
        sr_ref[...] = jnp.zeros_like(sr_ref)

    @pl.when((phase == 1) & (reset == 1))
    def _():
        sf_ref[...] = jnp.zeros_like(sf_ref)

    n_groups = RET_CHUNKS_PER_SEG // RET_GROUP

    def chunk_rows(c):
        return pl.ds(pl.multiple_of(c * RET_CHUNK, RET_CHUNK), RET_CHUNK)

    @pl.when(phase == 0)
    def _():
        def body(it, carry):
            top = RET_CHUNKS_PER_SEG - 1 - it * RET_GROUP
            kvs = []
            for j in range(RET_GROUP):
                rows = chunk_rows(top - j)
                kd = (k_ref[rows, :].astype(F32) * kdec_b).astype(BF16)
                kvs.append(kv_outer(kd, v_ref[rows, :]))
            state = sr_ref[...]
            for j in range(RET_GROUP):
                sb_ref[cbase + top - j] = state.astype(BF16)
                state = cd_b * state + kvs[j]
            sr_ref[...] = state
            return carry

        lax.fori_loop(0, n_groups, body, 0)

    @pl.when(phase == 1)
    def _():
        msk = mask_ref[0]
        gn = gn_ref[0]

        def body(it, carry):
            c0 = it * RET_GROUP
            lhs, vs, kvs = [], [], []
            for j in range(RET_GROUP):
                rows = chunk_rows(c0 + j)
                qb = q_ref[rows, :]
                kb = k_ref[rows, :]
                v = v_ref[rows, :]
                q = qb.astype(F32)
                s = lax.dot_general(qb, kb, (((1,), (1,)), ((), ())), preferred_element_type=F32)
                lhs.append(jnp.concatenate(
                    [(s * msk).astype(BF16), (q * qdec_f).astype(BF16), (q * qdec_b).astype(BF16)], axis=-1))
                vs.append(v)
                kvs.append(kv_outer((kb.astype(F32) * kdec_f).astype(BF16), v))
            state = sf_ref[...]
            for j in range(RET_GROUP):
                c = c0 + j
                rhs = jnp.concatenate([vs[j], state.astype(BF16), sb_ref[cbase + c]], axis=0)
                o = jnp.dot(lhs[j], rhs, preferred_element_type=F32)
                state = cd_f * state + kvs[j]
                mu = jnp.mean(o, axis=-1, keepdims=True)
                oc = o - mu
                var = jnp.mean(oc * oc, axis=-1, keepdims=True)
                on = oc * lax.rsqrt(var + EPS) * gn
                rows = chunk_rows(c)
                gate = g_ref[rows, :].astype(F32)
                y_ref[rows, :] = (gate * jax.nn.sigmoid(gate) * on).astype(BF16)
            sf_ref[...] = state
            return carry

        lax.fori_loop(0, n_groups, body, 0)


def _retention_schedule():
    seg, phase, reset, cbase = [], [], [], []
    for p in range(N_PROMPT_SEG):
        for ph in (0, 1):
            seg.append(p); phase.append(ph); reset.append(1); cbase.append(0)
    n_s = N_SEG - N_PROMPT_SEG
    for i in range(n_s):
        t = n_s - 1 - i
        seg.append(N_PROMPT_SEG + t); phase.append(0); reset.append(int(i == 0)); cbase.append(t * RET_CHUNKS_PER_SEG)
    for t in range(n_s):
        seg.append(N_PROMPT_SEG + t); phase.append(1); reset.append(int(t == 0)); cbase.append(t * RET_CHUNKS_PER_SEG)
    hold = list(seg)
    for i in range(len(seg)):
        if phase[i] == 0:
            nxt = next(j for j in range(i + 1, len(seg)) if phase[j] == 1)
            hold[i] = seg[nxt]
    arr = lambda a: jnp.asarray(np.asarray(a, np.int32))
    return arr(seg), arr(phase), arr(reset), arr(cbase), arr(hold)


def _retention(z, ret_mask, ret_dec, ret_cdec, gn_g):
    seg, phase, reset, cbase, hold = _retention_schedule()
    n_steps = int(seg.shape[0])
    qk_blk = lambda col: (lambda h, s, seg_r, ph_r, rs_r, cb_r, hold_r: (seg_r[s], col // RET_DK + h))
    hold_blk = lambda col, w: (lambda h, s, seg_r, ph_r, rs_r, cb_r, hold_r: (hold_r[s], col // w + h))
    v_blk = lambda h, s, seg_r, ph_r, rs_r, cb_r, hold_r: (seg_r[s], COL_RV // RET_DV + h)
    per_head = lambda h, s, *_: (h, 0, 0)
    per_head4 = lambda h, s, *_: (h, 0, 0, 0)
    grid_spec = pltpu.PrefetchScalarGridSpec(
        num_scalar_prefetch=5,
        grid=(RET_HEADS, n_steps),
        in_specs=[
            pl.BlockSpec((SEG, RET_DK), hold_blk(COL_RQ, RET_DK)),
            pl.BlockSpec((SEG, RET_DK), qk_blk(COL_RK)),
            pl.BlockSpec((SEG, RET_DV), v_blk),
            pl.BlockSpec((SEG, RET_DV), hold_blk(COL_RG, RET_DV)),
            pl.BlockSpec((1, RET_CHUNK, RET_CHUNK), per_head),
            pl.BlockSpec((1, 4, RET_CHUNK, 1), per_head4),
            pl.BlockSpec((1, 2, 1, RET_DV), per_head4),
            pl.BlockSpec((1, 1, RET_DV), per_head),
        ],
        out_specs=pl.BlockSpec((SEG, RET_DV), lambda h, s, seg_r, ph_r, rs_r, cb_r, hold_r: (hold_r[s], h)),
        scratch_shapes=[
            pltpu.VMEM((RET_MAX_CHUNKS, RET_DK, RET_DV), BF16),
            pltpu.VMEM((RET_DK, RET_DV), F32),
            pltpu.VMEM((RET_DK, RET_DV), F32),
        ],
    )

    def kernel(seg_r, ph_r, rs_r, cb_r, hold_r, *refs):
        _retention_kernel(seg_r, ph_r, rs_r, cb_r, *refs)

    return pl.pallas_call(
        kernel,
        grid_spec=grid_spec,
        out_shape=jax.ShapeDtypeStruct((T_ALL, RET_V_W), BF16),
        compiler_params=_cparams(("arbitrary", "arbitrary")),
        name="retention",
    )(seg, phase, reset, cbase, hold, z, z, z, z, ret_mask, ret_dec, ret_cdec, gn_g)


def _attention_kernel(first_ref, last_ref, q_ref, kp_ref, km_ref, kn_ref, vp_ref, vm_ref, vn_ref,
                      bias_ref, gq_ref, gk_ref, o_ref, lse_ref, kall, vall, qall, oacc, bvar, *, dil):
    nb = SEG // dil // ATT_BLOCK
    nqb = nb // 2
    n_slab = ATT_W // LANES
    c = pl.program_id(0)
    is_first = first_ref[c]
    is_last = last_ref[c]
    lane = lax.broadcasted_iota(jnp.int32, (1, LANES), 1)
    lo = lane < ATT_DH
    gq = gq_ref[...]
    gk = gk_ref[...]

    @pl.when(c == 0)
    def _():
        col = lax.broadcasted_iota(jnp.int32, (1, KW), 1)
        left = jnp.where(col < ATT_BLOCK, NEG, 0.0).astype(F32)
        right = jnp.where(col >= KW - ATT_BLOCK, NEG, 0.0).astype(F32)
        for h in range(ATT_HEADS):
            b = bias_ref[h]
            bvar[0, h] = b
            bvar[1, h] = b + left
            bvar[2, h] = b + right
            bvar[3, h] = b + left + right

    def head_norm(x, g):
        x2 = x * x
        s_lo = jnp.sum(jnp.where(lo, x2, 0.0), axis=-1, keepdims=True)
        s_hi = jnp.sum(jnp.where(lo, 0.0, x2), axis=-1, keepdims=True)
        ms = jnp.where(lo, s_lo, s_hi) * (1.0 / ATT_DH)
        return x * lax.rsqrt(ms + EPS) * g

    def norm_block(src):
        even, odd = [], []
        for s in range(n_slab):
            xn = head_norm(src[:, s * LANES:(s + 1) * LANES].astype(F32), gk)
            even.append(jnp.where(lo, xn, 0.0).astype(BF16))
            odd.append(jnp.where(lo, 0.0, xn).astype(BF16))
        return jnp.concatenate(even, axis=-1), jnp.concatenate(odd, axis=-1)

    lo_wide = lax.broadcasted_iota(jnp.int32, (1, ATT_W), 1) % LANES < ATT_DH

    def split_heads(v):
        zero = jnp.zeros_like(v)
        return jnp.where(lo_wide, v, zero), jnp.where(lo_wide, zero, v)

    def fill_main(it, carry):
        rho = it // nb
        blk = it % nb
        kall[0, rho, blk + 1], kall[1, rho, blk + 1] = norm_block(km_ref[rho, blk])
        vall[0, rho, blk + 1], vall[1, rho, blk + 1] = split_heads(vm_ref[rho, blk])
        qsrc = q_ref[rho, blk]
        qall[rho, blk] = jnp.concatenate(
            [head_norm(qsrc[:, s * LANES:(s + 1) * LANES].astype(F32), gq).astype(BF16) for s in range(n_slab)], axis=-1)
        return carry

    lax.fori_loop(0, dil * nb, fill_main, 0, unroll=4)

    def fill_halo(rho, carry):
        kall[0, rho, 0], kall[1, rho, 0] = norm_block(kp_ref[rho, 0])
        kall[0, rho, nb + 1], kall[1, rho, nb + 1] = norm_block(kn_ref[rho, 0])
        vall[0, rho, 0], vall[1, rho, 0] = split_heads(vp_ref[rho, 0])
        vall[0, rho, nb + 1], vall[1, rho, nb + 1] = split_heads(vn_ref[rho, 0])
        return carry

    lax.fori_loop(0, dil, fill_halo, 0)

    ones_even = jnp.broadcast_to(jnp.where(lo, 1.0, 0.0).astype(BF16), (KW, LANES))
    ones_odd = jnp.broadcast_to(jnp.where(lo, 0.0, 1.0).astype(BF16), (KW, LANES))

    def body(it, carry):
        rho = it // nqb
        qb = it % nqb
        var = (jnp.where((qb == 0) & (is_first == 1), 1, 0)
               + jnp.where((qb == nqb - 1) & (is_last == 1), 2, 0))
        start = rho + qb * (QB * dil)
        rows = pl.ds(start, QB) if dil == 1 else pl.ds(start, QB, stride=dil)
        for s in range(n_slab):
            sl = slice(s * LANES, (s + 1) * LANES)
            qn = qall[rho, pl.ds(2 * qb, 2), :, sl].reshape(QB, LANES)
            es, ms = [], []
            for hh in range(2):
                kw = kall[hh, rho, pl.ds(2 * qb, 4), :, sl].reshape(KW, LANES)
                sc = lax.dot_general(qn, kw, (((1,), (1,)), ((), ())), preferred_element_type=F32)
                sc = sc + bvar[var, 2 * s + hh]
                m = jnp.max(sc, axis=-1, keepdims=True)
                es.append(jnp.exp2(sc - m).astype(BF16))
                ms.append(m)
            v_even = vall[0, rho, pl.ds(2 * qb, 4), :, sl].reshape(KW, LANES)
            v_odd = vall[1, rho, pl.ds(2 * qb, 4), :, sl].reshape(KW, LANES)
            rhs = jnp.concatenate([jnp.concatenate([v_even, ones_even], axis=1),
                                   jnp.concatenate([v_odd, ones_odd], axis=1)], axis=0)
            res = jnp.dot(jnp.concatenate(es, axis=1), rhs, preferred_element_type=F32)
            den = res[:, LANES:]
            oacc[s, rows, :] = res[:, :LANES] * (1.0 / den)
            lse_ref[s, rows, :] = (jnp.where(lo, ms[0], ms[1]) + jnp.log2(den)) * LN2
        return carry

    lax.fori_loop(0, dil * nqb, body, 0, unroll=8)

    for s in range(n_slab):
        o_ref[:, s * LANES:(s + 1) * LANES] = oacc[s].astype(BF16)


def _attention_group(z, bias_g, gq, gk, first, last, gi, dil):
    nb = SEG // dil // ATT_BLOCK
    z5 = z.reshape(N_SEG, dil, nb, ATT_BLOCK, N_IN)
    cq = (COL_ATT + 3 * gi * ATT_W) // ATT_W
    ck, cv = cq + 1, cq + 2
    main = lambda cb: pl.BlockSpec((None, dil, nb, ATT_BLOCK, ATT_W), lambda c, f, l: (c, 0, 0, 0, cb))
    prev = lambda cb: pl.BlockSpec((None, dil, 1, ATT_BLOCK, ATT_W),
                                   lambda c, f, l: (c - 1 + f[c], 0, nb - 1, 0, cb), pipeline_mode=pl.Buffered(1))
    nxt = lambda cb: pl.BlockSpec((None, dil, 1, ATT_BLOCK, ATT_W),
                                  lambda c, f, l: (c + 1 - l[c], 0, 0, 0, cb), pipeline_mode=pl.Buffered(1))
    grid_spec = pltpu.PrefetchScalarGridSpec(
        num_scalar_prefetch=2,
        grid=(N_SEG,),
        in_specs=[
            main(cq), prev(ck), main(ck), nxt(ck), prev(cv), main(cv), nxt(cv),
            pl.BlockSpec((ATT_HEADS, QB, KW), lambda c, f, l: (0, 0, 0)),
            pl.BlockSpec((1, LANES), lambda c, f, l: (0, 0)),
            pl.BlockSpec((1, LANES), lambda c, f, l: (0, 0)),
        ],
        out_specs=[
            pl.BlockSpec((SEG, ATT_W), lambda c, f, l: (c, 0)),
            pl.BlockSpec((ATT_W // LANES, SEG, LANES), lambda c, f, l: (0, c, 0)),
        ],
        scratch_shapes=[
            pltpu.VMEM((2, dil, nb + 2, ATT_BLOCK, ATT_W), BF16),
            pltpu.VMEM((2, dil, nb + 2, ATT_BLOCK, ATT_W), BF16),
            pltpu.VMEM((dil, nb, ATT_BLOCK, ATT_W), BF16),
            pltpu.VMEM((ATT_W // LANES, SEG, LANES), F32),
            pltpu.VMEM((4, ATT_HEADS, QB, KW), F32),
        ],
    )
    return pl.pallas_call(
        functools.partial(_attention_kernel, dil=dil),
        grid_spec=grid_spec,
        out_shape=[jax.ShapeDtypeStruct((T_ALL, ATT_W), BF16), jax.ShapeDtypeStruct((ATT_W // LANES, T_ALL, LANES), F32)],
        compiler_params=_cparams(("arbitrary",), ATT_VMEM_LIMIT),
        name=f"attention_d{dil}",
    )(first, last, z5, z5, z5, z5, z5, z5, z5, bias_g, gq, gk)


def _merge_kernel(yret_ref, o0_ref, o1_ref, o2_ref, l0_ref, l1_ref, l2_ref,
                  gret_a_ref, gret_b_ref, gatt_a_ref, gatt_b_ref, xp_ref, xs_ref,
                  wret_ref, watt_ref, wout_ref, nffn_ref, wr_ref, wrhi_ref, br_ref,
                  x1_ref, hpa_ref, hpb_ref, idx_ref, gate_ref, rank_ref, cnt_ref, carry_ref):
    i = pl.program_id(0)
    tm = MERGE_TM
    sub = MERGE_SUB

    @pl.when(i == 0)
    def _():
        carry_ref[...] = jnp.zeros_like(carry_ref)

    is_prompt = i < T_PROMPT // tm
    lane = lax.broadcasted_iota(jnp.int32, (sub, LANES), 1)
    lane_f = lane.astype(F32)
    tri = jnp.where(lax.broadcasted_iota(jnp.int32, (sub, sub), 1) < lax.broadcasted_iota(jnp.int32, (sub, sub), 0),
                    1.0, 0.0).astype(BF16)
    carry = carry_ref[...]

    for r0 in range(0, tm, sub):
        rows = slice(r0, r0 + sub)
        l0, l1, l2 = [jnp.concatenate([r[s, rows, :] for s in range(ATT_W // LANES)], axis=-1)
                      for r in (l0_ref, l1_ref, l2_ref)]
        lm = jnp.maximum(jnp.maximum(l0, l1), l2)
        e0, e1, e2 = jnp.exp(l0 - lm), jnp.exp(l1 - lm), jnp.exp(l2 - lm)
        inv = 1.0 / (e0 + e1 + e2)
        y_att = ((e0 * inv) * o0_ref[rows, :].astype(F32) + (e1 * inv) * o1_ref[rows, :].astype(F32)
                 + (e2 * inv) * o2_ref[rows, :].astype(F32))

        p_ret = jnp.dot(yret_ref[rows, :], wret_ref[...], preferred_element_type=F32)
        p_att = jnp.dot(y_att.astype(BF16), watt_ref[...], preferred_element_type=F32)
        g_ret = jnp.concatenate([gret_a_ref[rows, :], gret_b_ref[rows, :]], axis=-1).astype(F32)
        g_att = jnp.concatenate([gatt_a_ref[rows, :], gatt_b_ref[rows, :]], axis=-1).astype(F32)
        merged = _sigmoid(g_ret) * p_ret + _sigmoid(g_att) * p_att
        x_in = jnp.where(is_prompt, xp_ref[rows, :], xs_ref[rows, :])
        x1 = x_in + jnp.dot(merged.astype(BF16), wout_ref[...], preferred_element_type=F32)
        x1_ref[rows, :] = x1

        ms = jnp.mean(x1 * x1, axis=-1, keepdims=True)
        h2 = x1 * lax.rsqrt(ms + EPS) * nffn_ref[...]
        hpa_ref[rows, :], hpb_ref[rows, :] = _pack_row_halves(h2)

        h_hi = h2.astype(BF16)
        h_lo = (h2 - h_hi.astype(F32)).astype(BF16)
        p1 = jnp.dot(h_hi, wr_ref[...], preferred_element_type=F32)
        p2 = jnp.dot(h_lo, wrhi_ref[...], preferred_element_type=F32)
        work = p1 + pltpu.roll(p1, LANES - N_EXPERTS, axis=1) + p2 + br_ref[...]
        vals, idxs = [], []
        for _ in range(TOP_K):
            m = jnp.max(work, axis=-1, keepdims=True)
            ix = jnp.min(jnp.where(work == m, lane_f, float(LANES)), axis=-1, keepdims=True)
            vals.append(m)
            idxs.append(ix)
            work = jnp.where(lane_f == ix, -3e38, work)
        es = [jnp.exp(v - vals[0]) for v in vals]
        den = es[0] + es[1] + es[2] + es[3]
        onehot = jnp.zeros((sub, LANES), F32)
        for ix in idxs:
            onehot = onehot + jnp.where(lane_f == ix, 1.0, 0.0)
        before = jnp.dot(tri, onehot.astype(BF16), preferred_element_type=F32) + carry
        idx_out = jnp.zeros((sub, LANES), F32)
        gate_out = jnp.zeros((sub, LANES), F32)
        rank_out = jnp.zeros((sub, LANES), F32)
        for k in range(TOP_K):
            rk = jnp.sum(jnp.where(lane_f == idxs[k], before, 0.0), axis=-1, keepdims=True)
            sel = lane == k
            idx_out = jnp.where(sel, idxs[k], idx_out)
            gate_out = jnp.where(sel, es[k] / den, gate_out)
            rank_out = jnp.where(sel, rk, rank_out)
        idx_ref[rows, :] = idx_out.astype(jnp.int32)
        gate_ref[rows, :] = gate_out
        rank_ref[rows, :] = rank_out.astype(jnp.int32)
        carry = carry + jnp.sum(onehot, axis=0, keepdims=True)

    carry_ref[...] = carry
    cnt_ref[...] = jnp.broadcast_to(carry, cnt_ref.shape)


def _merge(y_ret, o_list, lse_list, z, x_p, x_s, w_ret, w_att, w_out, n_ffn, w_router, w_router_hi, b_router):
    tm = MERGE_TM
    n_p = T_PROMPT // tm
    row = lambda w: pl.BlockSpec((tm, w), lambda i: (i, 0))
    full = lambda a: pl.BlockSpec(a.shape, lambda i: (0,) * a.ndim)
    zcol = lambda col: pl.BlockSpec((tm, COL_BLK), lambda i: (i, col // COL_BLK))
    lse_spec = pl.BlockSpec((ATT_W // LANES, tm, LANES), lambda i: (0, i, 0))
    return pl.pallas_call(
        _merge_kernel,
        grid=(T_ALL // tm,),
        in_specs=[row(RET_V_W), row(ATT_W), row(ATT_W), row(ATT_W), lse_spec, lse_spec, lse_spec,
                  zcol(COL_GATE_RET), zcol(COL_GATE_RET + COL_BLK), zcol(COL_GATE_ATT),
                  zcol(COL_GATE_ATT + COL_BLK),
                  pl.BlockSpec((tm, D_MODEL), lambda i: (jnp.minimum(i, n_p - 1), 0)),
                  pl.BlockSpec((tm, D_MODEL), lambda i: (jnp.maximum(i - n_p, 0), 0)),
                  full(w_ret), full(w_att), full(w_out), full(n_ffn), full(w_router), full(w_router_hi),
                  full(b_router)],
        out_specs=[row(D_MODEL), row(HALF_W), row(HALF_W), row(LANES), row(LANES), row(LANES),
                   pl.BlockSpec((8, LANES), lambda i: (0, 0))],
        out_shape=[jax.ShapeDtypeStruct((T_ALL, D_MODEL), F32),
                   jax.ShapeDtypeStruct((T_ALL, HALF_W), jnp.uint32),
                   jax.ShapeDtypeStruct((T_ALL, HALF_W), jnp.uint32),
                   jax.ShapeDtypeStruct((T_ALL, LANES), jnp.int32),
                   jax.ShapeDtypeStruct((T_ALL, LANES), F32),
                   jax.ShapeDtypeStruct((T_ALL, LANES), jnp.int32),
                   jax.ShapeDtypeStruct((8, LANES), F32)],
        scratch_shapes=[pltpu.VMEM((1, LANES), F32)],
        compiler_params=_cparams(("arbitrary",)),
        name="merge_router",
    )(y_ret, *o_list, *lse_list, z, z, z, z, x_p, x_s, w_ret, w_att, w_out, n_ffn, w_router, w_router_hi, b_router)


def _sc_mesh():
    return plsc.VectorSubcoreMesh(core_axis_name="core", subcore_axis_name="subcore")


def _sc_scatter_rows(xs, idx_kmajor, n_out):
    n_rows, width = xs[0].shape
    out_type = [jax.ShapeDtypeStruct((n_out, width), x.dtype) for x in xs]

    @pl.kernel(out_type=out_type, mesh=_sc_mesh(), scratch_types=[])
    def scatter(*refs):
        x_refs, i_hbm, o_refs = refs[:len(xs)], refs[len(xs)], refs[len(xs) + 1:]
        for x_hbm, o_hbm in zip(x_refs, o_refs):
            def body(x_vmem, i_vmem, o_hbm=o_hbm):
                for k in range(TOP_K):
                    pltpu.sync_copy(x_vmem, o_hbm.at[i_vmem.at[k]])

            pltpu.emit_pipeline(
                body,
                grid=(n_rows // SC_WINDOW,),
                in_specs=[pl.BlockSpec((SC_WINDOW, width), lambda i: (i, 0)),
                          pl.BlockSpec((TOP_K, SC_WINDOW), lambda i: (0, i))],
                out_specs=[],
                core_axis_name=("core", "subcore"),
                dimension_semantics=(pltpu.PARALLEL,),
            )(x_hbm, i_hbm)

    return scatter(*xs, idx_kmajor)


def _sc_gather_rows(datas, idx):
    n_idx = idx.shape[0]
    width = datas[0].shape[1]
    out_type = [jax.ShapeDtypeStruct((n_idx, width), d.dtype) for d in datas]

    @pl.kernel(out_type=out_type, mesh=_sc_mesh(), scratch_types=[])
    def gather(*refs):
        x_refs, i_hbm, o_refs = refs[:len(datas)], refs[len(datas)], refs[len(datas) + 1:]
        for x_hbm, o_hbm in zip(x_refs, o_refs):
            def body(i_vmem, o_vmem, x_hbm=x_hbm):
                pltpu.sync_copy(x_hbm.at[i_vmem.at[0]], o_vmem)

            pltpu.emit_pipeline(
                body,
                grid=(n_idx // SC_WINDOW,),
                in_specs=[pl.BlockSpec((1, SC_WINDOW), lambda i: (0, i))],
                out_specs=[pl.BlockSpec((SC_WINDOW, width), lambda i: (i, 0))],
                core_axis_name=("core", "subcore"),
                dimension_semantics=(pltpu.PARALLEL,),
            )(i_hbm, o_hbm)

    return gather(*datas, idx.reshape(1, n_idx))


def _expert_kernel(be_ref, nused_ref, nvalid_ref, xa_ref, xb_ref, wg_ref, bg_ref, wu_ref, bu_ref, wd_ref, bd_ref,
                   ya_ref, yb_ref, wbf_ref):
    b = pl.program_id(0)
    active = b < nused_ref[0]
    new_expert = (b == 0) | (be_ref[b] != be_ref[jnp.maximum(b - 1, 0)])

    @pl.when(active & new_expert)
    def _():
        rows = 128
        for wi, w_ref in enumerate((wg_ref, wu_ref, wd_ref)):
            for r in range(0, D_MODEL, rows):
                wbf_ref[wi, r:r + rows, :] = w_ref[0, r:r + rows, :].astype(BF16)

    @pl.when(active)
    def _():
        valid = lax.broadcasted_iota(jnp.int32, (MOE_BM, HALF_W), 0) < nvalid_ref[b]
        zero = jnp.zeros((MOE_BM, HALF_W), jnp.uint32)
        x = _unpack_row_halves(jnp.where(valid, xa_ref[...], zero), jnp.where(valid, xb_ref[...], zero)).astype(BF16)
        g = jnp.dot(x, wbf_ref[0], preferred_element_type=F32) + bg_ref[0]
        u = jnp.dot(x, wbf_ref[1], preferred_element_type=F32) + bu_ref[0]
        g = jnp.minimum(g, SWIGLU_LIMIT)
        u = jnp.clip(u, -SWIGLU_LIMIT, SWIGLU_LIMIT)
        glu = g * jax.nn.sigmoid(SWIGLU_ALPHA * g)
        act = ((u + 1.0) * glu).astype(BF16)
        y = jnp.dot(act, wbf_ref[2], preferred_element_type=F32) + bd_ref[0]
        ya_ref[...], yb_ref[...] = _pack_row_halves(y)

    @pl.when(jnp.logical_not(active))
    def _():
        ya_ref[...] = jnp.zeros_like(ya_ref)
        yb_ref[...] = jnp.zeros_like(yb_ref)


def _experts(block_expert, n_used, n_valid, xs_a, xs_b, wg, bg, wu, bu, wd, bd):
    assert D_FF == D_MODEL
    blk = lambda b, be, nu, nv: (jnp.minimum(b, nu[0] - 1), 0)
    wsp = lambda: pl.BlockSpec((1, D_MODEL, D_FF), lambda b, be, nu, nv: (be[b], 0, 0))
    bsp = lambda: pl.BlockSpec((1, 1, D_FF), lambda b, be, nu, nv: (be[b], 0, 0))
    xsp = lambda: pl.BlockSpec((MOE_BM, HALF_W), blk)
    ysp = lambda: pl.BlockSpec((MOE_BM, HALF_W), lambda b, be, nu, nv: (b, 0))
    slot_arr = jax.ShapeDtypeStruct((N_SLOTS, HALF_W), jnp.uint32)
    grid_spec = pltpu.PrefetchScalarGridSpec(
        num_scalar_prefetch=3,
        grid=(N_SLOT_BLOCKS,),
        in_specs=[xsp(), xsp(), wsp(), bsp(), wsp(), bsp(), wsp(), bsp()],
        out_specs=[ysp(), ysp()],
        scratch_shapes=[pltpu.VMEM((3, D_MODEL, D_FF), BF16)],
    )
    return pl.pallas_call(
        _expert_kernel,
        grid_spec=grid_spec,
        out_shape=[slot_arr, slot_arr],
        compiler_params=_cparams(("arbitrary",)),
        name="experts",
    )(block_expert, n_used, n_valid, xs_a, xs_b, wg, bg, wu, bu, wd, bd)


def _final_kernel(x1_ref, yga_ref, ygb_ref, gate_ref, p_ref, nple_ref, wpg_ref, wpp_ref, out_ref):
    x2 = x1_ref[...]
    gates = gate_ref[...]
    for k in range(TOP_K):
        x2 = x2 + gates[:, k:k + 1] * _unpack_row_halves(yga_ref[k], ygb_ref[k])
    ms = jnp.mean(x2 * x2, axis=-1, keepdims=True)
    h3 = (x2 * lax.rsqrt(ms + EPS) * nple_ref[...]).astype(BF16)
    gate = jax.nn.sigmoid(jnp.dot(h3, wpg_ref[...], preferred_element_type=F32))
    proj = jnp.dot(p_ref[...].astype(BF16), wpp_ref[...], preferred_element_type=F32)
    out_ref[...] = x2 + gate * proj


def _final(x1, yg_a, yg_b, gates, p, n_ple, w_pg, w_pp, row0, n_rows):
    tm = FINAL_TM
    off = row0 // tm
    full = lambda a: pl.BlockSpec(a.shape, lambda i: (0,) * a.ndim)
    return pl.pallas_call(
        _final_kernel,
        grid=(n_rows // tm,),
        in_specs=[pl.BlockSpec((tm, D_MODEL), lambda i: (i + off, 0)),
                  pl.BlockSpec((TOP_K, tm, HALF_W), lambda i: (0, i + off, 0)),
                  pl.BlockSpec((TOP_K, tm, HALF_W), lambda i: (0, i + off, 0)),
                  pl.BlockSpec((tm, LANES), lambda i: (i + off, 0)),
                  pl.BlockSpec((tm, PLE_DIM), lambda i: (i, 0)),
                  full(n_ple), full(w_pg), full(w_pp)],
        out_specs=pl.BlockSpec((tm, D_MODEL), lambda i: (i, 0)),
        out_shape=jax.ShapeDtypeStruct((n_rows, D_MODEL), F32),
        compiler_params=_cparams(("arbitrary",)),
        name="final_ple",
    )(x1, yg_a, yg_b, gates, p, n_ple, w_pg, w_pp)


def _rope_tables():
    half = RET_DK // 2
    step = 128
    freq = ROPE_THETA ** (-jnp.arange(half, dtype=F32) / half)
    ang_lo = jnp.arange(step, dtype=F32)[:, None] * freq[None, :]
    ang_hi = (jnp.arange(SAMPLE_SEQ // step, dtype=F32) * step)[:, None] * freq[None, :]
    c_lo, s_lo = jnp.cos(ang_lo)[None], jnp.sin(ang_lo)[None]
    c_hi, s_hi = jnp.cos(ang_hi)[:, None], jnp.sin(ang_hi)[:, None]
    cos = (c_hi * c_lo - s_hi * s_lo).reshape(SAMPLE_SEQ, half)
    sin = (s_hi * c_lo + c_hi * s_lo).reshape(SAMPLE_SEQ, half)
    return jnp.concatenate([cos, cos], axis=-1), jnp.concatenate([-sin, sin], axis=-1)


def _retention_tables(decay_logit):
    lg = jax.nn.log_sigmoid(decay_logit.astype(F32))
    c = RET_CHUNK
    idx = jnp.arange(c, dtype=F32)
    diff = idx[:, None] - idx[None, :]
    lf = lg[0][:, None, None]
    lb = lg[1][:, None, None]
    mask = jnp.where(diff[None] >= 0, jnp.exp(lf * jnp.maximum(diff, 0.0)[None]),
                     jnp.exp(lb * jnp.maximum(-diff, 0.0)[None]))
    kdec_f = jnp.exp(lg[0][:, None] * (c - 1.0 - idx)[None, :])
    qdec_f = jnp.exp(lg[0][:, None] * (idx + 1.0)[None, :])
    kdec_b = jnp.exp(lg[1][:, None] * idx[None, :])
    qdec_b = jnp.exp(lg[1][:, None] * (c - idx)[None, :])
    dec = jnp.stack([kdec_f, qdec_f, kdec_b, qdec_b], axis=1)[..., None]
    cdec = jnp.exp(lg * c).T
    cdec = jnp.broadcast_to(cdec[:, :, None, None], (RET_HEADS, 2, 1, RET_DV))
    return mask, dec, cdec


def _t5_bucket(rel):
    half = T5_BUCKETS // 2
    exact = half // 2
    n = np.abs(rel)
    ratio = np.log(np.maximum(n, 1).astype(np.float32) / np.float32(exact)) / np.float32(math.log(T5_MAX_DIST / exact))
    large = exact + (ratio * np.float32(half - exact)).astype(np.int32)
    large = np.minimum(large, half - 1)
    return np.where(rel > 0, half, 0) + np.where(n < exact, n, large)


def _attention_bias(rel_bias, gi, dil, radius):
    qi = np.arange(QB)
    ki = np.arange(KW) - ATT_BLOCK
    rel = ki[None, :] - qi[:, None]
    onehot = jnp.asarray(_t5_bucket(rel * dil)[..., None] == np.arange(T5_BUCKETS), F32)
    tab = rel_bias[:, gi * ATT_HEADS:(gi + 1) * ATT_HEADS].astype(F32)
    bias = jnp.einsum('qkb,bh->hqk', onehot, tab, precision=lax.Precision.HIGHEST)
    return jnp.where(jnp.asarray(np.abs(rel) <= radius)[None], bias * LOG2E, NEG)


def _seq_edge_flags():
    first = np.zeros((N_SEG,), np.int32)
    last = np.zeros((N_SEG,), np.int32)
    first[:N_PROMPT_SEG] = 1
    last[:N_PROMPT_SEG] = 1
    first[N_PROMPT_SEG] = 1
    last[N_SEG - 1] = 1
    return jnp.asarray(first), jnp.asarray(last)


def _pad_lanes(a, value=0.0):
    return jnp.pad(a, ((0, 0), (0, LANES - a.shape[-1])), constant_values=value)


def kernel(x_prompt, x_sample, p_prompt, p_sample, norm_mix_g, w_in, ret_decay_logit, ret_gn_g,
           att_q_norm_g, att_k_norm_g, rel_bias, w_ret_proj, w_att_proj, w_out, norm_ffn_g,
           w_router, b_router, w_gate, b_gate, w_up, b_up, w_down, b_down,
           norm_ple_g, w_ple_gate, w_ple_proj):
    assert norm_mix_g.shape[0] == 1, "one layer"
    x_p = x_prompt.reshape(T_PROMPT, D_MODEL)
    x_s = x_sample.reshape(SAMPLE_SEQ, D_MODEL)

    cos_t, sin_t = _rope_tables()
    z = _in_proj(x_p, x_s, norm_mix_g.astype(F32), w_in[0].astype(BF16), cos_t, sin_t)

    ret_mask, ret_dec, ret_cdec = _retention_tables(ret_decay_logit[0])
    y_ret = _retention(z, ret_mask, ret_dec, ret_cdec, ret_gn_g[0].reshape(RET_HEADS, 1, RET_DV).astype(F32))

    first, last = _seq_edge_flags()
    o_list, lse_list = [], []
    for gi, (window, dil) in enumerate(ATT_GROUPS):
        bias_g = _attention_bias(rel_bias, gi, dil, window // (2 * dil))
        gq = jnp.tile(att_q_norm_g[0, gi].astype(F32) * (ATT_DH ** -0.5 * LOG2E), LANES // ATT_DH)[None, :]
        gk = jnp.tile(att_k_norm_g[0, gi].astype(F32), LANES // ATT_DH)[None, :]
        o_g, lse_g = _attention_group(z, bias_g, gq, gk, first, last, gi, dil)
        o_list.append(o_g)
        lse_list.append(lse_g)

    w_r = w_router[0].astype(F32)
    w_r_hi = w_r.astype(BF16)
    w_r_lo = (w_r - w_r_hi.astype(F32)).astype(BF16)
    w_router_cat = _pad_lanes(jnp.concatenate([w_r_hi, w_r_lo], axis=1))
    w_router_hi = _pad_lanes(w_r_hi)
    b_router_p = _pad_lanes(b_router.astype(F32), NEG)
    x1, hp_a, hp_b, idx, gates, rank, cnt = _merge(
        y_ret, o_list, lse_list, z, x_p, x_s, w_ret_proj[0].astype(BF16), w_att_proj[0].astype(BF16),
        w_out[0].astype(BF16), norm_ffn_g.astype(F32), w_router_cat, w_router_hi, b_router_p)

    counts = cnt[0, :N_EXPERTS].astype(jnp.int32)
    padded = (counts + MOE_BM - 1) // MOE_BM * MOE_BM
    pad_end = jnp.cumsum(padded)
    pad_start = pad_end - padded
    expert_ids = jnp.arange(N_EXPERTS, dtype=jnp.int32)
    top_idx = idx[:, :TOP_K]
    start_of = jnp.sum(jnp.where(top_idx[:, :, None] == expert_ids, pad_start, 0), axis=-1)
    dest_kmajor = (start_of + rank[:, :TOP_K]).T
    n_used = (pad_end[-1] // MOE_BM).astype(jnp.int32).reshape(1)
    blk_row0 = jnp.arange(N_SLOT_BLOCKS, dtype=jnp.int32) * MOE_BM
    block_expert = jnp.minimum(jnp.sum((pad_end[None, :] <= blk_row0[:, None]).astype(jnp.int32), axis=1),
                               N_EXPERTS - 1).astype(jnp.int32)
    slot_end = jnp.sum(jnp.where(block_expert[:, None] == expert_ids, pad_start + counts, 0), axis=-1)
    n_valid = jnp.clip(slot_end - blk_row0, 0, MOE_BM).astype(jnp.int32)

    xs_a, xs_b = _sc_scatter_rows((hp_a, hp_b), dest_kmajor, N_SLOTS)
    ys_a, ys_b = _experts(block_expert, n_used, n_valid, xs_a, xs_b,
                          w_gate[0], b_gate[0].reshape(N_EXPERTS, 1, D_FF).astype(F32),
                          w_up[0], b_up[0].reshape(N_EXPERTS, 1, D_FF).astype(F32),
                          w_down[0], b_down[0].reshape(N_EXPERTS, 1, D_MODEL).astype(F32))
    dest_flat = dest_kmajor.reshape(-1)
    yg_a, yg_b = [y.reshape(TOP_K, T_ALL, HALF_W) for y in _sc_gather_rows((ys_a, ys_b), dest_flat)]

    n_ple = norm_ple_g.astype(F32)
    w_pg = w_ple_gate[0].astype(BF16)
    w_pp = w_ple_proj[0].astype(BF16)
    y_p = _final(x1, yg_a, yg_b, gates, p_prompt[0].reshape(T_PROMPT, PLE_DIM), n_ple, w_pg, w_pp, 0, T_PROMPT)
    y_s = _final(x1, yg_a, yg_b, gates, p_sample[0].reshape(SAMPLE_SEQ, PLE_DIM), n_ple, w_pg, w_pp, T_PROMPT, SAMPLE_SEQ)
    return (y_p.reshape(x_prompt.shape), y_s.reshape(x_sample.shape))
```

```python
import functools
import math

import jax
import jax.numpy as jnp
import numpy as np
from jax import lax
from jax.experimental import pallas as pl
from jax.experimental.pallas import tpu as pltpu
from jax.experimental.pallas import tpu_sc as plsc

F32 = jnp.float32
BF16 = jnp.bfloat16

D_MODEL = 1024
N_PROMPT_SEQ = 8
PROMPT_SEQ = 2048
SAMPLE_SEQ = 16384
T_PROMPT = N_PROMPT_SEQ * PROMPT_SEQ
T_ALL = T_PROMPT + SAMPLE_SEQ

RET_HEADS = 4
RET_DK = 128
RET_DV = 256
RET_CHUNK = 128
ROPE_THETA = 10000.0
ATT_GROUPS = ((128, 1), (512, 4), (2048, 16))
N_GROUPS = 3
ATT_HEADS = 8
ATT_DH = 64
ATT_BLOCK = 64
ATT_W = ATT_HEADS * ATT_DH
T5_BUCKETS = 32
T5_MAX_DIST = 1024
N_EXPERTS = 32
TOP_K = 4
D_FF = 1024
SWIGLU_ALPHA = 1.702
SWIGLU_LIMIT = 7.0
PLE_DIM = 256
EPS = 1e-6

RET_QK_W = RET_HEADS * RET_DK
RET_V_W = RET_HEADS * RET_DV
N_IN = 2 * RET_QK_W + 2 * RET_V_W + 3 * N_GROUPS * ATT_W + 2 * D_MODEL

COL_RQ = 0
COL_RK = RET_QK_W
COL_RV = 2 * RET_QK_W
COL_RG = COL_RV + RET_V_W
COL_ATT = COL_RG + RET_V_W
COL_GATE_RET = COL_ATT + 3 * N_GROUPS * ATT_W
COL_GATE_ATT = COL_GATE_RET + D_MODEL

LANES = 128
VMEM_LIMIT = 56 * 1024 * 1024
ATT_VMEM_LIMIT = 58 * 1024 * 1024

SEG = 2048
N_SEG = T_ALL // SEG
N_PROMPT_SEG = T_PROMPT // SEG
COL_BLK = 512
N_COL_BLK = N_IN // COL_BLK
QB = 128
KW = 256
NEG = -1e30
LOG2E = math.log2(math.e)
LN2 = math.log(2.0)
MERGE_TM = 512
MERGE_SUB = 512
FINAL_TM = 1024
MOE_BM = 1024
N_SLOT_BLOCKS = T_ALL * TOP_K // MOE_BM + N_EXPERTS
N_SLOTS = N_SLOT_BLOCKS * MOE_BM
HALF_W = D_MODEL // 4
SC_WINDOW = 128


def _cparams(sem, vmem=VMEM_LIMIT):
    return pltpu.CompilerParams(dimension_semantics=sem, vmem_limit_bytes=vmem)


def _sigmoid(x):
    return 0.5 * jnp.tanh(0.5 * x) + 0.5


def _pack_bf16_pair(x):
    w = x.shape[-1] // 2
    hi = pltpu.bitcast(x[:, :w].astype(BF16).astype(F32), jnp.uint32)
    lo = pltpu.bitcast(x[:, w:].astype(BF16).astype(F32), jnp.uint32)
    return hi | (lo >> 16)


def _unpack_bf16_pair(p):
    hi = pltpu.bitcast(p & jnp.uint32(0xFFFF0000), F32)
    lo = pltpu.bitcast(p << 16, F32)
    return jnp.concatenate([hi, lo], axis=-1)


def _pack_row_halves(x):
    half = x.shape[-1] // 2
    return _pack_bf16_pair(x[:, :half]), _pack_bf16_pair(x[:, half:])


def _unpack_row_halves(pa, pb):
    return jnp.concatenate([_unpack_bf16_pair(pa), _unpack_bf16_pair(pb)], axis=-1)


def _in_proj_kernel(xp_ref, xs_ref, g_ref, w_ref, cos_ref, sin_ref, z_ref, h_ref, p_ref, p2_ref):
    i = pl.program_id(0)
    j = pl.program_id(1)

    def norm_into_h(x_ref):
        xf = x_ref[...]
        ms = jnp.mean(xf * xf, axis=-1, keepdims=True)
        h_ref[...] = (xf * lax.rsqrt(ms + EPS) * g_ref[...]).astype(BF16)

    @pl.when((j == 0) & (i < N_PROMPT_SEG))
    def _():
        norm_into_h(xp_ref)

    @pl.when((j == 0) & (i >= N_PROMPT_SEG))
    def _():
        norm_into_h(xs_ref)

    n_slab = COL_BLK // LANES

    def project():
        return jnp.dot(h_ref[...], w_ref[...], preferred_element_type=F32)

    is_rope = j < (COL_RV // COL_BLK)
    att0 = COL_ATT // COL_BLK
    is_d4 = (j >= att0 + 3) & (j < att0 + 6)
    is_d16 = (j >= att0 + 6) & (j < att0 + 9)

    @pl.when(is_rope)
    def _():
        acc = project()
        scale = jnp.where(j == COL_RK // COL_BLK, RET_DK ** -0.5, 1.0).astype(F32)
        c = cos_ref[...]
        sn = sin_ref[...]
        for s in range(n_slab):
            xs = acc[:, s * LANES:(s + 1) * LANES]
            r = xs * c + pltpu.roll(xs, RET_DK // 2, axis=1) * sn
            z_ref[:, s * LANES:(s + 1) * LANES] = (r * scale).astype(BF16)

    @pl.when(is_d4)
    def _():
        acc = project()
        for s in range(n_slab):
            p_ref[s] = acc[:, s * LANES:(s + 1) * LANES]
        rows = SEG // 4
        for rho in range(4):
            for s in range(n_slab):
                piece = p_ref[s, pl.ds(rho, rows, stride=4), :]
                z_ref[rho * rows:(rho + 1) * rows, s * LANES:(s + 1) * LANES] = piece.astype(BF16)

    @pl.when(is_d16)
    def _():
        acc = project()
        for s in range(n_slab):
            p_ref[s] = acc[:, s * LANES:(s + 1) * LANES]
        quarter = SEG // 4
        rows = SEG // 16
        for r4 in range(4):
            for s in range(n_slab):
                p2_ref[s, r4 * quarter:(r4 + 1) * quarter, :] = p_ref[s, pl.ds(r4, quarter, stride=4), :]
        for r4 in range(4):
            for hi in range(4):
                rho = 4 * hi + r4
                for s in range(n_slab):
                    piece = p2_ref[s, pl.ds(r4 * quarter + hi, rows, stride=4), :]
                    z_ref[rho * rows:(rho + 1) * rows, s * LANES:(s + 1) * LANES] = piece.astype(BF16)

    @pl.when(jnp.logical_not(is_rope | is_d4 | is_d16))
    def _():
        z_ref[...] = project().astype(BF16)


def _in_proj(x_p, x_s, norm_g, w_in_bf, cos_t, sin_t):
    def pos_blk(i, j):
        return (jnp.maximum(i - N_PROMPT_SEG, 0), 0)

    return pl.pallas_call(
        _in_proj_kernel,
        grid=(N_SEG, N_COL_BLK),
        in_specs=[
            pl.BlockSpec((SEG, D_MODEL), lambda i, j: (jnp.minimum(i, N_PROMPT_SEG - 1), 0),
                         pipeline_mode=pl.Buffered(1)),
            pl.BlockSpec((SEG, D_MODEL), pos_blk, pipeline_mode=pl.Buffered(1)),
            pl.BlockSpec((1, D_MODEL), lambda i, j: (0, 0)),
            pl.BlockSpec((D_MODEL, COL_BLK), lambda i, j: (0, j)),
            pl.BlockSpec((SEG, LANES), pos_blk),
            pl.BlockSpec((SEG, LANES), pos_blk),
        ],
        out_specs=pl.BlockSpec((SEG, COL_BLK), lambda i, j: (i, j)),
        out_shape=jax.ShapeDtypeStruct((T_ALL, N_IN), BF16),
        scratch_shapes=[
            pltpu.VMEM((SEG, D_MODEL), BF16),
            pltpu.VMEM((COL_BLK // LANES, SEG, LANES), F32),
            pltpu.VMEM((COL_BLK // LANES, SEG, LANES), F32),
        ],
        compiler_params=_cparams(("arbitrary", "arbitrary")),
        name="in_proj",
    )(x_p, x_s, norm_g, w_in_bf, cos_t, sin_t)


RET_CHUNKS_PER_SEG = SEG // RET_CHUNK
RET_MAX_CHUNKS = SAMPLE_SEQ // RET_CHUNK
RET_GROUP = 16


def _retention_kernel(seg_ref, phase_ref, reset_ref, cbase_ref,
                      q_ref, k_ref, v_ref, g_ref, mask_ref, dec_ref, cdec_ref, gn_ref,
                      y_ref, sb_ref, sf_ref, sr_ref):
    step = pl.program_id(1)
    phase = phase_ref[step]
    reset = reset_ref[step]
    cbase = cbase_ref[step]
    kdec_f = dec_ref[0, 0]
    qdec_f = dec_ref[0, 1]
    kdec_b = dec_ref[0, 2]
    qdec_b = dec_ref[0, 3]
    cd_f = cdec_ref[0, 0]
    cd_b = cdec_ref[0, 1]

    def kv_outer(kd, v):
        return lax.dot_general(kd, v, (((0,), (0,)), ((), ())), preferred_element_type=F32)

    @pl.when((phase == 0) & (reset == 1))
    def _():
        sr_ref[...] = jnp.zeros_like(sr_ref)

    @pl.when((phase == 1) & (reset == 1))
    def _():
        sf_ref[...] = jnp.zeros_like(sf_ref)

    n_groups = RET_CHUNKS_PER_SEG // RET_GROUP

    def chunk_rows(c):
        return pl.ds(pl.multiple_of(c * RET_CHUNK, RET_CHUNK), RET_CHUNK)

    @pl.when(phase == 0)
    def _():
        def body(it, carry):
            top = RET_CHUNKS_PER_SEG - 1 - it * RET_GROUP
            kvs = []
            for j in range(RET_GROUP):
                rows = chunk_rows(top - j)
                kd = (k_ref[rows, :].astype(F32) * kdec_b).astype(BF16)
                kvs.append(kv_outer(kd, v_ref[rows, :]))
            state = sr_ref[...]
            for j in range(RET_GROUP):
                sb_ref[cbase + top - j] = state.astype(BF16)
                state = cd_b * state + kvs[j]
            sr_ref[...] = state
            return carry

        lax.fori_loop(0, n_groups, body, 0)

    @pl.when(phase == 1)
    def _():
        msk = mask_ref[0]
        gn = gn_ref[0]

        def body(it, carry):
            c0 = it * RET_GROUP
            lhs, vs, kvs = [], [], []
            for j in range(RET_GROUP):
                rows = chunk_rows(c0 + j)
                qb = q_ref[rows, :]
                kb = k_ref[rows, :]
                v = v_ref[rows, :]
                q = qb.astype(F32)
                s = lax.dot_general(qb, kb, (((1,), (1,)), ((), ())), preferred_element_type=F32)
                lhs.append(jnp.concatenate(
                    [(s * msk).astype(BF16), (q * qdec_f).astype(BF16), (q * qdec_b).astype(BF16)], axis=-1))
                vs.append(v)
                kvs.append(kv_outer((kb.astype(F32) * kdec_f).astype(BF16), v))
            state = sf_ref[...]
            for j in range(RET_GROUP):
                c = c0 + j
                rhs = jnp.concatenate([vs[j], state.astype(BF16), sb_ref[cbase + c]], axis=0)
                o = jnp.dot(lhs[j], rhs, preferred_element_type=F32)
                state = cd_f * state + kvs[j]
                mu = jnp.mean(o, axis=-1, keepdims=True)
                oc = o - mu
                var = jnp.mean(oc * oc, axis=-1, keepdims=True)
                on = oc * lax.rsqrt(var + EPS) * gn
                rows = chunk_rows(c)
                gate = g_ref[rows, :].astype(F32)
                y_ref[rows, :] = (gate * jax.nn.sigmoid(gate) * on).astype(BF16)
            sf_ref[...] = state
            return carry

        lax.fori_loop(0, n_groups, body, 0)


def _retention_schedule():
    seg, phase, reset, cbase = [], [], [], []
    for p in range(N_PROMPT_SEG):
        for ph in (0, 1):
            seg.append(p); phase.append(ph); reset.append(1); cbase.append(0)
    n_s = N_SEG - N_PROMPT_SEG
    for i in range(n_s):
        t = n_s - 1 - i
        seg.append(N_PROMPT_SEG + t); phase.append(0); reset.append(int(i == 0)); cbase.append(t * RET_CHUNKS_PER_SEG)
    for t in range(n_s):
        seg.append(N_PROMPT_SEG + t); phase.append(1); reset.append(int(t == 0)); cbase.append(t * RET_CHUNKS_PER_SEG)
    hold = list(seg)
    for i in range(len(seg)):
        if phase[i] == 0:
            nxt = next(j for j in range(i + 1, len(seg)) if phase[j] == 1)
            hold[i] = seg[nxt]
    arr = lambda a: jnp.asarray(np.asarray(a, np.int32))
    return arr(seg), arr(phase), arr(reset), arr(cbase), arr(hold)


def _retention(z, ret_mask, ret_dec, ret_cdec, gn_g):
    seg, phase, reset, cbase, hold = _retention_schedule()
    n_steps = int(seg.shape[0])
    qk_blk = lambda col: (lambda h, s, seg_r, ph_r, rs_r, cb_r, hold_r: (seg_r[s], col // RET_DK + h))
    hold_blk = lambda col, w: (lambda h, s, seg_r, ph_r, rs_r, cb_r, hold_r: (hold_r[s], col // w + h))
    v_blk = lambda h, s, seg_r, ph_r, rs_r, cb_r, hold_r: (seg_r[s], COL_RV // RET_DV + h)
    per_head = lambda h, s, *_: (h, 0, 0)
    per_head4 = lambda h, s, *_: (h, 0, 0, 0)
    grid_spec = pltpu.PrefetchScalarGridSpec(
        num_scalar_prefetch=5,
        grid=(RET_HEADS, n_steps),
        in_specs=[
            pl.BlockSpec((SEG, RET_DK), hold_blk(COL_RQ, RET_DK)),
            pl.BlockSpec((SEG, RET_DK), qk_blk(COL_RK)),
            pl.BlockSpec((SEG, RET_DV), v_blk),
            pl.BlockSpec((SEG, RET_DV), hold_blk(COL_RG, RET_DV)),
            pl.BlockSpec((1, RET_CHUNK, RET_CHUNK), per_head),
            pl.BlockSpec((1, 4, RET_CHUNK, 1), per_head4),
            pl.BlockSpec((1, 2, 1, RET_DV), per_head4),
            pl.BlockSpec((1, 1, RET_DV), per_head),
        ],
        out_specs=pl.BlockSpec((SEG, RET_DV), lambda h, s, seg_r, ph_r, rs_r, cb_r, hold_r: (hold_r[s], h)),
        scratch_shapes=[
            pltpu.VMEM((RET_MAX_CHUNKS, RET_DK, RET_DV), BF16),
            pltpu.VMEM((RET_DK, RET_DV), F32),
            pltpu.VMEM((RET_DK, RET_DV), F32),
        ],
    )

    def kernel(seg_r, ph_r, rs_r, cb_r, hold_r, *refs):
        _retention_kernel(seg_r, ph_r, rs_r, cb_r, *refs)

    return pl.pallas_call(
        kernel,
        grid_spec=grid_spec,
        out_shape=jax.ShapeDtypeStruct((T_ALL, RET_V_W), BF16),
        compiler_params=_cparams(("arbitrary", "arbitrary")),
        name="retention",
    )(seg, phase, reset, cbase, hold, z, z, z, z, ret_mask, ret_dec, ret_cdec, gn_g)


def _attention_kernel(first_ref, last_ref, q_ref, kp_ref, km_ref, kn_ref, vp_ref, vm_ref, vn_ref,
                      bias_ref, gq_ref, gk_ref, o_ref, lse_ref, kall, vall, qall, oacc, bvar, *, dil):
    nb = SEG // dil // ATT_BLOCK
    nqb = nb // 2
    n_slab = ATT_W // LANES
    c = pl.program_id(0)
    is_first = first_ref[c]
    is_last = last_ref[c]
    lane = lax.broadcasted_iota(jnp.int32, (1, LANES), 1)
    lo = lane < ATT_DH
    gq = gq_ref[...]
    gk = gk_ref[...]

    @pl.when(c == 0)
    def _():
        col = lax.broadcasted_iota(jnp.int32, (1, KW), 1)
        left = jnp.where(col < ATT_BLOCK, NEG, 0.0).astype(F32)
        right = jnp.where(col >= KW - ATT_BLOCK, NEG, 0.0).astype(F32)
        for h in range(ATT_HEADS):
            b = bias_ref[h]
            bvar[0, h] = b
            bvar[1, h] = b + left
            bvar[2, h] = b + right
            bvar[3, h] = b + left + right

    def head_norm(x, g):
        x2 = x * x
        s_lo = jnp.sum(jnp.where(lo, x2, 0.0), axis=-1, keepdims=True)
        s_hi = jnp.sum(jnp.where(lo, 0.0, x2), axis=-1, keepdims=True)
        ms = jnp.where(lo, s_lo, s_hi) * (1.0 / ATT_DH)
        return x * lax.rsqrt(ms + EPS) * g

    def norm_block(src):
        even, odd = [], []
        for s in range(n_slab):
            xn = head_norm(src[:, s * LANES:(s + 1) * LANES].astype(F32), gk)
            even.append(jnp.where(lo, xn, 0.0).astype(BF16))
            odd.append(jnp.where(lo, 0.0, xn).astype(BF16))
        return jnp.concatenate(even, axis=-1), jnp.concatenate(odd, axis=-1)

    lo_wide = lax.broadcasted_iota(jnp.int32, (1, ATT_W), 1) % LANES < ATT_DH

    def split_heads(v):
        zero = jnp.zeros_like(v)
        return jnp.where(lo_wide, v, zero), jnp.where(lo_wide, zero, v)

    def fill_main(it, carry):
        rho = it // nb
        blk = it % nb
        kall[0, rho, blk + 1], kall[1, rho, blk + 1] = norm_block(km_ref[rho, blk])
        vall[0, rho, blk + 1], vall[1, rho, blk + 1] = split_heads(vm_ref[rho, blk])
        qsrc = q_ref[rho, blk]
        qall[rho, blk] = jnp.concatenate(
            [head_norm(qsrc[:, s * LANES:(s + 1) * LANES].astype(F32), gq).astype(BF16) for s in range(n_slab)], axis=-1)
        return carry

    lax.fori_loop(0, dil * nb, fill_main, 0, unroll=4)

    def fill_halo(rho, carry):
        kall[0, rho, 0], kall[1, rho, 0] = norm_block(kp_ref[rho, 0])
        kall[0, rho, nb + 1], kall[1, rho, nb + 1] = norm_block(kn_ref[rho, 0])
        vall[0, rho, 0], vall[1, rho, 0] = split_heads(vp_ref[rho, 0])
        vall[0, rho, nb + 1], vall[1, rho, nb + 1] = split_heads(vn_ref[rho, 0])
        return carry

    lax.fori_loop(0, dil, fill_halo, 0)

    ones_even = jnp.broadcast_to(jnp.where(lo, 1.0, 0.0).astype(BF16), (KW, LANES))
    ones_odd = jnp.broadcast_to(jnp.where(lo, 0.0, 1.0).astype(BF16), (KW, LANES))

    def body(it, carry):
        rho = it // nqb
        qb = it % nqb
        var = (jnp.where((qb == 0) & (is_first == 1), 1, 0)
               + jnp.where((qb == nqb - 1) & (is_last == 1), 2, 0))
        start = rho + qb * (QB * dil)
        rows = pl.ds(start, QB) if dil == 1 else pl.ds(start, QB, stride=dil)
        for s in range(n_slab):
            sl = slice(s * LANES, (s + 1) * LANES)
            qn = qall[rho, pl.ds(2 * qb, 2), :, sl].reshape(QB, LANES)
            es, ms = [], []
            for hh in range(2):
                kw = kall[hh, rho, pl.ds(2 * qb, 4), :, sl].reshape(KW, LANES)
                sc = lax.dot_general(qn, kw, (((1,), (1,)), ((), ())), preferred_element_type=F32)
                sc = sc + bvar[var, 2 * s + hh]
                m = jnp.max(sc, axis=-1, keepdims=True)
                es.append(jnp.exp2(sc - m).astype(BF16))
                ms.append(m)
            v_even = vall[0, rho, pl.ds(2 * qb, 4), :, sl].reshape(KW, LANES)
            v_odd = vall[1, rho, pl.ds(2 * qb, 4), :, sl].reshape(KW, LANES)
            rhs = jnp.concatenate([jnp.concatenate([v_even, ones_even], axis=1),
                                   jnp.concatenate([v_odd, ones_odd], axis=1)], axis=0)
            res = jnp.dot(jnp.concatenate(es, axis=1), rhs, preferred_element_type=F32)
            den = res[:, LANES:]
            oacc[s, rows, :] = res[:, :LANES] * (1.0 / den)
            lse_ref[s, rows, :] = (jnp.where(lo, ms[0], ms[1]) + jnp.log2(den)) * LN2
        return carry

    lax.fori_loop(0, dil * nqb, body, 0, unroll=8)

    for s in range(n_slab):
        o_ref[:, s * LANES:(s + 1) * LANES] = oacc[s].astype(BF16)


def _attention_group(z, bias_g, gq, gk, first, last, gi, dil):
    nb = SEG // dil // ATT_BLOCK
    z5 = z.reshape(N_SEG, dil, nb, ATT_BLOCK, N_IN)
    cq = (COL_ATT + 3 * gi * ATT_W) // ATT_W
    ck, cv = cq + 1, cq + 2
    main = lambda cb: pl.BlockSpec((None, dil, nb, ATT_BLOCK, ATT_W), lambda c, f, l: (c, 0, 0, 0, cb))
    prev = lambda cb: pl.BlockSpec((None, dil, 1, ATT_BLOCK, ATT_W),
                                   lambda c, f, l: (c - 1 + f[c], 0, nb - 1, 0, cb), pipeline_mode=pl.Buffered(1))
    nxt = lambda cb: pl.BlockSpec((None, dil, 1, ATT_BLOCK, ATT_W),
                                  lambda c, f, l: (c + 1 - l[c], 0, 0, 0, cb), pipeline_mode=pl.Buffered(1))
    grid_spec = pltpu.PrefetchScalarGridSpec(
        num_scalar_prefetch=2,
        grid=(N_SEG,),
        in_specs=[
            main(cq), prev(ck), main(ck), nxt(ck), prev(cv), main(cv), nxt(cv),
            pl.BlockSpec((ATT_HEADS, QB, KW), lambda c, f, l: (0, 0, 0)),
            pl.BlockSpec((1, LANES), lambda c, f, l: (0, 0)),
            pl.BlockSpec((1, LANES), lambda c, f, l: (0, 0)),
        ],
        out_specs=[
            pl.BlockSpec((SEG, ATT_W), lambda c, f, l: (c, 0)),
            pl.BlockSpec((ATT_W // LANES, SEG, LANES), lambda c, f, l: (0, c, 0)),
        ],
        scratch_shapes=[
            pltpu.VMEM((2, dil, nb + 2, ATT_BLOCK, ATT_W), BF16),
            pltpu.VMEM((2, dil, nb + 2, ATT_BLOCK, ATT_W), BF16),
            pltpu.VMEM((dil, nb, ATT_BLOCK, ATT_W), BF16),
            pltpu.VMEM((ATT_W // LANES, SEG, LANES), F32),
            pltpu.VMEM((4, ATT_HEADS, QB, KW), F32),
        ],
    )
    return pl.pallas_call(
        functools.partial(_attention_kernel, dil=dil),
        grid_spec=grid_spec,
        out_shape=[jax.ShapeDtypeStruct((T_ALL, ATT_W), BF16), jax.ShapeDtypeStruct((ATT_W // LANES, T_ALL, LANES), F32)],
        compiler_params=_cparams(("arbitrary",), ATT_VMEM_LIMIT),
        name=f"attention_d{dil}",
    )(first, last, z5, z5, z5, z5, z5, z5, z5, bias_g, gq, gk)


def _merge_kernel(yret_ref, o0_ref, o1_ref, o2_ref, l0_ref, l1_ref, l2_ref,
                  gret_a_ref, gret_b_ref, gatt_a_ref, gatt_b_ref, xp_ref, xs_ref,
                  wret_ref, watt_ref, wout_ref, nffn_ref, wr_ref, wrhi_ref, br_ref,
                  x1_ref, hpa_ref, hpb_ref, idx_ref, gate_ref, rank_ref, cnt_ref, carry_ref):
    i = pl.program_id(0)
    tm = MERGE_TM
    sub = MERGE_SUB

    @pl.when(i == 0)
    def _():
        carry_ref[...] = jnp.zeros_like(carry_ref)

    is_prompt = i < T_PROMPT // tm
    lane = lax.broadcasted_iota(jnp.int32, (sub, LANES), 1)
    lane_f = lane.astype(F32)
    tri = jnp.where(lax.broadcasted_iota(jnp.int32, (sub, sub), 1) < lax.broadcasted_iota(jnp.int32, (sub, sub), 0),
                    1.0, 0.0).astype(BF16)
    carry = carry_ref[...]

    for r0 in range(0, tm, sub):
        rows = slice(r0, r0 + sub)
        l0, l1, l2 = [jnp.concatenate([r[s, rows, :] for s in range(ATT_W // LANES)], axis=-1)
                      for r in (l0_ref, l1_ref, l2_ref)]
        lm = jnp.maximum(jnp.maximum(l0, l1), l2)
        e0, e1, e2 = jnp.exp(l0 - lm), jnp.exp(l1 - lm), jnp.exp(l2 - lm)
        inv = 1.0 / (e0 + e1 + e2)
        y_att = ((e0 * inv) * o0_ref[rows, :].astype(F32) + (e1 * inv) * o1_ref[rows, :].astype(F32)
                 + (e2 * inv) * o2_ref[rows, :].astype(F32))

        p_ret = jnp.dot(yret_ref[rows, :], wret_ref[...], preferred_element_type=F32)
        p_att = jnp.dot(y_att.astype(BF16), watt_ref[...], preferred_element_type=F32)
        g_ret = jnp.concatenate([gret_a_ref[rows, :], gret_b_ref[rows, :]], axis=-1).astype(F32)
        g_att = jnp.concatenate([gatt_a_ref[rows, :], gatt_b_ref[rows, :]], axis=-1).astype(F32)
        merged = _sigmoid(g_ret) * p_ret + _sigmoid(g_att) * p_att
        x_in = jnp.where(is_prompt, xp_ref[rows, :], xs_ref[rows, :])
        x1 = x_in + jnp.dot(merged.astype(BF16), wout_ref[...], preferred_element_type=F32)
        x1_ref[rows, :] = x1

        ms = jnp.mean(x1 * x1, axis=-1, keepdims=True)
        h2 = x1 * lax.rsqrt(ms + EPS) * nffn_ref[...]
        hpa_ref[rows, :], hpb_ref[rows, :] = _pack_row_halves(h2)

        h_hi = h2.astype(BF16)
        h_lo = (h2 - h_hi.astype(F32)).astype(BF16)
        p1 = jnp.dot(h_hi, wr_ref[...], preferred_element_type=F32)
        p2 = jnp.dot(h_lo, wrhi_ref[...], preferred_element_type=F32)
        work = p1 + pltpu.roll(p1, LANES - N_EXPERTS, axis=1) + p2 + br_ref[...]
        vals, idxs = [], []
        for _ in range(TOP_K):
            m = jnp.max(work, axis=-1, keepdims=True)
            ix = jnp.min(jnp.where(work == m, lane_f, float(LANES)), axis=-1, keepdims=True)
            vals.append(m)
            idxs.append(ix)
            work = jnp.where(lane_f == ix, -3e38, work)
        es = [jnp.exp(v - vals[0]) for v in vals]
        den = es[0] + es[1] + es[2] + es[3]
        onehot = jnp.zeros((sub, LANES), F32)
        for ix in idxs:
            onehot = onehot + jnp.where(lane_f == ix, 1.0, 0.0)
        before = jnp.dot(tri, onehot.astype(BF16), preferred_element_type=F32) + carry
        idx_out = jnp.zeros((sub, LANES), F32)
        gate_out = jnp.zeros((sub, LANES), F32)
        rank_out = jnp.zeros((sub, LANES), F32)
        for k in range(TOP_K):
            rk = jnp.sum(jnp.where(lane_f == idxs[k], before, 0.0), axis=-1, keepdims=True)
            sel = lane == k
            idx_out = jnp.where(sel, idxs[k], idx_out)
            gate_out = jnp.where(sel, es[k] / den, gate_out)
            rank_out = jnp.where(sel, rk, rank_out)
        idx_ref[rows, :] = idx_out.astype(jnp.int32)
        gate_ref[rows, :] = gate_out
        rank_ref[rows, :] = rank_out.astype(jnp.int32)
        carry = carry + jnp.sum(onehot, axis=0, keepdims=True)

    carry_ref[...] = carry
    cnt_ref[...] = jnp.broadcast_to(carry, cnt_ref.shape)


def _merge(y_ret, o_list, lse_list, z, x_p, x_s, w_ret, w_att, w_out, n_ffn, w_router, w_router_hi, b_router):
    tm = MERGE_TM
    n_p = T_PROMPT // tm
    row = lambda w: pl.BlockSpec((tm, w), lambda i: (i, 0))
    full = lambda a: pl.BlockSpec(a.shape, lambda i: (0,) * a.ndim)
    zcol = lambda col: pl.BlockSpec((tm, COL_BLK), lambda i: (i, col // COL_BLK))
    lse_spec = pl.BlockSpec((ATT_W // LANES, tm, LANES), lambda i: (0, i, 0))
    return pl.pallas_call(
        _merge_kernel,
        grid=(T_ALL // tm,),
        in_specs=[row(RET_V_W), row(ATT_W), row(ATT_W), row(ATT_W), lse_spec, lse_spec, lse_spec,
                  zcol(COL_GATE_RET), zcol(COL_GATE_RET + COL_BLK), zcol(COL_GATE_ATT),
                  zcol(COL_GATE_ATT + COL_BLK),
                  pl.BlockSpec((tm, D_MODEL), lambda i: (jnp.minimum(i, n_p - 1), 0)),
                  pl.BlockSpec((tm, D_MODEL), lambda i: (jnp.maximum(i - n_p, 0), 0)),
                  full(w_ret), full(w_att), full(w_out), full(n_ffn), full(w_router), full(w_router_hi),
                  full(b_router)],
        out_specs=[row(D_MODEL), row(HALF_W), row(HALF_W), row(LANES), row(LANES), row(LANES),
                   pl.BlockSpec((8, LANES), lambda i: (0, 0))],
        out_shape=[jax.ShapeDtypeStruct((T_ALL, D_MODEL), F32),
                   jax.ShapeDtypeStruct((T_ALL, HALF_W), jnp.uint32),
                   jax.ShapeDtypeStruct((T_ALL, HALF_W), jnp.uint32),
                   jax.ShapeDtypeStruct((T_ALL, LANES), jnp.int32),
                   jax.ShapeDtypeStruct((T_ALL, LANES), F32),
                   jax.ShapeDtypeStruct((T_ALL, LANES), jnp.int32),
                   jax.ShapeDtypeStruct((8, LANES), F32)],
        scratch_shapes=[pltpu.VMEM((1, LANES), F32)],
        compiler_params=_cparams(("arbitrary",)),
        name="merge_router",
    )(y_ret, *o_list, *lse_list, z, z, z, z, x_p, x_s, w_ret, w_att, w_out, n_ffn, w_router, w_router_hi, b_router)


def _sc_mesh():
    return plsc.VectorSubcoreMesh(core_axis_name="core", subcore_axis_name="subcore")


def _sc_scatter_rows(xs, idx_kmajor, n_out):
    n_rows, width = xs[0].shape
    out_type = [jax.ShapeDtypeStruct((n_out, width), x.dtype) for x in xs]

    @pl.kernel(out_type=out_type, mesh=_sc_mesh(), scratch_types=[])
    def scatter(*refs):
        x_refs, i_hbm, o_refs = refs[:len(xs)], refs[len(xs)], refs[len(xs) + 1:]
        for x_hbm, o_hbm in zip(x_refs, o_refs):
            def body(x_vmem, i_vmem, o_hbm=o_hbm):
                for k in range(TOP_K):
                    pltpu.sync_copy(x_vmem, o_hbm.at[i_vmem.at[k]])

            pltpu.emit_pipeline(
                body,
                grid=(n_rows // SC_WINDOW,),
                in_specs=[pl.BlockSpec((SC_WINDOW, width), lambda i: (i, 0)),
                          pl.BlockSpec((TOP_K, SC_WINDOW), lambda i: (0, i))],
                out_specs=[],
                core_axis_name=("core", "subcore"),
                dimension_semantics=(pltpu.PARALLEL,),
            )(x_hbm, i_hbm)

    return scatter(*xs, idx_kmajor)


def _sc_gather_rows(datas, idx):
    n_idx = idx.shape[0]
    width = datas[0].shape[1]
    out_type = [jax.ShapeDtypeStruct((n_idx, width), d.dtype) for d in datas]

    @pl.kernel(out_type=out_type, mesh=_sc_mesh(), scratch_types=[])
    def gather(*refs):
        x_refs, i_hbm, o_refs = refs[:len(datas)], refs[len(datas)], refs[len(datas) + 1:]
        for x_hbm, o_hbm in zip(x_refs, o_refs):
            def body(i_vmem, o_vmem, x_hbm=x_hbm):
                pltpu.sync_copy(x_hbm.at[i_vmem.at[0]], o_vmem)

            pltpu.emit_pipeline(
                body,
                grid=(n_idx // SC_WINDOW,),
                in_specs=[pl.BlockSpec((1, SC_WINDOW), lambda i: (0, i))],
                out_specs=[pl.BlockSpec((SC_WINDOW, width), lambda i: (i, 0))],
                core_axis_name=("core", "subcore"),
                dimension_semantics=(pltpu.PARALLEL,),
            )(i_hbm, o_hbm)

    return gather(*datas, idx.reshape(1, n_idx))


def _expert_kernel(be_ref, nused_ref, nvalid_ref, xa_ref, xb_ref, wg_ref, bg_ref, wu_ref, bu_ref, wd_ref, bd_ref,
                   ya_ref, yb_ref, wbf_ref):
    b = pl.program_id(0)
    active = b < nused_ref[0]
    new_expert = (b == 0) | (be_ref[b] != be_ref[jnp.maximum(b - 1, 0)])

    @pl.when(active & new_expert)
    def _():
        rows = 128
        for wi, w_ref in enumerate((wg_ref, wu_ref, wd_ref)):
            for r in range(0, D_MODEL, rows):
                wbf_ref[wi, r:r + rows, :] = w_ref[0, r:r + rows, :].astype(BF16)

    @pl.when(active)
    def _():
        valid = lax.broadcasted_iota(jnp.int32, (MOE_BM, HALF_W), 0) < nvalid_ref[b]
        zero = jnp.zeros((MOE_BM, HALF_W), jnp.uint32)
        x = _unpack_row_halves(jnp.where(valid, xa_ref[...], zero), jnp.where(valid, xb_ref[...], zero)).astype(BF16)
        g = jnp.dot(x, wbf_ref[0], preferred_element_type=F32) + bg_ref[0]
        u = jnp.dot(x, wbf_ref[1], preferred_element_type=F32) + bu_ref[0]
        g = jnp.minimum(g, SWIGLU_LIMIT)
        u = jnp.clip(u, -SWIGLU_LIMIT, SWIGLU_LIMIT)
        glu = g * jax.nn.sigmoid(SWIGLU_ALPHA * g)
        act = ((u + 1.0) * glu).astype(BF16)
        y = jnp.dot(act, wbf_ref[2], preferred_element_type=F32) + bd_ref[0]
        ya_ref[...], yb_ref[...] = _pack_row_halves(y)

    @pl.when(jnp.logical_not(active))
    def _():
        ya_ref[...] = jnp.zeros_like(ya_ref)
        yb_ref[...] = jnp.zeros_like(yb_ref)


def _experts(block_expert, n_used, n_valid, xs_a, xs_b, wg, bg, wu, bu, wd, bd):
    assert D_FF == D_MODEL
    blk = lambda b, be, nu, nv: (jnp.minimum(b, nu[0] - 1), 0)
    wsp = lambda: pl.BlockSpec((1, D_MODEL, D_FF), lambda b, be, nu, nv: (be[b], 0, 0))
    bsp = lambda: pl.BlockSpec((1, 1, D_FF), lambda b, be, nu, nv: (be[b], 0, 0))
    xsp = lambda: pl.BlockSpec((MOE_BM, HALF_W), blk)
    ysp = lambda: pl.BlockSpec((MOE_BM, HALF_W), lambda b, be, nu, nv: (b, 0))
    slot_arr = jax.ShapeDtypeStruct((N_SLOTS, HALF_W), jnp.uint32)
    grid_spec = pltpu.PrefetchScalarGridSpec(
        num_scalar_prefetch=3,
        grid=(N_SLOT_BLOCKS,),
        in_specs=[xsp(), xsp(), wsp(), bsp(), wsp(), bsp(), wsp(), bsp()],
        out_specs=[ysp(), ysp()],
        scratch_shapes=[pltpu.VMEM((3, D_MODEL, D_FF), BF16)],
    )
    return pl.pallas_call(
        _expert_kernel,
        grid_spec=grid_spec,
        out_shape=[slot_arr, slot_arr],
        compiler_params=_cparams(("arbitrary",)),
        name="experts",
    )(block_expert, n_used, n_valid, xs_a, xs_b, wg, bg, wu, bu, wd, bd)


def _final_kernel(x1_ref, yga_ref, ygb_ref, gate_ref, p_ref, nple_ref, wpg_ref, wpp_ref, out_ref):
    x2 = x1_ref[...]
    gates = gate_ref[...]
    for k in range(TOP_K):
        x2 = x2 + gates[:, k:k + 1] * _unpack_row_halves(yga_ref[k], ygb_ref[k])
    ms = jnp.mean(x2 * x2, axis=-1, keepdims=True)
    h3 = (x2 * lax.rsqrt(ms + EPS) * nple_ref[...]).astype(BF16)
    gate = jax.nn.sigmoid(jnp.dot(h3, wpg_ref[...], preferred_element_type=F32))
    proj = jnp.dot(p_ref[...].astype(BF16), wpp_ref[...], preferred_element_type=F32)
    out_ref[...] = x2 + gate * proj


def _final(x1, yg_a, yg_b, gates, p, n_ple, w_pg, w_pp, row0, n_rows):
    tm = FINAL_TM
    off = row0 // tm
    full = lambda a: pl.BlockSpec(a.shape, lambda i: (0,) * a.ndim)
    return pl.pallas_call(
        _final_kernel,
        grid=(n_rows // tm,),
        in_specs=[pl.BlockSpec((tm, D_MODEL), lambda i: (i + off, 0)),
                  pl.BlockSpec((TOP_K, tm, HALF_W), lambda i: (0, i + off, 0)),
                  pl.BlockSpec((TOP_K, tm, HALF_W), lambda i: (0, i + off, 0)),
                  pl.BlockSpec((tm, LANES), lambda i: (i + off, 0)),
                  pl.BlockSpec((tm, PLE_DIM), lambda i: (i, 0)),
                  full(n_ple), full(w_pg), full(w_pp)],
        out_specs=pl.BlockSpec((tm, D_MODEL), lambda i: (i, 0)),
        out_shape=jax.ShapeDtypeStruct((n_rows, D_MODEL), F32),
        compiler_params=_cparams(("arbitrary",)),
        name="final_ple",
    )(x1, yg_a, yg_b, gates, p, n_ple, w_pg, w_pp)


def _rope_tables():
    half = RET_DK // 2
    step = 128
    freq = ROPE_THETA ** (-jnp.arange(half, dtype=F32) / half)
    ang_lo = jnp.arange(step, dtype=F32)[:, None] * freq[None, :]
    ang_hi = (jnp.arange(SAMPLE_SEQ // step, dtype=F32) * step)[:, None] * freq[None, :]
    c_lo, s_lo = jnp.cos(ang_lo)[None], jnp.sin(ang_lo)[None]
    c_hi, s_hi = jnp.cos(ang_hi)[:, None], jnp.sin(ang_hi)[:, None]
    cos = (c_hi * c_lo - s_hi * s_lo).reshape(SAMPLE_SEQ, half)
    sin = (s_hi * c_lo + c_hi * s_lo).reshape(SAMPLE_SEQ, half)
    return jnp.concatenate([cos, cos], axis=-1), jnp.concatenate([-sin, sin], axis=-1)


def _retention_tables(decay_logit):
    lg = jax.nn.log_sigmoid(decay_logit.astype(F32))
    c = RET_CHUNK
    idx = jnp.arange(c, dtype=F32)
    diff = idx[:, None] - idx[None, :]
    lf = lg[0][:, None, None]
    lb = lg[1][:, None, None]
    mask = jnp.where(diff[None] >= 0, jnp.exp(lf * jnp.maximum(diff, 0.0)[None]),
                     jnp.exp(lb * jnp.maximum(-diff, 0.0)[None]))
    kdec_f = jnp.exp(lg[0][:, None] * (c - 1.0 - idx)[None, :])
    qdec_f = jnp.exp(lg[0][:, None] * (idx + 1.0)[None, :])
    kdec_b = jnp.exp(lg[1][:, None] * idx[None, :])
    qdec_b = jnp.exp(lg[1][:, None] * (c - idx)[None, :])
    dec = jnp.stack([kdec_f, qdec_f, kdec_b, qdec_b], axis=1)[..., None]
    cdec = jnp.exp(lg * c).T
    cdec = jnp.broadcast_to(cdec[:, :, None, None], (RET_HEADS, 2, 1, RET_DV))
    return mask, dec, cdec


def _t5_bucket(rel):
    half = T5_BUCKETS // 2
    exact = half // 2
    n = np.abs(rel)
    ratio = np.log(np.maximum(n, 1).astype(np.float32) / np.float32(exact)) / np.float32(math.log(T5_MAX_DIST / exact))
    large = exact + (ratio * np.float32(half - exact)).astype(np.int32)
    large = np.minimum(large, half - 1)
    return np.where(rel > 0, half, 0) + np.where(n < exact, n, large)


def _attention_bias(rel_bias, gi, dil, radius):
    qi = np.arange(QB)
    ki = np.arange(KW) - ATT_BLOCK
    rel = ki[None, :] - qi[:, None]
    onehot = jnp.asarray(_t5_bucket(rel * dil)[..., None] == np.arange(T5_BUCKETS), F32)
    tab = rel_bias[:, gi * ATT_HEADS:(gi + 1) * ATT_HEADS].astype(F32)
    bias = jnp.einsum('qkb,bh->hqk', onehot, tab, precision=lax.Precision.HIGHEST)
    return jnp.where(jnp.asarray(np.abs(rel) <= radius)[None], bias * LOG2E, NEG)


def _seq_edge_flags():
    first = np.zeros((N_SEG,), np.int32)
    last = np.zeros((N_SEG,), np.int32)
    first[:N_PROMPT_SEG] = 1
    last[:N_PROMPT_SEG] = 1
    first[N_PROMPT_SEG] = 1
    last[N_SEG - 1] = 1
    return jnp.asarray(first), jnp.asarray(last)


def _pad_lanes(a, value=0.0):
    return jnp.pad(a, ((0, 0), (0, LANES - a.shape[-1])), constant_values=value)


def kernel(x_prompt, x_sample, p_prompt, p_sample, norm_mix_g, w_in, ret_decay_logit, ret_gn_g,
           att_q_norm_g, att_k_norm_g, rel_bias, w_ret_proj, w_att_proj, w_out, norm_ffn_g,
           w_router, b_router, w_gate, b_gate, w_up, b_up, w_down, b_down,
           norm_ple_g, w_ple_gate, w_ple_proj):
    assert norm_mix_g.shape[0] == 1, "one layer"
    x_p = x_prompt.reshape(T_PROMPT, D_MODEL)
    x_s = x_sample.reshape(SAMPLE_SEQ, D_MODEL)

    cos_t, sin_t = _rope_tables()
    z = _in_proj(x_p, x_s, norm_mix_g.astype(F32), w_in[0].astype(BF16), cos_t, sin_t)

    ret_mask, ret_dec, ret_cdec = _retention_tables(ret_decay_logit[0])
    y_ret = _retention(z, ret_mask, ret_dec, ret_cdec, ret_gn_g[0].reshape(RET_HEADS, 1, RET_DV).astype(F32))

    first, last = _seq_edge_flags()
    o_list, lse_list = [], []
    for gi, (window, dil) in enumerate(ATT_GROUPS):
        bias_g = _attention_bias(rel_bias, gi, dil, window // (2 * dil))
        gq = jnp.tile(att_q_norm_g[0, gi].astype(F32) * (ATT_DH ** -0.5 * LOG2E), LANES // ATT_DH)[None, :]
        gk = jnp.tile(att_k_norm_g[0, gi].astype(F32), LANES // ATT_DH)[None, :]
        o_g, lse_g = _attention_group(z, bias_g, gq, gk, first, last, gi, dil)
        o_list.append(o_g)
        lse_list.append(lse_g)

    w_r = w_router[0].astype(F32)
    w_r_hi = w_r.astype(BF16)
    w_r_lo = (w_r - w_r_hi.astype(F32)).astype(BF16)
    w_router_cat = _pad_lanes(jnp.concatenate([w_r_hi, w_r_lo], axis=1))
    w_router_hi = _pad_lanes(w_r_hi)
    b_router_p = _pad_lanes(b_router.astype(F32), NEG)
    x1, hp_a, hp_b, idx, gates, rank, cnt = _merge(
        y_ret, o_list, lse_list, z, x_p, x_s, w_ret_proj[0].astype(BF16), w_att_proj[0].astype(BF16),
        w_out[0].astype(BF16), norm_ffn_g.astype(F32), w_router_cat, w_router_hi, b_router_p)

    counts = cnt[0, :N_EXPERTS].astype(jnp.int32)
    padded = (counts + MOE_BM - 1) // MOE_BM * MOE_BM
    pad_end = jnp.cumsum(padded)
    pad_start = pad_end - padded
    expert_ids = jnp.arange(N_EXPERTS, dtype=jnp.int32)
    top_idx = idx[:, :TOP_K]
    start_of = jnp.sum(jnp.where(top_idx[:, :, None] == expert_ids, pad_start, 0), axis=-1)
    dest_kmajor = (start_of + rank[:, :TOP_K]).T
    n_used = (pad_end[-1] // MOE_BM).astype(jnp.int32).reshape(1)
    blk_row0 = jnp.arange(N_SLOT_BLOCKS, dtype=jnp.int32) * MOE_BM
    block_expert = jnp.minimum(jnp.sum((pad_end[None, :] <= blk_row0[:, None]).astype(jnp.int32), axis=1),
                               N_EXPERTS - 1).astype(jnp.int32)
    slot_end = jnp.sum(jnp.where(block_expert[:, None] == expert_ids, pad_start + counts, 0), axis=-1)
    n_valid = jnp.clip(slot_end - blk_row0, 0, MOE_BM).astype(jnp.int32)

    xs_a, xs_b = _sc_scatter_rows((hp_a, hp_b), dest_kmajor, N_SLOTS)
    ys_a, ys_b = _experts(block_expert, n_used, n_valid, xs_a, xs_b,
                          w_gate[0], b_gate[0].reshape(N_EXPERTS, 1, D_FF).astype(F32),
                          w_up[0], b_up[0].reshape(N_EXPERTS, 1, D_FF).astype(F32),
                          w_down[0], b_down[0].reshape(N_EXPERTS, 1, D_MODEL).astype(F32))
    dest_flat = dest_kmajor.reshape(-1)
    yg_a, yg_b = [y.reshape(TOP_K, T_ALL, HALF_W) for y in _sc_gather_rows((ys_a, ys_b), dest_flat)]

    n_ple = norm_ple_g.astype(F32)
    w_pg = w_ple_gate[0].astype(BF16)
    w_pp = w_ple_proj[0].astype(BF16)
    y_p = _final(x1, yg_a, yg_b, gates, p_prompt[0].reshape(T_PROMPT, PLE_DIM), n_ple, w_pg, w_pp, 0, T_PROMPT)
    y_s = _final(x1, yg_a, yg_b, gates, p_sample[0].reshape(SAMPLE_SEQ, PLE_DIM), n_ple, w_pg, w_pp, T_PROMPT, SAMPLE_SEQ)
    return (y_p.reshape(x_prompt.shape), y_s.reshape(x_sample.shape))
```

```python
import functools
import math

import jax
import jax.numpy as jnp
import numpy as np
from jax import lax
from jax.experimental import pallas as pl
from jax.experimental.pallas import tpu as pltpu
from jax.experimental.pallas import tpu_sc as plsc

F32 = jnp.float32
BF16 = jnp.bfloat16

D_MODEL = 1024
N_PROMPT_SEQ = 8
PROMPT_SEQ = 2048
SAMPLE_SEQ = 16384
T_PROMPT = N_PROMPT_SEQ * PROMPT_SEQ
T_ALL = T_PROMPT + SAMPLE_SEQ

RET_HEADS = 4
RET_DK = 128
RET_DV = 256
RET_CHUNK = 128
ROPE_THETA = 10000.0
ATT_GROUPS = ((128, 1), (512, 4), (2048, 16))
N_GROUPS = 3
ATT_HEADS = 8
ATT_DH = 64
ATT_BLOCK = 64
ATT_W = ATT_HEADS * ATT_DH
T5_BUCKETS = 32
T5_MAX_DIST = 1024
N_EXPERTS = 32
TOP_K = 4
D_FF = 1024
SWIGLU_ALPHA = 1.702
SWIGLU_LIMIT = 7.0
PLE_DIM = 256
EPS = 1e-6

RET_QK_W = RET_HEADS * RET_DK
RET_V_W = RET_HEADS * RET_DV
N_IN = 2 * RET_QK_W + 2 * RET_V_W + 3 * N_GROUPS * ATT_W + 2 * D_MODEL

COL_RQ = 0
COL_RK = RET_QK_W
COL_RV = 2 * RET_QK_W
COL_RG = COL_RV + RET_V_W
COL_ATT = COL_RG + RET_V_W
COL_GATE_RET = COL_ATT + 3 * N_GROUPS * ATT_W
COL_GATE_ATT = COL_GATE_RET + D_MODEL

LANES = 128
VMEM_LIMIT = 56 * 1024 * 1024
ATT_VMEM_LIMIT = 58 * 1024 * 1024

SEG = 2048
N_SEG = T_ALL // SEG
N_PROMPT_SEG = T_PROMPT // SEG
COL_BLK = 512
N_COL_BLK = N_IN // COL_BLK
QB = 128
KW = 256
NEG = -1e30
LOG2E = math.log2(math.e)
LN2 = math.log(2.0)
MERGE_TM = 512
MERGE_SUB = 512
FINAL_TM = 1024
MOE_BM = 1024
N_SLOT_BLOCKS = T_ALL * TOP_K // MOE_BM + N_EXPERTS
N_SLOTS = N_SLOT_BLOCKS * MOE_BM
HALF_W = D_MODEL // 4
SC_WINDOW = 128


def _cparams(sem, vmem=VMEM_LIMIT):
    return pltpu.CompilerParams(dimension_semantics=sem, vmem_limit_bytes=vmem)


def _sigmoid(x):
    return 0.5 * jnp.tanh(0.5 * x) + 0.5


def _pack_bf16_pair(x):
    w = x.shape[-1] // 2
    hi = pltpu.bitcast(x[:, :w].astype(BF16).astype(F32), jnp.uint32)
    lo = pltpu.bitcast(x[:, w:].astype(BF16).astype(F32), jnp.uint32)
    return hi | (lo >> 16)


def _unpack_bf16_pair(p):
    hi = pltpu.bitcast(p & jnp.uint32(0xFFFF0000), F32)
    lo = pltpu.bitcast(p << 16, F32)
    return jnp.concatenate([hi, lo], axis=-1)


def _pack_row_halves(x):
    half = x.shape[-1] // 2
    return _pack_bf16_pair(x[:, :half]), _pack_bf16_pair(x[:, half:])


def _unpack_row_halves(pa, pb):
    return jnp.concatenate([_unpack_bf16_pair(pa), _unpack_bf16_pair(pb)], axis=-1)


def _in_proj_kernel(xp_ref, xs_ref, g_ref, w_ref, cos_ref, sin_ref, z_ref, h_ref, p_ref, p2_ref):
    i = pl.program_id(0)
    j = pl.program_id(1)

    def norm_into_h(x_ref):
        xf = x_ref[...]
        ms = jnp.mean(xf * xf, axis=-1, keepdims=True)
        h_ref[...] = (xf * lax.rsqrt(ms + EPS) * g_ref[...]).astype(BF16)

    @pl.when((j == 0) & (i < N_PROMPT_SEG))
    def _():
        norm_into_h(xp_ref)

    @pl.when((j == 0) & (i >= N_PROMPT_SEG))
    def _():
        norm_into_h(xs_ref)

    n_slab = COL_BLK // LANES

    def project():
        return jnp.dot(h_ref[...], w_ref[...], preferred_element_type=F32)

    is_rope = j < (COL_RV // COL_BLK)
    att0 = COL_ATT // COL_BLK
    is_d4 = (j >= att0 + 3) & (j < att0 + 6)
    is_d16 = (j >= att0 + 6) & (j < att0 + 9)

    @pl.when(is_rope)
    def _():
        acc = project()
        scale = jnp.where(j == COL_RK // COL_BLK, RET_DK ** -0.5, 1.0).astype(F32)
        c = cos_ref[...]
        sn = sin_ref[...]
        for s in range(n_slab):
            xs = acc[:, s * LANES:(s + 1) * LANES]
            r = xs * c + pltpu.roll(xs, RET_DK // 2, axis=1) * sn
            z_ref[:, s * LANES:(s + 1) * LANES] = (r * scale).astype(BF16)

    @pl.when(is_d4)
    def _():
        acc = project()
        for s in range(n_slab):
            p_ref[s] = acc[:, s * LANES:(s + 1) * LANES]
        rows = SEG // 4
        for rho in range(4):
            for s in range(n_slab):
                piece = p_ref[s, pl.ds(rho, rows, stride=4), :]
                z_ref[rho * rows:(rho + 1) * rows, s * LANES:(s + 1) * LANES] = piece.astype(BF16)

    @pl.when(is_d16)
    def _():
        acc = project()
        for s in range(n_slab):
            p_ref[s] = acc[:, s * LANES:(s + 1) * LANES]
        quarter = SEG // 4
        rows = SEG // 16
        for r4 in range(4):
            for s in range(n_slab):
                p2_ref[s, r4 * quarter:(r4 + 1) * quarter, :] = p_ref[s, pl.ds(r4, quarter, stride=4), :]
        for r4 in range(4):
            for hi in range(4):
                rho = 4 * hi + r4
                for s in range(n_slab):
                    piece = p2_ref[s, pl.ds(r4 * quarter + hi, rows, stride=4), :]
                    z_ref[rho * rows:(rho + 1) * rows, s * LANES:(s + 1) * LANES] = piece.astype(BF16)

    @pl.when(jnp.logical_not(is_rope | is_d4 | is_d16))
    def _():
        z_ref[...] = project().astype(BF16)


def _in_proj(x_p, x_s, norm_g, w_in_bf, cos_t, sin_t):
    def pos_blk(i, j):
        return (jnp.maximum(i - N_PROMPT_SEG, 0), 0)

    return pl.pallas_call(
        _in_proj_kernel,
        grid=(N_SEG, N_COL_BLK),
        in_specs=[
            pl.BlockSpec((SEG, D_MODEL), lambda i, j: (jnp.minimum(i, N_PROMPT_SEG - 1), 0),
                         pipeline_mode=pl.Buffered(1)),
            pl.BlockSpec((SEG, D_MODEL), pos_blk, pipeline_mode=pl.Buffered(1)),
            pl.BlockSpec((1, D_MODEL), lambda i, j: (0, 0)),
            pl.BlockSpec((D_MODEL, COL_BLK), lambda i, j: (0, j)),
            pl.BlockSpec((SEG, LANES), pos_blk),
            pl.BlockSpec((SEG, LANES), pos_blk),
        ],
        out_specs=pl.BlockSpec((SEG, COL_BLK), lambda i, j: (i, j)),
        out_shape=jax.ShapeDtypeStruct((T_ALL, N_IN), BF16),
        scratch_shapes=[
            pltpu.VMEM((SEG, D_MODEL), BF16),
            pltpu.VMEM((COL_BLK // LANES, SEG, LANES), F32),
            pltpu.VMEM((COL_BLK // LANES, SEG, LANES), F32),
        ],
        compiler_params=_cparams(("arbitrary", "arbitrary")),
        name="in_proj",
    )(x_p, x_s, norm_g, w_in_bf, cos_t, sin_t)


RET_CHUNKS_PER_SEG = SEG // RET_CHUNK
RET_MAX_CHUNKS = SAMPLE_SEQ // RET_CHUNK
RET_GROUP = 16


def _retention_kernel(seg_ref, phase_ref, reset_ref, cbase_ref,
                      q_ref, k_ref, v_ref, g_ref, mask_ref, dec_ref, cdec_ref, gn_ref,
                      y_ref, sb_ref, sf_ref, sr_ref):
    step = pl.program_id(1)
    phase = phase_ref[step]
    reset = reset_ref[step]
    cbase = cbase_ref[step]
    kdec_f = dec_ref[0, 0]
    qdec_f = dec_ref[0, 1]
    kdec_b = dec_ref[0, 2]
    qdec_b = dec_ref[0, 3]
    cd_f = cdec_ref[0, 0]
    cd_b = cdec_ref[0, 1]

    def kv_outer(kd, v):
        return lax.dot_general(kd, v, (((0,), (0,)), ((), ())), preferred_element_type=F32)

    @pl.when((phase == 0) & (reset == 1))
    def _():
        sr_ref[...] = jnp.zeros_like(sr_ref)

    @pl.when((phase == 1) & (reset == 1))
    def _():
        sf_ref[...] = jnp.zeros_like(sf_ref)

    n_groups = RET_CHUNKS_PER_SEG // RET_GROUP

    def chunk_rows(c):
        return pl.ds(pl.multiple_of(c * RET_CHUNK, RET_CHUNK), RET_CHUNK)

    @pl.when(phase == 0)
    def _():
        def body(it, carry):
            top = RET_CHUNKS_PER_SEG - 1 - it * RET_GROUP
            kvs = []
            for j in range(RET_GROUP):
                rows = chunk_rows(top - j)
                kd = (k_ref[rows, :].astype(F32) * kdec_b).astype(BF16)
                kvs.append(kv_outer(kd, v_ref[rows, :]))
            state = sr_ref[...]
            for j in range(RET_GROUP):
                sb_ref[cbase + top - j] = state.astype(BF16)
                state = cd_b * state + kvs[j]
            sr_ref[...] = state
            return carry

        lax.fori_loop(0, n_groups, body, 0)

    @pl.when(phase == 1)
    def _():
        msk = mask_ref[0]
        gn = gn_ref[0]

        def body(it, carry):
            c0 = it * RET_GROUP
            lhs, vs, kvs = [], [], []
            for j in range(RET_GROUP):
                rows = chunk_rows(c0 + j)
                qb = q_ref[rows, :]
                kb = k_ref[rows, :]
                v = v_ref[rows, :]
                q = qb.astype(F32)
                s = lax.dot_general(qb, kb, (((1,), (1,)), ((), ())), preferred_element_type=F32)
                lhs.append(jnp.concatenate(
                    [(s * msk).astype(BF16), (q * qdec_f).astype(BF16), (q * qdec_b).astype(BF16)], axis=-1))
                vs.append(v)
                kvs.append(kv_outer((kb.astype(F32) * kdec_f).astype(BF16), v))
            state = sf_ref[...]
            for j in range(RET_GROUP):
                c = c0 + j
                rhs = jnp.concatenate([vs[j], state.astype(BF16), sb_ref[cbase + c]], axis=0)
                o = jnp.dot(lhs[j], rhs, preferred_element_type=F32)
                state = cd_f * state + kvs[j]
                mu = jnp.mean(o, axis=-1, keepdims=True)
                oc = o - mu
                var = jnp.mean(oc * oc, axis=-1, keepdims=True)
                on = oc * lax.rsqrt(var + EPS) * gn
                rows = chunk_rows(c)
                gate = g_ref[rows, :].astype(F32)
                y_ref[rows, :] = (gate * jax.nn.sigmoid(gate) * on).astype(BF16)
            sf_ref[...] = state
            return carry

        lax.fori_loop(0, n_groups, body, 0)


def _retention_schedule():
    seg, phase, reset, cbase = [], [], [], []
    for p in range(N_PROMPT_SEG):
        for ph in (0, 1):
            seg.append(p); phase.append(ph); reset.append(1); cbase.append(0)
    n_s = N_SEG - N_PROMPT_SEG
    for i in range(n_s):
        t = n_s - 1 - i
        seg.append(N_PROMPT_SEG + t); phase.append(0); reset.append(int(i == 0)); cbase.append(t * RET_CHUNKS_PER_SEG)
    for t in range(n_s):
        seg.append(N_PROMPT_SEG + t); phase.append(1); reset.append(int(t == 0)); cbase.append(t * RET_CHUNKS_PER_SEG)
    hold = list(seg)
    for i in range(len(seg)):
        if phase[i] == 0:
            nxt = next(j for j in range(i + 1, len(seg)) if phase[j] == 1)
            hold[i] = seg[nxt]
    arr = lambda a: jnp.asarray(np.asarray(a, np.int32))
    return arr(seg), arr(phase), arr(reset), arr(cbase), arr(hold)


def _retention(z, ret_mask, ret_dec, ret_cdec, gn_g):
    seg, phase, reset, cbase, hold = _retention_schedule()
    n_steps = int(seg.shape[0])
    qk_blk = lambda col: (lambda h, s, seg_r, ph_r, rs_r, cb_r, hold_r: (seg_r[s], col // RET_DK + h))
    hold_blk = lambda col, w: (lambda h, s, seg_r, ph_r, rs_r, cb_r, hold_r: (hold_r[s], col // w + h))
    v_blk = lambda h, s, seg_r, ph_r, rs_r, cb_r, hold_r: (seg_r[s], COL_RV // RET_DV + h)
    per_head = lambda h, s, *_: (h, 0, 0)
    per_head4 = lambda h, s, *_: (h, 0, 0, 0)
    grid_spec = pltpu.PrefetchScalarGridSpec(
        num_scalar_prefetch=5,
        grid=(RET_HEADS, n_steps),
        in_specs=[
            pl.BlockSpec((SEG, RET_DK), hold_blk(COL_RQ, RET_DK)),
            pl.BlockSpec((SEG, RET_DK), qk_blk(COL_RK)),
            pl.BlockSpec((SEG, RET_DV), v_blk),
            pl.BlockSpec((SEG, RET_DV), hold_blk(COL_RG, RET_DV)),
            pl.BlockSpec((1, RET_CHUNK, RET_CHUNK), per_head),
            pl.BlockSpec((1, 4, RET_CHUNK, 1), per_head4),
            pl.BlockSpec((1, 2, 1, RET_DV), per_head4),
            pl.BlockSpec((1, 1, RET_DV), per_head),
        ],
        out_specs=pl.BlockSpec((SEG, RET_DV), lambda h, s, seg_r, ph_r, rs_r, cb_r, hold_r: (hold_r[s], h)),
        scratch_shapes=[
            pltpu.VMEM((RET_MAX_CHUNKS, RET_DK, RET_DV), BF16),
            pltpu.VMEM((RET_DK, RET_DV), F32),
            pltpu.VMEM((RET_DK, RET_DV), F32),
        ],
    )

    def kernel(seg_r, ph_r, rs_r, cb_r, hold_r, *refs):
        _retention_kernel(seg_r, ph_r, rs_r, cb_r, *refs)

    return pl.pallas_call(
        kernel,
        grid_spec=grid_spec,
        out_shape=jax.ShapeDtypeStruct((T_ALL, RET_V_W), BF16),
        compiler_params=_cparams(("arbitrary", "arbitrary")),
        name="retention",
    )(seg, phase, reset, cbase, hold, z, z, z, z, ret_mask, ret_dec, ret_cdec, gn_g)


def _attention_kernel(first_ref, last_ref, q_ref, km_ref, kn_ref, vm_ref, vn_ref,
                      bias_ref, gq_ref, gk_ref, o_ref, lse_ref, kall, vall, qall, oacc, bvar, *, dil):
    nb = SEG // dil // ATT_BLOCK
    nqb = nb // 2
    n_slab = ATT_W // LANES
    c = pl.program_id(0)
    is_first = first_ref[c]
    is_last = last_ref[c]
    lane = lax.broadcasted_iota(jnp.int32, (1, LANES), 1)
    lo = lane < ATT_DH
    gq = gq_ref[...]
    gk = gk_ref[...]

    @pl.when(c == 0)
    def _():
        col = lax.broadcasted_iota(jnp.int32, (1, KW), 1)
        left = jnp.where(col < ATT_BLOCK, NEG, 0.0).astype(F32)
        right = jnp.where(col >= KW - ATT_BLOCK, NEG, 0.0).astype(F32)
        for h in range(ATT_HEADS):
            b = bias_ref[h]
            bvar[0, h] = b
            bvar[1, h] = b + left
            bvar[2, h] = b + right
            bvar[3, h] = b + left + right

    def head_norm(x, g):
        x2 = x * x
        s_lo = jnp.sum(jnp.where(lo, x2, 0.0), axis=-1, keepdims=True)
        s_hi = jnp.sum(jnp.where(lo, 0.0, x2), axis=-1, keepdims=True)
        ms = jnp.where(lo, s_lo, s_hi) * (1.0 / ATT_DH)
        return x * lax.rsqrt(ms + EPS) * g

    def norm_block(src):
        even, odd = [], []
        for s in range(n_slab):
            xn = head_norm(src[:, s * LANES:(s + 1) * LANES].astype(F32), gk)
            even.append(jnp.where(lo, xn, 0.0).astype(BF16))
            odd.append(jnp.where(lo, 0.0, xn).astype(BF16))
        return jnp.concatenate(even, axis=-1), jnp.concatenate(odd, axis=-1)

    lo_wide = lax.broadcasted_iota(jnp.int32, (1, ATT_W), 1) % LANES < ATT_DH

    def split_heads(v):
        zero = jnp.zeros_like(v)
        return jnp.where(lo_wide, v, zero), jnp.where(lo_wide, zero, v)

    @pl.when(c == 0)
    def _():
        kall[:, :, 0] = jnp.zeros((2, dil, ATT_BLOCK, ATT_W), BF16)
        vall[:, :, 0] = jnp.zeros((2, dil, ATT_BLOCK, ATT_W), BF16)

    @pl.when(c > 0)
    def _():
        def carry_over(rho, carry):
            for hh in range(2):
                kall[hh, rho, 0] = kall[hh, rho, nb]
                vall[hh, rho, 0] = vall[hh, rho, nb]
            return carry

        lax.fori_loop(0, dil, carry_over, 0)

    def fill_main(it, carry):
        rho = it // nb
        blk = it % nb
        kall[0, rho, blk + 1], kall[1, rho, blk + 1] = norm_block(km_ref[rho, blk])
        vall[0, rho, blk + 1], vall[1, rho, blk + 1] = split_heads(vm_ref[rho, blk])
        qsrc = q_ref[rho, blk]
        qall[rho, blk] = jnp.concatenate(
            [head_norm(qsrc[:, s * LANES:(s + 1) * LANES].astype(F32), gq).astype(BF16) for s in range(n_slab)], axis=-1)
        return carry

    lax.fori_loop(0, dil * nb, fill_main, 0, unroll=4)

    def fill_halo(rho, carry):
        kall[0, rho, nb + 1], kall[1, rho, nb + 1] = norm_block(kn_ref[rho, 0])
        vall[0, rho, nb + 1], vall[1, rho, nb + 1] = split_heads(vn_ref[rho, 0])
        return carry

    lax.fori_loop(0, dil, fill_halo, 0)

    ones_even = jnp.broadcast_to(jnp.where(lo, 1.0, 0.0).astype(BF16), (KW, LANES))
    ones_odd = jnp.broadcast_to(jnp.where(lo, 0.0, 1.0).astype(BF16), (KW, LANES))

    def body(it, carry):
        rho = it // nqb
        qb = it % nqb
        var = (jnp.where((qb == 0) & (is_first == 1), 1, 0)
               + jnp.where((qb == nqb - 1) & (is_last == 1), 2, 0))
        start = rho + qb * (QB * dil)
        rows = pl.ds(start, QB) if dil == 1 else pl.ds(start, QB, stride=dil)
        for s in range(n_slab):
            sl = slice(s * LANES, (s + 1) * LANES)
            qn = qall[rho, pl.ds(2 * qb, 2), :, sl].reshape(QB, LANES)
            es, ms = [], []
            for hh in range(2):
                kw = kall[hh, rho, pl.ds(2 * qb, 4), :, sl].reshape(KW, LANES)
                sc = lax.dot_general(qn, kw, (((1,), (1,)), ((), ())), preferred_element_type=F32)
                sc = sc + bvar[var, 2 * s + hh]
                m = jnp.max(sc, axis=-1, keepdims=True)
                es.append(jnp.exp2(sc - m).astype(BF16))
                ms.append(m)
            v_even = vall[0, rho, pl.ds(2 * qb, 4), :, sl].reshape(KW, LANES)
            v_odd = vall[1, rho, pl.ds(2 * qb, 4), :, sl].reshape(KW, LANES)
            rhs = jnp.concatenate([jnp.concatenate([v_even, ones_even], axis=1),
                                   jnp.concatenate([v_odd, ones_odd], axis=1)], axis=0)
            res = jnp.dot(jnp.concatenate(es, axis=1), rhs, preferred_element_type=F32)
            den = res[:, LANES:]
            oacc[s, rows, :] = res[:, :LANES] * (1.0 / den)
            lse_ref[s, rows, :] = (jnp.where(lo, ms[0], ms[1]) + jnp.log2(den)) * LN2
        return carry

    lax.fori_loop(0, dil * nqb, body, 0, unroll=8)

    for s in range(n_slab):
        o_ref[:, s * LANES:(s + 1) * LANES] = oacc[s].astype(BF16)


def _attention_group(z, bias_g, gq, gk, first, last, gi, dil):
    nb = SEG // dil // ATT_BLOCK
    z5 = z.reshape(N_SEG, dil, nb, ATT_BLOCK, N_IN)
    cq = (COL_ATT + 3 * gi * ATT_W) // ATT_W
    ck, cv = cq + 1, cq + 2
    main = lambda cb: pl.BlockSpec((None, dil, nb, ATT_BLOCK, ATT_W), lambda c, f, l: (c, 0, 0, 0, cb))
    nxt = lambda cb: pl.BlockSpec((None, dil, 1, ATT_BLOCK, ATT_W),
                                  lambda c, f, l: (c + 1 - l[c], 0, 0, 0, cb), pipeline_mode=pl.Buffered(1))
    grid_spec = pltpu.PrefetchScalarGridSpec(
        num_scalar_prefetch=2,
        grid=(N_SEG,),
        in_specs=[
            main(cq), main(ck), nxt(ck), main(cv), nxt(cv),
            pl.BlockSpec((ATT_HEADS, QB, KW), lambda c, f, l: (0, 0, 0)),
            pl.BlockSpec((1, LANES), lambda c, f, l: (0, 0)),
            pl.BlockSpec((1, LANES), lambda c, f, l: (0, 0)),
        ],
        out_specs=[
            pl.BlockSpec((SEG, ATT_W), lambda c, f, l: (c, 0)),
            pl.BlockSpec((ATT_W // LANES, SEG, LANES), lambda c, f, l: (0, c, 0)),
        ],
        scratch_shapes=[
            pltpu.VMEM((2, dil, nb + 2, ATT_BLOCK, ATT_W), BF16),
            pltpu.VMEM((2, dil, nb + 2, ATT_BLOCK, ATT_W), BF16),
            pltpu.VMEM((dil, nb, ATT_BLOCK, ATT_W), BF16),
            pltpu.VMEM((ATT_W // LANES, SEG, LANES), F32),
            pltpu.VMEM((4, ATT_HEADS, QB, KW), F32),
        ],
    )
    return pl.pallas_call(
        functools.partial(_attention_kernel, dil=dil),
        grid_spec=grid_spec,
        out_shape=[jax.ShapeDtypeStruct((T_ALL, ATT_W), BF16), jax.ShapeDtypeStruct((ATT_W // LANES, T_ALL, LANES), F32)],
        compiler_params=_cparams(("arbitrary",), ATT_VMEM_LIMIT),
        name=f"attention_d{dil}",
    )(first, last, z5, z5, z5, z5, z5, bias_g, gq, gk)


def _merge_kernel(yret_ref, o0_ref, o1_ref, o2_ref, l0_ref, l1_ref, l2_ref,
                  gret_a_ref, gret_b_ref, gatt_a_ref, gatt_b_ref, xp_ref, xs_ref,
                  wret_ref, watt_ref, wout_ref, nffn_ref, wr_ref, wrhi_ref, br_ref,
                  x1_ref, hpa_ref, hpb_ref, idx_ref, gate_ref, rank_ref, cnt_ref, carry_ref):
    i = pl.program_id(0)
    tm = MERGE_TM
    sub = MERGE_SUB

    @pl.when(i == 0)
    def _():
        carry_ref[...] = jnp.zeros_like(carry_ref)

    is_prompt = i < T_PROMPT // tm
    lane = lax.broadcasted_iota(jnp.int32, (sub, LANES), 1)
    lane_f = lane.astype(F32)
    tri = jnp.where(lax.broadcasted_iota(jnp.int32, (sub, sub), 1) < lax.broadcasted_iota(jnp.int32, (sub, sub), 0),
                    1.0, 0.0).astype(BF16)
    carry = carry_ref[...]

    for r0 in range(0, tm, sub):
        rows = slice(r0, r0 + sub)
        l0, l1, l2 = [jnp.concatenate([r[s, rows, :] for s in range(ATT_W // LANES)], axis=-1)
                      for r in (l0_ref, l1_ref, l2_ref)]
        lm = jnp.maximum(jnp.maximum(l0, l1), l2)
        e0, e1, e2 = jnp.exp(l0 - lm), jnp.exp(l1 - lm), jnp.exp(l2 - lm)
        inv = 1.0 / (e0 + e1 + e2)
        y_att = ((e0 * inv) * o0_ref[rows, :].astype(F32) + (e1 * inv) * o1_ref[rows, :].astype(F32)
                 + (e2 * inv) * o2_ref[rows, :].astype(F32))

        p_ret = jnp.dot(yret_ref[rows, :], wret_ref[...], preferred_element_type=F32)
        p_att = jnp.dot(y_att.astype(BF16), watt_ref[...], preferred_element_type=F32)
        g_ret = jnp.concatenate([gret_a_ref[rows, :], gret_b_ref[rows, :]], axis=-1).astype(F32)
        g_att = jnp.concatenate([gatt_a_ref[rows, :], gatt_b_ref[rows, :]], axis=-1).astype(F32)
        merged = _sigmoid(g_ret) * p_ret + _sigmoid(g_att) * p_att
        x_in = jnp.where(is_prompt, xp_ref[rows, :], xs_ref[rows, :])
        x1 = x_in + jnp.dot(merged.astype(BF16), wout_ref[...], preferred_element_type=F32)
        x1_ref[rows, :] = x1

        ms = jnp.mean(x1 * x1, axis=-1, keepdims=True)
        h2 = x1 * lax.rsqrt(ms + EPS) * nffn_ref[...]
        hpa_ref[rows, :], hpb_ref[rows, :] = _pack_row_halves(h2)

        h_hi = h2.astype(BF16)
        h_lo = (h2 - h_hi.astype(F32)).astype(BF16)
        p1 = jnp.dot(h_hi, wr_ref[...], preferred_element_type=F32)
        p2 = jnp.dot(h_lo, wrhi_ref[...], preferred_element_type=F32)
        work = p1 + pltpu.roll(p1, LANES - N_EXPERTS, axis=1) + p2 + br_ref[...]
        vals, idxs = [], []
        for _ in range(TOP_K):
            m = jnp.max(work, axis=-1, keepdims=True)
            ix = jnp.min(jnp.where(work == m, lane_f, float(LANES)), axis=-1, keepdims=True)
            vals.append(m)
            idxs.append(ix)
            work = jnp.where(lane_f == ix, -3e38, work)
        es = [jnp.exp(v - vals[0]) for v in vals]
        den = es[0] + es[1] + es[2] + es[3]
        onehot = jnp.zeros((sub, LANES), F32)
        for ix in idxs:
            onehot = onehot + jnp.where(lane_f == ix, 1.0, 0.0)
        before = jnp.dot(tri, onehot.astype(BF16), preferred_element_type=F32) + carry
        idx_out = jnp.zeros((sub, LANES), F32)
        gate_out = jnp.zeros((sub, LANES), F32)
        rank_out = jnp.zeros((sub, LANES), F32)
        for k in range(TOP_K):
            rk = jnp.sum(jnp.where(lane_f == idxs[k], before, 0.0), axis=-1, keepdims=True)
            sel = lane == k
            idx_out = jnp.where(sel, idxs[k], idx_out)
            gate_out = jnp.where(sel, es[k] / den, gate_out)
            rank_out = jnp.where(sel, rk, rank_out)
        idx_ref[rows, :] = idx_out.astype(jnp.int32)
        gate_ref[rows, :] = gate_out
        rank_ref[rows, :] = rank_out.astype(jnp.int32)
        carry = carry + jnp.sum(onehot, axis=0, keepdims=True)

    carry_ref[...] = carry
    cnt_ref[...] = jnp.broadcast_to(carry, cnt_ref.shape)


def _merge(y_ret, o_list, lse_list, z, x_p, x_s, w_ret, w_att, w_out, n_ffn, w_router, w_router_hi, b_router):
    tm = MERGE_TM
    n_p = T_PROMPT // tm
    row = lambda w: pl.BlockSpec((tm, w), lambda i: (i, 0))
    full = lambda a: pl.BlockSpec(a.shape, lambda i: (0,) * a.ndim)
    zcol = lambda col: pl.BlockSpec((tm, COL_BLK), lambda i: (i, col // COL_BLK))
    lse_spec = pl.BlockSpec((ATT_W // LANES, tm, LANES), lambda i: (0, i, 0))
    return pl.pallas_call(
        _merge_kernel,
        grid=(T_ALL // tm,),
        in_specs=[row(RET_V_W), row(ATT_W), row(ATT_W), row(ATT_W), lse_spec, lse_spec, lse_spec,
                  zcol(COL_GATE_RET), zcol(COL_GATE_RET + COL_BLK), zcol(COL_GATE_ATT),
                  zcol(COL_GATE_ATT + COL_BLK),
                  pl.BlockSpec((tm, D_MODEL), lambda i: (jnp.minimum(i, n_p - 1), 0)),
                  pl.BlockSpec((tm, D_MODEL), lambda i: (jnp.maximum(i - n_p, 0), 0)),
                  full(w_ret), full(w_att), full(w_out), full(n_ffn), full(w_router), full(w_router_hi),
                  full(b_router)],
        out_specs=[row(D_MODEL), row(HALF_W), row(HALF_W), row(LANES), row(LANES), row(LANES),
                   pl.BlockSpec((8, LANES), lambda i: (0, 0))],
        out_shape=[jax.ShapeDtypeStruct((T_ALL, D_MODEL), F32),
                   jax.ShapeDtypeStruct((T_ALL, HALF_W), jnp.uint32),
                   jax.ShapeDtypeStruct((T_ALL, HALF_W), jnp.uint32),
                   jax.ShapeDtypeStruct((T_ALL, LANES), jnp.int32),
                   jax.ShapeDtypeStruct((T_ALL, LANES), F32),
                   jax.ShapeDtypeStruct((T_ALL, LANES), jnp.int32),
                   jax.ShapeDtypeStruct((8, LANES), F32)],
        scratch_shapes=[pltpu.VMEM((1, LANES), F32)],
        compiler_params=_cparams(("arbitrary",)),
        name="merge_router",
    )(y_ret, *o_list, *lse_list, z, z, z, z, x_p, x_s, w_ret, w_att, w_out, n_ffn, w_router, w_router_hi, b_router)


def _sc_mesh():
    return plsc.VectorSubcoreMesh(core_axis_name="core", subcore_axis_name="subcore")


def _sc_scatter_rows(xs, idx_kmajor, n_out):
    n_rows, width = xs[0].shape
    out_type = [jax.ShapeDtypeStruct((n_out, width), x.dtype) for x in xs]

    @pl.kernel(out_type=out_type, mesh=_sc_mesh(), scratch_types=[])
    def scatter(*refs):
        x_refs, i_hbm, o_refs = refs[:len(xs)], refs[len(xs)], refs[len(xs) + 1:]
        for x_hbm, o_hbm in zip(x_refs, o_refs):
            def body(x_vmem, i_vmem, o_hbm=o_hbm):
                for k in range(TOP_K):
                    pltpu.sync_copy(x_vmem, o_hbm.at[i_vmem.at[k]])

            pltpu.emit_pipeline(
                body,
                grid=(n_rows // SC_WINDOW,),
                in_specs=[pl.BlockSpec((SC_WINDOW, width), lambda i: (i, 0)),
                          pl.BlockSpec((TOP_K, SC_WINDOW), lambda i: (0, i))],
                out_specs=[],
                core_axis_name=("core", "subcore"),
                dimension_semantics=(pltpu.PARALLEL,),
            )(x_hbm, i_hbm)

    return scatter(*xs, idx_kmajor)


def _sc_gather_rows(datas, idx):
    n_idx = idx.shape[0]
    width = datas[0].shape[1]
    out_type = [jax.ShapeDtypeStruct((n_idx, width), d.dtype) for d in datas]

    @pl.kernel(out_type=out_type, mesh=_sc_mesh(), scratch_types=[])
    def gather(*refs):
        x_refs, i_hbm, o_refs = refs[:len(datas)], refs[len(datas)], refs[len(datas) + 1:]
        for x_hbm, o_hbm in zip(x_refs, o_refs):
            def body(i_vmem, o_vmem, x_hbm=x_hbm):
                pltpu.sync_copy(x_hbm.at[i_vmem.at[0]], o_vmem)

            pltpu.emit_pipeline(
                body,
                grid=(n_idx // SC_WINDOW,),
                in_specs=[pl.BlockSpec((1, SC_WINDOW), lambda i: (0, i))],
                out_specs=[pl.BlockSpec((SC_WINDOW, width), lambda i: (i, 0))],
                core_axis_name=("core", "subcore"),
                dimension_semantics=(pltpu.PARALLEL,),
            )(i_hbm, o_hbm)

    return gather(*datas, idx.reshape(1, n_idx))


def _expert_kernel(be_ref, nused_ref, nvalid_ref, xa_ref, xb_ref, wg_ref, bg_ref, wu_ref, bu_ref, wd_ref, bd_ref,
                   ya_ref, yb_ref, wbf_ref):
    b = pl.program_id(0)
    active = b < nused_ref[0]
    new_expert = (b == 0) | (be_ref[b] != be_ref[jnp.maximum(b - 1, 0)])

    @pl.when(active & new_expert)
    def _():
        rows = 128
        for wi, w_ref in enumerate((wg_ref, wu_ref, wd_ref)):
            for r in range(0, D_MODEL, rows):
                wbf_ref[wi, r:r + rows, :] = w_ref[0, r:r + rows, :].astype(BF16)

    @pl.when(active)
    def _():
        valid = lax.broadcasted_iota(jnp.int32, (MOE_BM, HALF_W), 0) < nvalid_ref[b]
        zero = jnp.zeros((MOE_BM, HALF_W), jnp.uint32)
        x = _unpack_row_halves(jnp.where(valid, xa_ref[...], zero), jnp.where(valid, xb_ref[...], zero)).astype(BF16)
        g = jnp.dot(x, wbf_ref[0], preferred_element_type=F32) + bg_ref[0]
        u = jnp.dot(x, wbf_ref[1], preferred_element_type=F32) + bu_ref[0]
        g = jnp.minimum(g, SWIGLU_LIMIT)
        u = jnp.clip(u, -SWIGLU_LIMIT, SWIGLU_LIMIT)
        glu = g * jax.nn.sigmoid(SWIGLU_ALPHA * g)
        act = ((u + 1.0) * glu).astype(BF16)
        y = jnp.dot(act, wbf_ref[2], preferred_element_type=F32) + bd_ref[0]
        ya_ref[...], yb_ref[...] = _pack_row_halves(y)

    @pl.when(jnp.logical_not(active))
    def _():
        ya_ref[...] = jnp.zeros_like(ya_ref)
        yb_ref[...] = jnp.zeros_like(yb_ref)


def _experts(block_expert, n_used, n_valid, xs_a, xs_b, wg, bg, wu, bu, wd, bd):
    assert D_FF == D_MODEL
    blk = lambda b, be, nu, nv: (jnp.minimum(b, nu[0] - 1), 0)
    wsp = lambda: pl.BlockSpec((1, D_MODEL, D_FF), lambda b, be, nu, nv: (be[b], 0, 0))
    bsp = lambda: pl.BlockSpec((1, 1, D_FF), lambda b, be, nu, nv: (be[b], 0, 0))
    xsp = lambda: pl.BlockSpec((MOE_BM, HALF_W), blk)
    ysp = lambda: pl.BlockSpec((MOE_BM, HALF_W), lambda b, be, nu, nv: (b, 0))
    slot_arr = jax.ShapeDtypeStruct((N_SLOTS, HALF_W), jnp.uint32)
    grid_spec = pltpu.PrefetchScalarGridSpec(
        num_scalar_prefetch=3,
        grid=(N_SLOT_BLOCKS,),
        in_specs=[xsp(), xsp(), wsp(), bsp(), wsp(), bsp(), wsp(), bsp()],
        out_specs=[ysp(), ysp()],
        scratch_shapes=[pltpu.VMEM((3, D_MODEL, D_FF), BF16)],
    )
    return pl.pallas_call(
        _expert_kernel,
        grid_spec=grid_spec,
        out_shape=[slot_arr, slot_arr],
        compiler_params=_cparams(("arbitrary",)),
        name="experts",
    )(block_expert, n_used, n_valid, xs_a, xs_b, wg, bg, wu, bu, wd, bd)


def _final_kernel(x1_ref, yga_ref, ygb_ref, gate_ref, p_ref, nple_ref, wpg_ref, wpp_ref, out_ref):
    x2 = x1_ref[...]
    gates = gate_ref[...]
    for k in range(TOP_K):
        x2 = x2 + gates[:, k:k + 1] * _unpack_row_halves(yga_ref[k], ygb_ref[k])
    ms = jnp.mean(x2 * x2, axis=-1, keepdims=True)
    h3 = (x2 * lax.rsqrt(ms + EPS) * nple_ref[...]).astype(BF16)
    gate = jax.nn.sigmoid(jnp.dot(h3, wpg_ref[...], preferred_element_type=F32))
    proj = jnp.dot(p_ref[...].astype(BF16), wpp_ref[...], preferred_element_type=F32)
    out_ref[...] = x2 + gate * proj


def _final(x1, yg_a, yg_b, gates, p, n_ple, w_pg, w_pp, row0, n_rows):
    tm = FINAL_TM
    off = row0 // tm
    full = lambda a: pl.BlockSpec(a.shape, lambda i: (0,) * a.ndim)
    return pl.pallas_call(
        _final_kernel,
        grid=(n_rows // tm,),
        in_specs=[pl.BlockSpec((tm, D_MODEL), lambda i: (i + off, 0)),
                  pl.BlockSpec((TOP_K, tm, HALF_W), lambda i: (0, i + off, 0)),
                  pl.BlockSpec((TOP_K, tm, HALF_W), lambda i: (0, i + off, 0)),
                  pl.BlockSpec((tm, LANES), lambda i: (i + off, 0)),
                  pl.BlockSpec((tm, PLE_DIM), lambda i: (i, 0)),
                  full(n_ple), full(w_pg), full(w_pp)],
        out_specs=pl.BlockSpec((tm, D_MODEL), lambda i: (i, 0)),
        out_shape=jax.ShapeDtypeStruct((n_rows, D_MODEL), F32),
        compiler_params=_cparams(("arbitrary",)),
        name="final_ple",
    )(x1, yg_a, yg_b, gates, p, n_ple, w_pg, w_pp)


def _rope_tables():
    half = RET_DK // 2
    step = 128
    freq = ROPE_THETA ** (-jnp.arange(half, dtype=F32) / half)
    freq = jnp.concatenate([freq, freq])
    sign = jnp.where(jnp.arange(RET_DK) < half, -1.0, 1.0).astype(F32)
    ang_lo = jnp.arange(step, dtype=F32)[:, None] * freq[None, :]
    ang_hi = (jnp.arange(SAMPLE_SEQ // step, dtype=F32) * step)[:, None] * freq[None, :]
    c_lo, s_lo = jnp.cos(ang_lo)[None], jnp.sin(ang_lo)[None]
    c_hi, s_hi = jnp.cos(ang_hi)[:, None], jnp.sin(ang_hi)[:, None]
    cos = (c_hi * c_lo - s_hi * s_lo).reshape(SAMPLE_SEQ, RET_DK)
    sin = ((s_hi * c_lo + c_hi * s_lo) * sign).reshape(SAMPLE_SEQ, RET_DK)
    return cos, sin


def _retention_tables(decay_logit):
    lg = jax.nn.log_sigmoid(decay_logit.astype(F32))
    c = RET_CHUNK
    idx = jnp.arange(c, dtype=F32)
    diff = idx[:, None] - idx[None, :]
    lf = lg[0][:, None, None]
    lb = lg[1][:, None, None]
    mask = jnp.where(diff[None] >= 0, jnp.exp(lf * jnp.maximum(diff, 0.0)[None]),
                     jnp.exp(lb * jnp.maximum(-diff, 0.0)[None]))
    kdec_f = jnp.exp(lg[0][:, None] * (c - 1.0 - idx)[None, :])
    qdec_f = jnp.exp(lg[0][:, None] * (idx + 1.0)[None, :])
    kdec_b = jnp.exp(lg[1][:, None] * idx[None, :])
    qdec_b = jnp.exp(lg[1][:, None] * (c - idx)[None, :])
    dec = jnp.stack([kdec_f, qdec_f, kdec_b, qdec_b], axis=1)[..., None]
    cdec = jnp.exp(lg * c).T
    cdec = jnp.broadcast_to(cdec[:, :, None, None], (RET_HEADS, 2, 1, RET_DV))
    return mask, dec, cdec


def _t5_bucket(rel):
    half = T5_BUCKETS // 2
    exact = half // 2
    n = np.abs(rel)
    ratio = np.log(np.maximum(n, 1).astype(np.float32) / np.float32(exact)) / np.float32(math.log(T5_MAX_DIST / exact))
    large = exact + (ratio * np.float32(half - exact)).astype(np.int32)
    large = np.minimum(large, half - 1)
    return np.where(rel > 0, half, 0) + np.where(n < exact, n, large)


def _attention_bias(rel_bias, gi, dil, radius):
    qi = np.arange(QB)
    ki = np.arange(KW) - ATT_BLOCK
    rel = ki[None, :] - qi[:, None]
    onehot = jnp.asarray(_t5_bucket(rel * dil)[..., None] == np.arange(T5_BUCKETS), F32)
    tab = rel_bias[:, gi * ATT_HEADS:(gi + 1) * ATT_HEADS].astype(F32)
    bias = jnp.einsum('qkb,bh->hqk', onehot, tab, precision=lax.Precision.HIGHEST)
    return jnp.where(jnp.asarray(np.abs(rel) <= radius)[None], bias * LOG2E, NEG)


def _seq_edge_flags():
    first = np.zeros((N_SEG,), np.int32)
    last = np.zeros((N_SEG,), np.int32)
    first[:N_PROMPT_SEG] = 1
    last[:N_PROMPT_SEG] = 1
    first[N_PROMPT_SEG] = 1
    last[N_SEG - 1] = 1
    return jnp.asarray(first), jnp.asarray(last)


def _pad_lanes(a, value=0.0):
    return jnp.pad(a, ((0, 0), (0, LANES - a.shape[-1])), constant_values=value)


def kernel(x_prompt, x_sample, p_prompt, p_sample, norm_mix_g, w_in, ret_decay_logit, ret_gn_g,
           att_q_norm_g, att_k_norm_g, rel_bias, w_ret_proj, w_att_proj, w_out, norm_ffn_g,
           w_router, b_router, w_gate, b_gate, w_up, b_up, w_down, b_down,
           norm_ple_g, w_ple_gate, w_ple_proj):
    assert norm_mix_g.shape[0] == 1, "one layer"
    x_p = x_prompt.reshape(T_PROMPT, D_MODEL)
    x_s = x_sample.reshape(SAMPLE_SEQ, D_MODEL)

    cos_t, sin_t = _rope_tables()
    z = _in_proj(x_p, x_s, norm_mix_g.astype(F32), w_in[0].astype(BF16), cos_t, sin_t)

    ret_mask, ret_dec, ret_cdec = _retention_tables(ret_decay_logit[0])
    y_ret = _retention(z, ret_mask, ret_dec, ret_cdec, ret_gn_g[0].reshape(RET_HEADS, 1, RET_DV).astype(F32))

    first, last = _seq_edge_flags()
    o_list, lse_list = [], []
    for gi, (window, dil) in enumerate(ATT_GROUPS):
        bias_g = _attention_bias(rel_bias, gi, dil, window // (2 * dil))
        gq = jnp.tile(att_q_norm_g[0, gi].astype(F32) * (ATT_DH ** -0.5 * LOG2E), LANES // ATT_DH)[None, :]
        gk = jnp.tile(att_k_norm_g[0, gi].astype(F32), LANES // ATT_DH)[None, :]
        o_g, lse_g = _attention_group(z, bias_g, gq, gk, first, last, gi, dil)
        o_list.append(o_g)
        lse_list.append(lse_g)

    w_r = w_router[0].astype(F32)
    w_r_hi = w_r.astype(BF16)
    w_r_lo = (w_r - w_r_hi.astype(F32)).astype(BF16)
    w_router_cat = _pad_lanes(jnp.concatenate([w_r_hi, w_r_lo], axis=1))
    w_router_hi = _pad_lanes(w_r_hi)
    b_router_p = _pad_lanes(b_router.astype(F32), NEG)
    x1, hp_a, hp_b, idx, gates, rank, cnt = _merge(
        y_ret, o_list, lse_list, z, x_p, x_s, w_ret_proj[0].astype(BF16), w_att_proj[0].astype(BF16),
        w_out[0].astype(BF16), norm_ffn_g.astype(F32), w_router_cat, w_router_hi, b_router_p)

    counts = cnt[0, :N_EXPERTS].astype(jnp.int32)
    padded = (counts + MOE_BM - 1) // MOE_BM * MOE_BM
    pad_end = jnp.cumsum(padded)
    pad_start = pad_end - padded
    expert_ids = jnp.arange(N_EXPERTS, dtype=jnp.int32)
    top_idx = idx[:, :TOP_K]
    start_of = jnp.sum(jnp.where(top_idx[:, :, None] == expert_ids, pad_start, 0), axis=-1)
    dest_kmajor = (start_of + rank[:, :TOP_K]).T
    n_used = (pad_end[-1] // MOE_BM).astype(jnp.int32).reshape(1)
    blk_row0 = jnp.arange(N_SLOT_BLOCKS, dtype=jnp.int32) * MOE_BM
    block_expert = jnp.minimum(jnp.sum((pad_end[None, :] <= blk_row0[:, None]).astype(jnp.int32), axis=1),
                               N_EXPERTS - 1).astype(jnp.int32)
    slot_end = jnp.sum(jnp.where(block_expert[:, None] == expert_ids, pad_start + counts, 0), axis=-1)
    n_valid = jnp.clip(slot_end - blk_row0, 0, MOE_BM).astype(jnp.int32)

    xs_a, xs_b = _sc_scatter_rows((hp_a, hp_b), dest_kmajor, N_SLOTS)
    ys_a, ys_b = _experts(block_expert, n_used, n_valid, xs_a, xs_b,
                          w_gate[0], b_gate[0].reshape(N_EXPERTS, 1, D_FF).astype(F32),
                          w_up[0], b_up[0].reshape(N_EXPERTS, 1, D_FF).astype(F32),
                          w_down[0], b_down[0].reshape(N_EXPERTS, 1, D_MODEL).astype(F32))
    dest_flat = dest_kmajor.reshape(-1)
    yg_a, yg_b = [y.reshape(TOP_K, T_ALL, HALF_W) for y in _sc_gather_rows((ys_a, ys_b), dest_flat)]

    n_ple = norm_ple_g.astype(F32)
    w_pg = w_ple_gate[0].astype(BF16)
    w_pp = w_ple_proj[0].astype(BF16)
    y_p = _final(x1, yg_a, yg_b, gates, p_prompt[0].reshape(T_PROMPT, PLE_DIM), n_ple, w_pg, w_pp, 0, T_PROMPT)
    y_s = _final(x1, yg_a, yg_b, gates, p_sample[0].reshape(SAMPLE_SEQ, PLE_DIM), n_ple, w_pg, w_pp, T_PROMPT, SAMPLE_SEQ)
    return (y_p.reshape(x_prompt.shape), y_s.reshape(x_sample.shape))
```

```python
import functools
import math

import jax
import jax.numpy as jnp
import numpy as np
from jax import lax
from jax.experimental import pallas as pl
from jax.experimental.pallas import tpu as pltpu
from jax.experimental.pallas import tpu_sc as plsc

F32 = jnp.float32
BF16 = jnp.bfloat16

D_MODEL = 1024
N_PROMPT_SEQ = 8
PROMPT_SEQ = 2048
SAMPLE_SEQ = 16384
T_PROMPT = N_PROMPT_SEQ * PROMPT_SEQ
T_ALL = T_PROMPT + SAMPLE_SEQ

RET_HEADS = 4
RET_DK = 128
RET_DV = 256
RET_CHUNK = 128
ROPE_THETA = 10000.0
ATT_GROUPS = ((128, 1), (512, 4), (2048, 16))
N_GROUPS = 3
ATT_HEADS = 8
ATT_DH = 64
ATT_BLOCK = 64
ATT_W = ATT_HEADS * ATT_DH
T5_BUCKETS = 32
T5_MAX_DIST = 1024
N_EXPERTS = 32
TOP_K = 4
D_FF = 1024
SWIGLU_ALPHA = 1.702
SWIGLU_LIMIT = 7.0
PLE_DIM = 256
EPS = 1e-6

RET_QK_W = RET_HEADS * RET_DK
RET_V_W = RET_HEADS * RET_DV
N_IN = 2 * RET_QK_W + 2 * RET_V_W + 3 * N_GROUPS * ATT_W + 2 * D_MODEL

COL_RQ = 0
COL_RK = RET_QK_W
COL_RV = 2 * RET_QK_W
COL_RG = COL_RV + RET_V_W
COL_ATT = COL_RG + RET_V_W
COL_GATE_RET = COL_ATT + 3 * N_GROUPS * ATT_W
COL_GATE_ATT = COL_GATE_RET + D_MODEL

LANES = 128
VMEM_LIMIT = 56 * 1024 * 1024
IN_PROJ_VMEM_LIMIT = 60 * 1024 * 1024
ATT_VMEM_LIMIT = 58 * 1024 * 1024

SEG = 2048
N_SEG = T_ALL // SEG
N_PROMPT_SEG = T_PROMPT // SEG
COL_BLK = 512
N_COL_BLK = N_IN // COL_BLK
QB = 128
KW = 256
NEG = -1e30
LOG2E = math.log2(math.e)
LN2 = math.log(2.0)
MERGE_TM = 512
MERGE_SUB = 512
FINAL_TM = 1024
MOE_BM = 1024
N_SLOT_BLOCKS = T_ALL * TOP_K // MOE_BM + N_EXPERTS
N_SLOTS = N_SLOT_BLOCKS * MOE_BM
HALF_W = D_MODEL // 4
SC_WINDOW = 128


def _cparams(sem, vmem=VMEM_LIMIT):
    return pltpu.CompilerParams(dimension_semantics=sem, vmem_limit_bytes=vmem)


def _sigmoid(x):
    return 0.5 * jnp.tanh(0.5 * x) + 0.5


def _pack_bf16_pair(x):
    w = x.shape[-1] // 2
    hi = pltpu.bitcast(x[:, :w].astype(BF16).astype(F32), jnp.uint32)
    lo = pltpu.bitcast(x[:, w:].astype(BF16).astype(F32), jnp.uint32)
    return hi | (lo >> 16)


def _unpack_bf16_pair(p):
    hi = pltpu.bitcast(p & jnp.uint32(0xFFFF0000), F32)
    lo = pltpu.bitcast(p << 16, F32)
    return jnp.concatenate([hi, lo], axis=-1)


def _pack_row_halves(x):
    half = x.shape[-1] // 2
    return _pack_bf16_pair(x[:, :half]), _pack_bf16_pair(x[:, half:])


def _unpack_row_halves(pa, pb):
    return jnp.concatenate([_unpack_bf16_pair(pa), _unpack_bf16_pair(pb)], axis=-1)


def _in_proj_kernel(xp_ref, xs_ref, g_ref, w_ref, cos_ref, sin_ref, z_ref, h_ref, p_ref, p2_ref):
    i = pl.program_id(0)
    j = pl.program_id(1)

    def norm_into_h(x_ref):
        xf = x_ref[...]
        ms = jnp.mean(xf * xf, axis=-1, keepdims=True)
        h_ref[...] = (xf * lax.rsqrt(ms + EPS) * g_ref[...]).astype(BF16)

    @pl.when((j == 0) & (i < N_PROMPT_SEG))
    def _():
        norm_into_h(xp_ref)

    @pl.when((j == 0) & (i >= N_PROMPT_SEG))
    def _():
        norm_into_h(xs_ref)

    n_slab = COL_BLK // LANES

    def project():
        return jnp.dot(h_ref[...], w_ref[...], preferred_element_type=F32)

    is_rope = j < (COL_RV // COL_BLK)
    att0 = COL_ATT // COL_BLK
    is_d4 = (j >= att0 + 3) & (j < att0 + 6)
    is_d16 = (j >= att0 + 6) & (j < att0 + 9)

    @pl.when(is_rope)
    def _():
        acc = project()
        scale = jnp.where(j == COL_RK // COL_BLK, RET_DK ** -0.5, 1.0).astype(F32)
        c = cos_ref[...]
        sn = sin_ref[...]
        for s in range(n_slab):
            xs = acc[:, s * LANES:(s + 1) * LANES]
            r = xs * c + pltpu.roll(xs, RET_DK // 2, axis=1) * sn
            z_ref[:, s * LANES:(s + 1) * LANES] = (r * scale).astype(BF16)

    @pl.when(is_d4)
    def _():
        acc = project()
        for s in range(n_slab):
            p_ref[s] = acc[:, s * LANES:(s + 1) * LANES]
        rows = SEG // 4
        for rho in range(4):
            for s in range(n_slab):
                piece = p_ref[s, pl.ds(rho, rows, stride=4), :]
                z_ref[rho * rows:(rho + 1) * rows, s * LANES:(s + 1) * LANES] = piece.astype(BF16)

    @pl.when(is_d16)
    def _():
        acc = project()
        for s in range(n_slab):
            p_ref[s] = acc[:, s * LANES:(s + 1) * LANES]
        quarter = SEG // 4
        rows = SEG // 16
        for r4 in range(4):
            for s in range(n_slab):
                p2_ref[s, r4 * quarter:(r4 + 1) * quarter, :] = p_ref[s, pl.ds(r4, quarter, stride=4), :]
        for r4 in range(4):
            for hi in range(4):
                rho = 4 * hi + r4
                for s in range(n_slab):
                    piece = p2_ref[s, pl.ds(r4 * quarter + hi, rows, stride=4), :]
                    z_ref[rho * rows:(rho + 1) * rows, s * LANES:(s + 1) * LANES] = piece.astype(BF16)

    @pl.when(jnp.logical_not(is_rope | is_d4 | is_d16))
    def _():
        z_ref[...] = project().astype(BF16)


def _in_proj(x_p, x_s, norm_g, w_in_bf, cos_t, sin_t):
    def pos_blk(i, j):
        return (jnp.maximum(i - N_PROMPT_SEG, 0), 0)

    return pl.pallas_call(
        _in_proj_kernel,
        grid=(N_SEG, N_COL_BLK),
        in_specs=[
            pl.BlockSpec((SEG, D_MODEL), lambda i, j: (jnp.minimum(i, N_PROMPT_SEG - 1), 0)),
            pl.BlockSpec((SEG, D_MODEL), pos_blk),
            pl.BlockSpec((1, D_MODEL), lambda i, j: (0, 0)),
            pl.BlockSpec((D_MODEL, COL_BLK), lambda i, j: (0, j)),
            pl.BlockSpec((SEG, LANES), pos_blk, pipeline_mode=pl.Buffered(1)),
            pl.BlockSpec((SEG, LANES), pos_blk, pipeline_mode=pl.Buffered(1)),
        ],
        out_specs=pl.BlockSpec((SEG, COL_BLK), lambda i, j: (i, j)),
        out_shape=jax.ShapeDtypeStruct((T_ALL, N_IN), BF16),
        scratch_shapes=[
            pltpu.VMEM((SEG, D_MODEL), BF16),
            pltpu.VMEM((COL_BLK // LANES, SEG, LANES), F32),
            pltpu.VMEM((COL_BLK // LANES, SEG, LANES), F32),
        ],
        compiler_params=_cparams(("arbitrary", "arbitrary"), IN_PROJ_VMEM_LIMIT),
        name="in_proj",
    )(x_p, x_s, norm_g, w_in_bf, cos_t, sin_t)


RET_CHUNKS_PER_SEG = SEG // RET_CHUNK
RET_MAX_CHUNKS = SAMPLE_SEQ // RET_CHUNK
RET_GROUP = 16


def _retention_kernel(seg_ref, phase_ref, reset_ref, cbase_ref,
                      q_ref, k_ref, v_ref, g_ref, mask_ref, dec_ref, cdec_ref, gn_ref,
                      y_ref, sb_ref, sf_ref, sr_ref):
    step = pl.program_id(1)
    phase = phase_ref[step]
    reset = reset_ref[step]
    cbase = cbase_ref[step]
    kdec_f = dec_ref[0, 0]
    qdec_f = dec_ref[0, 1]
    kdec_b = dec_ref[0, 2]
    qdec_b = dec_ref[0, 3]
    cd_f = cdec_ref[0, 0]
    cd_b = cdec_ref[0, 1]

    def kv_outer(kd, v):
        return lax.dot_general(kd, v, (((0,), (0,)), ((), ())), preferred_element_type=F32)

    @pl.when((phase == 0) & (reset == 1))
    def _():
        sr_ref[...] = jnp.zeros_like(sr_ref)

    @pl.when((phase == 1) & (reset == 1))
    def _():
        sf_ref[...] = jnp.zeros_like(sf_ref)

    n_groups = RET_CHUNKS_PER_SEG // RET_GROUP

    def chunk_rows(c):
        return pl.ds(pl.multiple_of(c * RET_CHUNK, RET_CHUNK), RET_CHUNK)

    @pl.when(phase == 0)
    def _():
        def body(it, carry):
            top = RET_CHUNKS_PER_SEG - 1 - it * RET_GROUP
            kvs = []
            for j in range(RET_GROUP):
                rows = chunk_rows(top - j)
                kd = (k_ref[rows, :].astype(F32) * kdec_b).astype(BF16)
                kvs.append(kv_outer(kd, v_ref[rows, :]))
            state = sr_ref[...]
            for j in range(RET_GROUP):
                sb_ref[cbase + top - j] = state.astype(BF16)
                state = cd_b * state + kvs[j]
            sr_ref[...] = state
            return carry

        lax.fori_loop(0, n_groups, body, 0)

    @pl.when(phase == 1)
    def _():
        msk = mask_ref[0]
        gn = gn_ref[0]

        def body(it, carry):
            c0 = it * RET_GROUP
            lhs, vs, kvs = [], [], []
            for j in range(RET_GROUP):
                rows = chunk_rows(c0 + j)
                qb = q_ref[rows, :]
                kb = k_ref[rows, :]
                v = v_ref[rows, :]
                q = qb.astype(F32)
                s = lax.dot_general(qb, kb, (((1,), (1,)), ((), ())), preferred_element_type=F32)
                lhs.append(jnp.concatenate(
                    [(s * msk).astype(BF16), (q * qdec_f).astype(BF16), (q * qdec_b).astype(BF16)], axis=-1))
                vs.append(v)
                kvs.append(kv_outer((kb.astype(F32) * kdec_f).astype(BF16), v))
            state = sf_ref[...]
            for j in range(RET_GROUP):
                c = c0 + j
                rhs = jnp.concatenate([vs[j], state.astype(BF16), sb_ref[cbase + c]], axis=0)
                o = jnp.dot(lhs[j], rhs, preferred_element_type=F32)
                state = cd_f * state + kvs[j]
                mu = jnp.mean(o, axis=-1, keepdims=True)
                oc = o - mu
                var = jnp.mean(oc * oc, axis=-1, keepdims=True)
                on = oc * lax.rsqrt(var + EPS) * gn
                rows = chunk_rows(c)
                gate = g_ref[rows, :].astype(F32)
                y_ref[rows, :] = (gate * jax.nn.sigmoid(gate) * on).astype(BF16)
            sf_ref[...] = state
            return carry

        lax.fori_loop(0, n_groups, body, 0)


def _retention_schedule():
    seg, phase, reset, cbase = [], [], [], []
    for p in range(N_PROMPT_SEG):
        for ph in (0, 1):
            seg.append(p); phase.append(ph); reset.append(1); cbase.append(0)
    n_s = N_SEG - N_PROMPT_SEG
    for i in range(n_s):
        t = n_s - 1 - i
        seg.append(N_PROMPT_SEG + t); phase.append(0); reset.append(int(i == 0)); cbase.append(t * RET_CHUNKS_PER_SEG)
    for t in range(n_s):
        seg.append(N_PROMPT_SEG + t); phase.append(1); reset.append(int(t == 0)); cbase.append(t * RET_CHUNKS_PER_SEG)
    hold = list(seg)
    for i in range(len(seg)):
        if phase[i] == 0:
            nxt = next(j for j in range(i + 1, len(seg)) if phase[j] == 1)
            hold[i] = seg[nxt]
    arr = lambda a: jnp.asarray(np.asarray(a, np.int32))
    return arr(seg), arr(phase), arr(reset), arr(cbase), arr(hold)


def _retention(z, ret_mask, ret_dec, ret_cdec, gn_g):
    seg, phase, reset, cbase, hold = _retention_schedule()
    n_steps = int(seg.shape[0])
    qk_blk = lambda col: (lambda h, s, seg_r, ph_r, rs_r, cb_r, hold_r: (seg_r[s], col // RET_DK + h))
    hold_blk = lambda col, w: (lambda h, s, seg_r, ph_r, rs_r, cb_r, hold_r: (hold_r[s], col // w + h))
    v_blk = lambda h, s, seg_r, ph_r, rs_r, cb_r, hold_r: (seg_r[s], COL_RV // RET_DV + h)
    per_head = lambda h, s, *_: (h, 0, 0)
    per_head4 = lambda h, s, *_: (h, 0, 0, 0)
    grid_spec = pltpu.PrefetchScalarGridSpec(
        num_scalar_prefetch=5,
        grid=(RET_HEADS, n_steps),
        in_specs=[
            pl.BlockSpec((SEG, RET_DK), hold_blk(COL_RQ, RET_DK)),
            pl.BlockSpec((SEG, RET_DK), qk_blk(COL_RK)),
            pl.BlockSpec((SEG, RET_DV), v_blk),
            pl.BlockSpec((SEG, RET_DV), hold_blk(COL_RG, RET_DV)),
            pl.BlockSpec((1, RET_CHUNK, RET_CHUNK), per_head),
            pl.BlockSpec((1, 4, RET_CHUNK, 1), per_head4),
            pl.BlockSpec((1, 2, 1, RET_DV), per_head4),
            pl.BlockSpec((1, 1, RET_DV), per_head),
        ],
        out_specs=pl.BlockSpec((SEG, RET_DV), lambda h, s, seg_r, ph_r, rs_r, cb_r, hold_r: (hold_r[s], h)),
        scratch_shapes=[
            pltpu.VMEM((RET_MAX_CHUNKS, RET_DK, RET_DV), BF16),
            pltpu.VMEM((RET_DK, RET_DV), F32),
            pltpu.VMEM((RET_DK, RET_DV), F32),
        ],
    )

    def kernel(seg_r, ph_r, rs_r, cb_r, hold_r, *refs):
        _retention_kernel(seg_r, ph_r, rs_r, cb_r, *refs)

    return pl.pallas_call(
        kernel,
        grid_spec=grid_spec,
        out_shape=jax.ShapeDtypeStruct((T_ALL, RET_V_W), BF16),
        compiler_params=_cparams(("arbitrary", "arbitrary")),
        name="retention",
    )(seg, phase, reset, cbase, hold, z, z, z, z, ret_mask, ret_dec, ret_cdec, gn_g)


def _attention_kernel(first_ref, last_ref, q_ref, km_ref, kn_ref, vm_ref, vn_ref,
                      bias_ref, gq_ref, gk_ref, o_ref, lse_ref, kall, vall, qall, oacc, bvar, *, dil):
    nb = SEG // dil // ATT_BLOCK
    nqb = nb // 2
    n_slab = ATT_W // LANES
    c = pl.program_id(0)
    is_first = first_ref[c]
    is_last = last_ref[c]
    lane = lax.broadcasted_iota(jnp.int32, (1, LANES), 1)
    lo = lane < ATT_DH
    gq = gq_ref[...]
    gk = gk_ref[...]

    @pl.when(c == 0)
    def _():
        col = lax.broadcasted_iota(jnp.int32, (1, KW), 1)
        left = jnp.where(col < ATT_BLOCK, NEG, 0.0).astype(F32)
        right = jnp.where(col >= KW - ATT_BLOCK, NEG, 0.0).astype(F32)
        for h in range(ATT_HEADS):
            b = bias_ref[h]
            bvar[0, h] = b
            bvar[1, h] = b + left
            bvar[2, h] = b + right
            bvar[3, h] = b + left + right

    def head_norm(x, g):
        x2 = x * x
        s_lo = jnp.sum(jnp.where(lo, x2, 0.0), axis=-1, keepdims=True)
        s_hi = jnp.sum(jnp.where(lo, 0.0, x2), axis=-1, keepdims=True)
        ms = jnp.where(lo, s_lo, s_hi) * (1.0 / ATT_DH)
        return x * lax.rsqrt(ms + EPS) * g

    def norm_block(src):
        even, odd = [], []
        for s in range(n_slab):
            xn = head_norm(src[:, s * LANES:(s + 1) * LANES].astype(F32), gk)
            even.append(jnp.where(lo, xn, 0.0).astype(BF16))
            odd.append(jnp.where(lo, 0.0, xn).astype(BF16))
        return jnp.concatenate(even, axis=-1), jnp.concatenate(odd, axis=-1)

    lo_wide = lax.broadcasted_iota(jnp.int32, (1, ATT_W), 1) % LANES < ATT_DH

    def split_heads(v):
        zero = jnp.zeros_like(v)
        return jnp.where(lo_wide, v, zero), jnp.where(lo_wide, zero, v)

    @pl.when(c == 0)
    def _():
        kall[:, :, 0] = jnp.zeros((2, dil, ATT_BLOCK, ATT_W), BF16)
        vall[:, :, 0] = jnp.zeros((2, dil, ATT_BLOCK, ATT_W), BF16)

    @pl.when(c > 0)
    def _():
        def carry_over(rho, carry):
            for hh in range(2):
                kall[hh, rho, 0] = kall[hh, rho, nb]
                vall[hh, rho, 0] = vall[hh, rho, nb]
            return carry

        lax.fori_loop(0, dil, carry_over, 0)

    def fill_main(it, carry):
        rho = it // nb
        blk = it % nb
        kall[0, rho, blk + 1], kall[1, rho, blk + 1] = norm_block(km_ref[rho, blk])
        vall[0, rho, blk + 1], vall[1, rho, blk + 1] = split_heads(vm_ref[rho, blk])
        qsrc = q_ref[rho, blk]
        qall[rho, blk] = jnp.concatenate(
            [head_norm(qsrc[:, s * LANES:(s + 1) * LANES].astype(F32), gq).astype(BF16) for s in range(n_slab)], axis=-1)
        return carry

    lax.fori_loop(0, dil * nb, fill_main, 0, unroll=4)

    def fill_halo(rho, carry):
        kall[0, rho, nb + 1], kall[1, rho, nb + 1] = norm_block(kn_ref[rho, 0])
        vall[0, rho, nb + 1], vall[1, rho, nb + 1] = split_heads(vn_ref[rho, 0])
        return carry

    lax.fori_loop(0, dil, fill_halo, 0)

    ones_even = jnp.broadcast_to(jnp.where(lo, 1.0, 0.0).astype(BF16), (KW, LANES))
    ones_odd = jnp.broadcast_to(jnp.where(lo, 0.0, 1.0).astype(BF16), (KW, LANES))

    def body(it, carry):
        rho = it // nqb
        qb = it % nqb
        var = (jnp.where((qb == 0) & (is_first == 1), 1, 0)
               + jnp.where((qb == nqb - 1) & (is_last == 1), 2, 0))
        start = rho + qb * (QB * dil)
        rows = pl.ds(start, QB) if dil == 1 else pl.ds(start, QB, stride=dil)
        for s in range(n_slab):
            sl = slice(s * LANES, (s + 1) * LANES)
            qn = qall[rho, pl.ds(2 * qb, 2), :, sl].reshape(QB, LANES)
            es, ms = [], []
            for hh in range(2):
                kw = kall[hh, rho, pl.ds(2 * qb, 4), :, sl].reshape(KW, LANES)
                sc = lax.dot_general(qn, kw, (((1,), (1,)), ((), ())), preferred_element_type=F32)
                sc = sc + bvar[var, 2 * s + hh]
                m = jnp.max(sc, axis=-1, keepdims=True)
                es.append(jnp.exp2(sc - m).astype(BF16))
                ms.append(m)
            v_even = vall[0, rho, pl.ds(2 * qb, 4), :, sl].reshape(KW, LANES)
            v_odd = vall[1, rho, pl.ds(2 * qb, 4), :, sl].reshape(KW, LANES)
            rhs = jnp.concatenate([jnp.concatenate([v_even, ones_even], axis=1),
                                   jnp.concatenate([v_odd, ones_odd], axis=1)], axis=0)
            res = jnp.dot(jnp.concatenate(es, axis=1), rhs, preferred_element_type=F32)
            den = res[:, LANES:]
            oacc[s, rows, :] = res[:, :LANES] * (1.0 / den)
            lse_ref[s, rows, :] = (jnp.where(lo, ms[0], ms[1]) + jnp.log2(den)) * LN2
        return carry

    lax.fori_loop(0, dil * nqb, body, 0, unroll=8)

    for s in range(n_slab):
        o_ref[:, s * LANES:(s + 1) * LANES] = oacc[s].astype(BF16)


def _attention_group(z, bias_g, gq, gk, first, last, gi, dil):
    nb = SEG // dil // ATT_BLOCK
    z5 = z.reshape(N_SEG, dil, nb, ATT_BLOCK, N_IN)
    cq = (COL_ATT + 3 * gi * ATT_W) // ATT_W
    ck, cv = cq + 1, cq + 2
    main = lambda cb: pl.BlockSpec((None, dil, nb, ATT_BLOCK, ATT_W), lambda c, f, l: (c, 0, 0, 0, cb))
    nxt = lambda cb: pl.BlockSpec((None, dil, 1, ATT_BLOCK, ATT_W),
                                  lambda c, f, l: (c + 1 - l[c], 0, 0, 0, cb))
    grid_spec = pltpu.PrefetchScalarGridSpec(
        num_scalar_prefetch=2,
        grid=(N_SEG,),
        in_specs=[
            main(cq), main(ck), nxt(ck), main(cv), nxt(cv),
            pl.BlockSpec((ATT_HEADS, QB, KW), lambda c, f, l: (0, 0, 0)),
            pl.BlockSpec((1, LANES), lambda c, f, l: (0, 0)),
            pl.BlockSpec((1, LANES), lambda c, f, l: (0, 0)),
        ],
        out_specs=[
            pl.BlockSpec((SEG, ATT_W), lambda c, f, l: (c, 0)),
            pl.BlockSpec((ATT_W // LANES, SEG, LANES), lambda c, f, l: (0, c, 0)),
        ],
        scratch_shapes=[
            pltpu.VMEM((2, dil, nb + 2, ATT_BLOCK, ATT_W), BF16),
            pltpu.VMEM((2, dil, nb + 2, ATT_BLOCK, ATT_W), BF16),
            pltpu.VMEM((dil, nb, ATT_BLOCK, ATT_W), BF16),
            pltpu.VMEM((ATT_W // LANES, SEG, LANES), F32),
            pltpu.VMEM((4, ATT_HEADS, QB, KW), F32),
        ],
    )
    return pl.pallas_call(
        functools.partial(_attention_kernel, dil=dil),
        grid_spec=grid_spec,
        out_shape=[jax.ShapeDtypeStruct((T_ALL, ATT_W), BF16), jax.ShapeDtypeStruct((ATT_W // LANES, T_ALL, LANES), F32)],
        compiler_params=_cparams(("arbitrary",), ATT_VMEM_LIMIT),
        name=f"attention_d{dil}",
    )(first, last, z5, z5, z5, z5, z5, bias_g, gq, gk)


def _merge_kernel(yret_ref, o0_ref, o1_ref, o2_ref, l0_ref, l1_ref, l2_ref,
                  gret_a_ref, gret_b_ref, gatt_a_ref, gatt_b_ref, xp_ref, xs_ref,
                  wret_ref, watt_ref, wout_ref, nffn_ref, wr_ref, wrhi_ref, br_ref,
                  x1_ref, hpa_ref, hpb_ref, idx_ref, gate_ref, rank_ref, cnt_ref, carry_ref):
    i = pl.program_id(0)
    tm = MERGE_TM
    sub = MERGE_SUB

    @pl.when(i == 0)
    def _():
        carry_ref[...] = jnp.zeros_like(carry_ref)

    is_prompt = i < T_PROMPT // tm
    lane = lax.broadcasted_iota(jnp.int32, (sub, LANES), 1)
    lane_f = lane.astype(F32)
    tri = jnp.where(lax.broadcasted_iota(jnp.int32, (sub, sub), 1) < lax.broadcasted_iota(jnp.int32, (sub, sub), 0),
                    1.0, 0.0).astype(BF16)
    carry = carry_ref[...]

    for r0 in range(0, tm, sub):
        rows = slice(r0, r0 + sub)
        l0, l1, l2 = [jnp.concatenate([r[s, rows, :] for s in range(ATT_W // LANES)], axis=-1)
                      for r in (l0_ref, l1_ref, l2_ref)]
        lm = jnp.maximum(jnp.maximum(l0, l1), l2)
        e0, e1, e2 = jnp.exp(l0 - lm), jnp.exp(l1 - lm), jnp.exp(l2 - lm)
        inv = 1.0 / (e0 + e1 + e2)
        y_att = ((e0 * inv) * o0_ref[rows, :].astype(F32) + (e1 * inv) * o1_ref[rows, :].astype(F32)
                 + (e2 * inv) * o2_ref[rows, :].astype(F32))

        p_ret = jnp.dot(yret_ref[rows, :], wret_ref[...], preferred_element_type=F32)
        p_att = jnp.dot(y_att.astype(BF16), watt_ref[...], preferred_element_type=F32)
        g_ret = jnp.concatenate([gret_a_ref[rows, :], gret_b_ref[rows, :]], axis=-1).astype(F32)
        g_att = jnp.concatenate([gatt_a_ref[rows, :], gatt_b_ref[rows, :]], axis=-1).astype(F32)
        merged = _sigmoid(g_ret) * p_ret + _sigmoid(g_att) * p_att
        x_in = jnp.where(is_prompt, xp_ref[rows, :], xs_ref[rows, :])
        x1 = x_in + jnp.dot(merged.astype(BF16), wout_ref[...], preferred_element_type=F32)
        x1_ref[rows, :] = x1

        ms = jnp.mean(x1 * x1, axis=-1, keepdims=True)
        h2 = x1 * lax.rsqrt(ms + EPS) * nffn_ref[...]
        hpa_ref[rows, :], hpb_ref[rows, :] = _pack_row_halves(h2)

        h_hi = h2.astype(BF16)
        h_lo = (h2 - h_hi.astype(F32)).astype(BF16)
        p1 = jnp.dot(h_hi, wr_ref[...], preferred_element_type=F32)
        p2 = jnp.dot(h_lo, wrhi_ref[...], preferred_element_type=F32)
        work = p1 + pltpu.roll(p1, LANES - N_EXPERTS, axis=1) + p2 + br_ref[...]
        vals, idxs = [], []
        for _ in range(TOP_K):
            m = jnp.max(work, axis=-1, keepdims=True)
            ix = jnp.min(jnp.where(work == m, lane_f, float(LANES)), axis=-1, keepdims=True)
            vals.append(m)
            idxs.append(ix)
            work = jnp.where(lane_f == ix, -3e38, work)
        es = [jnp.exp(v - vals[0]) for v in vals]
        den = es[0] + es[1] + es[2] + es[3]
        onehot = jnp.zeros((sub, LANES), F32)
        for ix in idxs:
            onehot = onehot + jnp.where(lane_f == ix, 1.0, 0.0)
        before = jnp.dot(tri, onehot.astype(BF16), preferred_element_type=F32) + carry
        idx_out = jnp.zeros((sub, LANES), F32)
        gate_out = jnp.zeros((sub, LANES), F32)
        rank_out = jnp.zeros((sub, LANES), F32)
        for k in range(TOP_K):
            rk = jnp.sum(jnp.where(lane_f == idxs[k], before, 0.0), axis=-1, keepdims=True)
            sel = lane == k
            idx_out = jnp.where(sel, idxs[k], idx_out)
            gate_out = jnp.where(sel, es[k] / den, gate_out)
            rank_out = jnp.where(sel, rk, rank_out)
        idx_ref[rows, :] = idx_out.astype(jnp.int32)
        gate_ref[rows, :] = gate_out
        rank_ref[rows, :] = rank_out.astype(jnp.int32)
        carry = carry + jnp.sum(onehot, axis=0, keepdims=True)

    carry_ref[...] = carry
    cnt_ref[...] = jnp.broadcast_to(carry, cnt_ref.shape)


def _merge(y_ret, o_list, lse_list, z, x_p, x_s, w_ret, w_att, w_out, n_ffn, w_router, w_router_hi, b_router):
    tm = MERGE_TM
    n_p = T_PROMPT // tm
    row = lambda w: pl.BlockSpec((tm, w), lambda i: (i, 0))
    full = lambda a: pl.BlockSpec(a.shape, lambda i: (0,) * a.ndim)
    zcol = lambda col: pl.BlockSpec((tm, COL_BLK), lambda i: (i, col // COL_BLK))
    lse_spec = pl.BlockSpec((ATT_W // LANES, tm, LANES), lambda i: (0, i, 0))
    return pl.pallas_call(
        _merge_kernel,
        grid=(T_ALL // tm,),
        in_specs=[row(RET_V_W), row(ATT_W), row(ATT_W), row(ATT_W), lse_spec, lse_spec, lse_spec,
                  zcol(COL_GATE_RET), zcol(COL_GATE_RET + COL_BLK), zcol(COL_GATE_ATT),
                  zcol(COL_GATE_ATT + COL_BLK),
                  pl.BlockSpec((tm, D_MODEL), lambda i: (jnp.minimum(i, n_p - 1), 0)),
                  pl.BlockSpec((tm, D_MODEL), lambda i: (jnp.maximum(i - n_p, 0), 0)),
                  full(w_ret), full(w_att), full(w_out), full(n_ffn), full(w_router), full(w_router_hi),
                  full(b_router)],
        out_specs=[row(D_MODEL), row(HALF_W), row(HALF_W), row(LANES), row(LANES), row(LANES),
                   pl.BlockSpec((8, LANES), lambda i: (0, 0))],
        out_shape=[jax.ShapeDtypeStruct((T_ALL, D_MODEL), F32),
                   jax.ShapeDtypeStruct((T_ALL, HALF_W), jnp.uint32),
                   jax.ShapeDtypeStruct((T_ALL, HALF_W), jnp.uint32),
                   jax.ShapeDtypeStruct((T_ALL, LANES), jnp.int32),
                   jax.ShapeDtypeStruct((T_ALL, LANES), F32),
                   jax.ShapeDtypeStruct((T_ALL, LANES), jnp.int32),
                   jax.ShapeDtypeStruct((8, LANES), F32)],
        scratch_shapes=[pltpu.VMEM((1, LANES), F32)],
        compiler_params=_cparams(("arbitrary",)),
        name="merge_router",
    )(y_ret, *o_list, *lse_list, z, z, z, z, x_p, x_s, w_ret, w_att, w_out, n_ffn, w_router, w_router_hi, b_router)


def _sc_mesh():
    return plsc.VectorSubcoreMesh(core_axis_name="core", subcore_axis_name="subcore")


def _sc_scatter_rows(xs, idx_kmajor, n_out):
    n_rows, width = xs[0].shape
    out_type = [jax.ShapeDtypeStruct((n_out, width), x.dtype) for x in xs]

    @pl.kernel(out_type=out_type, mesh=_sc_mesh(), scratch_types=[])
    def scatter(*refs):
        x_refs, i_hbm, o_refs = refs[:len(xs)], refs[len(xs)], refs[len(xs) + 1:]
        for x_hbm, o_hbm in zip(x_refs, o_refs):
            def body(x_vmem, i_vmem, o_hbm=o_hbm):
                for k in range(TOP_K):
                    pltpu.sync_copy(x_vmem, o_hbm.at[i_vmem.at[k]])

            pltpu.emit_pipeline(
                body,
                grid=(n_rows // SC_WINDOW,),
                in_specs=[pl.BlockSpec((SC_WINDOW, width), lambda i: (i, 0)),
                          pl.BlockSpec((TOP_K, SC_WINDOW), lambda i: (0, i))],
                out_specs=[],
                core_axis_name=("core", "subcore"),
                dimension_semantics=(pltpu.PARALLEL,),
            )(x_hbm, i_hbm)

    return scatter(*xs, idx_kmajor)


def _sc_gather_rows(datas, idx):
    n_idx = idx.shape[0]
    width = datas[0].shape[1]
    out_type = [jax.ShapeDtypeStruct((n_idx, width), d.dtype) for d in datas]

    @pl.kernel(out_type=out_type, mesh=_sc_mesh(), scratch_types=[])
    def gather(*refs):
        x_refs, i_hbm, o_refs = refs[:len(datas)], refs[len(datas)], refs[len(datas) + 1:]
        for x_hbm, o_hbm in zip(x_refs, o_refs):
            def body(i_vmem, o_vmem, x_hbm=x_hbm):
                pltpu.sync_copy(x_hbm.at[i_vmem.at[0]], o_vmem)

            pltpu.emit_pipeline(
                body,
                grid=(n_idx // SC_WINDOW,),
                in_specs=[pl.BlockSpec((1, SC_WINDOW), lambda i: (0, i))],
                out_specs=[pl.BlockSpec((SC_WINDOW, width), lambda i: (i, 0))],
                core_axis_name=("core", "subcore"),
                dimension_semantics=(pltpu.PARALLEL,),
            )(i_hbm, o_hbm)

    return gather(*datas, idx.reshape(1, n_idx))


def _expert_kernel(be_ref, nused_ref, nvalid_ref, xa_ref, xb_ref, wg_ref, bg_ref, wu_ref, bu_ref, wd_ref, bd_ref,
                   ya_ref, yb_ref, wbf_ref):
    b = pl.program_id(0)
    active = b < nused_ref[0]
    new_expert = (b == 0) | (be_ref[b] != be_ref[jnp.maximum(b - 1, 0)])

    @pl.when(active & new_expert)
    def _():
        rows = 128
        for wi, w_ref in enumerate((wg_ref, wu_ref, wd_ref)):
            for r in range(0, D_MODEL, rows):
                wbf_ref[wi, r:r + rows, :] = w_ref[0, r:r + rows, :].astype(BF16)

    @pl.when(active)
    def _():
        valid = lax.broadcasted_iota(jnp.int32, (MOE_BM, HALF_W), 0) < nvalid_ref[b]
        zero = jnp.zeros((MOE_BM, HALF_W), jnp.uint32)
        x = _unpack_row_halves(jnp.where(valid, xa_ref[...], zero), jnp.where(valid, xb_ref[...], zero)).astype(BF16)
        g = jnp.dot(x, wbf_ref[0], preferred_element_type=F32) + bg_ref[0]
        u = jnp.dot(x, wbf_ref[1], preferred_element_type=F32) + bu_ref[0]
        g = jnp.minimum(g, SWIGLU_LIMIT)
        u = jnp.clip(u, -SWIGLU_LIMIT, SWIGLU_LIMIT)
        glu = g * jax.nn.sigmoid(SWIGLU_ALPHA * g)
        act = ((u + 1.0) * glu).astype(BF16)
        y = jnp.dot(act, wbf_ref[2], preferred_element_type=F32) + bd_ref[0]
        ya_ref[...], yb_ref[...] = _pack_row_halves(y)

    @pl.when(jnp.logical_not(active))
    def _():
        ya_ref[...] = jnp.zeros_like(ya_ref)
        yb_ref[...] = jnp.zeros_like(yb_ref)


def _experts(block_expert, n_used, n_valid, xs_a, xs_b, wg, bg, wu, bu, wd, bd):
    assert D_FF == D_MODEL
    blk = lambda b, be, nu, nv: (jnp.minimum(b, nu[0] - 1), 0)
    wsp = lambda: pl.BlockSpec((1, D_MODEL, D_FF), lambda b, be, nu, nv: (be[b], 0, 0))
    bsp = lambda: pl.BlockSpec((1, 1, D_FF), lambda b, be, nu, nv: (be[b], 0, 0))
    xsp = lambda: pl.BlockSpec((MOE_BM, HALF_W), blk)
    ysp = lambda: pl.BlockSpec((MOE_BM, HALF_W), lambda b, be, nu, nv: (b, 0))
    slot_arr = jax.ShapeDtypeStruct((N_SLOTS, HALF_W), jnp.uint32)
    grid_spec = pltpu.PrefetchScalarGridSpec(
        num_scalar_prefetch=3,
        grid=(N_SLOT_BLOCKS,),
        in_specs=[xsp(), xsp(), wsp(), bsp(), wsp(), bsp(), wsp(), bsp()],
        out_specs=[ysp(), ysp()],
        scratch_shapes=[pltpu.VMEM((3, D_MODEL, D_FF), BF16)],
    )
    return pl.pallas_call(
        _expert_kernel,
        grid_spec=grid_spec,
        out_shape=[slot_arr, slot_arr],
        compiler_params=_cparams(("arbitrary",)),
        name="experts",
    )(block_expert, n_used, n_valid, xs_a, xs_b, wg, bg, wu, bu, wd, bd)


def _final_kernel(x1_ref, yga_ref, ygb_ref, gate_ref, p_ref, nple_ref, wpg_ref, wpp_ref, out_ref):
    x2 = x1_ref[...]
    gates = gate_ref[...]
    for k in range(TOP_K):
        x2 = x2 + gates[:, k:k + 1] * _unpack_row_halves(yga_ref[k], ygb_ref[k])
    ms = jnp.mean(x2 * x2, axis=-1, keepdims=True)
    h3 = (x2 * lax.rsqrt(ms + EPS) * nple_ref[...]).astype(BF16)
    gate = jax.nn.sigmoid(jnp.dot(h3, wpg_ref[...], preferred_element_type=F32))
    proj = jnp.dot(p_ref[...].astype(BF16), wpp_ref[...], preferred_element_type=F32)
    out_ref[...] = x2 + gate * proj


def _final(x1, yg_a, yg_b, gates, p, n_ple, w_pg, w_pp, row0, n_rows):
    tm = FINAL_TM
    off = row0 // tm
    full = lambda a: pl.BlockSpec(a.shape, lambda i: (0,) * a.ndim)
    return pl.pallas_call(
        _final_kernel,
        grid=(n_rows // tm,),
        in_specs=[pl.BlockSpec((tm, D_MODEL), lambda i: (i + off, 0)),
                  pl.BlockSpec((TOP_K, tm, HALF_W), lambda i: (0, i + off, 0)),
                  pl.BlockSpec((TOP_K, tm, HALF_W), lambda i: (0, i + off, 0)),
                  pl.BlockSpec((tm, LANES), lambda i: (i + off, 0)),
                  pl.BlockSpec((tm, PLE_DIM), lambda i: (i, 0)),
                  full(n_ple), full(w_pg), full(w_pp)],
        out_specs=pl.BlockSpec((tm, D_MODEL), lambda i: (i, 0)),
        out_shape=jax.ShapeDtypeStruct((n_rows, D_MODEL), F32),
        compiler_params=_cparams(("arbitrary",)),
        name="final_ple",
    )(x1, yg_a, yg_b, gates, p, n_ple, w_pg, w_pp)


def _rope_tables():
    half = RET_DK // 2
    step = 128
    freq = ROPE_THETA ** (-jnp.arange(half, dtype=F32) / half)
    freq = jnp.concatenate([freq, freq])
    sign = jnp.where(jnp.arange(RET_DK) < half, -1.0, 1.0).astype(F32)
    ang_lo = jnp.arange(step, dtype=F32)[:, None] * freq[None, :]
    ang_hi = (jnp.arange(SAMPLE_SEQ // step, dtype=F32) * step)[:, None] * freq[None, :]
    c_lo, s_lo = jnp.cos(ang_lo)[None], jnp.sin(ang_lo)[None]
    c_hi, s_hi = jnp.cos(ang_hi)[:, None], jnp.sin(ang_hi)[:, None]
    cos = (c_hi * c_lo - s_hi * s_lo).reshape(SAMPLE_SEQ, RET_DK)
    sin = ((s_hi * c_lo + c_hi * s_lo) * sign).reshape(SAMPLE_SEQ, RET_DK)
    return cos, sin


def _retention_tables(decay_logit):
    lg = jax.nn.log_sigmoid(decay_logit.astype(F32))
    c = RET_CHUNK
    idx = jnp.arange(c, dtype=F32)
    diff = idx[:, None] - idx[None, :]
    lf = lg[0][:, None, None]
    lb = lg[1][:, None, None]
    mask = jnp.where(diff[None] >= 0, jnp.exp(lf * jnp.maximum(diff, 0.0)[None]),
                     jnp.exp(lb * jnp.maximum(-diff, 0.0)[None]))
    kdec_f = jnp.exp(lg[0][:, None] * (c - 1.0 - idx)[None, :])
    qdec_f = jnp.exp(lg[0][:, None] * (idx + 1.0)[None, :])
    kdec_b = jnp.exp(lg[1][:, None] * idx[None, :])
    qdec_b = jnp.exp(lg[1][:, None] * (c - idx)[None, :])
    dec = jnp.stack([kdec_f, qdec_f, kdec_b, qdec_b], axis=1)[..., None]
    cdec = jnp.exp(lg * c).T
    cdec = jnp.broadcast_to(cdec[:, :, None, None], (RET_HEADS, 2, 1, RET_DV))
    return mask, dec, cdec


def _t5_bucket(rel):
    half = T5_BUCKETS // 2
    exact = half // 2
    n = np.abs(rel)
    ratio = np.log(np.maximum(n, 1).astype(np.float32) / np.float32(exact)) / np.float32(math.log(T5_MAX_DIST / exact))
    large = exact + (ratio * np.float32(half - exact)).astype(np.int32)
    large = np.minimum(large, half - 1)
    return np.where(rel > 0, half, 0) + np.where(n < exact, n, large)


def _attention_bias(rel_bias, gi, dil, radius):
    qi = np.arange(QB)
    ki = np.arange(KW) - ATT_BLOCK
    rel = ki[None, :] - qi[:, None]
    onehot = jnp.asarray(_t5_bucket(rel * dil)[..., None] == np.arange(T5_BUCKETS), F32)
    tab = rel_bias[:, gi * ATT_HEADS:(gi + 1) * ATT_HEADS].astype(F32)
    bias = jnp.einsum('qkb,bh->hqk', onehot, tab, precision=lax.Precision.HIGHEST)
    return jnp.where(jnp.asarray(np.abs(rel) <= radius)[None], bias * LOG2E, NEG)


def _seq_edge_flags():
    first = np.zeros((N_SEG,), np.int32)
    last = np.zeros((N_SEG,), np.int32)
    first[:N_PROMPT_SEG] = 1
    last[:N_PROMPT_SEG] = 1
    first[N_PROMPT_SEG] = 1
    last[N_SEG - 1] = 1
    return jnp.asarray(first), jnp.asarray(last)


def _pad_lanes(a, value=0.0):
    return jnp.pad(a, ((0, 0), (0, LANES - a.shape[-1])), constant_values=value)


def kernel(x_prompt, x_sample, p_prompt, p_sample, norm_mix_g, w_in, ret_decay_logit, ret_gn_g,
           att_q_norm_g, att_k_norm_g, rel_bias, w_ret_proj, w_att_proj, w_out, norm_ffn_g,
           w_router, b_router, w_gate, b_gate, w_up, b_up, w_down, b_down,
           norm_ple_g, w_ple_gate, w_ple_proj):
    assert norm_mix_g.shape[0] == 1, "one layer"
    x_p = x_prompt.reshape(T_PROMPT, D_MODEL)
    x_s = x_sample.reshape(SAMPLE_SEQ, D_MODEL)

    cos_t, sin_t = _rope_tables()
    z = _in_proj(x_p, x_s, norm_mix_g.astype(F32), w_in[0].astype(BF16), cos_t, sin_t)

    ret_mask, ret_dec, ret_cdec = _retention_tables(ret_decay_logit[0])
    y_ret = _retention(z, ret_mask, ret_dec, ret_cdec, ret_gn_g[0].reshape(RET_HEADS, 1, RET_DV).astype(F32))

    first, last = _seq_edge_flags()
    o_list, lse_list = [], []
    for gi, (window, dil) in enumerate(ATT_GROUPS):
        bias_g = _attention_bias(rel_bias, gi, dil, window // (2 * dil))
        gq = jnp.tile(att_q_norm_g[0, gi].astype(F32) * (ATT_DH ** -0.5 * LOG2E), LANES // ATT_DH)[None, :]
        gk = jnp.tile(att_k_norm_g[0, gi].astype(F32), LANES // ATT_DH)[None, :]
        o_g, lse_g = _attention_group(z, bias_g, gq, gk, first, last, gi, dil)
        o_list.append(o_g)
        lse_list.append(lse_g)

    w_r = w_router[0].astype(F32)
    w_r_hi = w_r.astype(BF16)
    w_r_lo = (w_r - w_r_hi.astype(F32)).astype(BF16)
    w_router_cat = _pad_lanes(jnp.concatenate([w_r_hi, w_r_lo], axis=1))
    w_router_hi = _pad_lanes(w_r_hi)
    b_router_p = _pad_lanes(b_router.astype(F32), NEG)
    x1, hp_a, hp_b, idx, gates, rank, cnt = _merge(
        y_ret, o_list, lse_list, z, x_p, x_s, w_ret_proj[0].astype(BF16), w_att_proj[0].astype(BF16),
        w_out[0].astype(BF16), norm_ffn_g.astype(F32), w_router_cat, w_router_hi, b_router_p)

    counts = cnt[0, :N_EXPERTS].astype(jnp.int32)
    padded = (counts + MOE_BM - 1) // MOE_BM * MOE_BM
    pad_end = jnp.cumsum(padded)
    pad_start = pad_end - padded
    expert_ids = jnp.arange(N_EXPERTS, dtype=jnp.int32)
    top_idx = idx[:, :TOP_K]
    start_of = jnp.sum(jnp.where(top_idx[:, :, None] == expert_ids, pad_start, 0), axis=-1)
    dest_kmajor = (start_of + rank[:, :TOP_K]).T
    n_used = (pad_end[-1] // MOE_BM).astype(jnp.int32).reshape(1)
    blk_row0 = jnp.arange(N_SLOT_BLOCKS, dtype=jnp.int32) * MOE_BM
    block_expert = jnp.minimum(jnp.sum((pad_end[None, :] <= blk_row0[:, None]).astype(jnp.int32), axis=1),
                               N_EXPERTS - 1).astype(jnp.int32)
    slot_end = jnp.sum(jnp.where(block_expert[:, None] == expert_ids, pad_start + counts, 0), axis=-1)
    n_valid = jnp.clip(slot_end - blk_row0, 0, MOE_BM).astype(jnp.int32)

    xs_a, xs_b = _sc_scatter_rows((hp_a, hp_b), dest_kmajor, N_SLOTS)
    ys_a, ys_b = _experts(block_expert, n_used, n_valid, xs_a, xs_b,
                          w_gate[0], b_gate[0].reshape(N_EXPERTS, 1, D_FF).astype(F32),
                          w_up[0], b_up[0].reshape(N_EXPERTS, 1, D_FF).astype(F32),
                          w_down[0], b_down[0].reshape(N_EXPERTS, 1, D_MODEL).astype(F32))
    dest_flat = dest_kmajor.reshape(-1)
    yg_a, yg_b = [y.reshape(TOP_K, T_ALL, HALF_W) for y in _sc_gather_rows((ys_a, ys_b), dest_flat)]

    n_ple = norm_ple_g.astype(F32)
    w_pg = w_ple_gate[0].astype(BF16)
    w_pp = w_ple_proj[0].astype(BF16)
    y_p = _final(x1, yg_a, yg_b, gates, p_prompt[0].reshape(T_PROMPT, PLE_DIM), n_ple, w_pg, w_pp, 0, T_PROMPT)
    y_s = _final(x1, yg_a, yg_b, gates, p_sample[0].reshape(SAMPLE_SEQ, PLE_DIM), n_ple, w_pg, w_pp, T_PROMPT, SAMPLE_SEQ)
    return (y_p.reshape(x_prompt.shape), y_s.reshape(x_sample.shape))
```

```python
import functools
import math

import jax
import jax.numpy as jnp
import numpy as np
from jax import lax
from jax.experimental import pallas as pl
from jax.experimental.pallas import tpu as pltpu
from jax.experimental.pallas import tpu_sc as plsc

F32 = jnp.float32
BF16 = jnp.bfloat16

D_MODEL = 1024
N_PROMPT_SEQ = 8
PROMPT_SEQ = 2048
SAMPLE_SEQ = 16384
T_PROMPT = N_PROMPT_SEQ * PROMPT_SEQ
T_ALL = T_PROMPT + SAMPLE_SEQ

RET_HEADS = 4
RET_DK = 128
RET_DV = 256
RET_CHUNK = 128
ROPE_THETA = 10000.0
ATT_GROUPS = ((128, 1), (512, 4), (2048, 16))
N_GROUPS = 3
ATT_HEADS = 8
ATT_DH = 64
ATT_BLOCK = 64
ATT_W = ATT_HEADS * ATT_DH
T5_BUCKETS = 32
T5_MAX_DIST = 1024
N_EXPERTS = 32
TOP_K = 4
D_FF = 1024
SWIGLU_ALPHA = 1.702
SWIGLU_LIMIT = 7.0
PLE_DIM = 256
EPS = 1e-6

RET_QK_W = RET_HEADS * RET_DK
RET_V_W = RET_HEADS * RET_DV
N_IN = 2 * RET_QK_W + 2 * RET_V_W + 3 * N_GROUPS * ATT_W + 2 * D_MODEL

COL_RQ = 0
COL_RK = RET_QK_W
COL_RV = 2 * RET_QK_W
COL_RG = COL_RV + RET_V_W
COL_ATT = COL_RG + RET_V_W
COL_GATE_RET = COL_ATT + 3 * N_GROUPS * ATT_W
COL_GATE_ATT = COL_GATE_RET + D_MODEL

LANES = 128
VMEM_LIMIT = 56 * 1024 * 1024
IN_PROJ_VMEM_LIMIT = 60 * 1024 * 1024
ATT_VMEM_LIMIT = 58 * 1024 * 1024

SEG = 2048
N_SEG = T_ALL // SEG
N_PROMPT_SEG = T_PROMPT // SEG
COL_BLK = 512
N_COL_BLK = N_IN // COL_BLK
QB = 128
KW = 256
NEG = -1e30
LOG2E = math.log2(math.e)
LN2 = math.log(2.0)
MERGE_TM = 512
MERGE_SUB = MERGE_TM
FINAL_TM = 1024
MOE_BM = 1024
N_SLOT_BLOCKS = T_ALL * TOP_K // MOE_BM + N_EXPERTS
N_SLOTS = N_SLOT_BLOCKS * MOE_BM
HALF_W = D_MODEL // 4
SC_WINDOW = 128


def _cparams(sem, vmem=VMEM_LIMIT):
    return pltpu.CompilerParams(dimension_semantics=sem, vmem_limit_bytes=vmem)


def _sigmoid(x):
    return 0.5 * jnp.tanh(0.5 * x) + 0.5


def _pack_bf16_pair(x):
    w = x.shape[-1] // 2
    hi = pltpu.bitcast(x[:, :w].astype(BF16).astype(F32), jnp.uint32)
    lo = pltpu.bitcast(x[:, w:].astype(BF16).astype(F32), jnp.uint32)
    return hi | (lo >> 16)


def _unpack_bf16_pair(p):
    hi = pltpu.bitcast(p & jnp.uint32(0xFFFF0000), F32)
    lo = pltpu.bitcast(p << 16, F32)
    return jnp.concatenate([hi, lo], axis=-1)


def _pack_row_halves(x):
    half = x.shape[-1] // 2
    return _pack_bf16_pair(x[:, :half]), _pack_bf16_pair(x[:, half:])


def _unpack_row_halves(pa, pb):
    return jnp.concatenate([_unpack_bf16_pair(pa), _unpack_bf16_pair(pb)], axis=-1)


def _in_proj_kernel(xp_ref, xs_ref, g_ref, w_ref, cos_ref, sin_ref, z_ref, h_ref, p_ref, p2_ref):
    i = pl.program_id(0)
    j = pl.program_id(1)

    def norm_into_h(x_ref):
        xf = x_ref[...]
        ms = jnp.mean(xf * xf, axis=-1, keepdims=True)
        h_ref[...] = (xf * lax.rsqrt(ms + EPS) * g_ref[...]).astype(BF16)

    @pl.when((j == 0) & (i < N_PROMPT_SEG))
    def _():
        norm_into_h(xp_ref)

    @pl.when((j == 0) & (i >= N_PROMPT_SEG))
    def _():
        norm_into_h(xs_ref)

    n_slab = COL_BLK // LANES

    def project():
        return jnp.dot(h_ref[...], w_ref[...], preferred_element_type=F32)

    is_rope = j < (COL_RV // COL_BLK)
    att0 = COL_ATT // COL_BLK
    is_d4 = (j >= att0 + 3) & (j < att0 + 6)
    is_d16 = (j >= att0 + 6) & (j < att0 + 9)

    @pl.when(is_rope)
    def _():
        acc = project()
        scale = jnp.where(j == COL_RK // COL_BLK, RET_DK ** -0.5, 1.0).astype(F32)
        c = cos_ref[...]
        sn = sin_ref[...]
        for s in range(n_slab):
            xs = acc[:, s * LANES:(s + 1) * LANES]
            r = xs * c + pltpu.roll(xs, RET_DK // 2, axis=1) * sn
            z_ref[:, s * LANES:(s + 1) * LANES] = (r * scale).astype(BF16)

    @pl.when(is_d4)
    def _():
        acc = project()
        for s in range(n_slab):
            p_ref[s] = acc[:, s * LANES:(s + 1) * LANES]
        rows = SEG // 4
        for rho in range(4):
            for s in range(n_slab):
                piece = p_ref[s, pl.ds(rho, rows, stride=4), :]
                z_ref[rho * rows:(rho + 1) * rows, s * LANES:(s + 1) * LANES] = piece.astype(BF16)

    @pl.when(is_d16)
    def _():
        acc = project()
        for s in range(n_slab):
            p_ref[s] = acc[:, s * LANES:(s + 1) * LANES]
        quarter = SEG // 4
        rows = SEG // 16
        for r4 in range(4):
            for s in range(n_slab):
                p2_ref[s, r4 * quarter:(r4 + 1) * quarter, :] = p_ref[s, pl.ds(r4, quarter, stride=4), :]
        for r4 in range(4):
            for hi in range(4):
                rho = 4 * hi + r4
                for s in range(n_slab):
                    piece = p2_ref[s, pl.ds(r4 * quarter + hi, rows, stride=4), :]
                    z_ref[rho * rows:(rho + 1) * rows, s * LANES:(s + 1) * LANES] = piece.astype(BF16)

    @pl.when(jnp.logical_not(is_rope | is_d4 | is_d16))
    def _():
        z_ref[...] = project().astype(BF16)


def _in_proj(x_p, x_s, norm_g, w_in_bf, cos_t, sin_t):
    def pos_blk(i, j):
        return (jnp.maximum(i - N_PROMPT_SEG, 0), 0)

    return pl.pallas_call(
        _in_proj_kernel,
        grid=(N_SEG, N_COL_BLK),
        in_specs=[
            pl.BlockSpec((SEG, D_MODEL), lambda i, j: (jnp.minimum(i, N_PROMPT_SEG - 1), 0)),
            pl.BlockSpec((SEG, D_MODEL), pos_blk),
            pl.BlockSpec((1, D_MODEL), lambda i, j: (0, 0)),
            pl.BlockSpec((D_MODEL, COL_BLK), lambda i, j: (0, j)),
            pl.BlockSpec((SEG, LANES), pos_blk, pipeline_mode=pl.Buffered(1)),
            pl.BlockSpec((SEG, LANES), pos_blk, pipeline_mode=pl.Buffered(1)),
        ],
        out_specs=pl.BlockSpec((SEG, COL_BLK), lambda i, j: (i, j)),
        out_shape=jax.ShapeDtypeStruct((T_ALL, N_IN), BF16),
        scratch_shapes=[
            pltpu.VMEM((SEG, D_MODEL), BF16),
            pltpu.VMEM((COL_BLK // LANES, SEG, LANES), F32),
            pltpu.VMEM((COL_BLK // LANES, SEG, LANES), F32),
        ],
        compiler_params=_cparams(("arbitrary", "arbitrary"), IN_PROJ_VMEM_LIMIT),
        name="in_proj",
    )(x_p, x_s, norm_g, w_in_bf, cos_t, sin_t)


RET_CHUNKS_PER_SEG = SEG // RET_CHUNK
RET_MAX_CHUNKS = SAMPLE_SEQ // RET_CHUNK
RET_GROUP = 16


def _retention_kernel(seg_ref, phase_ref, reset_ref, cbase_ref,
                      q_ref, k_ref, v_ref, g_ref, mask_ref, dec_ref, cdec_ref, gn_ref,
                      y_ref, sb_ref, sf_ref, sr_ref):
    step = pl.program_id(1)
    phase = phase_ref[step]
    reset = reset_ref[step]
    cbase = cbase_ref[step]
    kdec_f = dec_ref[0, 0]
    qdec_f = dec_ref[0, 1]
    kdec_b = dec_ref[0, 2]
    qdec_b = dec_ref[0, 3]
    cd_f = cdec_ref[0, 0]
    cd_b = cdec_ref[0, 1]

    def kv_outer(kd, v):
        return lax.dot_general(kd, v, (((0,), (0,)), ((), ())), preferred_element_type=F32)

    @pl.when((phase == 0) & (reset == 1))
    def _():
        sr_ref[...] = jnp.zeros_like(sr_ref)

    @pl.when((phase == 1) & (reset == 1))
    def _():
        sf_ref[...] = jnp.zeros_like(sf_ref)

    n_groups = RET_CHUNKS_PER_SEG // RET_GROUP

    def chunk_rows(c):
        return pl.ds(pl.multiple_of(c * RET_CHUNK, RET_CHUNK), RET_CHUNK)

    @pl.when(phase == 0)
    def _():
        def body(it, carry):
            top = RET_CHUNKS_PER_SEG - 1 - it * RET_GROUP
            kvs = []
            for j in range(RET_GROUP):
                rows = chunk_rows(top - j)
                kd = (k_ref[rows, :].astype(F32) * kdec_b).astype(BF16)
                kvs.append(kv_outer(kd, v_ref[rows, :]))
            state = sr_ref[...]
            for j in range(RET_GROUP):
                sb_ref[cbase + top - j] = state.astype(BF16)
                state = cd_b * state + kvs[j]
            sr_ref[...] = state
            return carry

        lax.fori_loop(0, n_groups, body, 0)

    @pl.when(phase == 1)
    def _():
        msk = mask_ref[0]
        gn = gn_ref[0]

        def body(it, carry):
            c0 = it * RET_GROUP
            lhs, vs, kvs = [], [], []
            for j in range(RET_GROUP):
                rows = chunk_rows(c0 + j)
                qb = q_ref[rows, :]
                kb = k_ref[rows, :]
                v = v_ref[rows, :]
                q = qb.astype(F32)
                s = lax.dot_general(qb, kb, (((1,), (1,)), ((), ())), preferred_element_type=F32)
                lhs.append(jnp.concatenate(
                    [(s * msk).astype(BF16), (q * qdec_f).astype(BF16), (q * qdec_b).astype(BF16)], axis=-1))
                vs.append(v)
                kvs.append(kv_outer((kb.astype(F32) * kdec_f).astype(BF16), v))
            state = sf_ref[...]
            for j in range(RET_GROUP):
                c = c0 + j
                rhs = jnp.concatenate([vs[j], state.astype(BF16), sb_ref[cbase + c]], axis=0)
                o = jnp.dot(lhs[j], rhs, preferred_element_type=F32)
                state = cd_f * state + kvs[j]
                mu = jnp.mean(o, axis=-1, keepdims=True)
                oc = o - mu
                var = jnp.mean(oc * oc, axis=-1, keepdims=True)
                on = oc * lax.rsqrt(var + EPS) * gn
                rows = chunk_rows(c)
                gate = g_ref[rows, :].astype(F32)
                y_ref[rows, :] = (gate * jax.nn.sigmoid(gate) * on).astype(BF16)
            sf_ref[...] = state
            return carry

        lax.fori_loop(0, n_groups, body, 0)


def _retention_schedule():
    seg, phase, reset, cbase = [], [], [], []
    for p in range(N_PROMPT_SEG):
        for ph in (0, 1):
            seg.append(p); phase.append(ph); reset.append(1); cbase.append(0)
    n_s = N_SEG - N_PROMPT_SEG
    for i in range(n_s):
        t = n_s - 1 - i
        seg.append(N_PROMPT_SEG + t); phase.append(0); reset.append(int(i == 0)); cbase.append(t * RET_CHUNKS_PER_SEG)
    for t in range(n_s):
        seg.append(N_PROMPT_SEG + t); phase.append(1); reset.append(int(t == 0)); cbase.append(t * RET_CHUNKS_PER_SEG)
    hold = list(seg)
    for i in range(len(seg)):
        if phase[i] == 0:
            nxt = next(j for j in range(i + 1, len(seg)) if phase[j] == 1)
            hold[i] = seg[nxt]
    arr = lambda a: jnp.asarray(np.asarray(a, np.int32))
    return arr(seg), arr(phase), arr(reset), arr(cbase), arr(hold)


def _retention(z, ret_mask, ret_dec, ret_cdec, gn_g):
    seg, phase, reset, cbase, hold = _retention_schedule()
    n_steps = int(seg.shape[0])
    qk_blk = lambda col: (lambda h, s, seg_r, ph_r, rs_r, cb_r, hold_r: (seg_r[s], col // RET_DK + h))
    hold_blk = lambda col, w: (lambda h, s, seg_r, ph_r, rs_r, cb_r, hold_r: (hold_r[s], col // w + h))
    v_blk = lambda h, s, seg_r, ph_r, rs_r, cb_r, hold_r: (seg_r[s], COL_RV // RET_DV + h)
    per_head = lambda h, s, *_: (h, 0, 0)
    per_head4 = lambda h, s, *_: (h, 0, 0, 0)
    grid_spec = pltpu.PrefetchScalarGridSpec(
        num_scalar_prefetch=5,
        grid=(RET_HEADS, n_steps),
        in_specs=[
            pl.BlockSpec((SEG, RET_DK), hold_blk(COL_RQ, RET_DK)),
            pl.BlockSpec((SEG, RET_DK), qk_blk(COL_RK)),
            pl.BlockSpec((SEG, RET_DV), v_blk),
            pl.BlockSpec((SEG, RET_DV), hold_blk(COL_RG, RET_DV)),
            pl.BlockSpec((1, RET_CHUNK, RET_CHUNK), per_head),
            pl.BlockSpec((1, 4, RET_CHUNK, 1), per_head4),
            pl.BlockSpec((1, 2, 1, RET_DV), per_head4),
            pl.BlockSpec((1, 1, RET_DV), per_head),
        ],
        out_specs=pl.BlockSpec((SEG, RET_DV), lambda h, s, seg_r, ph_r, rs_r, cb_r, hold_r: (hold_r[s], h)),
        scratch_shapes=[
            pltpu.VMEM((RET_MAX_CHUNKS, RET_DK, RET_DV), BF16),
            pltpu.VMEM((RET_DK, RET_DV), F32),
            pltpu.VMEM((RET_DK, RET_DV), F32),
        ],
    )

    def kernel(seg_r, ph_r, rs_r, cb_r, hold_r, *refs):
        _retention_kernel(seg_r, ph_r, rs_r, cb_r, *refs)

    return pl.pallas_call(
        kernel,
        grid_spec=grid_spec,
        out_shape=jax.ShapeDtypeStruct((T_ALL, RET_V_W), BF16),
        compiler_params=_cparams(("arbitrary", "arbitrary")),
        name="retention",
    )(seg, phase, reset, cbase, hold, z, z, z, z, ret_mask, ret_dec, ret_cdec, gn_g)


def _attention_kernel(first_ref, last_ref, q_ref, km_ref, kn_ref, vm_ref, vn_ref,
                      bias_ref, gq_ref, gk_ref, o_ref, lse_ref, kall, vall, qall, oacc, bvar, *, dil):
    nb = SEG // dil // ATT_BLOCK
    nqb = nb // 2
    n_slab = ATT_W // LANES
    c = pl.program_id(0)
    is_first = first_ref[c]
    is_last = last_ref[c]
    lane = lax.broadcasted_iota(jnp.int32, (1, LANES), 1)
    lo = lane < ATT_DH
    gq = gq_ref[...]
    gk = gk_ref[...]

    @pl.when(c == 0)
    def _():
        col = lax.broadcasted_iota(jnp.int32, (1, KW), 1)
        left = jnp.where(col < ATT_BLOCK, NEG, 0.0).astype(F32)
        right = jnp.where(col >= KW - ATT_BLOCK, NEG, 0.0).astype(F32)
        for h in range(ATT_HEADS):
            b = bias_ref[h]
            bvar[0, h] = b
            bvar[1, h] = b + left
            bvar[2, h] = b + right
            bvar[3, h] = b + left + right

    def head_norm(x, g):
        x2 = x * x
        s_lo = jnp.sum(jnp.where(lo, x2, 0.0), axis=-1, keepdims=True)
        s_hi = jnp.sum(jnp.where(lo, 0.0, x2), axis=-1, keepdims=True)
        ms = jnp.where(lo, s_lo, s_hi) * (1.0 / ATT_DH)
        return x * lax.rsqrt(ms + EPS) * g

    def norm_block(src):
        even, odd = [], []
        for s in range(n_slab):
            xn = head_norm(src[:, s * LANES:(s + 1) * LANES].astype(F32), gk)
            even.append(jnp.where(lo, xn, 0.0).astype(BF16))
            odd.append(jnp.where(lo, 0.0, xn).astype(BF16))
        return jnp.concatenate(even, axis=-1), jnp.concatenate(odd, axis=-1)

    lo_wide = lax.broadcasted_iota(jnp.int32, (1, ATT_W), 1) % LANES < ATT_DH

    def split_heads(v):
        zero = jnp.zeros_like(v)
        return jnp.where(lo_wide, v, zero), jnp.where(lo_wide, zero, v)

    @pl.when(c == 0)
    def _():
        kall[:, :, 0] = jnp.zeros((2, dil, ATT_BLOCK, ATT_W), BF16)
        vall[:, :, 0] = jnp.zeros((2, dil, ATT_BLOCK, ATT_W), BF16)

    @pl.when(c > 0)
    def _():
        def carry_over(rho, carry):
            for hh in range(2):
                kall[hh, rho, 0] = kall[hh, rho, nb]
                vall[hh, rho, 0] = vall[hh, rho, nb]
            return carry

        lax.fori_loop(0, dil, carry_over, 0)

    def fill_main(it, carry):
        rho = it // nb
        blk = it % nb
        kall[0, rho, blk + 1], kall[1, rho, blk + 1] = norm_block(km_ref[rho, blk])
        vall[0, rho, blk + 1], vall[1, rho, blk + 1] = split_heads(vm_ref[rho, blk])
        qsrc = q_ref[rho, blk]
        qall[rho, blk] = jnp.concatenate(
            [head_norm(qsrc[:, s * LANES:(s + 1) * LANES].astype(F32), gq).astype(BF16) for s in range(n_slab)], axis=-1)
        return carry

    lax.fori_loop(0, dil * nb, fill_main, 0, unroll=4)

    def fill_halo(rho, carry):
        kall[0, rho, nb + 1], kall[1, rho, nb + 1] = norm_block(kn_ref[rho, 0])
        vall[0, rho, nb + 1], vall[1, rho, nb + 1] = split_heads(vn_ref[rho, 0])
        return carry

    lax.fori_loop(0, dil, fill_halo, 0)

    ones_even = jnp.broadcast_to(jnp.where(lo, 1.0, 0.0).astype(BF16), (KW, LANES))
    ones_odd = jnp.broadcast_to(jnp.where(lo, 0.0, 1.0).astype(BF16), (KW, LANES))

    def body(it, carry):
        rho = it // nqb
        qb = it % nqb
        var = (jnp.where((qb == 0) & (is_first == 1), 1, 0)
               + jnp.where((qb == nqb - 1) & (is_last == 1), 2, 0))
        start = rho + qb * (QB * dil)
        rows = pl.ds(start, QB) if dil == 1 else pl.ds(start, QB, stride=dil)
        for s in range(n_slab):
            sl = slice(s * LANES, (s + 1) * LANES)
            qn = qall[rho, pl.ds(2 * qb, 2), :, sl].reshape(QB, LANES)
            es, ms = [], []
            for hh in range(2):
                kw = kall[hh, rho, pl.ds(2 * qb, 4), :, sl].reshape(KW, LANES)
                sc = lax.dot_general(qn, kw, (((1,), (1,)), ((), ())), preferred_element_type=F32)
                sc = sc + bvar[var, 2 * s + hh]
                m = jnp.max(sc, axis=-1, keepdims=True)
                es.append(jnp.exp2(sc - m).astype(BF16))
                ms.append(m)
            v_even = vall[0, rho, pl.ds(2 * qb, 4), :, sl].reshape(KW, LANES)
            v_odd = vall[1, rho, pl.ds(2 * qb, 4), :, sl].reshape(KW, LANES)
            rhs = jnp.concatenate([jnp.concatenate([v_even, ones_even], axis=1),
                                   jnp.concatenate([v_odd, ones_odd], axis=1)], axis=0)
            res = jnp.dot(jnp.concatenate(es, axis=1), rhs, preferred_element_type=F32)
            den = res[:, LANES:]
            oacc[s, rows, :] = res[:, :LANES] * (1.0 / den)
            lse_ref[s, rows, :] = (jnp.where(lo, ms[0], ms[1]) + jnp.log2(den)) * LN2
        return carry

    lax.fori_loop(0, dil * nqb, body, 0, unroll=8)

    for s in range(n_slab):
        o_ref[:, s * LANES:(s + 1) * LANES] = oacc[s].astype(BF16)


def _attention_group(z, bias_g, gq, gk, first, last, gi, dil):
    nb = SEG // dil // ATT_BLOCK
    z5 = z.reshape(N_SEG, dil, nb, ATT_BLOCK, N_IN)
    cq = (COL_ATT + 3 * gi * ATT_W) // ATT_W
    ck, cv = cq + 1, cq + 2
    main = lambda cb: pl.BlockSpec((None, dil, nb, ATT_BLOCK, ATT_W), lambda c, f, l: (c, 0, 0, 0, cb))
    nxt = lambda cb: pl.BlockSpec((None, dil, 1, ATT_BLOCK, ATT_W),
                                  lambda c, f, l: (c + 1 - l[c], 0, 0, 0, cb))
    grid_spec = pltpu.PrefetchScalarGridSpec(
        num_scalar_prefetch=2,
        grid=(N_SEG,),
        in_specs=[
            main(cq), main(ck), nxt(ck), main(cv), nxt(cv),
            pl.BlockSpec((ATT_HEADS, QB, KW), lambda c, f, l: (0, 0, 0)),
            pl.BlockSpec((1, LANES), lambda c, f, l: (0, 0)),
            pl.BlockSpec((1, LANES), lambda c, f, l: (0, 0)),
        ],
        out_specs=[
            pl.BlockSpec((SEG, ATT_W), lambda c, f, l: (c, 0)),
            pl.BlockSpec((ATT_W // LANES, SEG, LANES), lambda c, f, l: (0, c, 0)),
        ],
        scratch_shapes=[
            pltpu.VMEM((2, dil, nb + 2, ATT_BLOCK, ATT_W), BF16),
            pltpu.VMEM((2, dil, nb + 2, ATT_BLOCK, ATT_W), BF16),
            pltpu.VMEM((dil, nb, ATT_BLOCK, ATT_W), BF16),
            pltpu.VMEM((ATT_W // LANES, SEG, LANES), F32),
            pltpu.VMEM((4, ATT_HEADS, QB, KW), F32),
        ],
    )
    return pl.pallas_call(
        functools.partial(_attention_kernel, dil=dil),
        grid_spec=grid_spec,
        out_shape=[jax.ShapeDtypeStruct((T_ALL, ATT_W), BF16), jax.ShapeDtypeStruct((ATT_W // LANES, T_ALL, LANES), F32)],
        compiler_params=_cparams(("arbitrary",), ATT_VMEM_LIMIT),
        name=f"attention_d{dil}",
    )(first, last, z5, z5, z5, z5, z5, bias_g, gq, gk)


def _merge_kernel(yret_ref, o0_ref, o1_ref, o2_ref, l0_ref, l1_ref, l2_ref,
                  gret_a_ref, gret_b_ref, gatt_a_ref, gatt_b_ref, xp_ref, xs_ref,
                  wret_ref, watt_ref, wout_ref, nffn_ref, wr_ref, wrhi_ref, br_ref,
                  x1_ref, hpa_ref, hpb_ref, idx_ref, gate_ref, rank_ref, cnt_ref, carry_ref):
    i = pl.program_id(0)
    tm = MERGE_TM
    sub = MERGE_SUB

    @pl.when(i == 0)
    def _():
        carry_ref[...] = jnp.zeros_like(carry_ref)

    is_prompt = i < T_PROMPT // tm
    lane = lax.broadcasted_iota(jnp.int32, (sub, LANES), 1)
    lane_f = lane.astype(F32)
    tri = jnp.where(lax.broadcasted_iota(jnp.int32, (sub, sub), 1) < lax.broadcasted_iota(jnp.int32, (sub, sub), 0),
                    1.0, 0.0).astype(BF16)
    carry = carry_ref[...]

    for r0 in range(0, tm, sub):
        rows = slice(r0, r0 + sub)
        l0, l1, l2 = [jnp.concatenate([r[s, rows, :] for s in range(ATT_W // LANES)], axis=-1)
                      for r in (l0_ref, l1_ref, l2_ref)]
        lm = jnp.maximum(jnp.maximum(l0, l1), l2)
        e0, e1, e2 = jnp.exp(l0 - lm), jnp.exp(l1 - lm), jnp.exp(l2 - lm)
        inv = 1.0 / (e0 + e1 + e2)
        y_att = ((e0 * inv) * o0_ref[rows, :].astype(F32) + (e1 * inv) * o1_ref[rows, :].astype(F32)
                 + (e2 * inv) * o2_ref[rows, :].astype(F32))

        p_ret = jnp.dot(yret_ref[rows, :], wret_ref[...], preferred_element_type=F32)
        p_att = jnp.dot(y_att.astype(BF16), watt_ref[...], preferred_element_type=F32)
        g_ret = jnp.concatenate([gret_a_ref[rows, :], gret_b_ref[rows, :]], axis=-1).astype(F32)
        g_att = jnp.concatenate([gatt_a_ref[rows, :], gatt_b_ref[rows, :]], axis=-1).astype(F32)
        merged = _sigmoid(g_ret) * p_ret + _sigmoid(g_att) * p_att
        x_in = jnp.where(is_prompt, xp_ref[rows, :], xs_ref[rows, :])
        x1 = x_in + jnp.dot(merged.astype(BF16), wout_ref[...], preferred_element_type=F32)
        x1_ref[rows, :] = x1

        ms = jnp.mean(x1 * x1, axis=-1, keepdims=True)
        h2 = x1 * lax.rsqrt(ms + EPS) * nffn_ref[...]
        hpa_ref[rows, :], hpb_ref[rows, :] = _pack_row_halves(h2)

        h_hi = h2.astype(BF16)
        h_lo = (h2 - h_hi.astype(F32)).astype(BF16)
        p1 = jnp.dot(h_hi, wr_ref[...], preferred_element_type=F32)
        p2 = jnp.dot(h_lo, wrhi_ref[...], preferred_element_type=F32)
        work = p1 + pltpu.roll(p1, LANES - N_EXPERTS, axis=1) + p2 + br_ref[...]
        vals, idxs = [], []
        for _ in range(TOP_K):
            m = jnp.max(work, axis=-1, keepdims=True)
            ix = jnp.min(jnp.where(work == m, lane_f, float(LANES)), axis=-1, keepdims=True)
            vals.append(m)
            idxs.append(ix)
            work = jnp.where(lane_f == ix, -3e38, work)
        es = [jnp.exp(v - vals[0]) for v in vals]
        den = es[0] + es[1] + es[2] + es[3]
        onehot = jnp.zeros((sub, LANES), F32)
        for ix in idxs:
            onehot = onehot + jnp.where(lane_f == ix, 1.0, 0.0)
        before = jnp.dot(tri, onehot.astype(BF16), preferred_element_type=F32) + carry
        idx_out = jnp.zeros((sub, LANES), F32)
        gate_out = jnp.zeros((sub, LANES), F32)
        rank_out = jnp.zeros((sub, LANES), F32)
        for k in range(TOP_K):
            rk = jnp.sum(jnp.where(lane_f == idxs[k], before, 0.0), axis=-1, keepdims=True)
            sel = lane == k
            idx_out = jnp.where(sel, idxs[k], idx_out)
            gate_out = jnp.where(sel, es[k] / den, gate_out)
            rank_out = jnp.where(sel, rk, rank_out)
        idx_ref[rows, :] = idx_out.astype(jnp.int32)
        gate_ref[rows, :] = gate_out
        rank_ref[rows, :] = rank_out.astype(jnp.int32)
        carry = carry + jnp.sum(onehot, axis=0, keepdims=True)

    carry_ref[...] = carry
    cnt_ref[...] = jnp.broadcast_to(carry, cnt_ref.shape)


def _merge(y_ret, o_list, lse_list, z, x_p, x_s, w_ret, w_att, w_out, n_ffn, w_router, w_router_hi, b_router):
    tm = MERGE_TM
    n_p = T_PROMPT // tm
    row = lambda w: pl.BlockSpec((tm, w), lambda i: (i, 0))
    full = lambda a: pl.BlockSpec(a.shape, lambda i: (0,) * a.ndim)
    zcol = lambda col: pl.BlockSpec((tm, COL_BLK), lambda i: (i, col // COL_BLK))
    lse_spec = pl.BlockSpec((ATT_W // LANES, tm, LANES), lambda i: (0, i, 0))
    return pl.pallas_call(
        _merge_kernel,
        grid=(T_ALL // tm,),
        in_specs=[row(RET_V_W), row(ATT_W), row(ATT_W), row(ATT_W), lse_spec, lse_spec, lse_spec,
                  zcol(COL_GATE_RET), zcol(COL_GATE_RET + COL_BLK), zcol(COL_GATE_ATT),
                  zcol(COL_GATE_ATT + COL_BLK),
                  pl.BlockSpec((tm, D_MODEL), lambda i: (jnp.minimum(i, n_p - 1), 0)),
                  pl.BlockSpec((tm, D_MODEL), lambda i: (jnp.maximum(i - n_p, 0), 0)),
                  full(w_ret), full(w_att), full(w_out), full(n_ffn), full(w_router), full(w_router_hi),
                  full(b_router)],
        out_specs=[row(D_MODEL), row(HALF_W), row(HALF_W), row(LANES), row(LANES), row(LANES),
                   pl.BlockSpec((8, LANES), lambda i: (0, 0))],
        out_shape=[jax.ShapeDtypeStruct((T_ALL, D_MODEL), F32),
                   jax.ShapeDtypeStruct((T_ALL, HALF_W), jnp.uint32),
                   jax.ShapeDtypeStruct((T_ALL, HALF_W), jnp.uint32),
                   jax.ShapeDtypeStruct((T_ALL, LANES), jnp.int32),
                   jax.ShapeDtypeStruct((T_ALL, LANES), F32),
                   jax.ShapeDtypeStruct((T_ALL, LANES), jnp.int32),
                   jax.ShapeDtypeStruct((8, LANES), F32)],
        scratch_shapes=[pltpu.VMEM((1, LANES), F32)],
        compiler_params=_cparams(("arbitrary",)),
        name="merge_router",
    )(y_ret, *o_list, *lse_list, z, z, z, z, x_p, x_s, w_ret, w_att, w_out, n_ffn, w_router, w_router_hi, b_router)


def _sc_mesh():
    return plsc.VectorSubcoreMesh(core_axis_name="core", subcore_axis_name="subcore")


def _sc_scatter_rows(xs, idx_kmajor, n_out):
    n_rows, width = xs[0].shape
    out_type = [jax.ShapeDtypeStruct((n_out, width), x.dtype) for x in xs]

    @pl.kernel(out_type=out_type, mesh=_sc_mesh(), scratch_types=[])
    def scatter(*refs):
        x_refs, i_hbm, o_refs = refs[:len(xs)], refs[len(xs)], refs[len(xs) + 1:]
        for x_hbm, o_hbm in zip(x_refs, o_refs):
            def body(x_vmem, i_vmem, o_hbm=o_hbm):
                for k in range(TOP_K):
                    pltpu.sync_copy(x_vmem, o_hbm.at[i_vmem.at[k]])

            pltpu.emit_pipeline(
                body,
                grid=(n_rows // SC_WINDOW,),
                in_specs=[pl.BlockSpec((SC_WINDOW, width), lambda i: (i, 0)),
                          pl.BlockSpec((TOP_K, SC_WINDOW), lambda i: (0, i))],
                out_specs=[],
                core_axis_name=("core", "subcore"),
                dimension_semantics=(pltpu.PARALLEL,),
            )(x_hbm, i_hbm)

    return scatter(*xs, idx_kmajor)


def _sc_gather_rows(datas, idx):
    n_idx = idx.shape[0]
    width = datas[0].shape[1]
    out_type = [jax.ShapeDtypeStruct((n_idx, width), d.dtype) for d in datas]

    @pl.kernel(out_type=out_type, mesh=_sc_mesh(), scratch_types=[])
    def gather(*refs):
        x_refs, i_hbm, o_refs = refs[:len(datas)], refs[len(datas)], refs[len(datas) + 1:]
        for x_hbm, o_hbm in zip(x_refs, o_refs):
            def body(i_vmem, o_vmem, x_hbm=x_hbm):
                pltpu.sync_copy(x_hbm.at[i_vmem.at[0]], o_vmem)

            pltpu.emit_pipeline(
                body,
                grid=(n_idx // SC_WINDOW,),
                in_specs=[pl.BlockSpec((1, SC_WINDOW), lambda i: (0, i))],
                out_specs=[pl.BlockSpec((SC_WINDOW, width), lambda i: (i, 0))],
                core_axis_name=("core", "subcore"),
                dimension_semantics=(pltpu.PARALLEL,),
            )(i_hbm, o_hbm)

    return gather(*datas, idx.reshape(1, n_idx))


def _expert_kernel(be_ref, nused_ref, nvalid_ref, xa_ref, xb_ref, wg_ref, bg_ref, wu_ref, bu_ref, wd_ref, bd_ref,
                   ya_ref, yb_ref, wbf_ref):
    b = pl.program_id(0)
    active = b < nused_ref[0]
    new_expert = (b == 0) | (be_ref[b] != be_ref[jnp.maximum(b - 1, 0)])

    @pl.when(active & new_expert)
    def _():
        rows = 128
        for wi, w_ref in enumerate((wg_ref, wu_ref, wd_ref)):
            for r in range(0, D_MODEL, rows):
                wbf_ref[wi, r:r + rows, :] = w_ref[0, r:r + rows, :].astype(BF16)

    @pl.when(active)
    def _():
        valid = lax.broadcasted_iota(jnp.int32, (MOE_BM, HALF_W), 0) < nvalid_ref[b]
        zero = jnp.zeros((MOE_BM, HALF_W), jnp.uint32)
        x = _unpack_row_halves(jnp.where(valid, xa_ref[...], zero), jnp.where(valid, xb_ref[...], zero)).astype(BF16)
        g = jnp.dot(x, wbf_ref[0], preferred_element_type=F32) + bg_ref[0]
        u = jnp.dot(x, wbf_ref[1], preferred_element_type=F32) + bu_ref[0]
        g = jnp.minimum(g, SWIGLU_LIMIT)
        u = jnp.clip(u, -SWIGLU_LIMIT, SWIGLU_LIMIT)
        glu = g * jax.nn.sigmoid(SWIGLU_ALPHA * g)
        act = ((u + 1.0) * glu).astype(BF16)
        y = jnp.dot(act, wbf_ref[2], preferred_element_type=F32) + bd_ref[0]
        ya_ref[...], yb_ref[...] = _pack_row_halves(y)

    @pl.when(jnp.logical_not(active))
    def _():
        ya_ref[...] = jnp.zeros_like(ya_ref)
        yb_ref[...] = jnp.zeros_like(yb_ref)


def _experts(block_expert, n_used, n_valid, xs_a, xs_b, wg, bg, wu, bu, wd, bd):
    assert D_FF == D_MODEL
    blk = lambda b, be, nu, nv: (jnp.minimum(b, nu[0] - 1), 0)
    wsp = lambda: pl.BlockSpec((1, D_MODEL, D_FF), lambda b, be, nu, nv: (be[b], 0, 0))
    bsp = lambda: pl.BlockSpec((1, 1, D_FF), lambda b, be, nu, nv: (be[b], 0, 0))
    xsp = lambda: pl.BlockSpec((MOE_BM, HALF_W), blk)
    ysp = lambda: pl.BlockSpec((MOE_BM, HALF_W), lambda b, be, nu, nv: (b, 0))
    slot_arr = jax.ShapeDtypeStruct((N_SLOTS, HALF_W), jnp.uint32)
    grid_spec = pltpu.PrefetchScalarGridSpec(
        num_scalar_prefetch=3,
        grid=(N_SLOT_BLOCKS,),
        in_specs=[xsp(), xsp(), wsp(), bsp(), wsp(), bsp(), wsp(), bsp()],
        out_specs=[ysp(), ysp()],
        scratch_shapes=[pltpu.VMEM((3, D_MODEL, D_FF), BF16)],
    )
    return pl.pallas_call(
        _expert_kernel,
        grid_spec=grid_spec,
        out_shape=[slot_arr, slot_arr],
        compiler_params=_cparams(("arbitrary",)),
        name="experts",
    )(block_expert, n_used, n_valid, xs_a, xs_b, wg, bg, wu, bu, wd, bd)


def _final_kernel(x1_ref, yga_ref, ygb_ref, gate_ref, p_ref, nple_ref, wpg_ref, wpp_ref, out_ref):
    x2 = x1_ref[...]
    gates = gate_ref[...]
    for k in range(TOP_K):
        x2 = x2 + gates[:, k:k + 1] * _unpack_row_halves(yga_ref[k], ygb_ref[k])
    ms = jnp.mean(x2 * x2, axis=-1, keepdims=True)
    h3 = (x2 * lax.rsqrt(ms + EPS) * nple_ref[...]).astype(BF16)
    gate = jax.nn.sigmoid(jnp.dot(h3, wpg_ref[...], preferred_element_type=F32))
    proj = jnp.dot(p_ref[...].astype(BF16), wpp_ref[...], preferred_element_type=F32)
    out_ref[...] = x2 + gate * proj


def _final(x1, yg_a, yg_b, gates, p, n_ple, w_pg, w_pp, row0, n_rows):
    tm = FINAL_TM
    off = row0 // tm
    full = lambda a: pl.BlockSpec(a.shape, lambda i: (0,) * a.ndim)
    return pl.pallas_call(
        _final_kernel,
        grid=(n_rows // tm,),
        in_specs=[pl.BlockSpec((tm, D_MODEL), lambda i: (i + off, 0)),
                  pl.BlockSpec((TOP_K, tm, HALF_W), lambda i: (0, i + off, 0)),
                  pl.BlockSpec((TOP_K, tm, HALF_W), lambda i: (0, i + off, 0)),
                  pl.BlockSpec((tm, LANES), lambda i: (i + off, 0)),
                  pl.BlockSpec((tm, PLE_DIM), lambda i: (i, 0)),
                  full(n_ple), full(w_pg), full(w_pp)],
        out_specs=pl.BlockSpec((tm, D_MODEL), lambda i: (i, 0)),
        out_shape=jax.ShapeDtypeStruct((n_rows, D_MODEL), F32),
        compiler_params=_cparams(("arbitrary",)),
        name="final_ple",
    )(x1, yg_a, yg_b, gates, p, n_ple, w_pg, w_pp)


def _rope_tables():
    half = RET_DK // 2
    step = 128
    freq = ROPE_THETA ** (-jnp.arange(half, dtype=F32) / half)
    freq = jnp.concatenate([freq, freq])
    sign = jnp.where(jnp.arange(RET_DK) < half, -1.0, 1.0).astype(F32)
    ang_lo = jnp.arange(step, dtype=F32)[:, None] * freq[None, :]
    ang_hi = (jnp.arange(SAMPLE_SEQ // step, dtype=F32) * step)[:, None] * freq[None, :]
    c_lo, s_lo = jnp.cos(ang_lo)[None], jnp.sin(ang_lo)[None]
    c_hi, s_hi = jnp.cos(ang_hi)[:, None], jnp.sin(ang_hi)[:, None]
    cos = (c_hi * c_lo - s_hi * s_lo).reshape(SAMPLE_SEQ, RET_DK)
    sin = ((s_hi * c_lo + c_hi * s_lo) * sign).reshape(SAMPLE_SEQ, RET_DK)
    return cos, sin


def _retention_tables(decay_logit):
    lg = jax.nn.log_sigmoid(decay_logit.astype(F32))
    c = RET_CHUNK
    idx = jnp.arange(c, dtype=F32)
    diff = idx[:, None] - idx[None, :]
    lf = lg[0][:, None, None]
    lb = lg[1][:, None, None]
    mask = jnp.where(diff[None] >= 0, jnp.exp(lf * jnp.maximum(diff, 0.0)[None]),
                     jnp.exp(lb * jnp.maximum(-diff, 0.0)[None]))
    kdec_f = jnp.exp(lg[0][:, None] * (c - 1.0 - idx)[None, :])
    qdec_f = jnp.exp(lg[0][:, None] * (idx + 1.0)[None, :])
    kdec_b = jnp.exp(lg[1][:, None] * idx[None, :])
    qdec_b = jnp.exp(lg[1][:, None] * (c - idx)[None, :])
    dec = jnp.stack([kdec_f, qdec_f, kdec_b, qdec_b], axis=1)[..., None]
    cdec = jnp.exp(lg * c).T
    cdec = jnp.broadcast_to(cdec[:, :, None, None], (RET_HEADS, 2, 1, RET_DV))
    return mask, dec, cdec


def _t5_bucket(rel):
    half = T5_BUCKETS // 2
    exact = half // 2
    n = np.abs(rel)
    ratio = np.log(np.maximum(n, 1).astype(np.float32) / np.float32(exact)) / np.float32(math.log(T5_MAX_DIST / exact))
    large = exact + (ratio * np.float32(half - exact)).astype(np.int32)
    large = np.minimum(large, half - 1)
    return np.where(rel > 0, half, 0) + np.where(n < exact, n, large)


def _attention_bias(rel_bias, gi, dil, radius):
    qi = np.arange(QB)
    ki = np.arange(KW) - ATT_BLOCK
    rel = ki[None, :] - qi[:, None]
    onehot = jnp.asarray(_t5_bucket(rel * dil)[..., None] == np.arange(T5_BUCKETS), F32)
    tab = rel_bias[:, gi * ATT_HEADS:(gi + 1) * ATT_HEADS].astype(F32)
    bias = jnp.einsum('qkb,bh->hqk', onehot, tab, precision=lax.Precision.HIGHEST)
    return jnp.where(jnp.asarray(np.abs(rel) <= radius)[None], bias * LOG2E, NEG)


def _seq_edge_flags():
    first = np.zeros((N_SEG,), np.int32)
    last = np.zeros((N_SEG,), np.int32)
    first[:N_PROMPT_SEG] = 1
    last[:N_PROMPT_SEG] = 1
    first[N_PROMPT_SEG] = 1
    last[N_SEG - 1] = 1
    return jnp.asarray(first), jnp.asarray(last)


def _pad_lanes(a, value=0.0):
    return jnp.pad(a, ((0, 0), (0, LANES - a.shape[-1])), constant_values=value)


def kernel(x_prompt, x_sample, p_prompt, p_sample, norm_mix_g, w_in, ret_decay_logit, ret_gn_g,
           att_q_norm_g, att_k_norm_g, rel_bias, w_ret_proj, w_att_proj, w_out, norm_ffn_g,
           w_router, b_router, w_gate, b_gate, w_up, b_up, w_down, b_down,
           norm_ple_g, w_ple_gate, w_ple_proj):
    assert norm_mix_g.shape[0] == 1, "one layer"
    x_p = x_prompt.reshape(T_PROMPT, D_MODEL)
    x_s = x_sample.reshape(SAMPLE_SEQ, D_MODEL)

    cos_t, sin_t = _rope_tables()
    z = _in_proj(x_p, x_s, norm_mix_g.astype(F32), w_in[0].astype(BF16), cos_t, sin_t)

    ret_mask, ret_dec, ret_cdec = _retention_tables(ret_decay_logit[0])
    y_ret = _retention(z, ret_mask, ret_dec, ret_cdec, ret_gn_g[0].reshape(RET_HEADS, 1, RET_DV).astype(F32))

    first, last = _seq_edge_flags()
    o_list, lse_list = [], []
    for gi, (window, dil) in enumerate(ATT_GROUPS):
        bias_g = _attention_bias(rel_bias, gi, dil, window // (2 * dil))
        gq = jnp.tile(att_q_norm_g[0, gi].astype(F32) * (ATT_DH ** -0.5 * LOG2E), LANES // ATT_DH)[None, :]
        gk = jnp.tile(att_k_norm_g[0, gi].astype(F32), LANES // ATT_DH)[None, :]
        o_g, lse_g = _attention_group(z, bias_g, gq, gk, first, last, gi, dil)
        o_list.append(o_g)
        lse_list.append(lse_g)

    w_r = w_router[0].astype(F32)
    w_r_hi = w_r.astype(BF16)
    w_r_lo = (w_r - w_r_hi.astype(F32)).astype(BF16)
    w_router_cat = _pad_lanes(jnp.concatenate([w_r_hi, w_r_lo], axis=1))
    w_router_hi = _pad_lanes(w_r_hi)
    b_router_p = _pad_lanes(b_router.astype(F32), NEG)
    x1, hp_a, hp_b, idx, gates, rank, cnt = _merge(
        y_ret, o_list, lse_list, z, x_p, x_s, w_ret_proj[0].astype(BF16), w_att_proj[0].astype(BF16),
        w_out[0].astype(BF16), norm_ffn_g.astype(F32), w_router_cat, w_router_hi, b_router_p)

    counts = cnt[0, :N_EXPERTS].astype(jnp.int32)
    padded = (counts + MOE_BM - 1) // MOE_BM * MOE_BM
    pad_end = jnp.cumsum(padded)
    pad_start = pad_end - padded
    expert_ids = jnp.arange(N_EXPERTS, dtype=jnp.int32)
    top_idx = idx[:, :TOP_K]
    start_of = jnp.sum(jnp.where(top_idx[:, :, None] == expert_ids, pad_start, 0), axis=-1)
    dest_kmajor = (start_of + rank[:, :TOP_K]).T
    n_used = (pad_end[-1] // MOE_BM).astype(jnp.int32).reshape(1)
    blk_row0 = jnp.arange(N_SLOT_BLOCKS, dtype=jnp.int32) * MOE_BM
    block_expert = jnp.minimum(jnp.sum((pad_end[None, :] <= blk_row0[:, None]).astype(jnp.int32), axis=1),
                               N_EXPERTS - 1).astype(jnp.int32)
    slot_end = jnp.sum(jnp.where(block_expert[:, None] == expert_ids, pad_start + counts, 0), axis=-1)
    n_valid = jnp.clip(slot_end - blk_row0, 0, MOE_BM).astype(jnp.int32)

    xs_a, xs_b = _sc_scatter_rows((hp_a, hp_b), dest_kmajor, N_SLOTS)
    ys_a, ys_b = _experts(block_expert, n_used, n_valid, xs_a, xs_b,
                          w_gate[0], b_gate[0].reshape(N_EXPERTS, 1, D_FF).astype(F32),
                          w_up[0], b_up[0].reshape(N_EXPERTS, 1, D_FF).astype(F32),
                          w_down[0], b_down[0].reshape(N_EXPERTS, 1, D_MODEL).astype(F32))
    dest_flat = dest_kmajor.reshape(-1)
    yg_a, yg_b = [y.reshape(TOP_K, T_ALL, HALF_W) for y in _sc_gather_rows((ys_a, ys_b), dest_flat)]

    n_ple = norm_ple_g.astype(F32)
    w_pg = w_ple_gate[0].astype(BF16)
    w_pp = w_ple_proj[0].astype(BF16)
    y_p = _final(x1, yg_a, yg_b, gates, p_prompt[0].reshape(T_PROMPT, PLE_DIM), n_ple, w_pg, w_pp, 0, T_PROMPT)
    y_s = _final(x1, yg_a, yg_b, gates, p_sample[0].reshape(SAMPLE_SEQ, PLE_DIM), n_ple, w_pg, w_pp, T_PROMPT, SAMPLE_SEQ)
    return (y_p.reshape(x_prompt.shape), y_s.reshape(x_sample.shape))
```

```python
import functools
import math

import jax
import jax.numpy as jnp
import numpy as np
from jax import lax
from jax.experimental import pallas as pl
from jax.experimental.pallas import tpu as pltpu
from jax.experimental.pallas import tpu_sc as plsc

F32 = jnp.float32
BF16 = jnp.bfloat16

D_MODEL = 1024
N_PROMPT_SEQ = 8
PROMPT_SEQ = 2048
SAMPLE_SEQ = 16384
T_PROMPT = N_PROMPT_SEQ * PROMPT_SEQ
T_ALL = T_PROMPT + SAMPLE_SEQ

RET_HEADS = 4
RET_DK = 128
RET_DV = 256
RET_CHUNK = 128
ROPE_THETA = 10000.0
ATT_GROUPS = ((128, 1), (512, 4), (2048, 16))
N_GROUPS = 3
ATT_HEADS = 8
ATT_DH = 64
ATT_BLOCK = 64
ATT_W = ATT_HEADS * ATT_DH
T5_BUCKETS = 32
T5_MAX_DIST = 1024
N_EXPERTS = 32
TOP_K = 4
D_FF = 1024
SWIGLU_ALPHA = 1.702
SWIGLU_LIMIT = 7.0
PLE_DIM = 256
EPS = 1e-6

RET_QK_W = RET_HEADS * RET_DK
RET_V_W = RET_HEADS * RET_DV
N_IN = 2 * RET_QK_W + 2 * RET_V_W + 3 * N_GROUPS * ATT_W + 2 * D_MODEL

COL_RQ = 0
COL_RK = RET_QK_W
COL_RV = 2 * RET_QK_W
COL_RG = COL_RV + RET_V_W
COL_ATT = COL_RG + RET_V_W
COL_GATE_RET = COL_ATT + 3 * N_GROUPS * ATT_W
COL_GATE_ATT = COL_GATE_RET + D_MODEL

LANES = 128
VMEM_LIMIT = 56 * 1024 * 1024
IN_PROJ_VMEM_LIMIT = 60 * 1024 * 1024
ATT_VMEM_LIMIT = 58 * 1024 * 1024

SEG = 2048
N_SEG = T_ALL // SEG
N_PROMPT_SEG = T_PROMPT // SEG
COL_BLK = 512
N_COL_BLK = N_IN // COL_BLK
QB = 128
KW = 256
NEG = -1e30
LOG2E = math.log2(math.e)
LN2 = math.log(2.0)
MERGE_TM = 512
MERGE_SUB = MERGE_TM
FINAL_TM = 1024
MOE_BM = 1024
N_SLOT_BLOCKS = T_ALL * TOP_K // MOE_BM + N_EXPERTS
N_SLOTS = N_SLOT_BLOCKS * MOE_BM
HALF_W = D_MODEL // 4
SC_WINDOW = 128


def _cparams(sem, vmem=VMEM_LIMIT):
    return pltpu.CompilerParams(dimension_semantics=sem, vmem_limit_bytes=vmem)


def _sigmoid(x):
    return 0.5 * jnp.tanh(0.5 * x) + 0.5


def _pack_bf16_pair(x):
    w = x.shape[-1] // 2
    hi = pltpu.bitcast(x[:, :w].astype(BF16).astype(F32), jnp.uint32)
    lo = pltpu.bitcast(x[:, w:].astype(BF16).astype(F32), jnp.uint32)
    return hi | (lo >> 16)


def _unpack_bf16_pair(p):
    hi = pltpu.bitcast(p & jnp.uint32(0xFFFF0000), F32)
    lo = pltpu.bitcast(p << 16, F32)
    return jnp.concatenate([hi, lo], axis=-1)


def _pack_row_halves(x):
    half = x.shape[-1] // 2
    return _pack_bf16_pair(x[:, :half]), _pack_bf16_pair(x[:, half:])


def _unpack_row_halves(pa, pb):
    return jnp.concatenate([_unpack_bf16_pair(pa), _unpack_bf16_pair(pb)], axis=-1)


def _in_proj_kernel(xp_ref, xs_ref, g_ref, w_ref, cos_ref, sin_ref, z_ref, h_ref, p_ref, p2_ref):
    i = pl.program_id(0)
    j = pl.program_id(1)

    def norm_into_h(x_ref):
        xf = x_ref[...]
        ms = jnp.mean(xf * xf, axis=-1, keepdims=True)
        h_ref[...] = (xf * lax.rsqrt(ms + EPS) * g_ref[...]).astype(BF16)

    @pl.when((j == 0) & (i < N_PROMPT_SEG))
    def _():
        norm_into_h(xp_ref)

    @pl.when((j == 0) & (i >= N_PROMPT_SEG))
    def _():
        norm_into_h(xs_ref)

    n_slab = COL_BLK // LANES

    def project():
        return jnp.dot(h_ref[...], w_ref[...], preferred_element_type=F32)

    is_rope = j < (COL_RV // COL_BLK)
    att0 = COL_ATT // COL_BLK
    is_d4 = (j >= att0 + 3) & (j < att0 + 6)
    is_d16 = (j >= att0 + 6) & (j < att0 + 9)

    @pl.when(is_rope)
    def _():
        acc = project()
        scale = jnp.where(j == COL_RK // COL_BLK, RET_DK ** -0.5, 1.0).astype(F32)
        c = cos_ref[...]
        sn = sin_ref[...]
        for s in range(n_slab):
            xs = acc[:, s * LANES:(s + 1) * LANES]
            r = xs * c + pltpu.roll(xs, RET_DK // 2, axis=1) * sn
            z_ref[:, s * LANES:(s + 1) * LANES] = (r * scale).astype(BF16)

    @pl.when(is_d4)
    def _():
        acc = project()
        for s in range(n_slab):
            p_ref[s] = acc[:, s * LANES:(s + 1) * LANES]
        rows = SEG // 4
        for rho in range(4):
            for s in range(n_slab):
                piece = p_ref[s, pl.ds(rho, rows, stride=4), :]
                z_ref[rho * rows:(rho + 1) * rows, s * LANES:(s + 1) * LANES] = piece.astype(BF16)

    @pl.when(is_d16)
    def _():
        acc = project()
        for s in range(n_slab):
            p_ref[s] = acc[:, s * LANES:(s + 1) * LANES]
        quarter = SEG // 4
        rows = SEG // 16
        for r4 in range(4):
            for s in range(n_slab):
                p2_ref[s, r4 * quarter:(r4 + 1) * quarter, :] = p_ref[s, pl.ds(r4, quarter, stride=4), :]
        for r4 in range(4):
            for hi in range(4):
                rho = 4 * hi + r4
                for s in range(n_slab):
                    piece = p2_ref[s, pl.ds(r4 * quarter + hi, rows, stride=4), :]
                    z_ref[rho * rows:(rho + 1) * rows, s * LANES:(s + 1) * LANES] = piece.astype(BF16)

    @pl.when(jnp.logical_not(is_rope | is_d4 | is_d16))
    def _():
        z_ref[...] = project().astype(BF16)


def _in_proj(x_p, x_s, norm_g, w_in_bf, cos_t, sin_t):
    def pos_blk(i, j):
        return (jnp.maximum(i - N_PROMPT_SEG, 0), 0)

    return pl.pallas_call(
        _in_proj_kernel,
        grid=(N_SEG, N_COL_BLK),
        in_specs=[
            pl.BlockSpec((SEG, D_MODEL), lambda i, j: (jnp.minimum(i, N_PROMPT_SEG - 1), 0)),
            pl.BlockSpec((SEG, D_MODEL), pos_blk),
            pl.BlockSpec((1, D_MODEL), lambda i, j: (0, 0)),
            pl.BlockSpec((D_MODEL, COL_BLK), lambda i, j: (0, j)),
            pl.BlockSpec((SEG, LANES), pos_blk, pipeline_mode=pl.Buffered(1)),
            pl.BlockSpec((SEG, LANES), pos_blk, pipeline_mode=pl.Buffered(1)),
        ],
        out_specs=pl.BlockSpec((SEG, COL_BLK), lambda i, j: (i, j)),
        out_shape=jax.ShapeDtypeStruct((T_ALL, N_IN), BF16),
        scratch_shapes=[
            pltpu.VMEM((SEG, D_MODEL), BF16),
            pltpu.VMEM((COL_BLK // LANES, SEG, LANES), F32),
            pltpu.VMEM((COL_BLK // LANES, SEG, LANES), F32),
        ],
        compiler_params=_cparams(("arbitrary", "arbitrary"), IN_PROJ_VMEM_LIMIT),
        name="in_proj",
    )(x_p, x_s, norm_g, w_in_bf, cos_t, sin_t)


RET_CHUNKS_PER_SEG = SEG // RET_CHUNK
RET_MAX_CHUNKS = SAMPLE_SEQ // RET_CHUNK
RET_GROUP = 16


def _retention_kernel(seg_ref, phase_ref, reset_ref, cbase_ref,
                      q_ref, k_ref, v_ref, g_ref, mask_ref, dec_ref, cdec_ref, gn_ref,
                      y_ref, sb_ref, sf_ref, sr_ref):
    step = pl.program_id(1)
    phase = phase_ref[step]
    reset = reset_ref[step]
    cbase = cbase_ref[step]
    kdec_f = dec_ref[0, 0]
    qdec_f = dec_ref[0, 1]
    kdec_b = dec_ref[0, 2]
    qdec_b = dec_ref[0, 3]
    cd_f = cdec_ref[0, 0]
    cd_b = cdec_ref[0, 1]

    def kv_outer(kd, v):
        return lax.dot_general(kd, v, (((0,), (0,)), ((), ())), preferred_element_type=F32)

    @pl.when((phase == 0) & (reset == 1))
    def _():
        sr_ref[...] = jnp.zeros_like(sr_ref)

    @pl.when((phase == 1) & (reset == 1))
    def _():
        sf_ref[...] = jnp.zeros_like(sf_ref)

    n_groups = RET_CHUNKS_PER_SEG // RET_GROUP

    def chunk_rows(c):
        return pl.ds(pl.multiple_of(c * RET_CHUNK, RET_CHUNK), RET_CHUNK)

    @pl.when(phase == 0)
    def _():
        def body(it, carry):
            top = RET_CHUNKS_PER_SEG - 1 - it * RET_GROUP
            kvs = []
            for j in range(RET_GROUP):
                rows = chunk_rows(top - j)
                kd = (k_ref[rows, :].astype(F32) * kdec_b).astype(BF16)
                kvs.append(kv_outer(kd, v_ref[rows, :]))
            state = sr_ref[...]
            for j in range(RET_GROUP):
                sb_ref[cbase + top - j] = state.astype(BF16)
                state = cd_b * state + kvs[j]
            sr_ref[...] = state
            return carry

        lax.fori_loop(0, n_groups, body, 0)

    @pl.when(phase == 1)
    def _():
        msk = mask_ref[0]
        gn = gn_ref[0]

        def body(it, carry):
            c0 = it * RET_GROUP
            lhs, vs, kvs = [], [], []
            for j in range(RET_GROUP):
                rows = chunk_rows(c0 + j)
                qb = q_ref[rows, :]
                kb = k_ref[rows, :]
                v = v_ref[rows, :]
                q = qb.astype(F32)
                s = lax.dot_general(qb, kb, (((1,), (1,)), ((), ())), preferred_element_type=F32)
                lhs.append(jnp.concatenate(
                    [(s * msk).astype(BF16), (q * qdec_f).astype(BF16), (q * qdec_b).astype(BF16)], axis=-1))
                vs.append(v)
                kvs.append(kv_outer((kb.astype(F32) * kdec_f).astype(BF16), v))
            state = sf_ref[...]
            for j in range(RET_GROUP):
                c = c0 + j
                rhs = jnp.concatenate([vs[j], state.astype(BF16), sb_ref[cbase + c]], axis=0)
                o = jnp.dot(lhs[j], rhs, preferred_element_type=F32)
                state = cd_f * state + kvs[j]
                mu = jnp.mean(o, axis=-1, keepdims=True)
                oc = o - mu
                var = jnp.mean(oc * oc, axis=-1, keepdims=True)
                on = oc * lax.rsqrt(var + EPS) * gn
                rows = chunk_rows(c)
                gate = g_ref[rows, :].astype(F32)
                y_ref[rows, :] = (gate * jax.nn.sigmoid(gate) * on).astype(BF16)
            sf_ref[...] = state
            return carry

        lax.fori_loop(0, n_groups, body, 0)


def _retention_schedule():
    seg, phase, reset, cbase = [], [], [], []
    for p in range(N_PROMPT_SEG):
        for ph in (0, 1):
            seg.append(p); phase.append(ph); reset.append(1); cbase.append(0)
    n_s = N_SEG - N_PROMPT_SEG
    for i in range(n_s):
        t = n_s - 1 - i
        seg.append(N_PROMPT_SEG + t); phase.append(0); reset.append(int(i == 0)); cbase.append(t * RET_CHUNKS_PER_SEG)
    for t in range(n_s):
        seg.append(N_PROMPT_SEG + t); phase.append(1); reset.append(int(t == 0)); cbase.append(t * RET_CHUNKS_PER_SEG)
    hold = list(seg)
    for i in range(len(seg)):
        if phase[i] == 0:
            nxt = next(j for j in range(i + 1, len(seg)) if phase[j] == 1)
            hold[i] = seg[nxt]
    arr = lambda a: jnp.asarray(np.asarray(a, np.int32))
    return arr(seg), arr(phase), arr(reset), arr(cbase), arr(hold)


def _retention(z, ret_mask, ret_dec, ret_cdec, gn_g):
    seg, phase, reset, cbase, hold = _retention_schedule()
    n_steps = int(seg.shape[0])
    qk_blk = lambda col: (lambda h, s, seg_r, ph_r, rs_r, cb_r, hold_r: (seg_r[s], col // RET_DK + h))
    hold_blk = lambda col, w: (lambda h, s, seg_r, ph_r, rs_r, cb_r, hold_r: (hold_r[s], col // w + h))
    v_blk = lambda h, s, seg_r, ph_r, rs_r, cb_r, hold_r: (seg_r[s], COL_RV // RET_DV + h)
    per_head = lambda h, s, *_: (h, 0, 0)
    per_head4 = lambda h, s, *_: (h, 0, 0, 0)
    grid_spec = pltpu.PrefetchScalarGridSpec(
        num_scalar_prefetch=5,
        grid=(RET_HEADS, n_steps),
        in_specs=[
            pl.BlockSpec((SEG, RET_DK), hold_blk(COL_RQ, RET_DK)),
            pl.BlockSpec((SEG, RET_DK), qk_blk(COL_RK)),
            pl.BlockSpec((SEG, RET_DV), v_blk),
            pl.BlockSpec((SEG, RET_DV), hold_blk(COL_RG, RET_DV)),
            pl.BlockSpec((1, RET_CHUNK, RET_CHUNK), per_head),
            pl.BlockSpec((1, 4, RET_CHUNK, 1), per_head4),
            pl.BlockSpec((1, 2, 1, RET_DV), per_head4),
            pl.BlockSpec((1, 1, RET_DV), per_head),
        ],
        out_specs=pl.BlockSpec((SEG, RET_DV), lambda h, s, seg_r, ph_r, rs_r, cb_r, hold_r: (hold_r[s], h)),
        scratch_shapes=[
            pltpu.VMEM((RET_MAX_CHUNKS, RET_DK, RET_DV), BF16),
            pltpu.VMEM((RET_DK, RET_DV), F32),
            pltpu.VMEM((RET_DK, RET_DV), F32),
        ],
    )

    def kernel(seg_r, ph_r, rs_r, cb_r, hold_r, *refs):
        _retention_kernel(seg_r, ph_r, rs_r, cb_r, *refs)

    return pl.pallas_call(
        kernel,
        grid_spec=grid_spec,
        out_shape=jax.ShapeDtypeStruct((T_ALL, RET_V_W), BF16),
        compiler_params=_cparams(("arbitrary", "arbitrary")),
        name="retention",
    )(seg, phase, reset, cbase, hold, z, z, z, z, ret_mask, ret_dec, ret_cdec, gn_g)


def _attention_kernel(first_ref, last_ref, q_ref, km_ref, kn_ref, vm_ref, vn_ref,
                      bias_ref, gq_ref, gk_ref, o_ref, lse_ref, kall, vall, qall, oacc, bvar, *, dil):
    nb = SEG // dil // ATT_BLOCK
    nqb = nb // 2
    n_slab = ATT_W // LANES
    c = pl.program_id(0)
    is_first = first_ref[c]
    is_last = last_ref[c]
    lane = lax.broadcasted_iota(jnp.int32, (1, LANES), 1)
    lo = lane < ATT_DH
    gq = gq_ref[...]
    gk = gk_ref[...]

    @pl.when(c == 0)
    def _():
        col = lax.broadcasted_iota(jnp.int32, (1, KW), 1)
        left = jnp.where(col < ATT_BLOCK, NEG, 0.0).astype(F32)
        right = jnp.where(col >= KW - ATT_BLOCK, NEG, 0.0).astype(F32)
        for h in range(ATT_HEADS):
            b = bias_ref[h]
            bvar[0, h] = b
            bvar[1, h] = b + left
            bvar[2, h] = b + right
            bvar[3, h] = b + left + right

    def head_norm(x, g):
        x2 = x * x
        s_lo = jnp.sum(jnp.where(lo, x2, 0.0), axis=-1, keepdims=True)
        s_hi = jnp.sum(jnp.where(lo, 0.0, x2), axis=-1, keepdims=True)
        ms = jnp.where(lo, s_lo, s_hi) * (1.0 / ATT_DH)
        return x * lax.rsqrt(ms + EPS) * g

    def norm_block(src):
        even, odd = [], []
        for s in range(n_slab):
            xn = head_norm(src[:, s * LANES:(s + 1) * LANES].astype(F32), gk)
            even.append(jnp.where(lo, xn, 0.0).astype(BF16))
            odd.append(jnp.where(lo, 0.0, xn).astype(BF16))
        return jnp.concatenate(even, axis=-1), jnp.concatenate(odd, axis=-1)

    lo_wide = lax.broadcasted_iota(jnp.int32, (1, ATT_W), 1) % LANES < ATT_DH

    def split_heads(v):
        zero = jnp.zeros_like(v)
        return jnp.where(lo_wide, v, zero), jnp.where(lo_wide, zero, v)

    @pl.when(c == 0)
    def _():
        kall[:, :, 0] = jnp.zeros((2, dil, ATT_BLOCK, ATT_W), BF16)
        vall[:, :, 0] = jnp.zeros((2, dil, ATT_BLOCK, ATT_W), BF16)

    @pl.when(c > 0)
    def _():
        def carry_over(rho, carry):
            for hh in range(2):
                kall[hh, rho, 0] = kall[hh, rho, nb]
                vall[hh, rho, 0] = vall[hh, rho, nb]
            return carry

        lax.fori_loop(0, dil, carry_over, 0)

    def fill_main(it, carry):
        rho = it // nb
        blk = it % nb
        kall[0, rho, blk + 1], kall[1, rho, blk + 1] = norm_block(km_ref[rho, blk])
        vall[0, rho, blk + 1], vall[1, rho, blk + 1] = split_heads(vm_ref[rho, blk])
        qsrc = q_ref[rho, blk]
        qall[rho, blk] = jnp.concatenate(
            [head_norm(qsrc[:, s * LANES:(s + 1) * LANES].astype(F32), gq).astype(BF16) for s in range(n_slab)], axis=-1)
        return carry

    lax.fori_loop(0, dil * nb, fill_main, 0, unroll=4)

    def fill_halo(rho, carry):
        kall[0, rho, nb + 1], kall[1, rho, nb + 1] = norm_block(kn_ref[rho, 0])
        vall[0, rho, nb + 1], vall[1, rho, nb + 1] = split_heads(vn_ref[rho, 0])
        return carry

    lax.fori_loop(0, dil, fill_halo, 0)

    ones_even = jnp.broadcast_to(jnp.where(lo, 1.0, 0.0).astype(BF16), (KW, LANES))
    ones_odd = jnp.broadcast_to(jnp.where(lo, 0.0, 1.0).astype(BF16), (KW, LANES))

    def body(it, carry):
        rho = it // nqb
        qb = it % nqb
        var = (jnp.where((qb == 0) & (is_first == 1), 1, 0)
               + jnp.where((qb == nqb - 1) & (is_last == 1), 2, 0))
        start = rho + qb * (QB * dil)
        rows = pl.ds(start, QB) if dil == 1 else pl.ds(start, QB, stride=dil)
        for s in range(n_slab):
            sl = slice(s * LANES, (s + 1) * LANES)
            qn = qall[rho, pl.ds(2 * qb, 2), :, sl].reshape(QB, LANES)
            es, ms = [], []
            for hh in range(2):
                kw = kall[hh, rho, pl.ds(2 * qb, 4), :, sl].reshape(KW, LANES)
                sc = lax.dot_general(qn, kw, (((1,), (1,)), ((), ())), preferred_element_type=F32)
                sc = sc + bvar[var, 2 * s + hh]
                m = jnp.max(sc, axis=-1, keepdims=True)
                es.append(jnp.exp2(sc - m).astype(BF16))
                ms.append(m)
            v_even = vall[0, rho, pl.ds(2 * qb, 4), :, sl].reshape(KW, LANES)
            v_odd = vall[1, rho, pl.ds(2 * qb, 4), :, sl].reshape(KW, LANES)
            rhs = jnp.concatenate([jnp.concatenate([v_even, ones_even], axis=1),
                                   jnp.concatenate([v_odd, ones_odd], axis=1)], axis=0)
            res = jnp.dot(jnp.concatenate(es, axis=1), rhs, preferred_element_type=F32)
            den = res[:, LANES:]
            oacc[s, rows, :] = res[:, :LANES] * (1.0 / den)
            lse_ref[s, rows, :] = (jnp.where(lo, ms[0], ms[1]) + jnp.log2(den)) * LN2
        return carry

    lax.fori_loop(0, dil * nqb, body, 0, unroll=8)

    for s in range(n_slab):
        o_ref[:, s * LANES:(s + 1) * LANES] = oacc[s].astype(BF16)


def _attention_group(z, bias_g, gq, gk, first, last, gi, dil):
    nb = SEG // dil // ATT_BLOCK
    z5 = z.reshape(N_SEG, dil, nb, ATT_BLOCK, N_IN)
    cq = (COL_ATT + 3 * gi * ATT_W) // ATT_W
    ck, cv = cq + 1, cq + 2
    main = lambda cb: pl.BlockSpec((None, dil, nb, ATT_BLOCK, ATT_W), lambda c, f, l: (c, 0, 0, 0, cb))
    nxt = lambda cb: pl.BlockSpec((None, dil, 1, ATT_BLOCK, ATT_W),
                                  lambda c, f, l: (c + 1 - l[c], 0, 0, 0, cb))
    grid_spec = pltpu.PrefetchScalarGridSpec(
        num_scalar_prefetch=2,
        grid=(N_SEG,),
        in_specs=[
            main(cq), main(ck), nxt(ck), main(cv), nxt(cv),
            pl.BlockSpec((ATT_HEADS, QB, KW), lambda c, f, l: (0, 0, 0)),
            pl.BlockSpec((1, LANES), lambda c, f, l: (0, 0)),
            pl.BlockSpec((1, LANES), lambda c, f, l: (0, 0)),
        ],
        out_specs=[
            pl.BlockSpec((SEG, ATT_W), lambda c, f, l: (c, 0)),
            pl.BlockSpec((ATT_W // LANES, SEG, LANES), lambda c, f, l: (0, c, 0)),
        ],
        scratch_shapes=[
            pltpu.VMEM((2, dil, nb + 2, ATT_BLOCK, ATT_W), BF16),
            pltpu.VMEM((2, dil, nb + 2, ATT_BLOCK, ATT_W), BF16),
            pltpu.VMEM((dil, nb, ATT_BLOCK, ATT_W), BF16),
            pltpu.VMEM((ATT_W // LANES, SEG, LANES), F32),
            pltpu.VMEM((4, ATT_HEADS, QB, KW), F32),
        ],
    )
    return pl.pallas_call(
        functools.partial(_attention_kernel, dil=dil),
        grid_spec=grid_spec,
        out_shape=[jax.ShapeDtypeStruct((T_ALL, ATT_W), BF16), jax.ShapeDtypeStruct((ATT_W // LANES, T_ALL, LANES), F32)],
        compiler_params=_cparams(("arbitrary",), ATT_VMEM_LIMIT),
        name=f"attention_d{dil}",
    )(first, last, z5, z5, z5, z5, z5, bias_g, gq, gk)


def _merge_kernel(yret_ref, o0_ref, o1_ref, o2_ref, l0_ref, l1_ref, l2_ref,
                  gret_a_ref, gret_b_ref, gatt_a_ref, gatt_b_ref, xp_ref, xs_ref,
                  wret_ref, watt_ref, wout_ref, nffn_ref, wr_ref, wrhi_ref, br_ref,
                  x1_ref, hpa_ref, hpb_ref, idx_ref, gate_ref, rank_ref, cnt_ref, carry_ref):
    i = pl.program_id(0)
    tm = MERGE_TM
    sub = MERGE_SUB

    @pl.when(i == 0)
    def _():
        carry_ref[...] = jnp.zeros_like(carry_ref)

    is_prompt = i < T_PROMPT // tm
    nt_dims = (((1,), (1,)), ((), ()))
    erow_f = lax.broadcasted_iota(jnp.int32, (N_EXPERTS, sub), 0).astype(F32)
    tok_r = lax.broadcasted_iota(jnp.int32, (sub, sub), 0)
    tok_c = lax.broadcasted_iota(jnp.int32, (sub, sub), 1)
    earlier = jnp.where(tok_r < tok_c, 1.0, 0.0).astype(BF16)
    same_tok = jnp.where(tok_r == tok_c, 1.0, 0.0).astype(BF16)
    carry = carry_ref[:, 0:1]

    for r0 in range(0, tm, sub):
        rows = slice(r0, r0 + sub)
        l0, l1, l2 = [jnp.concatenate([r[s, rows, :] for s in range(ATT_W // LANES)], axis=-1)
                      for r in (l0_ref, l1_ref, l2_ref)]
        lm = jnp.maximum(jnp.maximum(l0, l1), l2)
        e0, e1, e2 = jnp.exp(l0 - lm), jnp.exp(l1 - lm), jnp.exp(l2 - lm)
        inv = 1.0 / (e0 + e1 + e2)
        y_att = ((e0 * inv) * o0_ref[rows, :].astype(F32) + (e1 * inv) * o1_ref[rows, :].astype(F32)
                 + (e2 * inv) * o2_ref[rows, :].astype(F32))

        p_ret = jnp.dot(yret_ref[rows, :], wret_ref[...], preferred_element_type=F32)
        p_att = jnp.dot(y_att.astype(BF16), watt_ref[...], preferred_element_type=F32)
        g_ret = jnp.concatenate([gret_a_ref[rows, :], gret_b_ref[rows, :]], axis=-1).astype(F32)
        g_att = jnp.concatenate([gatt_a_ref[rows, :], gatt_b_ref[rows, :]], axis=-1).astype(F32)
        merged = _sigmoid(g_ret) * p_ret + _sigmoid(g_att) * p_att
        x_in = jnp.where(is_prompt, xp_ref[rows, :], xs_ref[rows, :])
        x1 = x_in + jnp.dot(merged.astype(BF16), wout_ref[...], preferred_element_type=F32)
        x1_ref[rows, :] = x1

        ms = jnp.mean(x1 * x1, axis=-1, keepdims=True)
        h2 = x1 * lax.rsqrt(ms + EPS) * nffn_ref[...]
        hpa_ref[rows, :], hpb_ref[rows, :] = _pack_row_halves(h2)

        h_hi = h2.astype(BF16)
        h_lo = (h2 - h_hi.astype(F32)).astype(BF16)
        p1 = lax.dot_general(wr_ref[...], h_hi, nt_dims, preferred_element_type=F32)
        p2 = lax.dot_general(wrhi_ref[...], h_lo, nt_dims, preferred_element_type=F32)
        work = p1[:N_EXPERTS] + p1[N_EXPERTS:2 * N_EXPERTS] + p2[:N_EXPERTS] + br_ref[...]
        vals, idxs = [], []
        for _ in range(TOP_K):
            m = jnp.max(work, axis=0, keepdims=True)
            ix = jnp.min(jnp.where(work == m, erow_f, float(N_EXPERTS)), axis=0, keepdims=True)
            vals.append(m)
            idxs.append(ix)
            work = jnp.where(erow_f == ix, -3e38, work)
        es = [jnp.exp(v - vals[0]) for v in vals]
        den = es[0] + es[1] + es[2] + es[3]
        onehot = jnp.zeros((N_EXPERTS, sub), F32)
        for ix in idxs:
            onehot = onehot + jnp.where(erow_f == ix, 1.0, 0.0)
        before = jnp.dot(onehot.astype(BF16), earlier, preferred_element_type=F32) + carry
        ranks = [jnp.sum(jnp.where(erow_f == ix, before, 0.0), axis=0, keepdims=True) for ix in idxs]
        pad_rows = jnp.zeros((8 - TOP_K, sub), F32)
        idx_ref[:, rows] = jnp.concatenate(idxs + [pad_rows], axis=0).astype(jnp.int32)
        rank_ref[:, rows] = jnp.concatenate(ranks + [pad_rows], axis=0).astype(jnp.int32)
        g_t = jnp.concatenate([e / den for e in es] + [jnp.zeros((LANES - TOP_K, sub), F32)], axis=0)
        g_hi = g_t.astype(BF16)
        g_lo = (g_t - g_hi.astype(F32)).astype(BF16)
        gate_ref[rows, :] = (lax.dot_general(same_tok, g_hi, nt_dims, preferred_element_type=F32)
                             + lax.dot_general(same_tok, g_lo, nt_dims, preferred_element_type=F32))
        carry = carry + jnp.sum(onehot, axis=1, keepdims=True)

    carry_ref[...] = jnp.broadcast_to(carry, carry_ref.shape)
    cnt_ref[...] = jnp.broadcast_to(carry, cnt_ref.shape)


def _merge(y_ret, o_list, lse_list, z, x_p, x_s, w_ret, w_att, w_out, n_ffn, w_router, w_router_hi, b_router):
    tm = MERGE_TM
    n_p = T_PROMPT // tm
    row = lambda w: pl.BlockSpec((tm, w), lambda i: (i, 0))
    full = lambda a: pl.BlockSpec(a.shape, lambda i: (0,) * a.ndim)
    zcol = lambda col: pl.BlockSpec((tm, COL_BLK), lambda i: (i, col // COL_BLK))
    lse_spec = pl.BlockSpec((ATT_W // LANES, tm, LANES), lambda i: (0, i, 0))
    kmajor = pl.BlockSpec((8, tm), lambda i: (0, i))
    return pl.pallas_call(
        _merge_kernel,
        grid=(T_ALL // tm,),
        in_specs=[row(RET_V_W), row(ATT_W), row(ATT_W), row(ATT_W), lse_spec, lse_spec, lse_spec,
                  zcol(COL_GATE_RET), zcol(COL_GATE_RET + COL_BLK), zcol(COL_GATE_ATT),
                  zcol(COL_GATE_ATT + COL_BLK),
                  pl.BlockSpec((tm, D_MODEL), lambda i: (jnp.minimum(i, n_p - 1), 0)),
                  pl.BlockSpec((tm, D_MODEL), lambda i: (jnp.maximum(i - n_p, 0), 0)),
                  full(w_ret), full(w_att), full(w_out), full(n_ffn), full(w_router), full(w_router_hi),
                  full(b_router)],
        out_specs=[row(D_MODEL), row(HALF_W), row(HALF_W), kmajor, row(LANES), kmajor,
                   pl.BlockSpec((N_EXPERTS, LANES), lambda i: (0, 0))],
        out_shape=[jax.ShapeDtypeStruct((T_ALL, D_MODEL), F32),
                   jax.ShapeDtypeStruct((T_ALL, HALF_W), jnp.uint32),
                   jax.ShapeDtypeStruct((T_ALL, HALF_W), jnp.uint32),
                   jax.ShapeDtypeStruct((8, T_ALL), jnp.int32),
                   jax.ShapeDtypeStruct((T_ALL, LANES), F32),
                   jax.ShapeDtypeStruct((8, T_ALL), jnp.int32),
                   jax.ShapeDtypeStruct((N_EXPERTS, LANES), F32)],
        scratch_shapes=[pltpu.VMEM((N_EXPERTS, LANES), F32)],
        compiler_params=_cparams(("arbitrary",)),
        name="merge_router",
    )(y_ret, *o_list, *lse_list, z, z, z, z, x_p, x_s, w_ret, w_att, w_out, n_ffn, w_router, w_router_hi, b_router)


def _sc_mesh():
    return plsc.VectorSubcoreMesh(core_axis_name="core", subcore_axis_name="subcore")


def _sc_scatter_rows(xs, idx_kmajor, n_out):
    n_rows, width = xs[0].shape
    out_type = [jax.ShapeDtypeStruct((n_out, width), x.dtype) for x in xs]

    @pl.kernel(out_type=out_type, mesh=_sc_mesh(), scratch_types=[])
    def scatter(*refs):
        x_refs, i_hbm, o_refs = refs[:len(xs)], refs[len(xs)], refs[len(xs) + 1:]
        for x_hbm, o_hbm in zip(x_refs, o_refs):
            def body(x_vmem, i_vmem, o_hbm=o_hbm):
                for k in range(TOP_K):
                    pltpu.sync_copy(x_vmem, o_hbm.at[i_vmem.at[k]])

            pltpu.emit_pipeline(
                body,
                grid=(n_rows // SC_WINDOW,),
                in_specs=[pl.BlockSpec((SC_WINDOW, width), lambda i: (i, 0)),
                          pl.BlockSpec((TOP_K, SC_WINDOW), lambda i: (0, i))],
                out_specs=[],
                core_axis_name=("core", "subcore"),
                dimension_semantics=(pltpu.PARALLEL,),
            )(x_hbm, i_hbm)

    return scatter(*xs, idx_kmajor)


def _sc_gather_rows(datas, idx):
    n_idx = idx.shape[0]
    width = datas[0].shape[1]
    out_type = [jax.ShapeDtypeStruct((n_idx, width), d.dtype) for d in datas]

    @pl.kernel(out_type=out_type, mesh=_sc_mesh(), scratch_types=[])
    def gather(*refs):
        x_refs, i_hbm, o_refs = refs[:len(datas)], refs[len(datas)], refs[len(datas) + 1:]
        for x_hbm, o_hbm in zip(x_refs, o_refs):
            def body(i_vmem, o_vmem, x_hbm=x_hbm):
                pltpu.sync_copy(x_hbm.at[i_vmem.at[0]], o_vmem)

            pltpu.emit_pipeline(
                body,
                grid=(n_idx // SC_WINDOW,),
                in_specs=[pl.BlockSpec((1, SC_WINDOW), lambda i: (0, i))],
                out_specs=[pl.BlockSpec((SC_WINDOW, width), lambda i: (i, 0))],
                core_axis_name=("core", "subcore"),
                dimension_semantics=(pltpu.PARALLEL,),
            )(i_hbm, o_hbm)

    return gather(*datas, idx.reshape(1, n_idx))


def _expert_kernel(be_ref, nused_ref, nvalid_ref, xa_ref, xb_ref, wg_ref, bg_ref, wu_ref, bu_ref, wd_ref, bd_ref,
                   ya_ref, yb_ref, wbf_ref):
    b = pl.program_id(0)
    active = b < nused_ref[0]
    new_expert = (b == 0) | (be_ref[b] != be_ref[jnp.maximum(b - 1, 0)])

    @pl.when(active & new_expert)
    def _():
        rows = 128
        for wi, w_ref in enumerate((wg_ref, wu_ref, wd_ref)):
            for r in range(0, D_MODEL, rows):
                wbf_ref[wi, r:r + rows, :] = w_ref[0, r:r + rows, :].astype(BF16)

    @pl.when(active)
    def _():
        valid = lax.broadcasted_iota(jnp.int32, (MOE_BM, HALF_W), 0) < nvalid_ref[b]
        zero = jnp.zeros((MOE_BM, HALF_W), jnp.uint32)
        x = _unpack_row_halves(jnp.where(valid, xa_ref[...], zero), jnp.where(valid, xb_ref[...], zero)).astype(BF16)
        g = jnp.dot(x, wbf_ref[0], preferred_element_type=F32) + bg_ref[0]
        u = jnp.dot(x, wbf_ref[1], preferred_element_type=F32) + bu_ref[0]
        g = jnp.minimum(g, SWIGLU_LIMIT)
        u = jnp.clip(u, -SWIGLU_LIMIT, SWIGLU_LIMIT)
        glu = g * jax.nn.sigmoid(SWIGLU_ALPHA * g)
        act = ((u + 1.0) * glu).astype(BF16)
        y = jnp.dot(act, wbf_ref[2], preferred_element_type=F32) + bd_ref[0]
        ya_ref[...], yb_ref[...] = _pack_row_halves(y)

    @pl.when(jnp.logical_not(active))
    def _():
        ya_ref[...] = jnp.zeros_like(ya_ref)
        yb_ref[...] = jnp.zeros_like(yb_ref)


def _experts(block_expert, n_used, n_valid, xs_a, xs_b, wg, bg, wu, bu, wd, bd):
    assert D_FF == D_MODEL
    blk = lambda b, be, nu, nv: (jnp.minimum(b, nu[0] - 1), 0)
    wsp = lambda: pl.BlockSpec((1, D_MODEL, D_FF), lambda b, be, nu, nv: (be[b], 0, 0))
    bsp = lambda: pl.BlockSpec((1, 1, D_FF), lambda b, be, nu, nv: (be[b], 0, 0))
    xsp = lambda: pl.BlockSpec((MOE_BM, HALF_W), blk)
    ysp = lambda: pl.BlockSpec((MOE_BM, HALF_W), lambda b, be, nu, nv: (b, 0))
    slot_arr = jax.ShapeDtypeStruct((N_SLOTS, HALF_W), jnp.uint32)
    grid_spec = pltpu.PrefetchScalarGridSpec(
        num_scalar_prefetch=3,
        grid=(N_SLOT_BLOCKS,),
        in_specs=[xsp(), xsp(), wsp(), bsp(), wsp(), bsp(), wsp(), bsp()],
        out_specs=[ysp(), ysp()],
        scratch_shapes=[pltpu.VMEM((3, D_MODEL, D_FF), BF16)],
    )
    return pl.pallas_call(
        _expert_kernel,
        grid_spec=grid_spec,
        out_shape=[slot_arr, slot_arr],
        compiler_params=_cparams(("arbitrary",)),
        name="experts",
    )(block_expert, n_used, n_valid, xs_a, xs_b, wg, bg, wu, bu, wd, bd)


def _final_kernel(x1_ref, yga_ref, ygb_ref, gate_ref, p_ref, nple_ref, wpg_ref, wpp_ref, out_ref):
    x2 = x1_ref[...]
    gates = gate_ref[...]
    for k in range(TOP_K):
        x2 = x2 + gates[:, k:k + 1] * _unpack_row_halves(yga_ref[k], ygb_ref[k])
    ms = jnp.mean(x2 * x2, axis=-1, keepdims=True)
    h3 = (x2 * lax.rsqrt(ms + EPS) * nple_ref[...]).astype(BF16)
    gate = jax.nn.sigmoid(jnp.dot(h3, wpg_ref[...], preferred_element_type=F32))
    proj = jnp.dot(p_ref[...].astype(BF16), wpp_ref[...], preferred_element_type=F32)
    out_ref[...] = x2 + gate * proj


def _final(x1, yg_a, yg_b, gates, p, n_ple, w_pg, w_pp, row0, n_rows):
    tm = FINAL_TM
    off = row0 // tm
    full = lambda a: pl.BlockSpec(a.shape, lambda i: (0,) * a.ndim)
    return pl.pallas_call(
        _final_kernel,
        grid=(n_rows // tm,),
        in_specs=[pl.BlockSpec((tm, D_MODEL), lambda i: (i + off, 0)),
                  pl.BlockSpec((TOP_K, tm, HALF_W), lambda i: (0, i + off, 0)),
                  pl.BlockSpec((TOP_K, tm, HALF_W), lambda i: (0, i + off, 0)),
                  pl.BlockSpec((tm, LANES), lambda i: (i + off, 0)),
                  pl.BlockSpec((tm, PLE_DIM), lambda i: (i, 0)),
                  full(n_ple), full(w_pg), full(w_pp)],
        out_specs=pl.BlockSpec((tm, D_MODEL), lambda i: (i, 0)),
        out_shape=jax.ShapeDtypeStruct((n_rows, D_MODEL), F32),
        compiler_params=_cparams(("arbitrary",)),
        name="final_ple",
    )(x1, yg_a, yg_b, gates, p, n_ple, w_pg, w_pp)


def _rope_tables():
    half = RET_DK // 2
    step = 128
    freq = ROPE_THETA ** (-jnp.arange(half, dtype=F32) / half)
    freq = jnp.concatenate([freq, freq])
    sign = jnp.where(jnp.arange(RET_DK) < half, -1.0, 1.0).astype(F32)
    ang_lo = jnp.arange(step, dtype=F32)[:, None] * freq[None, :]
    ang_hi = (jnp.arange(SAMPLE_SEQ // step, dtype=F32) * step)[:, None] * freq[None, :]
    c_lo, s_lo = jnp.cos(ang_lo)[None], jnp.sin(ang_lo)[None]
    c_hi, s_hi = jnp.cos(ang_hi)[:, None], jnp.sin(ang_hi)[:, None]
    cos = (c_hi * c_lo - s_hi * s_lo).reshape(SAMPLE_SEQ, RET_DK)
    sin = ((s_hi * c_lo + c_hi * s_lo) * sign).reshape(SAMPLE_SEQ, RET_DK)
    return cos, sin


def _retention_tables(decay_logit):
    lg = jax.nn.log_sigmoid(decay_logit.astype(F32))
    c = RET_CHUNK
    idx = jnp.arange(c, dtype=F32)
    diff = idx[:, None] - idx[None, :]
    lf = lg[0][:, None, None]
    lb = lg[1][:, None, None]
    mask = jnp.where(diff[None] >= 0, jnp.exp(lf * jnp.maximum(diff, 0.0)[None]),
                     jnp.exp(lb * jnp.maximum(-diff, 0.0)[None]))
    kdec_f = jnp.exp(lg[0][:, None] * (c - 1.0 - idx)[None, :])
    qdec_f = jnp.exp(lg[0][:, None] * (idx + 1.0)[None, :])
    kdec_b = jnp.exp(lg[1][:, None] * idx[None, :])
    qdec_b = jnp.exp(lg[1][:, None] * (c - idx)[None, :])
    dec = jnp.stack([kdec_f, qdec_f, kdec_b, qdec_b], axis=1)[..., None]
    cdec = jnp.exp(lg * c).T
    cdec = jnp.broadcast_to(cdec[:, :, None, None], (RET_HEADS, 2, 1, RET_DV))
    return mask, dec, cdec


def _t5_bucket(rel):
    half = T5_BUCKETS // 2
    exact = half // 2
    n = np.abs(rel)
    ratio = np.log(np.maximum(n, 1).astype(np.float32) / np.float32(exact)) / np.float32(math.log(T5_MAX_DIST / exact))
    large = exact + (ratio * np.float32(half - exact)).astype(np.int32)
    large = np.minimum(large, half - 1)
    return np.where(rel > 0, half, 0) + np.where(n < exact, n, large)


def _attention_bias(rel_bias, gi, dil, radius):
    qi = np.arange(QB)
    ki = np.arange(KW) - ATT_BLOCK
    rel = ki[None, :] - qi[:, None]
    onehot = jnp.asarray(_t5_bucket(rel * dil)[..., None] == np.arange(T5_BUCKETS), F32)
    tab = rel_bias[:, gi * ATT_HEADS:(gi + 1) * ATT_HEADS].astype(F32)
    bias = jnp.einsum('qkb,bh->hqk', onehot, tab, precision=lax.Precision.HIGHEST)
    return jnp.where(jnp.asarray(np.abs(rel) <= radius)[None], bias * LOG2E, NEG)


def _seq_edge_flags():
    first = np.zeros((N_SEG,), np.int32)
    last = np.zeros((N_SEG,), np.int32)
    first[:N_PROMPT_SEG] = 1
    last[:N_PROMPT_SEG] = 1
    first[N_PROMPT_SEG] = 1
    last[N_SEG - 1] = 1
    return jnp.asarray(first), jnp.asarray(last)


def _pad_lanes(a, value=0.0):
    return jnp.pad(a, ((0, 0), (0, LANES - a.shape[-1])), constant_values=value)


def kernel(x_prompt, x_sample, p_prompt, p_sample, norm_mix_g, w_in, ret_decay_logit, ret_gn_g,
           att_q_norm_g, att_k_norm_g, rel_bias, w_ret_proj, w_att_proj, w_out, norm_ffn_g,
           w_router, b_router, w_gate, b_gate, w_up, b_up, w_down, b_down,
           norm_ple_g, w_ple_gate, w_ple_proj):
    assert norm_mix_g.shape[0] == 1, "one layer"
    x_p = x_prompt.reshape(T_PROMPT, D_MODEL)
    x_s = x_sample.reshape(SAMPLE_SEQ, D_MODEL)

    cos_t, sin_t = _rope_tables()
    z = _in_proj(x_p, x_s, norm_mix_g.astype(F32), w_in[0].astype(BF16), cos_t, sin_t)

    ret_mask, ret_dec, ret_cdec = _retention_tables(ret_decay_logit[0])
    y_ret = _retention(z, ret_mask, ret_dec, ret_cdec, ret_gn_g[0].reshape(RET_HEADS, 1, RET_DV).astype(F32))

    first, last = _seq_edge_flags()
    o_list, lse_list = [], []
    for gi, (window, dil) in enumerate(ATT_GROUPS):
        bias_g = _attention_bias(rel_bias, gi, dil, window // (2 * dil))
        gq = jnp.tile(att_q_norm_g[0, gi].astype(F32) * (ATT_DH ** -0.5 * LOG2E), LANES // ATT_DH)[None, :]
        gk = jnp.tile(att_k_norm_g[0, gi].astype(F32), LANES // ATT_DH)[None, :]
        o_g, lse_g = _attention_group(z, bias_g, gq, gk, first, last, gi, dil)
        o_list.append(o_g)
        lse_list.append(lse_g)

    w_r = w_router[0].astype(F32)
    w_r_hi = w_r.astype(BF16)
    w_r_lo = (w_r - w_r_hi.astype(F32)).astype(BF16)
    w_router_cat = _pad_lanes(jnp.concatenate([w_r_hi, w_r_lo], axis=1)).T
    w_router_hi = _pad_lanes(w_r_hi).T
    b_router_p = b_router.astype(F32).reshape(N_EXPERTS, 1)
    x1, hp_a, hp_b, idx, gates, rank, cnt = _merge(
        y_ret, o_list, lse_list, z, x_p, x_s, w_ret_proj[0].astype(BF16), w_att_proj[0].astype(BF16),
        w_out[0].astype(BF16), norm_ffn_g.astype(F32), w_router_cat, w_router_hi, b_router_p)

    counts = cnt[:, 0].astype(jnp.int32)
    padded = (counts + MOE_BM - 1) // MOE_BM * MOE_BM
    pad_end = jnp.cumsum(padded)
    pad_start = pad_end - padded
    expert_ids = jnp.arange(N_EXPERTS, dtype=jnp.int32)
    top_idx = idx[:TOP_K]
    start_of = jnp.sum(jnp.where(top_idx[:, :, None] == expert_ids, pad_start, 0), axis=-1)
    dest_kmajor = start_of + rank[:TOP_K]
    n_used = (pad_end[-1] // MOE_BM).astype(jnp.int32).reshape(1)
    blk_row0 = jnp.arange(N_SLOT_BLOCKS, dtype=jnp.int32) * MOE_BM
    block_expert = jnp.minimum(jnp.sum((pad_end[None, :] <= blk_row0[:, None]).astype(jnp.int32), axis=1),
                               N_EXPERTS - 1).astype(jnp.int32)
    slot_end = jnp.sum(jnp.where(block_expert[:, None] == expert_ids, pad_start + counts, 0), axis=-1)
    n_valid = jnp.clip(slot_end - blk_row0, 0, MOE_BM).astype(jnp.int32)

    xs_a, xs_b = _sc_scatter_rows((hp_a, hp_b), dest_kmajor, N_SLOTS)
    ys_a, ys_b = _experts(block_expert, n_used, n_valid, xs_a, xs_b,
                          w_gate[0], b_gate[0].reshape(N_EXPERTS, 1, D_FF).astype(F32),
                          w_up[0], b_up[0].reshape(N_EXPERTS, 1, D_FF).astype(F32),
                          w_down[0], b_down[0].reshape(N_EXPERTS, 1, D_MODEL).astype(F32))
    dest_flat = dest_kmajor.reshape(-1)
    yg_a, yg_b = [y.reshape(TOP_K, T_ALL, HALF_W) for y in _sc_gather_rows((ys_a, ys_b), dest_flat)]

    n_ple = norm_ple_g.astype(F32)
    w_pg = w_ple_gate[0].astype(BF16)
    w_pp = w_ple_proj[0].astype(BF16)
    y_p = _final(x1, yg_a, yg_b, gates, p_prompt[0].reshape(T_PROMPT, PLE_DIM), n_ple, w_pg, w_pp, 0, T_PROMPT)
    y_s = _final(x1, yg_a, yg_b, gates, p_sample[0].reshape(SAMPLE_SEQ, PLE_DIM), n_ple, w_pg, w_pp, T_PROMPT, SAMPLE_SEQ)
    return (y_p.reshape(x_prompt.shape), y_s.reshape(x_sample.shape))
```

```python
import functools
import math

import jax
import jax.numpy as jnp
import numpy as np
from jax import lax
from jax.experimental import pallas as pl
from jax.experimental.pallas import tpu as pltpu
from jax.experimental.pallas import tpu_sc as plsc

F32 = jnp.float32
BF16 = jnp.bfloat16

D_MODEL = 1024
N_PROMPT_SEQ = 8
PROMPT_SEQ = 2048
SAMPLE_SEQ = 16384
T_PROMPT = N_PROMPT_SEQ * PROMPT_SEQ
T_ALL = T_PROMPT + SAMPLE_SEQ

RET_HEADS = 4
RET_DK = 128
RET_DV = 256
RET_CHUNK = 128
ROPE_THETA = 10000.0
ATT_GROUPS = ((128, 1), (512, 4), (2048, 16))
N_GROUPS = 3
ATT_HEADS = 8
ATT_DH = 64
ATT_BLOCK = 64
ATT_W = ATT_HEADS * ATT_DH
T5_BUCKETS = 32
T5_MAX_DIST = 1024
N_EXPERTS = 32
TOP_K = 4
D_FF = 1024
SWIGLU_ALPHA = 1.702
SWIGLU_LIMIT = 7.0
PLE_DIM = 256
EPS = 1e-6

RET_QK_W = RET_HEADS * RET_DK
RET_V_W = RET_HEADS * RET_DV
N_IN = 2 * RET_QK_W + 2 * RET_V_W + 3 * N_GROUPS * ATT_W + 2 * D_MODEL

COL_RQ = 0
COL_RK = RET_QK_W
COL_RV = 2 * RET_QK_W
COL_RG = COL_RV + RET_V_W
COL_ATT = COL_RG + RET_V_W
COL_GATE_RET = COL_ATT + 3 * N_GROUPS * ATT_W
COL_GATE_ATT = COL_GATE_RET + D_MODEL

LANES = 128
VMEM_LIMIT = 56 * 1024 * 1024
IN_PROJ_VMEM_LIMIT = 60 * 1024 * 1024
ATT_VMEM_LIMIT = 58 * 1024 * 1024

SEG = 2048
N_SEG = T_ALL // SEG
N_PROMPT_SEG = T_PROMPT // SEG
COL_BLK = 512
N_COL_BLK = N_IN // COL_BLK
QB = 128
KW = 256
NEG = -1e30
LOG2E = math.log2(math.e)
LN2 = math.log(2.0)
MERGE_TM = 512
MERGE_SUB = MERGE_TM
FINAL_TM = 1024
MOE_BM = 1024
N_SLOT_BLOCKS = T_ALL * TOP_K // MOE_BM + N_EXPERTS
N_SLOTS = N_SLOT_BLOCKS * MOE_BM
HALF_W = D_MODEL // 4
SC_WINDOW = 128


def _cparams(sem, vmem=VMEM_LIMIT):
    return pltpu.CompilerParams(dimension_semantics=sem, vmem_limit_bytes=vmem)


def _sigmoid(x):
    return 0.5 * jnp.tanh(0.5 * x) + 0.5


def _pack_bf16_pair(x):
    w = x.shape[-1] // 2
    hi = pltpu.bitcast(x[:, :w].astype(BF16).astype(F32), jnp.uint32)
    lo = pltpu.bitcast(x[:, w:].astype(BF16).astype(F32), jnp.uint32)
    return hi | (lo >> 16)


def _unpack_bf16_pair(p):
    hi = pltpu.bitcast(p & jnp.uint32(0xFFFF0000), F32)
    lo = pltpu.bitcast(p << 16, F32)
    return jnp.concatenate([hi, lo], axis=-1)


def _pack_row_halves(x):
    half = x.shape[-1] // 2
    return _pack_bf16_pair(x[:, :half]), _pack_bf16_pair(x[:, half:])


def _unpack_row_halves(pa, pb):
    return jnp.concatenate([_unpack_bf16_pair(pa), _unpack_bf16_pair(pb)], axis=-1)


def _in_proj_kernel(xp_ref, xs_ref, g_ref, w_ref, cos_ref, sin_ref, z_ref, h_ref, p_ref, p2_ref):
    i = pl.program_id(0)
    j = pl.program_id(1)

    def norm_into_h(x_ref):
        xf = x_ref[...]
        ms = jnp.mean(xf * xf, axis=-1, keepdims=True)
        h_ref[...] = (xf * lax.rsqrt(ms + EPS) * g_ref[...]).astype(BF16)

    @pl.when((j == 0) & (i < N_PROMPT_SEG))
    def _():
        norm_into_h(xp_ref)

    @pl.when((j == 0) & (i >= N_PROMPT_SEG))
    def _():
        norm_into_h(xs_ref)

    n_slab = COL_BLK // LANES

    def project():
        return jnp.dot(h_ref[...], w_ref[...], preferred_element_type=F32)

    is_rope = j < (COL_RV // COL_BLK)
    att0 = COL_ATT // COL_BLK
    is_d4 = (j >= att0 + 3) & (j < att0 + 6)
    is_d16 = (j >= att0 + 6) & (j < att0 + 9)

    @pl.when(is_rope)
    def _():
        acc = project()
        scale = jnp.where(j == COL_RK // COL_BLK, RET_DK ** -0.5, 1.0).astype(F32)
        c = cos_ref[...]
        sn = sin_ref[...]
        for s in range(n_slab):
            xs = acc[:, s * LANES:(s + 1) * LANES]
            r = xs * c + pltpu.roll(xs, RET_DK // 2, axis=1) * sn
            z_ref[:, s * LANES:(s + 1) * LANES] = (r * scale).astype(BF16)

    @pl.when(is_d4)
    def _():
        acc = project()
        for s in range(n_slab):
            p_ref[s] = acc[:, s * LANES:(s + 1) * LANES]
        rows = SEG // 4
        for rho in range(4):
            for s in range(n_slab):
                piece = p_ref[s, pl.ds(rho, rows, stride=4), :]
                z_ref[rho * rows:(rho + 1) * rows, s * LANES:(s + 1) * LANES] = piece.astype(BF16)

    @pl.when(is_d16)
    def _():
        acc = project()
        for s in range(n_slab):
            p_ref[s] = acc[:, s * LANES:(s + 1) * LANES]
        quarter = SEG // 4
        rows = SEG // 16
        for r4 in range(4):
            for s in range(n_slab):
                p2_ref[s, r4 * quarter:(r4 + 1) * quarter, :] = p_ref[s, pl.ds(r4, quarter, stride=4), :]
        for r4 in range(4):
            for hi in range(4):
                rho = 4 * hi + r4
                for s in range(n_slab):
                    piece = p2_ref[s, pl.ds(r4 * quarter + hi, rows, stride=4), :]
                    z_ref[rho * rows:(rho + 1) * rows, s * LANES:(s + 1) * LANES] = piece.astype(BF16)

    @pl.when(jnp.logical_not(is_rope | is_d4 | is_d16))
    def _():
        z_ref[...] = project().astype(BF16)


def _in_proj(x_p, x_s, norm_g, w_in_bf, cos_t, sin_t):
    def pos_blk(i, j):
        return (jnp.maximum(i - N_PROMPT_SEG, 0), 0)

    return pl.pallas_call(
        _in_proj_kernel,
        grid=(N_SEG, N_COL_BLK),
        in_specs=[
            pl.BlockSpec((SEG, D_MODEL), lambda i, j: (jnp.minimum(i, N_PROMPT_SEG - 1), 0)),
            pl.BlockSpec((SEG, D_MODEL), pos_blk),
            pl.BlockSpec((1, D_MODEL), lambda i, j: (0, 0)),
            pl.BlockSpec((D_MODEL, COL_BLK), lambda i, j: (0, j)),
            pl.BlockSpec((SEG, LANES), pos_blk, pipeline_mode=pl.Buffered(1)),
            pl.BlockSpec((SEG, LANES), pos_blk, pipeline_mode=pl.Buffered(1)),
        ],
        out_specs=pl.BlockSpec((SEG, COL_BLK), lambda i, j: (i, j)),
        out_shape=jax.ShapeDtypeStruct((T_ALL, N_IN), BF16),
        scratch_shapes=[
            pltpu.VMEM((SEG, D_MODEL), BF16),
            pltpu.VMEM((COL_BLK // LANES, SEG, LANES), F32),
            pltpu.VMEM((COL_BLK // LANES, SEG, LANES), F32),
        ],
        compiler_params=_cparams(("arbitrary", "arbitrary"), IN_PROJ_VMEM_LIMIT),
        name="in_proj",
    )(x_p, x_s, norm_g, w_in_bf, cos_t, sin_t)


RET_CHUNKS_PER_SEG = SEG // RET_CHUNK
RET_MAX_CHUNKS = SAMPLE_SEQ // RET_CHUNK
RET_GROUP = 16


def _retention_kernel(seg_ref, phase_ref, reset_ref, cbase_ref,
                      q_ref, k_ref, v_ref, g_ref, mask_ref, dec_ref, cdec_ref, gn_ref,
                      y_ref, sb_ref, sf_ref, sr_ref):
    step = pl.program_id(1)
    phase = phase_ref[step]
    reset = reset_ref[step]
    cbase = cbase_ref[step]
    kdec_f = dec_ref[0, 0]
    qdec_f = dec_ref[0, 1]
    kdec_b = dec_ref[0, 2]
    qdec_b = dec_ref[0, 3]
    cd_f = cdec_ref[0, 0]
    cd_b = cdec_ref[0, 1]

    def kv_outer(kd, v):
        return lax.dot_general(kd, v, (((0,), (0,)), ((), ())), preferred_element_type=F32)

    @pl.when((phase == 0) & (reset == 1))
    def _():
        sr_ref[...] = jnp.zeros_like(sr_ref)

    @pl.when((phase == 1) & (reset == 1))
    def _():
        sf_ref[...] = jnp.zeros_like(sf_ref)

    n_groups = RET_CHUNKS_PER_SEG // RET_GROUP

    def chunk_rows(c):
        return pl.ds(pl.multiple_of(c * RET_CHUNK, RET_CHUNK), RET_CHUNK)

    @pl.when(phase == 0)
    def _():
        def body(it, carry):
            top = RET_CHUNKS_PER_SEG - 1 - it * RET_GROUP
            kvs = []
            for j in range(RET_GROUP):
                rows = chunk_rows(top - j)
                kd = (k_ref[rows, :].astype(F32) * kdec_b).astype(BF16)
                kvs.append(kv_outer(kd, v_ref[rows, :]))
            state = sr_ref[...]
            for j in range(RET_GROUP):
                sb_ref[cbase + top - j] = state.astype(BF16)
                state = cd_b * state + kvs[j]
            sr_ref[...] = state
            return carry

        lax.fori_loop(0, n_groups, body, 0)

    @pl.when(phase == 1)
    def _():
        msk = mask_ref[0]
        gn = gn_ref[0]

        def body(it, carry):
            c0 = it * RET_GROUP
            lhs, vs, kvs = [], [], []
            for j in range(RET_GROUP):
                rows = chunk_rows(c0 + j)
                qb = q_ref[rows, :]
                kb = k_ref[rows, :]
                v = v_ref[rows, :]
                q = qb.astype(F32)
                s = lax.dot_general(qb, kb, (((1,), (1,)), ((), ())), preferred_element_type=F32)
                lhs.append(jnp.concatenate(
                    [(s * msk).astype(BF16), (q * qdec_f).astype(BF16), (q * qdec_b).astype(BF16)], axis=-1))
                vs.append(v)
                kvs.append(kv_outer((kb.astype(F32) * kdec_f).astype(BF16), v))
            state = sf_ref[...]
            for j in range(RET_GROUP):
                c = c0 + j
                rhs = jnp.concatenate([vs[j], state.astype(BF16), sb_ref[cbase + c]], axis=0)
                o = jnp.dot(lhs[j], rhs, preferred_element_type=F32)
                state = cd_f * state + kvs[j]
                mu = jnp.mean(o, axis=-1, keepdims=True)
                oc = o - mu
                var = jnp.mean(oc * oc, axis=-1, keepdims=True)
                on = oc * lax.rsqrt(var + EPS) * gn
                rows = chunk_rows(c)
                gate = g_ref[rows, :].astype(F32)
                y_ref[rows, :] = (gate * jax.nn.sigmoid(gate) * on).astype(BF16)
            sf_ref[...] = state
            return carry

        lax.fori_loop(0, n_groups, body, 0)


def _retention_schedule():
    seg, phase, reset, cbase = [], [], [], []
    for p in range(N_PROMPT_SEG):
        for ph in (0, 1):
            seg.append(p); phase.append(ph); reset.append(1); cbase.append(0)
    n_s = N_SEG - N_PROMPT_SEG
    for i in range(n_s):
        t = n_s - 1 - i
        seg.append(N_PROMPT_SEG + t); phase.append(0); reset.append(int(i == 0)); cbase.append(t * RET_CHUNKS_PER_SEG)
    for t in range(n_s):
        seg.append(N_PROMPT_SEG + t); phase.append(1); reset.append(int(t == 0)); cbase.append(t * RET_CHUNKS_PER_SEG)
    hold = list(seg)
    for i in range(len(seg)):
        if phase[i] == 0:
            nxt = next(j for j in range(i + 1, len(seg)) if phase[j] == 1)
            hold[i] = seg[nxt]
    arr = lambda a: jnp.asarray(np.asarray(a, np.int32))
    return arr(seg), arr(phase), arr(reset), arr(cbase), arr(hold)


def _retention(z, ret_mask, ret_dec, ret_cdec, gn_g):
    seg, phase, reset, cbase, hold = _retention_schedule()
    n_steps = int(seg.shape[0])
    qk_blk = lambda col: (lambda h, s, seg_r, ph_r, rs_r, cb_r, hold_r: (seg_r[s], col // RET_DK + h))
    hold_blk = lambda col, w: (lambda h, s, seg_r, ph_r, rs_r, cb_r, hold_r: (hold_r[s], col // w + h))
    v_blk = lambda h, s, seg_r, ph_r, rs_r, cb_r, hold_r: (seg_r[s], COL_RV // RET_DV + h)
    per_head = lambda h, s, *_: (h, 0, 0)
    per_head4 = lambda h, s, *_: (h, 0, 0, 0)
    grid_spec = pltpu.PrefetchScalarGridSpec(
        num_scalar_prefetch=5,
        grid=(RET_HEADS, n_steps),
        in_specs=[
            pl.BlockSpec((SEG, RET_DK), hold_blk(COL_RQ, RET_DK)),
            pl.BlockSpec((SEG, RET_DK), qk_blk(COL_RK)),
            pl.BlockSpec((SEG, RET_DV), v_blk),
            pl.BlockSpec((SEG, RET_DV), hold_blk(COL_RG, RET_DV)),
            pl.BlockSpec((1, RET_CHUNK, RET_CHUNK), per_head),
            pl.BlockSpec((1, 4, RET_CHUNK, 1), per_head4),
            pl.BlockSpec((1, 2, 1, RET_DV), per_head4),
            pl.BlockSpec((1, 1, RET_DV), per_head),
        ],
        out_specs=pl.BlockSpec((SEG, RET_DV), lambda h, s, seg_r, ph_r, rs_r, cb_r, hold_r: (hold_r[s], h)),
        scratch_shapes=[
            pltpu.VMEM((RET_MAX_CHUNKS, RET_DK, RET_DV), BF16),
            pltpu.VMEM((RET_DK, RET_DV), F32),
            pltpu.VMEM((RET_DK, RET_DV), F32),
        ],
    )

    def kernel(seg_r, ph_r, rs_r, cb_r, hold_r, *refs):
        _retention_kernel(seg_r, ph_r, rs_r, cb_r, *refs)

    return pl.pallas_call(
        kernel,
        grid_spec=grid_spec,
        out_shape=jax.ShapeDtypeStruct((T_ALL, RET_V_W), BF16),
        compiler_params=_cparams(("arbitrary", "arbitrary")),
        name="retention",
    )(seg, phase, reset, cbase, hold, z, z, z, z, ret_mask, ret_dec, ret_cdec, gn_g)


def _attention_kernel(first_ref, last_ref, q_ref, km_ref, kn_ref, vm_ref, vn_ref,
                      bias_ref, gq_ref, gk_ref, o_ref, lse_ref, kall, vall, qall, oacc, bvar, *, dil):
    nb = SEG // dil // ATT_BLOCK
    nqb = nb // 2
    n_slab = ATT_W // LANES
    c = pl.program_id(0)
    is_first = first_ref[c]
    is_last = last_ref[c]
    lane = lax.broadcasted_iota(jnp.int32, (1, LANES), 1)
    lo = lane < ATT_DH
    gq = gq_ref[...]
    gk = gk_ref[...]

    @pl.when(c == 0)
    def _():
        col = lax.broadcasted_iota(jnp.int32, (1, KW), 1)
        left = jnp.where(col < ATT_BLOCK, NEG, 0.0).astype(F32)
        right = jnp.where(col >= KW - ATT_BLOCK, NEG, 0.0).astype(F32)
        for h in range(ATT_HEADS):
            b = bias_ref[h]
            bvar[0, h] = b
            bvar[1, h] = b + left
            bvar[2, h] = b + right
            bvar[3, h] = b + left + right

    def head_norm(x, g):
        x2 = x * x
        s_lo = jnp.sum(jnp.where(lo, x2, 0.0), axis=-1, keepdims=True)
        s_hi = jnp.sum(jnp.where(lo, 0.0, x2), axis=-1, keepdims=True)
        ms = jnp.where(lo, s_lo, s_hi) * (1.0 / ATT_DH)
        return x * lax.rsqrt(ms + EPS) * g

    def norm_block(src):
        even, odd = [], []
        for s in range(n_slab):
            xn = head_norm(src[:, s * LANES:(s + 1) * LANES].astype(F32), gk)
            even.append(jnp.where(lo, xn, 0.0).astype(BF16))
            odd.append(jnp.where(lo, 0.0, xn).astype(BF16))
        return jnp.concatenate(even, axis=-1), jnp.concatenate(odd, axis=-1)

    lo_wide = lax.broadcasted_iota(jnp.int32, (1, ATT_W), 1) % LANES < ATT_DH

    def split_heads(v):
        zero = jnp.zeros_like(v)
        return jnp.where(lo_wide, v, zero), jnp.where(lo_wide, zero, v)

    @pl.when(c == 0)
    def _():
        kall[:, :, 0] = jnp.zeros((2, dil, ATT_BLOCK, ATT_W), BF16)
        vall[:, :, 0] = jnp.zeros((2, dil, ATT_BLOCK, ATT_W), BF16)

    @pl.when(c > 0)
    def _():
        def carry_over(rho, carry):
            for hh in range(2):
                kall[hh, rho, 0] = kall[hh, rho, nb]
                vall[hh, rho, 0] = vall[hh, rho, nb]
            return carry

        lax.fori_loop(0, dil, carry_over, 0)

    def fill_main(it, carry):
        rho = it // nb
        blk = it % nb
        kall[0, rho, blk + 1], kall[1, rho, blk + 1] = norm_block(km_ref[rho, blk])
        vall[0, rho, blk + 1], vall[1, rho, blk + 1] = split_heads(vm_ref[rho, blk])
        qsrc = q_ref[rho, blk]
        qall[rho, blk] = jnp.concatenate(
            [head_norm(qsrc[:, s * LANES:(s + 1) * LANES].astype(F32), gq).astype(BF16) for s in range(n_slab)], axis=-1)
        return carry

    lax.fori_loop(0, dil * nb, fill_main, 0, unroll=8)

    def fill_halo(rho, carry):
        kall[0, rho, nb + 1], kall[1, rho, nb + 1] = norm_block(kn_ref[rho, 0])
        vall[0, rho, nb + 1], vall[1, rho, nb + 1] = split_heads(vn_ref[rho, 0])
        return carry

    lax.fori_loop(0, dil, fill_halo, 0, unroll=min(dil, 4))

    ones_even = jnp.broadcast_to(jnp.where(lo, 1.0, 0.0).astype(BF16), (KW, LANES))
    ones_odd = jnp.broadcast_to(jnp.where(lo, 0.0, 1.0).astype(BF16), (KW, LANES))

    def body(it, carry):
        rho = it // nqb
        qb = it % nqb
        var = (jnp.where((qb == 0) & (is_first == 1), 1, 0)
               + jnp.where((qb == nqb - 1) & (is_last == 1), 2, 0))
        start = rho + qb * (QB * dil)
        rows = pl.ds(start, QB) if dil == 1 else pl.ds(start, QB, stride=dil)
        for s in range(n_slab):
            sl = slice(s * LANES, (s + 1) * LANES)
            qn = qall[rho, pl.ds(2 * qb, 2), :, sl].reshape(QB, LANES)
            es, ms = [], []
            for hh in range(2):
                kw = kall[hh, rho, pl.ds(2 * qb, 4), :, sl].reshape(KW, LANES)
                sc = lax.dot_general(qn, kw, (((1,), (1,)), ((), ())), preferred_element_type=F32)
                sc = sc + bvar[var, 2 * s + hh]
                m = jnp.max(sc, axis=-1, keepdims=True)
                es.append(jnp.exp2(sc - m).astype(BF16))
                ms.append(m)
            v_even = vall[0, rho, pl.ds(2 * qb, 4), :, sl].reshape(KW, LANES)
            v_odd = vall[1, rho, pl.ds(2 * qb, 4), :, sl].reshape(KW, LANES)
            rhs = jnp.concatenate([jnp.concatenate([v_even, ones_even], axis=1),
                                   jnp.concatenate([v_odd, ones_odd], axis=1)], axis=0)
            res = jnp.dot(jnp.concatenate(es, axis=1), rhs, preferred_element_type=F32)
            den = res[:, LANES:]
            oacc[s, rows, :] = res[:, :LANES] * (1.0 / den)
            lse_ref[s, rows, :] = (jnp.where(lo, ms[0], ms[1]) + jnp.log2(den)) * LN2
        return carry

    lax.fori_loop(0, dil * nqb, body, 0, unroll=8)

    for s in range(n_slab):
        o_ref[:, s * LANES:(s + 1) * LANES] = oacc[s].astype(BF16)


def _attention_group(z, bias_g, gq, gk, first, last, gi, dil):
    nb = SEG // dil // ATT_BLOCK
    z5 = z.reshape(N_SEG, dil, nb, ATT_BLOCK, N_IN)
    cq = (COL_ATT + 3 * gi * ATT_W) // ATT_W
    ck, cv = cq + 1, cq + 2
    main = lambda cb: pl.BlockSpec((None, dil, nb, ATT_BLOCK, ATT_W), lambda c, f, l: (c, 0, 0, 0, cb))
    nxt = lambda cb: pl.BlockSpec((None, dil, 1, ATT_BLOCK, ATT_W),
                                  lambda c, f, l: (c + 1 - l[c], 0, 0, 0, cb))
    grid_spec = pltpu.PrefetchScalarGridSpec(
        num_scalar_prefetch=2,
        grid=(N_SEG,),
        in_specs=[
            main(cq), main(ck), nxt(ck), main(cv), nxt(cv),
            pl.BlockSpec((ATT_HEADS, QB, KW), lambda c, f, l: (0, 0, 0)),
            pl.BlockSpec((1, LANES), lambda c, f, l: (0, 0)),
            pl.BlockSpec((1, LANES), lambda c, f, l: (0, 0)),
        ],
        out_specs=[
            pl.BlockSpec((SEG, ATT_W), lambda c, f, l: (c, 0)),
            pl.BlockSpec((ATT_W // LANES, SEG, LANES), lambda c, f, l: (0, c, 0)),
        ],
        scratch_shapes=[
            pltpu.VMEM((2, dil, nb + 2, ATT_BLOCK, ATT_W), BF16),
            pltpu.VMEM((2, dil, nb + 2, ATT_BLOCK, ATT_W), BF16),
            pltpu.VMEM((dil, nb, ATT_BLOCK, ATT_W), BF16),
            pltpu.VMEM((ATT_W // LANES, SEG, LANES), F32),
            pltpu.VMEM((4, ATT_HEADS, QB, KW), F32),
        ],
    )
    return pl.pallas_call(
        functools.partial(_attention_kernel, dil=dil),
        grid_spec=grid_spec,
        out_shape=[jax.ShapeDtypeStruct((T_ALL, ATT_W), BF16), jax.ShapeDtypeStruct((ATT_W // LANES, T_ALL, LANES), F32)],
        compiler_params=_cparams(("arbitrary",), ATT_VMEM_LIMIT),
        name=f"attention_d{dil}",
    )(first, last, z5, z5, z5, z5, z5, bias_g, gq, gk)


def _merge_kernel(yret_ref, o0_ref, o1_ref, o2_ref, l0_ref, l1_ref, l2_ref,
                  gret_a_ref, gret_b_ref, gatt_a_ref, gatt_b_ref, xp_ref, xs_ref,
                  wret_ref, watt_ref, wout_ref, nffn_ref, wr_ref, wrhi_ref, br_ref,
                  x1_ref, hpa_ref, hpb_ref, idx_ref, gate_ref, rank_ref, cnt_ref, carry_ref):
    i = pl.program_id(0)
    tm = MERGE_TM
    sub = MERGE_SUB

    @pl.when(i == 0)
    def _():
        carry_ref[...] = jnp.zeros_like(carry_ref)

    is_prompt = i < T_PROMPT // tm
    nt_dims = (((1,), (1,)), ((), ()))
    erow_f = lax.broadcasted_iota(jnp.int32, (N_EXPERTS, sub), 0).astype(F32)
    tok_r = lax.broadcasted_iota(jnp.int32, (sub, sub), 0)
    tok_c = lax.broadcasted_iota(jnp.int32, (sub, sub), 1)
    earlier = jnp.where(tok_r < tok_c, 1.0, 0.0).astype(BF16)
    same_tok = jnp.where(tok_r == tok_c, 1.0, 0.0).astype(BF16)
    carry = carry_ref[:, 0:1]

    for r0 in range(0, tm, sub):
        rows = slice(r0, r0 + sub)
        l0, l1, l2 = [jnp.concatenate([r[s, rows, :] for s in range(ATT_W // LANES)], axis=-1)
                      for r in (l0_ref, l1_ref, l2_ref)]
        lm = jnp.maximum(jnp.maximum(l0, l1), l2)
        e0, e1, e2 = jnp.exp(l0 - lm), jnp.exp(l1 - lm), jnp.exp(l2 - lm)
        inv = 1.0 / (e0 + e1 + e2)
        y_att = ((e0 * inv) * o0_ref[rows, :].astype(F32) + (e1 * inv) * o1_ref[rows, :].astype(F32)
                 + (e2 * inv) * o2_ref[rows, :].astype(F32))

        p_ret = jnp.dot(yret_ref[rows, :], wret_ref[...], preferred_element_type=F32)
        p_att = jnp.dot(y_att.astype(BF16), watt_ref[...], preferred_element_type=F32)
        g_ret = jnp.concatenate([gret_a_ref[rows, :], gret_b_ref[rows, :]], axis=-1).astype(F32)
        g_att = jnp.concatenate([gatt_a_ref[rows, :], gatt_b_ref[rows, :]], axis=-1).astype(F32)
        merged = _sigmoid(g_ret) * p_ret + _sigmoid(g_att) * p_att
        x_in = jnp.where(is_prompt, xp_ref[rows, :], xs_ref[rows, :])
        x1 = x_in + jnp.dot(merged.astype(BF16), wout_ref[...], preferred_element_type=F32)
        x1_ref[rows, :] = x1

        ms = jnp.mean(x1 * x1, axis=-1, keepdims=True)
        h2 = x1 * lax.rsqrt(ms + EPS) * nffn_ref[...]
        hpa_ref[rows, :], hpb_ref[rows, :] = _pack_row_halves(h2)

        h_hi = h2.astype(BF16)
        h_lo = (h2 - h_hi.astype(F32)).astype(BF16)
        p1 = lax.dot_general(wr_ref[...], h_hi, nt_dims, preferred_element_type=F32)
        p2 = lax.dot_general(wrhi_ref[...], h_lo, nt_dims, preferred_element_type=F32)
        work = p1[:N_EXPERTS] + p1[N_EXPERTS:2 * N_EXPERTS] + p2[:N_EXPERTS] + br_ref[...]
        vals, idxs = [], []
        for _ in range(TOP_K):
            m = jnp.max(work, axis=0, keepdims=True)
            ix = jnp.min(jnp.where(work == m, erow_f, float(N_EXPERTS)), axis=0, keepdims=True)
            vals.append(m)
            idxs.append(ix)
            work = jnp.where(erow_f == ix, -3e38, work)
        es = [jnp.exp(v - vals[0]) for v in vals]
        den = es[0] + es[1] + es[2] + es[3]
        onehot = jnp.zeros((N_EXPERTS, sub), F32)
        for ix in idxs:
            onehot = onehot + jnp.where(erow_f == ix, 1.0, 0.0)
        before = jnp.dot(onehot.astype(BF16), earlier, preferred_element_type=F32) + carry
        ranks = [jnp.sum(jnp.where(erow_f == ix, before, 0.0), axis=0, keepdims=True) for ix in idxs]
        pad_rows = jnp.zeros((8 - TOP_K, sub), F32)
        idx_ref[:, rows] = jnp.concatenate(idxs + [pad_rows], axis=0).astype(jnp.int32)
        rank_ref[:, rows] = jnp.concatenate(ranks + [pad_rows], axis=0).astype(jnp.int32)
        g_t = jnp.concatenate([e / den for e in es] + [jnp.zeros((LANES - TOP_K, sub), F32)], axis=0)
        g_hi = g_t.astype(BF16)
        g_lo = (g_t - g_hi.astype(F32)).astype(BF16)
        gate_ref[rows, :] = (lax.dot_general(same_tok, g_hi, nt_dims, preferred_element_type=F32)
                             + lax.dot_general(same_tok, g_lo, nt_dims, preferred_element_type=F32))
        carry = carry + jnp.sum(onehot, axis=1, keepdims=True)

    carry_ref[...] = jnp.broadcast_to(carry, carry_ref.shape)
    cnt_ref[...] = jnp.broadcast_to(carry, cnt_ref.shape)


def _merge(y_ret, o_list, lse_list, z, x_p, x_s, w_ret, w_att, w_out, n_ffn, w_router, w_router_hi, b_router):
    tm = MERGE_TM
    n_p = T_PROMPT // tm
    row = lambda w: pl.BlockSpec((tm, w), lambda i: (i, 0))
    full = lambda a: pl.BlockSpec(a.shape, lambda i: (0,) * a.ndim)
    zcol = lambda col: pl.BlockSpec((tm, COL_BLK), lambda i: (i, col // COL_BLK))
    lse_spec = pl.BlockSpec((ATT_W // LANES, tm, LANES), lambda i: (0, i, 0))
    kmajor = pl.BlockSpec((8, tm), lambda i: (0, i))
    return pl.pallas_call(
        _merge_kernel,
        grid=(T_ALL // tm,),
        in_specs=[row(RET_V_W), row(ATT_W), row(ATT_W), row(ATT_W), lse_spec, lse_spec, lse_spec,
                  zcol(COL_GATE_RET), zcol(COL_GATE_RET + COL_BLK), zcol(COL_GATE_ATT),
                  zcol(COL_GATE_ATT + COL_BLK),
                  pl.BlockSpec((tm, D_MODEL), lambda i: (jnp.minimum(i, n_p - 1), 0)),
                  pl.BlockSpec((tm, D_MODEL), lambda i: (jnp.maximum(i - n_p, 0), 0)),
                  full(w_ret), full(w_att), full(w_out), full(n_ffn), full(w_router), full(w_router_hi),
                  full(b_router)],
        out_specs=[row(D_MODEL), row(HALF_W), row(HALF_W), kmajor, row(LANES), kmajor,
                   pl.BlockSpec((N_EXPERTS, LANES), lambda i: (0, 0))],
        out_shape=[jax.ShapeDtypeStruct((T_ALL, D_MODEL), F32),
                   jax.ShapeDtypeStruct((T_ALL, HALF_W), jnp.uint32),
                   jax.ShapeDtypeStruct((T_ALL, HALF_W), jnp.uint32),
                   jax.ShapeDtypeStruct((8, T_ALL), jnp.int32),
                   jax.ShapeDtypeStruct((T_ALL, LANES), F32),
                   jax.ShapeDtypeStruct((8, T_ALL), jnp.int32),
                   jax.ShapeDtypeStruct((N_EXPERTS, LANES), F32)],
        scratch_shapes=[pltpu.VMEM((N_EXPERTS, LANES), F32)],
        compiler_params=_cparams(("arbitrary",)),
        name="merge_router",
    )(y_ret, *o_list, *lse_list, z, z, z, z, x_p, x_s, w_ret, w_att, w_out, n_ffn, w_router, w_router_hi, b_router)


def _sc_mesh():
    return plsc.VectorSubcoreMesh(core_axis_name="core", subcore_axis_name="subcore")


def _sc_scatter_rows(xs, idx_kmajor, n_out):
    n_rows, width = xs[0].shape
    out_type = [jax.ShapeDtypeStruct((n_out, width), x.dtype) for x in xs]

    @pl.kernel(out_type=out_type, mesh=_sc_mesh(), scratch_types=[])
    def scatter(*refs):
        x_refs, i_hbm, o_refs = refs[:len(xs)], refs[len(xs)], refs[len(xs) + 1:]
        for x_hbm, o_hbm in zip(x_refs, o_refs):
            def body(x_vmem, i_vmem, o_hbm=o_hbm):
                for k in range(TOP_K):
                    pltpu.sync_copy(x_vmem, o_hbm.at[i_vmem.at[k]])

            pltpu.emit_pipeline(
                body,
                grid=(n_rows // SC_WINDOW,),
                in_specs=[pl.BlockSpec((SC_WINDOW, width), lambda i: (i, 0)),
                          pl.BlockSpec((TOP_K, SC_WINDOW), lambda i: (0, i))],
                out_specs=[],
                core_axis_name=("core", "subcore"),
                dimension_semantics=(pltpu.PARALLEL,),
            )(x_hbm, i_hbm)

    return scatter(*xs, idx_kmajor)


def _sc_gather_rows(datas, idx):
    n_idx = idx.shape[0]
    width = datas[0].shape[1]
    out_type = [jax.ShapeDtypeStruct((n_idx, width), d.dtype) for d in datas]

    @pl.kernel(out_type=out_type, mesh=_sc_mesh(), scratch_types=[])
    def gather(*refs):
        x_refs, i_hbm, o_refs = refs[:len(datas)], refs[len(datas)], refs[len(datas) + 1:]
        for x_hbm, o_hbm in zip(x_refs, o_refs):
            def body(i_vmem, o_vmem, x_hbm=x_hbm):
                pltpu.sync_copy(x_hbm.at[i_vmem.at[0]], o_vmem)

            pltpu.emit_pipeline(
                body,
                grid=(n_idx // SC_WINDOW,),
                in_specs=[pl.BlockSpec((1, SC_WINDOW), lambda i: (0, i))],
                out_specs=[pl.BlockSpec((SC_WINDOW, width), lambda i: (i, 0))],
                core_axis_name=("core", "subcore"),
                dimension_semantics=(pltpu.PARALLEL,),
            )(i_hbm, o_hbm)

    return gather(*datas, idx.reshape(1, n_idx))


def _expert_kernel(be_ref, nused_ref, nvalid_ref, xa_ref, xb_ref, wg_ref, bg_ref, wu_ref, bu_ref, wd_ref, bd_ref,
                   ya_ref, yb_ref, wbf_ref):
    b = pl.program_id(0)
    active = b < nused_ref[0]
    new_expert = (b == 0) | (be_ref[b] != be_ref[jnp.maximum(b - 1, 0)])

    @pl.when(active & new_expert)
    def _():
        rows = 128
        for wi, w_ref in enumerate((wg_ref, wu_ref, wd_ref)):
            for r in range(0, D_MODEL, rows):
                wbf_ref[wi, r:r + rows, :] = w_ref[0, r:r + rows, :].astype(BF16)

    @pl.when(active)
    def _():
        valid = lax.broadcasted_iota(jnp.int32, (MOE_BM, HALF_W), 0) < nvalid_ref[b]
        zero = jnp.zeros((MOE_BM, HALF_W), jnp.uint32)
        x = _unpack_row_halves(jnp.where(valid, xa_ref[...], zero), jnp.where(valid, xb_ref[...], zero)).astype(BF16)
        g = jnp.dot(x, wbf_ref[0], preferred_element_type=F32) + bg_ref[0]
        u = jnp.dot(x, wbf_ref[1], preferred_element_type=F32) + bu_ref[0]
        g = jnp.minimum(g, SWIGLU_LIMIT)
        u = jnp.clip(u, -SWIGLU_LIMIT, SWIGLU_LIMIT)
        glu = g * jax.nn.sigmoid(SWIGLU_ALPHA * g)
        act = ((u + 1.0) * glu).astype(BF16)
        y = jnp.dot(act, wbf_ref[2], preferred_element_type=F32) + bd_ref[0]
        ya_ref[...], yb_ref[...] = _pack_row_halves(y)

    @pl.when(jnp.logical_not(active))
    def _():
        ya_ref[...] = jnp.zeros_like(ya_ref)
        yb_ref[...] = jnp.zeros_like(yb_ref)


def _experts(block_expert, n_used, n_valid, xs_a, xs_b, wg, bg, wu, bu, wd, bd):
    assert D_FF == D_MODEL
    blk = lambda b, be, nu, nv: (jnp.minimum(b, nu[0] - 1), 0)
    wsp = lambda: pl.BlockSpec((1, D_MODEL, D_FF), lambda b, be, nu, nv: (be[b], 0, 0))
    bsp = lambda: pl.BlockSpec((1, 1, D_FF), lambda b, be, nu, nv: (be[b], 0, 0))
    xsp = lambda: pl.BlockSpec((MOE_BM, HALF_W), blk)
    ysp = lambda: pl.BlockSpec((MOE_BM, HALF_W), lambda b, be, nu, nv: (b, 0))
    slot_arr = jax.ShapeDtypeStruct((N_SLOTS, HALF_W), jnp.uint32)
    grid_spec = pltpu.PrefetchScalarGridSpec(
        num_scalar_prefetch=3,
        grid=(N_SLOT_BLOCKS,),
        in_specs=[xsp(), xsp(), wsp(), bsp(), wsp(), bsp(), wsp(), bsp()],
        out_specs=[ysp(), ysp()],
        scratch_shapes=[pltpu.VMEM((3, D_MODEL, D_FF), BF16)],
    )
    return pl.pallas_call(
        _expert_kernel,
        grid_spec=grid_spec,
        out_shape=[slot_arr, slot_arr],
        compiler_params=_cparams(("arbitrary",)),
        name="experts",
    )(block_expert, n_used, n_valid, xs_a, xs_b, wg, bg, wu, bu, wd, bd)


def _final_kernel(x1_ref, yga_ref, ygb_ref, gate_ref, p_ref, nple_ref, wpg_ref, wpp_ref, out_ref):
    x2 = x1_ref[...]
    gates = gate_ref[...]
    for k in range(TOP_K):
        x2 = x2 + gates[:, k:k + 1] * _unpack_row_halves(yga_ref[k], ygb_ref[k])
    ms = jnp.mean(x2 * x2, axis=-1, keepdims=True)
    h3 = (x2 * lax.rsqrt(ms + EPS) * nple_ref[...]).astype(BF16)
    gate = jax.nn.sigmoid(jnp.dot(h3, wpg_ref[...], preferred_element_type=F32))
    proj = jnp.dot(p_ref[...].astype(BF16), wpp_ref[...], preferred_element_type=F32)
    out_ref[...] = x2 + gate * proj


def _final(x1, yg_a, yg_b, gates, p, n_ple, w_pg, w_pp, row0, n_rows):
    tm = FINAL_TM
    off = row0 // tm
    full = lambda a: pl.BlockSpec(a.shape, lambda i: (0,) * a.ndim)
    return pl.pallas_call(
        _final_kernel,
        grid=(n_rows // tm,),
        in_specs=[pl.BlockSpec((tm, D_MODEL), lambda i: (i + off, 0)),
                  pl.BlockSpec((TOP_K, tm, HALF_W), lambda i: (0, i + off, 0)),
                  pl.BlockSpec((TOP_K, tm, HALF_W), lambda i: (0, i + off, 0)),
                  pl.BlockSpec((tm, LANES), lambda i: (i + off, 0)),
                  pl.BlockSpec((tm, PLE_DIM), lambda i: (i, 0)),
                  full(n_ple), full(w_pg), full(w_pp)],
        out_specs=pl.BlockSpec((tm, D_MODEL), lambda i: (i, 0)),
        out_shape=jax.ShapeDtypeStruct((n_rows, D_MODEL), F32),
        compiler_params=_cparams(("arbitrary",)),
        name="final_ple",
    )(x1, yg_a, yg_b, gates, p, n_ple, w_pg, w_pp)


def _rope_tables():
    half = RET_DK // 2
    step = 128
    freq = ROPE_THETA ** (-jnp.arange(half, dtype=F32) / half)
    freq = jnp.concatenate([freq, freq])
    sign = jnp.where(jnp.arange(RET_DK) < half, -1.0, 1.0).astype(F32)
    ang_lo = jnp.arange(step, dtype=F32)[:, None] * freq[None, :]
    ang_hi = (jnp.arange(SAMPLE_SEQ // step, dtype=F32) * step)[:, None] * freq[None, :]
    c_lo, s_lo = jnp.cos(ang_lo)[None], jnp.sin(ang_lo)[None]
    c_hi, s_hi = jnp.cos(ang_hi)[:, None], jnp.sin(ang_hi)[:, None]
    cos = (c_hi * c_lo - s_hi * s_lo).reshape(SAMPLE_SEQ, RET_DK)
    sin = ((s_hi * c_lo + c_hi * s_lo) * sign).reshape(SAMPLE_SEQ, RET_DK)
    return cos, sin


def _retention_tables(decay_logit):
    lg = jax.nn.log_sigmoid(decay_logit.astype(F32))
    c = RET_CHUNK
    idx = jnp.arange(c, dtype=F32)
    diff = idx[:, None] - idx[None, :]
    lf = lg[0][:, None, None]
    lb = lg[1][:, None, None]
    mask = jnp.where(diff[None] >= 0, jnp.exp(lf * jnp.maximum(diff, 0.0)[None]),
                     jnp.exp(lb * jnp.maximum(-diff, 0.0)[None]))
    kdec_f = jnp.exp(lg[0][:, None] * (c - 1.0 - idx)[None, :])
    qdec_f = jnp.exp(lg[0][:, None] * (idx + 1.0)[None, :])
    kdec_b = jnp.exp(lg[1][:, None] * idx[None, :])
    qdec_b = jnp.exp(lg[1][:, None] * (c - idx)[None, :])
    dec = jnp.stack([kdec_f, qdec_f, kdec_b, qdec_b], axis=1)[..., None]
    cdec = jnp.exp(lg * c).T
    cdec = jnp.broadcast_to(cdec[:, :, None, None], (RET_HEADS, 2, 1, RET_DV))
    return mask, dec, cdec


def _t5_bucket(rel):
    half = T5_BUCKETS // 2
    exact = half // 2
    n = np.abs(rel)
    ratio = np.log(np.maximum(n, 1).astype(np.float32) / np.float32(exact)) / np.float32(math.log(T5_MAX_DIST / exact))
    large = exact + (ratio * np.float32(half - exact)).astype(np.int32)
    large = np.minimum(large, half - 1)
    return np.where(rel > 0, half, 0) + np.where(n < exact, n, large)


def _attention_bias(rel_bias, gi, dil, radius):
    qi = np.arange(QB)
    ki = np.arange(KW) - ATT_BLOCK
    rel = ki[None, :] - qi[:, None]
    onehot = jnp.asarray(_t5_bucket(rel * dil)[..., None] == np.arange(T5_BUCKETS), F32)
    tab = rel_bias[:, gi * ATT_HEADS:(gi + 1) * ATT_HEADS].astype(F32)
    bias = jnp.einsum('qkb,bh->hqk', onehot, tab, precision=lax.Precision.HIGHEST)
    return jnp.where(jnp.asarray(np.abs(rel) <= radius)[None], bias * LOG2E, NEG)


def _seq_edge_flags():
    first = np.zeros((N_SEG,), np.int32)
    last = np.zeros((N_SEG,), np.int32)
    first[:N_PROMPT_SEG] = 1
    last[:N_PROMPT_SEG] = 1
    first[N_PROMPT_SEG] = 1
    last[N_SEG - 1] = 1
    return jnp.asarray(first), jnp.asarray(last)


def _pad_lanes(a, value=0.0):
    return jnp.pad(a, ((0, 0), (0, LANES - a.shape[-1])), constant_values=value)


def kernel(x_prompt, x_sample, p_prompt, p_sample, norm_mix_g, w_in, ret_decay_logit, ret_gn_g,
           att_q_norm_g, att_k_norm_g, rel_bias, w_ret_proj, w_att_proj, w_out, norm_ffn_g,
           w_router, b_router, w_gate, b_gate, w_up, b_up, w_down, b_down,
           norm_ple_g, w_ple_gate, w_ple_proj):
    assert norm_mix_g.shape[0] == 1, "one layer"
    x_p = x_prompt.reshape(T_PROMPT, D_MODEL)
    x_s = x_sample.reshape(SAMPLE_SEQ, D_MODEL)

    cos_t, sin_t = _rope_tables()
    z = _in_proj(x_p, x_s, norm_mix_g.astype(F32), w_in[0].astype(BF16), cos_t, sin_t)

    ret_mask, ret_dec, ret_cdec = _retention_tables(ret_decay_logit[0])
    y_ret = _retention(z, ret_mask, ret_dec, ret_cdec, ret_gn_g[0].reshape(RET_HEADS, 1, RET_DV).astype(F32))

    first, last = _seq_edge_flags()
    o_list, lse_list = [], []
    for gi, (window, dil) in enumerate(ATT_GROUPS):
        bias_g = _attention_bias(rel_bias, gi, dil, window // (2 * dil))
        gq = jnp.tile(att_q_norm_g[0, gi].astype(F32) * (ATT_DH ** -0.5 * LOG2E), LANES // ATT_DH)[None, :]
        gk = jnp.tile(att_k_norm_g[0, gi].astype(F32), LANES // ATT_DH)[None, :]
        o_g, lse_g = _attention_group(z, bias_g, gq, gk, first, last, gi, dil)
        o_list.append(o_g)
        lse_list.append(lse_g)

    w_r = w_router[0].astype(F32)
    w_r_hi = w_r.astype(BF16)
    w_r_lo = (w_r - w_r_hi.astype(F32)).astype(BF16)
    w_router_cat = _pad_lanes(jnp.concatenate([w_r_hi, w_r_lo], axis=1)).T
    w_router_hi = _pad_lanes(w_r_hi).T
    b_router_p = b_router.astype(F32).reshape(N_EXPERTS, 1)
    x1, hp_a, hp_b, idx, gates, rank, cnt = _merge(
        y_ret, o_list, lse_list, z, x_p, x_s, w_ret_proj[0].astype(BF16), w_att_proj[0].astype(BF16),
        w_out[0].astype(BF16), norm_ffn_g.astype(F32), w_router_cat, w_router_hi, b_router_p)

    counts = cnt[:, 0].astype(jnp.int32)
    padded = (counts + MOE_BM - 1) // MOE_BM * MOE_BM
    pad_end = jnp.cumsum(padded)
    pad_start = pad_end - padded
    expert_ids = jnp.arange(N_EXPERTS, dtype=jnp.int32)
    top_idx = idx[:TOP_K]
    start_of = jnp.sum(jnp.where(top_idx[:, :, None] == expert_ids, pad_start, 0), axis=-1)
    dest_kmajor = start_of + rank[:TOP_K]
    n_used = (pad_end[-1] // MOE_BM).astype(jnp.int32).reshape(1)
    blk_row0 = jnp.arange(N_SLOT_BLOCKS, dtype=jnp.int32) * MOE_BM
    block_expert = jnp.minimum(jnp.sum((pad_end[None, :] <= blk_row0[:, None]).astype(jnp.int32), axis=1),
                               N_EXPERTS - 1).astype(jnp.int32)
    slot_end = jnp.sum(jnp.where(block_expert[:, None] == expert_ids, pad_start + counts, 0), axis=-1)
    n_valid = jnp.clip(slot_end - blk_row0, 0, MOE_BM).astype(jnp.int32)

    xs_a, xs_b = _sc_scatter_rows((hp_a, hp_b), dest_kmajor, N_SLOTS)
    ys_a, ys_b = _experts(block_expert, n_used, n_valid, xs_a, xs_b,
                          w_gate[0], b_gate[0].reshape(N_EXPERTS, 1, D_FF).astype(F32),
                          w_up[0], b_up[0].reshape(N_EXPERTS, 1, D_FF).astype(F32),
                          w_down[0], b_down[0].reshape(N_EXPERTS, 1, D_MODEL).astype(F32))
    dest_flat = dest_kmajor.reshape(-1)
    yg_a, yg_b = [y.reshape(TOP_K, T_ALL, HALF_W) for y in _sc_gather_rows((ys_a, ys_b), dest_flat)]

    n_ple = norm_ple_g.astype(F32)
    w_pg = w_ple_gate[0].astype(BF16)
    w_pp = w_ple_proj[0].astype(BF16)
    y_p = _final(x1, yg_a, yg_b, gates, p_prompt[0].reshape(T_PROMPT, PLE_DIM), n_ple, w_pg, w_pp, 0, T_PROMPT)
    y_s = _final(x1, yg_a, yg_b, gates, p_sample[0].reshape(SAMPLE_SEQ, PLE_DIM), n_ple, w_pg, w_pp, T_PROMPT, SAMPLE_SEQ)
    return (y_p.reshape(x_prompt.shape), y_s.reshape(x_sample.shape))
```

```python
import functools
import math

import jax
import jax.numpy as jnp
import numpy as np
from jax import lax
from jax.experimental import pallas as pl
from jax.experimental.pallas import tpu as pltpu
from jax.experimental.pallas import tpu_sc as plsc

F32 = jnp.float32
BF16 = jnp.bfloat16

D_MODEL = 1024
N_PROMPT_SEQ = 8
PROMPT_SEQ = 2048
SAMPLE_SEQ = 16384
T_PROMPT = N_PROMPT_SEQ * PROMPT_SEQ
T_ALL = T_PROMPT + SAMPLE_SEQ

RET_HEADS = 4
RET_DK = 128
RET_DV = 256
RET_CHUNK = 128
ROPE_THETA = 10000.0
ATT_GROUPS = ((128, 1), (512, 4), (2048, 16))
N_GROUPS = 3
ATT_HEADS = 8
ATT_DH = 64
ATT_BLOCK = 64
ATT_W = ATT_HEADS * ATT_DH
T5_BUCKETS = 32
T5_MAX_DIST = 1024
N_EXPERTS = 32
TOP_K = 4
D_FF = 1024
SWIGLU_ALPHA = 1.702
SWIGLU_LIMIT = 7.0
PLE_DIM = 256
EPS = 1e-6

RET_QK_W = RET_HEADS * RET_DK
RET_V_W = RET_HEADS * RET_DV
N_IN = 2 * RET_QK_W + 2 * RET_V_W + 3 * N_GROUPS * ATT_W + 2 * D_MODEL

COL_RQ = 0
COL_RK = RET_QK_W
COL_RV = 2 * RET_QK_W
COL_RG = COL_RV + RET_V_W
COL_ATT = COL_RG + RET_V_W
COL_GATE_RET = COL_ATT + 3 * N_GROUPS * ATT_W
COL_GATE_ATT = COL_GATE_RET + D_MODEL

LANES = 128
VMEM_LIMIT = 56 * 1024 * 1024
IN_PROJ_VMEM_LIMIT = 60 * 1024 * 1024
ATT_VMEM_LIMIT = 58 * 1024 * 1024

SEG = 2048
N_SEG = T_ALL // SEG
N_PROMPT_SEG = T_PROMPT // SEG
COL_BLK = 512
N_COL_BLK = N_IN // COL_BLK
QB = 128
KW = 256
NEG = -1e30
LOG2E = math.log2(math.e)
LN2 = math.log(2.0)
MERGE_TM = 512
MERGE_SUB = MERGE_TM
FINAL_TM = 1024
MOE_BM = 1024
N_SLOT_BLOCKS = T_ALL * TOP_K // MOE_BM + N_EXPERTS
N_SLOTS = N_SLOT_BLOCKS * MOE_BM
HALF_W = D_MODEL // 4
SC_WINDOW = 128


def _cparams(sem, vmem=VMEM_LIMIT):
    return pltpu.CompilerParams(dimension_semantics=sem, vmem_limit_bytes=vmem)


def _sigmoid(x):
    return 0.5 * jnp.tanh(0.5 * x) + 0.5


def _pack_bf16_pair(x):
    w = x.shape[-1] // 2
    hi = pltpu.bitcast(x[:, :w].astype(BF16).astype(F32), jnp.uint32)
    lo = pltpu.bitcast(x[:, w:].astype(BF16).astype(F32), jnp.uint32)
    return hi | (lo >> 16)


def _unpack_bf16_pair(p):
    hi = pltpu.bitcast(p & jnp.uint32(0xFFFF0000), F32)
    lo = pltpu.bitcast(p << 16, F32)
    return jnp.concatenate([hi, lo], axis=-1)


def _pack_row_halves(x):
    half = x.shape[-1] // 2
    return _pack_bf16_pair(x[:, :half]), _pack_bf16_pair(x[:, half:])


def _unpack_row_halves(pa, pb):
    return jnp.concatenate([_unpack_bf16_pair(pa), _unpack_bf16_pair(pb)], axis=-1)


def _in_proj_kernel(xp_ref, xs_ref, g_ref, w_ref, cos_ref, sin_ref, z_ref, h_ref, p_ref, p2_ref):
    i = pl.program_id(0)
    j = pl.program_id(1)

    def norm_into_h(x_ref):
        xf = x_ref[...]
        ms = jnp.mean(xf * xf, axis=-1, keepdims=True)
        h_ref[...] = (xf * lax.rsqrt(ms + EPS) * g_ref[...]).astype(BF16)

    @pl.when((j == 0) & (i < N_PROMPT_SEG))
    def _():
        norm_into_h(xp_ref)

    @pl.when((j == 0) & (i >= N_PROMPT_SEG))
    def _():
        norm_into_h(xs_ref)

    n_slab = COL_BLK // LANES

    def project():
        return jnp.dot(h_ref[...], w_ref[...], preferred_element_type=F32)

    is_rope = j < (COL_RV // COL_BLK)
    att0 = COL_ATT // COL_BLK
    is_d4 = (j >= att0 + 3) & (j < att0 + 6)
    is_d16 = (j >= att0 + 6) & (j < att0 + 9)

    @pl.when(is_rope)
    def _():
        acc = project()
        scale = jnp.where(j == COL_RK // COL_BLK, RET_DK ** -0.5, 1.0).astype(F32)
        c = cos_ref[...]
        sn = sin_ref[...]
        for s in range(n_slab):
            xs = acc[:, s * LANES:(s + 1) * LANES]
            r = xs * c + pltpu.roll(xs, RET_DK // 2, axis=1) * sn
            z_ref[:, s * LANES:(s + 1) * LANES] = (r * scale).astype(BF16)

    @pl.when(is_d4)
    def _():
        acc = project()
        for s in range(n_slab):
            p_ref[s] = acc[:, s * LANES:(s + 1) * LANES]
        rows = SEG // 4
        for rho in range(4):
            for s in range(n_slab):
                piece = p_ref[s, pl.ds(rho, rows, stride=4), :]
                z_ref[rho * rows:(rho + 1) * rows, s * LANES:(s + 1) * LANES] = piece.astype(BF16)

    @pl.when(is_d16)
    def _():
        acc = project()
        for s in range(n_slab):
            p_ref[s] = acc[:, s * LANES:(s + 1) * LANES]
        quarter = SEG // 4
        rows = SEG // 16
        for r4 in range(4):
            for s in range(n_slab):
                p2_ref[s, r4 * quarter:(r4 + 1) * quarter, :] = p_ref[s, pl.ds(r4, quarter, stride=4), :]
        for r4 in range(4):
            for hi in range(4):
                rho = 4 * hi + r4
                for s in range(n_slab):
                    piece = p2_ref[s, pl.ds(r4 * quarter + hi, rows, stride=4), :]
                    z_ref[rho * rows:(rho + 1) * rows, s * LANES:(s + 1) * LANES] = piece.astype(BF16)

    @pl.when(jnp.logical_not(is_rope | is_d4 | is_d16))
    def _():
        z_ref[...] = project().astype(BF16)


def _in_proj(x_p, x_s, norm_g, w_in_bf, cos_t, sin_t):
    def pos_blk(i, j):
        return (jnp.maximum(i - N_PROMPT_SEG, 0), 0)

    return pl.pallas_call(
        _in_proj_kernel,
        grid=(N_SEG, N_COL_BLK),
        in_specs=[
            pl.BlockSpec((SEG, D_MODEL), lambda i, j: (jnp.minimum(i, N_PROMPT_SEG - 1), 0)),
            pl.BlockSpec((SEG, D_MODEL), pos_blk),
            pl.BlockSpec((1, D_MODEL), lambda i, j: (0, 0)),
            pl.BlockSpec((D_MODEL, COL_BLK), lambda i, j: (0, j)),
            pl.BlockSpec((SEG, LANES), pos_blk, pipeline_mode=pl.Buffered(1)),
            pl.BlockSpec((SEG, LANES), pos_blk, pipeline_mode=pl.Buffered(1)),
        ],
        out_specs=pl.BlockSpec((SEG, COL_BLK), lambda i, j: (i, j)),
        out_shape=jax.ShapeDtypeStruct((T_ALL, N_IN), BF16),
        scratch_shapes=[
            pltpu.VMEM((SEG, D_MODEL), BF16),
            pltpu.VMEM((COL_BLK // LANES, SEG, LANES), F32),
            pltpu.VMEM((COL_BLK // LANES, SEG, LANES), F32),
        ],
        compiler_params=_cparams(("arbitrary", "arbitrary"), IN_PROJ_VMEM_LIMIT),
        name="in_proj",
    )(x_p, x_s, norm_g, w_in_bf, cos_t, sin_t)


RET_CHUNKS_PER_SEG = SEG // RET_CHUNK
RET_MAX_CHUNKS = SAMPLE_SEQ // RET_CHUNK
RET_GROUP = 16


def _retention_kernel(seg_ref, phase_ref, reset_ref, cbase_ref,
                      q_ref, k_ref, v_ref, g_ref, mask_ref, dec_ref, cdec_ref, gn_ref,
                      y_ref, sb_ref, sf_ref, sr_ref):
    step = pl.program_id(1)
    phase = phase_ref[step]
    reset = reset_ref[step]
    cbase = cbase_ref[step]
    kdec_f = dec_ref[0, 0]
    qdec_f = dec_ref[0, 1]
    kdec_b = dec_ref[0, 2]
    qdec_b = dec_ref[0, 3]
    cd_f = cdec_ref[0, 0]
    cd_b = cdec_ref[0, 1]

    def kv_outer(kd, v):
        return lax.dot_general(kd, v, (((0,), (0,)), ((), ())), preferred_element_type=F32)

    @pl.when((phase == 0) & (reset == 1))
    def _():
        sr_ref[...] = jnp.zeros_like(sr_ref)

    @pl.when((phase == 1) & (reset == 1))
    def _():
        sf_ref[...] = jnp.zeros_like(sf_ref)

    n_groups = RET_CHUNKS_PER_SEG // RET_GROUP

    def chunk_rows(c):
        return pl.ds(pl.multiple_of(c * RET_CHUNK, RET_CHUNK), RET_CHUNK)

    @pl.when(phase == 0)
    def _():
        def body(it, carry):
            top = RET_CHUNKS_PER_SEG - 1 - it * RET_GROUP
            kvs = []
            for j in range(RET_GROUP):
                rows = chunk_rows(top - j)
                kd = (k_ref[rows, :].astype(F32) * kdec_b).astype(BF16)
                kvs.append(kv_outer(kd, v_ref[rows, :]))
            state = sr_ref[...]
            for j in range(RET_GROUP):
                sb_ref[cbase + top - j] = state.astype(BF16)
                state = cd_b * state + kvs[j]
            sr_ref[...] = state
            return carry

        lax.fori_loop(0, n_groups, body, 0)

    @pl.when(phase == 1)
    def _():
        msk = mask_ref[0]
        gn = gn_ref[0]

        def body(it, carry):
            c0 = it * RET_GROUP
            lhs, vs, kvs = [], [], []
            for j in range(RET_GROUP):
                rows = chunk_rows(c0 + j)
                qb = q_ref[rows, :]
                kb = k_ref[rows, :]
                v = v_ref[rows, :]
                q = qb.astype(F32)
                s = lax.dot_general(qb, kb, (((1,), (1,)), ((), ())), preferred_element_type=F32)
                lhs.append(jnp.concatenate(
                    [(s * msk).astype(BF16), (q * qdec_f).astype(BF16), (q * qdec_b).astype(BF16)], axis=-1))
                vs.append(v)
                kvs.append(kv_outer((kb.astype(F32) * kdec_f).astype(BF16), v))
            state = sf_ref[...]
            for j in range(RET_GROUP):
                c = c0 + j
                rhs = jnp.concatenate([vs[j], state.astype(BF16), sb_ref[cbase + c]], axis=0)
                o = jnp.dot(lhs[j], rhs, preferred_element_type=F32)
                state = cd_f * state + kvs[j]
                mu = jnp.mean(o, axis=-1, keepdims=True)
                oc = o - mu
                var = jnp.mean(oc * oc, axis=-1, keepdims=True)
                on = oc * lax.rsqrt(var + EPS) * gn
                rows = chunk_rows(c)
                gate = g_ref[rows, :].astype(F32)
                y_ref[rows, :] = (gate * jax.nn.sigmoid(gate) * on).astype(BF16)
            sf_ref[...] = state
            return carry

        lax.fori_loop(0, n_groups, body, 0)


def _retention_schedule():
    seg, phase, reset, cbase = [], [], [], []
    for p in range(N_PROMPT_SEG):
        for ph in (0, 1):
            seg.append(p); phase.append(ph); reset.append(1); cbase.append(0)
    n_s = N_SEG - N_PROMPT_SEG
    for i in range(n_s):
        t = n_s - 1 - i
        seg.append(N_PROMPT_SEG + t); phase.append(0); reset.append(int(i == 0)); cbase.append(t * RET_CHUNKS_PER_SEG)
    for t in range(n_s):
        seg.append(N_PROMPT_SEG + t); phase.append(1); reset.append(int(t == 0)); cbase.append(t * RET_CHUNKS_PER_SEG)
    hold = list(seg)
    for i in range(len(seg)):
        if phase[i] == 0:
            nxt = next(j for j in range(i + 1, len(seg)) if phase[j] == 1)
            hold[i] = seg[nxt]
    arr = lambda a: jnp.asarray(np.asarray(a, np.int32))
    return arr(seg), arr(phase), arr(reset), arr(cbase), arr(hold)


def _retention(z, ret_mask, ret_dec, ret_cdec, gn_g):
    seg, phase, reset, cbase, hold = _retention_schedule()
    n_steps = int(seg.shape[0])
    qk_blk = lambda col: (lambda h, s, seg_r, ph_r, rs_r, cb_r, hold_r: (seg_r[s], col // RET_DK + h))
    hold_blk = lambda col, w: (lambda h, s, seg_r, ph_r, rs_r, cb_r, hold_r: (hold_r[s], col // w + h))
    v_blk = lambda h, s, seg_r, ph_r, rs_r, cb_r, hold_r: (seg_r[s], COL_RV // RET_DV + h)
    per_head = lambda h, s, *_: (h, 0, 0)
    per_head4 = lambda h, s, *_: (h, 0, 0, 0)
    grid_spec = pltpu.PrefetchScalarGridSpec(
        num_scalar_prefetch=5,
        grid=(RET_HEADS, n_steps),
        in_specs=[
            pl.BlockSpec((SEG, RET_DK), hold_blk(COL_RQ, RET_DK)),
            pl.BlockSpec((SEG, RET_DK), qk_blk(COL_RK)),
            pl.BlockSpec((SEG, RET_DV), v_blk),
            pl.BlockSpec((SEG, RET_DV), hold_blk(COL_RG, RET_DV)),
            pl.BlockSpec((1, RET_CHUNK, RET_CHUNK), per_head),
            pl.BlockSpec((1, 4, RET_CHUNK, 1), per_head4),
            pl.BlockSpec((1, 2, 1, RET_DV), per_head4),
            pl.BlockSpec((1, 1, RET_DV), per_head),
        ],
        out_specs=pl.BlockSpec((SEG, RET_DV), lambda h, s, seg_r, ph_r, rs_r, cb_r, hold_r: (hold_r[s], h)),
        scratch_shapes=[
            pltpu.VMEM((RET_MAX_CHUNKS, RET_DK, RET_DV), BF16),
            pltpu.VMEM((RET_DK, RET_DV), F32),
            pltpu.VMEM((RET_DK, RET_DV), F32),
        ],
    )

    def kernel(seg_r, ph_r, rs_r, cb_r, hold_r, *refs):
        _retention_kernel(seg_r, ph_r, rs_r, cb_r, *refs)

    return pl.pallas_call(
        kernel,
        grid_spec=grid_spec,
        out_shape=jax.ShapeDtypeStruct((T_ALL, RET_V_W), BF16),
        compiler_params=_cparams(("arbitrary", "arbitrary")),
        name="retention",
    )(seg, phase, reset, cbase, hold, z, z, z, z, ret_mask, ret_dec, ret_cdec, gn_g)


def _attention_kernel(first_ref, last_ref, q_ref, km_ref, kn_ref, vm_ref, vn_ref,
                      bias_ref, gq_ref, gk_ref, o_ref, lse_ref, kall, vall, qall, oacc, bvar, *, dil):
    nb = SEG // dil // ATT_BLOCK
    nqb = nb // 2
    n_slab = ATT_W // LANES
    c = pl.program_id(0)
    is_first = first_ref[c]
    is_last = last_ref[c]
    lane = lax.broadcasted_iota(jnp.int32, (1, LANES), 1)
    lo = lane < ATT_DH
    gq = gq_ref[...]
    gk = gk_ref[...]

    @pl.when(c == 0)
    def _():
        col = lax.broadcasted_iota(jnp.int32, (1, KW), 1)
        left = jnp.where(col < ATT_BLOCK, NEG, 0.0).astype(F32)
        right = jnp.where(col >= KW - ATT_BLOCK, NEG, 0.0).astype(F32)
        for h in range(ATT_HEADS):
            b = bias_ref[h]
            bvar[0, h] = b
            bvar[1, h] = b + left
            bvar[2, h] = b + right
            bvar[3, h] = b + left + right

    def head_norm(x, g):
        x2 = x * x
        s_lo = jnp.sum(jnp.where(lo, x2, 0.0), axis=-1, keepdims=True)
        s_hi = jnp.sum(jnp.where(lo, 0.0, x2), axis=-1, keepdims=True)
        ms = jnp.where(lo, s_lo, s_hi) * (1.0 / ATT_DH)
        return x * lax.rsqrt(ms + EPS) * g

    def norm_block(src):
        even, odd = [], []
        for s in range(n_slab):
            xn = head_norm(src[:, s * LANES:(s + 1) * LANES].astype(F32), gk)
            even.append(jnp.where(lo, xn, 0.0).astype(BF16))
            odd.append(jnp.where(lo, 0.0, xn).astype(BF16))
        return jnp.concatenate(even, axis=-1), jnp.concatenate(odd, axis=-1)

    lo_wide = lax.broadcasted_iota(jnp.int32, (1, ATT_W), 1) % LANES < ATT_DH

    def split_heads(v):
        zero = jnp.zeros_like(v)
        return jnp.where(lo_wide, v, zero), jnp.where(lo_wide, zero, v)

    @pl.when(c == 0)
    def _():
        kall[:, :, 0] = jnp.zeros((2, dil, ATT_BLOCK, ATT_W), BF16)
        vall[:, :, 0] = jnp.zeros((2, dil, ATT_BLOCK, ATT_W), BF16)

    @pl.when(c > 0)
    def _():
        def carry_over(rho, carry):
            for hh in range(2):
                kall[hh, rho, 0] = kall[hh, rho, nb]
                vall[hh, rho, 0] = vall[hh, rho, nb]
            return carry

        lax.fori_loop(0, dil, carry_over, 0)

    def fill_main(it, carry):
        rho = it // nb
        blk = it % nb
        kall[0, rho, blk + 1], kall[1, rho, blk + 1] = norm_block(km_ref[rho, blk])
        vall[0, rho, blk + 1], vall[1, rho, blk + 1] = split_heads(vm_ref[rho, blk])
        qsrc = q_ref[rho, blk]
        qall[rho, blk] = jnp.concatenate(
            [head_norm(qsrc[:, s * LANES:(s + 1) * LANES].astype(F32), gq).astype(BF16) for s in range(n_slab)], axis=-1)
        return carry

    lax.fori_loop(0, dil * nb, fill_main, 0, unroll=8)

    def fill_halo(rho, carry):
        kall[0, rho, nb + 1], kall[1, rho, nb + 1] = norm_block(kn_ref[rho, 0])
        vall[0, rho, nb + 1], vall[1, rho, nb + 1] = split_heads(vn_ref[rho, 0])
        return carry

    lax.fori_loop(0, dil, fill_halo, 0, unroll=min(dil, 4))

    ones_even = jnp.broadcast_to(jnp.where(lo, 1.0, 0.0).astype(BF16), (KW, LANES))
    ones_odd = jnp.broadcast_to(jnp.where(lo, 0.0, 1.0).astype(BF16), (KW, LANES))

    def body(it, carry):
        rho = it // nqb
        qb = it % nqb
        var = (jnp.where((qb == 0) & (is_first == 1), 1, 0)
               + jnp.where((qb == nqb - 1) & (is_last == 1), 2, 0))
        start = rho + qb * (QB * dil)
        rows = pl.ds(start, QB) if dil == 1 else pl.ds(start, QB, stride=dil)
        for s in range(n_slab):
            sl = slice(s * LANES, (s + 1) * LANES)
            qn = qall[rho, pl.ds(2 * qb, 2), :, sl].reshape(QB, LANES)
            es, ms = [], []
            for hh in range(2):
                kw = kall[hh, rho, pl.ds(2 * qb, 4), :, sl].reshape(KW, LANES)
                sc = lax.dot_general(qn, kw, (((1,), (1,)), ((), ())), preferred_element_type=F32)
                sc = sc + bvar[var, 2 * s + hh]
                m = jnp.max(sc, axis=-1, keepdims=True)
                es.append(jnp.exp2(sc - m).astype(BF16))
                ms.append(m)
            v_even = vall[0, rho, pl.ds(2 * qb, 4), :, sl].reshape(KW, LANES)
            v_odd = vall[1, rho, pl.ds(2 * qb, 4), :, sl].reshape(KW, LANES)
            rhs = jnp.concatenate([jnp.concatenate([v_even, ones_even], axis=1),
                                   jnp.concatenate([v_odd, ones_odd], axis=1)], axis=0)
            res = jnp.dot(jnp.concatenate(es, axis=1), rhs, preferred_element_type=F32)
            den = res[:, LANES:]
            oacc[s, rows, :] = res[:, :LANES] * (1.0 / den)
            lse_ref[s, rows, :] = (jnp.where(lo, ms[0], ms[1]) + jnp.log2(den)) * LN2
        return carry

    lax.fori_loop(0, dil * nqb, body, 0, unroll=16)

    for s in range(n_slab):
        o_ref[:, s * LANES:(s + 1) * LANES] = oacc[s].astype(BF16)


def _attention_group(z, bias_g, gq, gk, first, last, gi, dil):
    nb = SEG // dil // ATT_BLOCK
    z5 = z.reshape(N_SEG, dil, nb, ATT_BLOCK, N_IN)
    cq = (COL_ATT + 3 * gi * ATT_W) // ATT_W
    ck, cv = cq + 1, cq + 2
    main = lambda cb: pl.BlockSpec((None, dil, nb, ATT_BLOCK, ATT_W), lambda c, f, l: (c, 0, 0, 0, cb))
    nxt = lambda cb: pl.BlockSpec((None, dil, 1, ATT_BLOCK, ATT_W),
                                  lambda c, f, l: (c + 1 - l[c], 0, 0, 0, cb))
    grid_spec = pltpu.PrefetchScalarGridSpec(
        num_scalar_prefetch=2,
        grid=(N_SEG,),
        in_specs=[
            main(cq), main(ck), nxt(ck), main(cv), nxt(cv),
            pl.BlockSpec((ATT_HEADS, QB, KW), lambda c, f, l: (0, 0, 0)),
            pl.BlockSpec((1, LANES), lambda c, f, l: (0, 0)),
            pl.BlockSpec((1, LANES), lambda c, f, l: (0, 0)),
        ],
        out_specs=[
            pl.BlockSpec((SEG, ATT_W), lambda c, f, l: (c, 0)),
            pl.BlockSpec((ATT_W // LANES, SEG, LANES), lambda c, f, l: (0, c, 0)),
        ],
        scratch_shapes=[
            pltpu.VMEM((2, dil, nb + 2, ATT_BLOCK, ATT_W), BF16),
            pltpu.VMEM((2, dil, nb + 2, ATT_BLOCK, ATT_W), BF16),
            pltpu.VMEM((dil, nb, ATT_BLOCK, ATT_W), BF16),
            pltpu.VMEM((ATT_W // LANES, SEG, LANES), F32),
            pltpu.VMEM((4, ATT_HEADS, QB, KW), F32),
        ],
    )
    return pl.pallas_call(
        functools.partial(_attention_kernel, dil=dil),
        grid_spec=grid_spec,
        out_shape=[jax.ShapeDtypeStruct((T_ALL, ATT_W), BF16), jax.ShapeDtypeStruct((ATT_W // LANES, T_ALL, LANES), F32)],
        compiler_params=_cparams(("arbitrary",), ATT_VMEM_LIMIT),
        name=f"attention_d{dil}",
    )(first, last, z5, z5, z5, z5, z5, bias_g, gq, gk)


def _merge_kernel(yret_ref, o0_ref, o1_ref, o2_ref, l0_ref, l1_ref, l2_ref,
                  gret_a_ref, gret_b_ref, gatt_a_ref, gatt_b_ref, xp_ref, xs_ref,
                  wret_ref, watt_ref, wout_ref, nffn_ref, wr_ref, wrhi_ref, br_ref,
                  x1_ref, hpa_ref, hpb_ref, idx_ref, gate_ref, rank_ref, cnt_ref, carry_ref):
    i = pl.program_id(0)
    tm = MERGE_TM
    sub = MERGE_SUB

    @pl.when(i == 0)
    def _():
        carry_ref[...] = jnp.zeros_like(carry_ref)

    is_prompt = i < T_PROMPT // tm
    nt_dims = (((1,), (1,)), ((), ()))
    erow_f = lax.broadcasted_iota(jnp.int32, (N_EXPERTS, sub), 0).astype(F32)
    tok_r = lax.broadcasted_iota(jnp.int32, (sub, sub), 0)
    tok_c = lax.broadcasted_iota(jnp.int32, (sub, sub), 1)
    earlier = jnp.where(tok_r < tok_c, 1.0, 0.0).astype(BF16)
    same_tok = jnp.where(tok_r == tok_c, 1.0, 0.0).astype(BF16)
    carry = carry_ref[:, 0:1]

    for r0 in range(0, tm, sub):
        rows = slice(r0, r0 + sub)
        l0, l1, l2 = [jnp.concatenate([r[s, rows, :] for s in range(ATT_W // LANES)], axis=-1)
                      for r in (l0_ref, l1_ref, l2_ref)]
        lm = jnp.maximum(jnp.maximum(l0, l1), l2)
        e0, e1, e2 = jnp.exp(l0 - lm), jnp.exp(l1 - lm), jnp.exp(l2 - lm)
        inv = 1.0 / (e0 + e1 + e2)
        y_att = ((e0 * inv) * o0_ref[rows, :].astype(F32) + (e1 * inv) * o1_ref[rows, :].astype(F32)
                 + (e2 * inv) * o2_ref[rows, :].astype(F32))

        p_ret = jnp.dot(yret_ref[rows, :], wret_ref[...], preferred_element_type=F32)
        p_att = jnp.dot(y_att.astype(BF16), watt_ref[...], preferred_element_type=F32)
        g_ret = jnp.concatenate([gret_a_ref[rows, :], gret_b_ref[rows, :]], axis=-1).astype(F32)
        g_att = jnp.concatenate([gatt_a_ref[rows, :], gatt_b_ref[rows, :]], axis=-1).astype(F32)
        merged = _sigmoid(g_ret) * p_ret + _sigmoid(g_att) * p_att
        x_in = jnp.where(is_prompt, xp_ref[rows, :], xs_ref[rows, :])
        x1 = x_in + jnp.dot(merged.astype(BF16), wout_ref[...], preferred_element_type=F32)
        x1_ref[rows, :] = x1

        ms = jnp.mean(x1 * x1, axis=-1, keepdims=True)
        h2 = x1 * lax.rsqrt(ms + EPS) * nffn_ref[...]
        hpa_ref[rows, :], hpb_ref[rows, :] = _pack_row_halves(h2)

        h_hi = h2.astype(BF16)
        h_lo = (h2 - h_hi.astype(F32)).astype(BF16)
        p1 = lax.dot_general(wr_ref[...], h_hi, nt_dims, preferred_element_type=F32)
        p2 = lax.dot_general(wrhi_ref[...], h_lo, nt_dims, preferred_element_type=F32)
        work = p1[:N_EXPERTS] + p1[N_EXPERTS:2 * N_EXPERTS] + p2[:N_EXPERTS] + br_ref[...]
        vals, idxs = [], []
        for _ in range(TOP_K):
            m = jnp.max(work, axis=0, keepdims=True)
            ix = jnp.min(jnp.where(work == m, erow_f, float(N_EXPERTS)), axis=0, keepdims=True)
            vals.append(m)
            idxs.append(ix)
            work = jnp.where(erow_f == ix, -3e38, work)
        es = [jnp.exp(v - vals[0]) for v in vals]
        den = es[0] + es[1] + es[2] + es[3]
        onehot = jnp.zeros((N_EXPERTS, sub), F32)
        for ix in idxs:
            onehot = onehot + jnp.where(erow_f == ix, 1.0, 0.0)
        before = jnp.dot(onehot.astype(BF16), earlier, preferred_element_type=F32) + carry
        ranks = [jnp.sum(jnp.where(erow_f == ix, before, 0.0), axis=0, keepdims=True) for ix in idxs]
        pad_rows = jnp.zeros((8 - TOP_K, sub), F32)
        idx_ref[:, rows] = jnp.concatenate(idxs + [pad_rows], axis=0).astype(jnp.int32)
        rank_ref[:, rows] = jnp.concatenate(ranks + [pad_rows], axis=0).astype(jnp.int32)
        g_t = jnp.concatenate([e / den for e in es] + [jnp.zeros((LANES - TOP_K, sub), F32)], axis=0)
        g_hi = g_t.astype(BF16)
        g_lo = (g_t - g_hi.astype(F32)).astype(BF16)
        gate_ref[rows, :] = (lax.dot_general(same_tok, g_hi, nt_dims, preferred_element_type=F32)
                             + lax.dot_general(same_tok, g_lo, nt_dims, preferred_element_type=F32))
        carry = carry + jnp.sum(onehot, axis=1, keepdims=True)

    carry_ref[...] = jnp.broadcast_to(carry, carry_ref.shape)
    cnt_ref[...] = jnp.broadcast_to(carry, cnt_ref.shape)


def _merge(y_ret, o_list, lse_list, z, x_p, x_s, w_ret, w_att, w_out, n_ffn, w_router, w_router_hi, b_router):
    tm = MERGE_TM
    n_p = T_PROMPT // tm
    row = lambda w: pl.BlockSpec((tm, w), lambda i: (i, 0))
    full = lambda a: pl.BlockSpec(a.shape, lambda i: (0,) * a.ndim)
    zcol = lambda col: pl.BlockSpec((tm, COL_BLK), lambda i: (i, col // COL_BLK))
    lse_spec = pl.BlockSpec((ATT_W // LANES, tm, LANES), lambda i: (0, i, 0))
    kmajor = pl.BlockSpec((8, tm), lambda i: (0, i))
    return pl.pallas_call(
        _merge_kernel,
        grid=(T_ALL // tm,),
        in_specs=[row(RET_V_W), row(ATT_W), row(ATT_W), row(ATT_W), lse_spec, lse_spec, lse_spec,
                  zcol(COL_GATE_RET), zcol(COL_GATE_RET + COL_BLK), zcol(COL_GATE_ATT),
                  zcol(COL_GATE_ATT + COL_BLK),
                  pl.BlockSpec((tm, D_MODEL), lambda i: (jnp.minimum(i, n_p - 1), 0)),
                  pl.BlockSpec((tm, D_MODEL), lambda i: (jnp.maximum(i - n_p, 0), 0)),
                  full(w_ret), full(w_att), full(w_out), full(n_ffn), full(w_router), full(w_router_hi),
                  full(b_router)],
        out_specs=[row(D_MODEL), row(HALF_W), row(HALF_W), kmajor, row(LANES), kmajor,
                   pl.BlockSpec((N_EXPERTS, LANES), lambda i: (0, 0))],
        out_shape=[jax.ShapeDtypeStruct((T_ALL, D_MODEL), F32),
                   jax.ShapeDtypeStruct((T_ALL, HALF_W), jnp.uint32),
                   jax.ShapeDtypeStruct((T_ALL, HALF_W), jnp.uint32),
                   jax.ShapeDtypeStruct((8, T_ALL), jnp.int32),
                   jax.ShapeDtypeStruct((T_ALL, LANES), F32),
                   jax.ShapeDtypeStruct((8, T_ALL), jnp.int32),
                   jax.ShapeDtypeStruct((N_EXPERTS, LANES), F32)],
        scratch_shapes=[pltpu.VMEM((N_EXPERTS, LANES), F32)],
        compiler_params=_cparams(("arbitrary",)),
        name="merge_router",
    )(y_ret, *o_list, *lse_list, z, z, z, z, x_p, x_s, w_ret, w_att, w_out, n_ffn, w_router, w_router_hi, b_router)


def _sc_mesh():
    return plsc.VectorSubcoreMesh(core_axis_name="core", subcore_axis_name="subcore")


def _sc_scatter_rows(xs, idx_kmajor, n_out):
    n_rows, width = xs[0].shape
    out_type = [jax.ShapeDtypeStruct((n_out, width), x.dtype) for x in xs]

    @pl.kernel(out_type=out_type, mesh=_sc_mesh(), scratch_types=[])
    def scatter(*refs):
        x_refs, i_hbm, o_refs = refs[:len(xs)], refs[len(xs)], refs[len(xs) + 1:]
        for x_hbm, o_hbm in zip(x_refs, o_refs):
            def body(x_vmem, i_vmem, o_hbm=o_hbm):
                for k in range(TOP_K):
                    pltpu.sync_copy(x_vmem, o_hbm.at[i_vmem.at[k]])

            pltpu.emit_pipeline(
                body,
                grid=(n_rows // SC_WINDOW,),
                in_specs=[pl.BlockSpec((SC_WINDOW, width), lambda i: (i, 0)),
                          pl.BlockSpec((TOP_K, SC_WINDOW), lambda i: (0, i))],
                out_specs=[],
                core_axis_name=("core", "subcore"),
                dimension_semantics=(pltpu.PARALLEL,),
            )(x_hbm, i_hbm)

    return scatter(*xs, idx_kmajor)


def _sc_gather_rows(datas, idx):
    n_idx = idx.shape[0]
    width = datas[0].shape[1]
    out_type = [jax.ShapeDtypeStruct((n_idx, width), d.dtype) for d in datas]

    @pl.kernel(out_type=out_type, mesh=_sc_mesh(), scratch_types=[])
    def gather(*refs):
        x_refs, i_hbm, o_refs = refs[:len(datas)], refs[len(datas)], refs[len(datas) + 1:]
        for x_hbm, o_hbm in zip(x_refs, o_refs):
            def body(i_vmem, o_vmem, x_hbm=x_hbm):
                pltpu.sync_copy(x_hbm.at[i_vmem.at[0]], o_vmem)

            pltpu.emit_pipeline(
                body,
                grid=(n_idx // SC_WINDOW,),
                in_specs=[pl.BlockSpec((1, SC_WINDOW), lambda i: (0, i))],
                out_specs=[pl.BlockSpec((SC_WINDOW, width), lambda i: (i, 0))],
                core_axis_name=("core", "subcore"),
                dimension_semantics=(pltpu.PARALLEL,),
            )(i_hbm, o_hbm)

    return gather(*datas, idx.reshape(1, n_idx))


def _expert_kernel(be_ref, nused_ref, nvalid_ref, xa_ref, xb_ref, wg_ref, bg_ref, wu_ref, bu_ref, wd_ref, bd_ref,
                   ya_ref, yb_ref, wbf_ref):
    b = pl.program_id(0)
    active = b < nused_ref[0]
    new_expert = (b == 0) | (be_ref[b] != be_ref[jnp.maximum(b - 1, 0)])

    @pl.when(active & new_expert)
    def _():
        rows = 128
        for wi, w_ref in enumerate((wg_ref, wu_ref, wd_ref)):
            for r in range(0, D_MODEL, rows):
                wbf_ref[wi, r:r + rows, :] = w_ref[0, r:r + rows, :].astype(BF16)

    @pl.when(active)
    def _():
        valid = lax.broadcasted_iota(jnp.int32, (MOE_BM, HALF_W), 0) < nvalid_ref[b]
        zero = jnp.zeros((MOE_BM, HALF_W), jnp.uint32)
        x = _unpack_row_halves(jnp.where(valid, xa_ref[...], zero), jnp.where(valid, xb_ref[...], zero)).astype(BF16)
        g = jnp.dot(x, wbf_ref[0], preferred_element_type=F32) + bg_ref[0]
        u = jnp.dot(x, wbf_ref[1], preferred_element_type=F32) + bu_ref[0]
        g = jnp.minimum(g, SWIGLU_LIMIT)
        u = jnp.clip(u, -SWIGLU_LIMIT, SWIGLU_LIMIT)
        glu = g * jax.nn.sigmoid(SWIGLU_ALPHA * g)
        act = ((u + 1.0) * glu).astype(BF16)
        y = jnp.dot(act, wbf_ref[2], preferred_element_type=F32) + bd_ref[0]
        ya_ref[...], yb_ref[...] = _pack_row_halves(y)

    @pl.when(jnp.logical_not(active))
    def _():
        ya_ref[...] = jnp.zeros_like(ya_ref)
        yb_ref[...] = jnp.zeros_like(yb_ref)


def _experts(block_expert, n_used, n_valid, xs_a, xs_b, wg, bg, wu, bu, wd, bd):
    assert D_FF == D_MODEL
    blk = lambda b, be, nu, nv: (jnp.minimum(b, nu[0] - 1), 0)
    wsp = lambda: pl.BlockSpec((1, D_MODEL, D_FF), lambda b, be, nu, nv: (be[b], 0, 0))
    bsp = lambda: pl.BlockSpec((1, 1, D_FF), lambda b, be, nu, nv: (be[b], 0, 0))
    xsp = lambda: pl.BlockSpec((MOE_BM, HALF_W), blk)
    ysp = lambda: pl.BlockSpec((MOE_BM, HALF_W), lambda b, be, nu, nv: (b, 0))
    slot_arr = jax.ShapeDtypeStruct((N_SLOTS, HALF_W), jnp.uint32)
    grid_spec = pltpu.PrefetchScalarGridSpec(
        num_scalar_prefetch=3,
        grid=(N_SLOT_BLOCKS,),
        in_specs=[xsp(), xsp(), wsp(), bsp(), wsp(), bsp(), wsp(), bsp()],
        out_specs=[ysp(), ysp()],
        scratch_shapes=[pltpu.VMEM((3, D_MODEL, D_FF), BF16)],
    )
    return pl.pallas_call(
        _expert_kernel,
        grid_spec=grid_spec,
        out_shape=[slot_arr, slot_arr],
        compiler_params=_cparams(("arbitrary",)),
        name="experts",
    )(block_expert, n_used, n_valid, xs_a, xs_b, wg, bg, wu, bu, wd, bd)


def _final_kernel(x1_ref, yga_ref, ygb_ref, gate_ref, p_ref, nple_ref, wpg_ref, wpp_ref, out_ref):
    x2 = x1_ref[...]
    gates = gate_ref[...]
    for k in range(TOP_K):
        x2 = x2 + gates[:, k:k + 1] * _unpack_row_halves(yga_ref[k], ygb_ref[k])
    ms = jnp.mean(x2 * x2, axis=-1, keepdims=True)
    h3 = (x2 * lax.rsqrt(ms + EPS) * nple_ref[...]).astype(BF16)
    gate = jax.nn.sigmoid(jnp.dot(h3, wpg_ref[...], preferred_element_type=F32))
    proj = jnp.dot(p_ref[...].astype(BF16), wpp_ref[...], preferred_element_type=F32)
    out_ref[...] = x2 + gate * proj


def _final(x1, yg_a, yg_b, gates, p, n_ple, w_pg, w_pp, row0, n_rows):
    tm = FINAL_TM
    off = row0 // tm
    full = lambda a: pl.BlockSpec(a.shape, lambda i: (0,) * a.ndim)
    return pl.pallas_call(
        _final_kernel,
        grid=(n_rows // tm,),
        in_specs=[pl.BlockSpec((tm, D_MODEL), lambda i: (i + off, 0)),
                  pl.BlockSpec((TOP_K, tm, HALF_W), lambda i: (0, i + off, 0)),
                  pl.BlockSpec((TOP_K, tm, HALF_W), lambda i: (0, i + off, 0)),
                  pl.BlockSpec((tm, LANES), lambda i: (i + off, 0)),
                  pl.BlockSpec((tm, PLE_DIM), lambda i: (i, 0)),
                  full(n_ple), full(w_pg), full(w_pp)],
        out_specs=pl.BlockSpec((tm, D_MODEL), lambda i: (i, 0)),
        out_shape=jax.ShapeDtypeStruct((n_rows, D_MODEL), F32),
        compiler_params=_cparams(("arbitrary",)),
        name="final_ple",
    )(x1, yg_a, yg_b, gates, p, n_ple, w_pg, w_pp)


def _rope_tables():
    half = RET_DK // 2
    step = 128
    freq = ROPE_THETA ** (-jnp.arange(half, dtype=F32) / half)
    freq = jnp.concatenate([freq, freq])
    sign = jnp.where(jnp.arange(RET_DK) < half, -1.0, 1.0).astype(F32)
    ang_lo = jnp.arange(step, dtype=F32)[:, None] * freq[None, :]
    ang_hi = (jnp.arange(SAMPLE_SEQ // step, dtype=F32) * step)[:, None] * freq[None, :]
    c_lo, s_lo = jnp.cos(ang_lo)[None], jnp.sin(ang_lo)[None]
    c_hi, s_hi = jnp.cos(ang_hi)[:, None], jnp.sin(ang_hi)[:, None]
    cos = (c_hi * c_lo - s_hi * s_lo).reshape(SAMPLE_SEQ, RET_DK)
    sin = ((s_hi * c_lo + c_hi * s_lo) * sign).reshape(SAMPLE_SEQ, RET_DK)
    return cos, sin


def _retention_tables(decay_logit):
    lg = jax.nn.log_sigmoid(decay_logit.astype(F32))
    c = RET_CHUNK
    idx = jnp.arange(c, dtype=F32)
    diff = idx[:, None] - idx[None, :]
    lf = lg[0][:, None, None]
    lb = lg[1][:, None, None]
    mask = jnp.where(diff[None] >= 0, jnp.exp(lf * jnp.maximum(diff, 0.0)[None]),
                     jnp.exp(lb * jnp.maximum(-diff, 0.0)[None]))
    kdec_f = jnp.exp(lg[0][:, None] * (c - 1.0 - idx)[None, :])
    qdec_f = jnp.exp(lg[0][:, None] * (idx + 1.0)[None, :])
    kdec_b = jnp.exp(lg[1][:, None] * idx[None, :])
    qdec_b = jnp.exp(lg[1][:, None] * (c - idx)[None, :])
    dec = jnp.stack([kdec_f, qdec_f, kdec_b, qdec_b], axis=1)[..., None]
    cdec = jnp.exp(lg * c).T
    cdec = jnp.broadcast_to(cdec[:, :, None, None], (RET_HEADS, 2, 1, RET_DV))
    return mask, dec, cdec


def _t5_bucket(rel):
    half = T5_BUCKETS // 2
    exact = half // 2
    n = np.abs(rel)
    ratio = np.log(np.maximum(n, 1).astype(np.float32) / np.float32(exact)) / np.float32(math.log(T5_MAX_DIST / exact))
    large = exact + (ratio * np.float32(half - exact)).astype(np.int32)
    large = np.minimum(large, half - 1)
    return np.where(rel > 0, half, 0) + np.where(n < exact, n, large)


def _attention_bias(rel_bias, gi, dil, radius):
    qi = np.arange(QB)
    ki = np.arange(KW) - ATT_BLOCK
    rel = ki[None, :] - qi[:, None]
    onehot = jnp.asarray(_t5_bucket(rel * dil)[..., None] == np.arange(T5_BUCKETS), F32)
    tab = rel_bias[:, gi * ATT_HEADS:(gi + 1) * ATT_HEADS].astype(F32)
    bias = jnp.einsum('qkb,bh->hqk', onehot, tab, precision=lax.Precision.HIGHEST)
    return jnp.where(jnp.asarray(np.abs(rel) <= radius)[None], bias * LOG2E, NEG)


def _seq_edge_flags():
    first = np.zeros((N_SEG,), np.int32)
    last = np.zeros((N_SEG,), np.int32)
    first[:N_PROMPT_SEG] = 1
    last[:N_PROMPT_SEG] = 1
    first[N_PROMPT_SEG] = 1
    last[N_SEG - 1] = 1
    return jnp.asarray(first), jnp.asarray(last)


def _pad_lanes(a, value=0.0):
    return jnp.pad(a, ((0, 0), (0, LANES - a.shape[-1])), constant_values=value)


def kernel(x_prompt, x_sample, p_prompt, p_sample, norm_mix_g, w_in, ret_decay_logit, ret_gn_g,
           att_q_norm_g, att_k_norm_g, rel_bias, w_ret_proj, w_att_proj, w_out, norm_ffn_g,
           w_router, b_router, w_gate, b_gate, w_up, b_up, w_down, b_down,
           norm_ple_g, w_ple_gate, w_ple_proj):
    assert norm_mix_g.shape[0] == 1, "one layer"
    x_p = x_prompt.reshape(T_PROMPT, D_MODEL)
    x_s = x_sample.reshape(SAMPLE_SEQ, D_MODEL)

    cos_t, sin_t = _rope_tables()
    z = _in_proj(x_p, x_s, norm_mix_g.astype(F32), w_in[0].astype(BF16), cos_t, sin_t)

    ret_mask, ret_dec, ret_cdec = _retention_tables(ret_decay_logit[0])
    y_ret = _retention(z, ret_mask, ret_dec, ret_cdec, ret_gn_g[0].reshape(RET_HEADS, 1, RET_DV).astype(F32))

    first, last = _seq_edge_flags()
    o_list, lse_list = [], []
    for gi, (window, dil) in enumerate(ATT_GROUPS):
        bias_g = _attention_bias(rel_bias, gi, dil, window // (2 * dil))
        gq = jnp.tile(att_q_norm_g[0, gi].astype(F32) * (ATT_DH ** -0.5 * LOG2E), LANES // ATT_DH)[None, :]
        gk = jnp.tile(att_k_norm_g[0, gi].astype(F32), LANES // ATT_DH)[None, :]
        o_g, lse_g = _attention_group(z, bias_g, gq, gk, first, last, gi, dil)
        o_list.append(o_g)
        lse_list.append(lse_g)

    w_r = w_router[0].astype(F32)
    w_r_hi = w_r.astype(BF16)
    w_r_lo = (w_r - w_r_hi.astype(F32)).astype(BF16)
    w_router_cat = _pad_lanes(jnp.concatenate([w_r_hi, w_r_lo], axis=1)).T
    w_router_hi = _pad_lanes(w_r_hi).T
    b_router_p = b_router.astype(F32).reshape(N_EXPERTS, 1)
    x1, hp_a, hp_b, idx, gates, rank, cnt = _merge(
        y_ret, o_list, lse_list, z, x_p, x_s, w_ret_proj[0].astype(BF16), w_att_proj[0].astype(BF16),
        w_out[0].astype(BF16), norm_ffn_g.astype(F32), w_router_cat, w_router_hi, b_router_p)

    counts = cnt[:, 0].astype(jnp.int32)
    padded = (counts + MOE_BM - 1) // MOE_BM * MOE_BM
    pad_end = jnp.cumsum(padded)
    pad_start = pad_end - padded
    expert_ids = jnp.arange(N_EXPERTS, dtype=jnp.int32)
    top_idx = idx[:TOP_K]
    start_of = jnp.sum(jnp.where(top_idx[:, :, None] == expert_ids, pad_start, 0), axis=-1)
    dest_kmajor = start_of + rank[:TOP_K]
    n_used = (pad_end[-1] // MOE_BM).astype(jnp.int32).reshape(1)
    blk_row0 = jnp.arange(N_SLOT_BLOCKS, dtype=jnp.int32) * MOE_BM
    block_expert = jnp.minimum(jnp.sum((pad_end[None, :] <= blk_row0[:, None]).astype(jnp.int32), axis=1),
                               N_EXPERTS - 1).astype(jnp.int32)
    slot_end = jnp.sum(jnp.where(block_expert[:, None] == expert_ids, pad_start + counts, 0), axis=-1)
    n_valid = jnp.clip(slot_end - blk_row0, 0, MOE_BM).astype(jnp.int32)

    xs_a, xs_b = _sc_scatter_rows((hp_a, hp_b), dest_kmajor, N_SLOTS)
    ys_a, ys_b = _experts(block_expert, n_used, n_valid, xs_a, xs_b,
                          w_gate[0], b_gate[0].reshape(N_EXPERTS, 1, D_FF).astype(F32),
                          w_up[0], b_up[0].reshape(N_EXPERTS, 1, D_FF).astype(F32),
                          w_down[0], b_down[0].reshape(N_EXPERTS, 1, D_MODEL).astype(F32))
    dest_flat = dest_kmajor.reshape(-1)
    yg_a, yg_b = [y.reshape(TOP_K, T_ALL, HALF_W) for y in _sc_gather_rows((ys_a, ys_b), dest_flat)]

    n_ple = norm_ple_g.astype(F32)
    w_pg = w_ple_gate[0].astype(BF16)
    w_pp = w_ple_proj[0].astype(BF16)
    y_p = _final(x1, yg_a, yg_b, gates, p_prompt[0].reshape(T_PROMPT, PLE_DIM), n_ple, w_pg, w_pp, 0, T_PROMPT)
    y_s = _final(x1, yg_a, yg_b, gates, p_sample[0].reshape(SAMPLE_SEQ, PLE_DIM), n_ple, w_pg, w_pp, T_PROMPT, SAMPLE_SEQ)
    return (y_p.reshape(x_prompt.shape), y_s.reshape(x_sample.shape))
```
